```python
import jax, jax.numpy as jnp
from jax import lax

D_MODEL = 2048
BATCH = 4
SEQ = 2048
DEPTH = 1
DEC_BATCH = 128
DEC_SEQ = 1
PAST_LEN = 16384
PAGE_SIZE = 128

D_MIX = D_MODEL
D_GDN = D_MIX // 2
D_POOL = D_MIX - D_GDN
GDN_HEAD_DIM = 128
GDN_HEADS = D_GDN // GDN_HEAD_DIM
D_QKV = 3 * D_GDN
CONV_W = 4
CHUNK = 64
POOL_WINDOWS = (2, 4, 8, 16)
POOL_GROUPS = len(POOL_WINDOWS)
POOL_GROUP_DIM = D_POOL // POOL_GROUPS
POOL_BUF = max(POOL_WINDOWS) - 1
D_FF = 4 * D_MODEL
N_IN = D_QKV + D_GDN + 2 * GDN_HEADS + D_POOL
EPS = 1e-6

kernel_name = "hymba_gdn_pool_sandwich_decode"


def _rmsnorm(x, w):
    xf = x.astype(jnp.float32)
    y = xf * lax.rsqrt(jnp.mean(xf * xf, axis=-1, keepdims=True) + EPS)
    return (y * w.astype(jnp.float32)).astype(x.dtype)


def _l2norm(x):
    return x * lax.rsqrt(jnp.sum(x * x, axis=-1, keepdims=True) + EPS)


def _gdn_chunked(q, k, v, g, beta, s0):
    B, L, H, DK = q.shape
    DV = v.shape[-1]
    nc = L // CHUNK
    to_c = lambda t: t.reshape(B, nc, CHUNK, H, -1).transpose(0, 3, 1, 2, 4)
    q, k, v = to_c(q), to_c(k), to_c(v)
    g = g.reshape(B, nc, CHUNK, H).transpose(0, 3, 1, 2)
    beta = beta.reshape(B, nc, CHUNK, H).transpose(0, 3, 1, 2)
    gc = jnp.cumsum(g, axis=-1)
    tril = jnp.tril(jnp.ones((CHUNK, CHUNK), bool))
    strict = jnp.tril(jnp.ones((CHUNK, CHUNK), bool), -1)
    decay = jnp.exp(jnp.where(tril, gc[..., :, None] - gc[..., None, :], -jnp.inf))
    kb = k * beta[..., None]
    lmat = jnp.where(strict, jnp.einsum('bhncd,bhnsd->bhncs', kb, k) * decay, 0.0)
    a_mat = lmat + jnp.eye(CHUNK, dtype=jnp.float32)
    rhs = jnp.concatenate([v * beta[..., None], kb * jnp.exp(gc)[..., None]], axis=-1)
    sol = lax.linalg.triangular_solve(a_mat, rhs, left_side=True, lower=True)
    u, w = sol[..., :DV], sol[..., DV:]
    intra = jnp.where(tril, jnp.einsum('bhncd,bhnsd->bhncs', q, k) * decay, 0.0)

    def step(s, inp):
        qi, ki, ui, wi, gci, ai = inp
        v_new = ui - jnp.einsum('bhcd,bhde->bhce', wi, s)
        o = jnp.einsum('bhcd,bhde->bhce', qi * jnp.exp(gci)[..., None], s) + jnp.einsum('bhcs,bhse->bhce', ai, v_new)
        g_last = gci[..., -1]
        s = s * jnp.exp(g_last)[..., None, None] + jnp.einsum(
            'bhcd,bhce->bhde', ki * jnp.exp(g_last[..., None] - gci)[..., None], v_new)
        return s, o

    mv = lambda t: jnp.moveaxis(t, 2, 0)
    s_fin, o = lax.scan(step, s0, (mv(q), mv(k), mv(u), mv(w), mv(gc), mv(intra)))
    o = o.transpose(1, 0, 3, 2, 4).reshape(B, L, H, DV)
    return o, s_fin


def _gdn_recurrent(q, k, v, g, beta, s0):
    def step(s, inp):
        qt, kt, vt, gt, bt = inp
        s = s * jnp.exp(gt)[..., None, None]
        delta = (vt - jnp.einsum('bhd,bhde->bhe', kt, s)) * bt[..., None]
        s = s + kt[..., :, None] * delta[..., None, :]
        return s, jnp.einsum('bhd,bhde->bhe', qt, s)

    mv = lambda t: jnp.moveaxis(t, 1, 0)
    s_fin, o = lax.scan(step, s0, (mv(q), mv(k), mv(v), mv(g), mv(beta)))
    return jnp.moveaxis(o, 0, 1), s_fin


def _layer(x, pos0, conv_buf, pool_buf, s0, chunked, norm_pre_mix, w_in, conv_w, a_log, dt_bias,
           norm_gdn_out, w_pool, pool_scale, w_out, norm_post_mix, norm_pre_mlp, w_up, w_down, norm_post_mlp):
    B, L, _ = x.shape
    h = _rmsnorm(x, norm_pre_mix)
    proj = h @ w_in
    o1 = D_QKV; o2 = o1 + D_GDN; o3 = o2 + GDN_HEADS; o4 = o3 + GDN_HEADS
    qkv_in, gate, a_in, b_in, u = proj[..., :o1], proj[..., o1:o2], proj[..., o2:o3], proj[..., o3:o4], proj[..., o4:]

    xp = jnp.concatenate([conv_buf.astype(qkv_in.dtype), qkv_in], axis=1)
    conv = sum(conv_w[j] * xp[:, j:j + L] for j in range(CONV_W))
    new_conv = xp[:, -(CONV_W - 1):]
    qkv = jax.nn.silu(conv.astype(jnp.float32)).reshape(B, L, 3, GDN_HEADS, GDN_HEAD_DIM)
    q = _l2norm(qkv[:, :, 0]) * (GDN_HEAD_DIM ** -0.5)
    k = _l2norm(qkv[:, :, 1])
    v = qkv[:, :, 2]
    beta = jax.nn.sigmoid(b_in.astype(jnp.float32))
    g = -jnp.exp(a_log.astype(jnp.float32)) * jax.nn.softplus(a_in.astype(jnp.float32) + dt_bias.astype(jnp.float32))
    core = _gdn_chunked if chunked else _gdn_recurrent
    o_a, s_new = core(q, k, v, g, beta, s0.astype(jnp.float32))
    o_a = _rmsnorm(o_a, norm_gdn_out) * jax.nn.silu(gate.astype(jnp.float32)).reshape(B, L, GDN_HEADS, GDN_HEAD_DIM)
    o_a = o_a.reshape(B, L, D_GDN).astype(x.dtype)

    up = jnp.concatenate([pool_buf.astype(u.dtype), u], axis=1)
    upf = up.astype(jnp.float32)
    c0 = jnp.concatenate([jnp.zeros((B, 1, D_POOL), jnp.float32), jnp.cumsum(upf, axis=1)], axis=1)
    pos = pos0 + jnp.arange(L)
    cur = upf[:, POOL_BUF:]
    groups = []
    for i, win in enumerate(POOL_WINDOWS):
        sl = slice(i * POOL_GROUP_DIM, (i + 1) * POOL_GROUP_DIM)
        wsum = c0[:, POOL_BUF + 1:POOL_BUF + 1 + L, sl] - c0[:, POOL_BUF + 1 - win:POOL_BUF + 1 - win + L, sl]
        cnt = jnp.minimum(pos + 1, win).astype(jnp.float32)[None, :, None]
        groups.append(wsum / cnt - cur[..., sl])
    pooled = jnp.stack(groups, axis=2).astype(x.dtype)
    o_b = jnp.einsum('blgc,gcd->blgd', pooled, w_pool).reshape(B, L, D_POOL) * pool_scale
    new_pool = up[:, -POOL_BUF:]

    mix = jnp.concatenate([o_a, o_b.astype(x.dtype)], axis=-1) @ w_out
    x = x + _rmsnorm(mix, norm_post_mix)

    hm = _rmsnorm(x, norm_pre_mlp)
    ff = jnp.square(jax.nn.relu(hm @ w_up)) @ w_down
    x = x + _rmsnorm(ff, norm_post_mlp)
    return x, s_new.astype(x.dtype), new_conv, new_pool


def setup_inputs(seed: int = 0) -> dict:
    key = jax.random.key(seed)
    ks = jax.random.split(key, 20)
    f32 = jnp.float32
    nrm = lambda k, shape, scale: jax.random.normal(k, shape, f32) * scale
    gain = lambda k, n: 1.0 + 0.05 * jax.random.normal(k, (DEPTH, n), f32)
    return {
        "x_prompt": nrm(ks[0], (BATCH, SEQ, D_MODEL), 1.0),
        "x_sample": nrm(ks[1], (DEC_BATCH, DEC_SEQ, D_MODEL), 1.0),
        "state_gdn": nrm(ks[2], (DEPTH, DEC_BATCH, GDN_HEADS, GDN_HEAD_DIM, GDN_HEAD_DIM), 0.1),
        "state_conv": nrm(ks[3], (DEPTH, DEC_BATCH, CONV_W - 1, D_QKV), 1.0),
        "state_pool": nrm(ks[4], (DEPTH, DEC_BATCH, POOL_BUF, D_POOL), 1.0),
        "norm_pre_mix": gain(ks[5], D_MODEL),
        "w_in": nrm(ks[6], (DEPTH, D_MODEL, N_IN), D_MODEL ** -0.5),
        "conv_w": nrm(ks[7], (DEPTH, CONV_W, D_QKV), CONV_W ** -0.5),
        "a_log": jnp.log(jax.random.uniform(ks[8], (DEPTH, GDN_HEADS), f32, 1.0, 16.0)),
        "dt_bias": nrm(ks[9], (DEPTH, GDN_HEADS), 0.1),
        "norm_gdn_out": gain(ks[10], GDN_HEAD_DIM),
        "w_pool": nrm(ks[11], (DEPTH, POOL_GROUPS, POOL_GROUP_DIM, POOL_GROUP_DIM), POOL_GROUP_DIM ** -0.5),
        "pool_scale": 1.0 + 0.1 * jax.random.normal(ks[12], (DEPTH, D_POOL), f32),
        "w_out": nrm(ks[13], (DEPTH, D_MIX, D_MODEL), D_MIX ** -0.5),
        "norm_post_mix": gain(ks[14], D_MODEL),
        "norm_pre_mlp": gain(ks[15], D_MODEL),
        "w_up": nrm(ks[16], (DEPTH, D_MODEL, D_FF), D_MODEL ** -0.5),
        "w_down": nrm(ks[17], (DEPTH, D_FF, D_MODEL), D_FF ** -0.5),
        "norm_post_mlp": gain(ks[18], D_MODEL),
    }


def reference(x_prompt, x_sample, state_gdn, state_conv, state_pool, norm_pre_mix, w_in, conv_w, a_log,
              dt_bias, norm_gdn_out, w_pool, pool_scale, w_out, norm_post_mix, norm_pre_mlp, w_up, w_down,
              norm_post_mlp):
    yp, ys = x_prompt, x_sample
    gdn_p, conv_p, pool_p, gdn_s, conv_s, pool_s = [], [], [], [], [], []
    for l in range(DEPTH):
        params = (norm_pre_mix[l], w_in[l], conv_w[l], a_log[l], dt_bias[l], norm_gdn_out[l], w_pool[l],
                  pool_scale[l], w_out[l], norm_post_mix[l], norm_pre_mlp[l], w_up[l], w_down[l], norm_post_mlp[l])
        B = yp.shape[0]
        yp, s1, c1, p1 = _layer(
            yp, 0,
            jnp.zeros((B, CONV_W - 1, D_QKV), yp.dtype),
            jnp.zeros((B, POOL_BUF, D_POOL), yp.dtype),
            jnp.zeros((B, GDN_HEADS, GDN_HEAD_DIM, GDN_HEAD_DIM), jnp.float32),
            True, *params)
        ys, s2, c2, p2 = _layer(ys, PAST_LEN, state_conv[l], state_pool[l], state_gdn[l], False, *params)
        gdn_p.append(s1); conv_p.append(c1); pool_p.append(p1)
        gdn_s.append(s2); conv_s.append(c2); pool_s.append(p2)
    new_gdn_prompt = jnp.stack(gdn_p)
    new_conv_prompt = jnp.stack(conv_p)
    new_pool_prompt = jnp.stack(pool_p)
    new_gdn_sample = jnp.stack(gdn_s)
    new_conv_sample = jnp.stack(conv_s)
    new_pool_sample = jnp.stack(pool_s)
    return (yp, ys, new_gdn_prompt, new_conv_prompt, new_pool_prompt, new_gdn_sample, new_conv_sample, new_pool_sample)
```

```python
import functools

import jax
import jax.numpy as jnp
from jax import lax
from jax.experimental import pallas as pl
from jax.experimental.pallas import tpu as pltpu

F32 = jnp.float32
BF16 = jnp.bfloat16

EPS = 1e-6
HEAD_DIM = 128
CONV_W = 4
POOL_WINDOWS = (2, 4, 8, 16)
POOL_BUF = max(POOL_WINDOWS) - 1
PAST_LEN = 16384
HALO_CONV = 8
HALO_POOL = 16
GDN_CHUNK = 128
VMEM_LIMIT = 56 * 1024 * 1024


def _cparams(n_axes):
    return pltpu.CompilerParams(
        dimension_semantics=("arbitrary",) * n_axes, vmem_limit_bytes=VMEM_LIMIT)


def _dot(a, b):
    return jnp.dot(a.astype(BF16), b.astype(BF16), preferred_element_type=F32)


def _dot_nt(a, b):
    return lax.dot_general(a.astype(BF16), b.astype(BF16), (((1,), (1,)), ((), ())),
                           preferred_element_type=F32)


def _rms(x, w):
    return x * lax.rsqrt(jnp.mean(x * x, axis=-1, keepdims=True) + EPS) * w


def _silu(x):
    return x * jax.nn.sigmoid(x)


def _softplus(x):
    return jnp.maximum(x, 0.0) + jnp.log(1.0 + jnp.exp(-jnp.abs(x)))


def _inproj_kernel(x_ref, nw_ref, w_ref, wab_ref, out_ref, ab_ref, h_ref):
    @pl.when(pl.program_id(1) == 0)
    def _():
        h = _rms(x_ref[...], nw_ref[...]).astype(BF16)
        h_ref[...] = h
        ab_ref[...] = jnp.dot(h, wab_ref[...], preferred_element_type=F32)

    out_ref[...] = jnp.dot(h_ref[...], w_ref[...], preferred_element_type=F32)


def _in_proj(x, norm_w, w_main, w_ab, tm, tn):
    m, d = x.shape
    n = w_main.shape[1]
    return pl.pallas_call(
        _inproj_kernel,
        grid=(m // tm, n // tn),
        in_specs=[
            pl.BlockSpec((tm, d), lambda i, j: (i, 0)),
            pl.BlockSpec((1, d), lambda i, j: (0, 0)),
            pl.BlockSpec((d, tn), lambda i, j: (0, j)),
            pl.BlockSpec((d, HEAD_DIM), lambda i, j: (0, 0)),
        ],
        out_specs=[
            pl.BlockSpec((tm, tn), lambda i, j: (i, j)),
            pl.BlockSpec((tm, HEAD_DIM), lambda i, j: (i, 0)),
        ],
        out_shape=[jax.ShapeDtypeStruct((m, n), F32), jax.ShapeDtypeStruct((m, HEAD_DIM), F32)],
        scratch_shapes=[pltpu.VMEM((tm, d), BF16)],
        compiler_params=_cparams(2),
        name="in_proj",
    )(x, norm_w, w_main, w_ab)


def _gate_columns(ab, alog, dtb, head, n_heads):
    lane = lax.broadcasted_iota(jnp.int32, ab.shape, 1)
    g_all = -jnp.exp(alog) * _softplus(ab + dtb)
    beta_all = jax.nn.sigmoid(ab)
    g = jnp.sum(jnp.where(lane == head, g_all, 0.0), axis=1, keepdims=True)
    beta = jnp.sum(jnp.where(lane == head + n_heads, beta_all, 0.0), axis=1, keepdims=True)
    return jnp.broadcast_to(g, ab.shape), jnp.broadcast_to(beta, ab.shape)


def _unit_lower_inverse(lm):
    n = lm.shape[0]
    row = lax.broadcasted_iota(jnp.int32, (n, n), 0)
    col = lax.broadcasted_iota(jnp.int32, (n, n), 1)
    nk = -lm
    q = jnp.where(row == col, 1.0, 0.0) + nk
    nk = _dot(nk, nk)
    power = 2
    while 2 * power < n:
        res = _dot(nk, jnp.concatenate([nk, q], axis=1))
        nk, q = res[:, :n], q + res[:, n:]
        power *= 2
    return q + _dot(nk, q)


def _gdn_prompt_kernel(q_ref, k_ref, v_ref, gate_ref, hq_ref, hk_ref, hv_ref, ab_ref,
                       cwq_ref, cwk_ref, cwv_ref, alog_ref, dtb_ref, gnw_ref,
                       o_ref, sout_ref, s_ref, ext_ref, *, n_heads, rows):
    head = pl.program_id(1)
    blk = pl.program_id(2)
    c = GDN_CHUNK

    @pl.when(blk == 0)
    def _():
        s_ref[...] = jnp.zeros_like(s_ref)

    def conv_silu(x_ref, halo_ref, cw_ref):
        ext_ref[0:HALO_CONV, :] = jnp.where(blk > 0, halo_ref[...], 0.0)
        ext_ref[HALO_CONV:, :] = x_ref[...]
        cw = cw_ref[...]
        acc = cw[CONV_W - 1:CONV_W, :] * x_ref[...]
        for j in range(CONV_W - 1):
            acc = acc + cw[j:j + 1, :] * ext_ref[pl.ds(HALO_CONV - (CONV_W - 1) + j, rows), :]
        return _silu(acc)

    qa = conv_silu(q_ref, hq_ref, cwq_ref)
    ka = conv_silu(k_ref, hk_ref, cwk_ref)
    va = conv_silu(v_ref, hv_ref, cwv_ref)
    qa = qa * lax.rsqrt(jnp.sum(qa * qa, axis=-1, keepdims=True) + EPS) * (HEAD_DIM ** -0.5)
    ka = ka * lax.rsqrt(jnp.sum(ka * ka, axis=-1, keepdims=True) + EPS)

    g, beta = _gate_columns(ab_ref[...], alog_ref[...], dtb_ref[...], head, n_heads)

    row = lax.broadcasted_iota(jnp.int32, (c, c), 0)
    col = lax.broadcasted_iota(jnp.int32, (c, c), 1)
    tril = row >= col
    strict = row > col
    tril_f = jnp.where(tril, 1.0, 0.0)

    s = s_ref[...]
    outs = []
    for ci in range(rows // c):
        sl = slice(ci * c, (ci + 1) * c)
        qc, kc, vc, bc = qa[sl], ka[sl], va[sl], beta[sl]
        gc = jnp.dot(tril_f, g[sl], preferred_element_type=F32, precision=lax.Precision.HIGHEST)
        diff = gc - gc.T
        decay = jnp.exp(jnp.where(tril, diff, -1e30))
        egc = jnp.exp(gc)
        g_last = gc[c - 1:c, :]
        lm = jnp.where(strict, _dot_nt(kc, kc) * bc * decay, 0.0)
        t_inv = _unit_lower_inverse(lm)
        uw = _dot(t_inv, jnp.concatenate([vc * bc, kc * (bc * egc)], axis=1))
        u, w = uw[:, :HEAD_DIM], uw[:, HEAD_DIM:]
        intra = jnp.where(tril, _dot_nt(qc, kc) * decay, 0.0)
        kdt = (kc * jnp.exp(g_last - gc)).T
        r1 = _dot(jnp.concatenate([w, qc * egc], axis=0), s)
        v_new = u - r1[:c]
        r2 = _dot(jnp.concatenate([intra, kdt], axis=0), v_new)
        outs.append(r1[c:] + r2[:c])
        s = s * jnp.exp(g_last) + r2[c:]
    s_ref[...] = s

    o = jnp.concatenate(outs, axis=0)
    o = _rms(o, gnw_ref[...]) * _silu(gate_ref[...])
    o_ref[...] = o.astype(o_ref.dtype)

    @pl.when(blk == pl.num_programs(2) - 1)
    def _():
        sout_ref[0, 0] = s


def _gdn_prompt(proj, ab, conv_w, alog, dtb, gnw, batch, seq, n_heads, rows):
    nblk = seq // rows
    hb = rows // HALO_CONV

    def main(off):
        return pl.BlockSpec((rows, HEAD_DIM), lambda b, h, k: (b * nblk + k, off * n_heads + h))

    def halo(off):
        return pl.BlockSpec(
            (HALO_CONV, HEAD_DIM),
            lambda b, h, k: (jnp.maximum((b * nblk + k) * hb - 1, 0), off * n_heads + h))

    def cw(off):
        return pl.BlockSpec((CONV_W, HEAD_DIM), lambda b, h, k: (0, off * n_heads + h))

    vec = pl.BlockSpec((1, HEAD_DIM), lambda b, h, k: (0, 0))
    kern = functools.partial(_gdn_prompt_kernel, n_heads=n_heads, rows=rows)
    return pl.pallas_call(
        kern,
        grid=(batch, n_heads, nblk),
        in_specs=[main(0), main(1), main(2), main(3), halo(0), halo(1), halo(2),
                  pl.BlockSpec((rows, HEAD_DIM), lambda b, h, k: (b * nblk + k, 0)),
                  cw(0), cw(1), cw(2), vec, vec, vec],
        out_specs=[
            pl.BlockSpec((rows, HEAD_DIM), lambda b, h, k: (b * nblk + k, h)),
            pl.BlockSpec((1, 1, HEAD_DIM, HEAD_DIM), lambda b, h, k: (b, h, 0, 0)),
        ],
        out_shape=[
            jax.ShapeDtypeStruct((batch * seq, n_heads * HEAD_DIM), BF16),
            jax.ShapeDtypeStruct((batch, n_heads, HEAD_DIM, HEAD_DIM), F32),
        ],
        scratch_shapes=[pltpu.VMEM((HEAD_DIM, HEAD_DIM), F32),
                        pltpu.VMEM((rows + HALO_CONV, HEAD_DIM), F32)],
        compiler_params=_cparams(3),
        name="gdn_prompt",
    )(proj, proj, proj, proj, proj, proj, proj, ab, conv_w, conv_w, conv_w, alog, dtb, gnw)


def _gdn_sample_kernel(x_ref, cs_ref, ab_ref, cw_ref, alog_ref, dtb_ref, gnw_ref, s_in_ref,
                       o_ref, s_out_ref, o_scr, *, n_heads, tb):
    d = n_heads * HEAD_DIM
    x = x_ref[...]
    cw = cw_ref[...]
    conv = cw[CONV_W - 1:CONV_W, :] * x[:, :3 * d]
    for j in range(CONV_W - 1):
        conv = conv + cw[j:j + 1, :] * cs_ref[:, j * 3 * d:(j + 1) * 3 * d]
    qkv = _silu(conv)

    ab = ab_ref[...]
    eg_all = jnp.exp(-jnp.exp(alog_ref[...]) * _softplus(ab + dtb_ref[...]))
    beta_all = jax.nn.sigmoid(ab)
    pad = jnp.zeros((HEAD_DIM - tb, HEAD_DIM), F32)

    for h in range(n_heads):
        qh = qkv[:, h * HEAD_DIM:(h + 1) * HEAD_DIM]
        kh = qkv[:, d + h * HEAD_DIM:d + (h + 1) * HEAD_DIM]
        vh = qkv[:, 2 * d + h * HEAD_DIM:2 * d + (h + 1) * HEAD_DIM]
        qh = qh * lax.rsqrt(jnp.sum(qh * qh, axis=-1, keepdims=True) + EPS) * (HEAD_DIM ** -0.5)
        kh = kh * lax.rsqrt(jnp.sum(kh * kh, axis=-1, keepdims=True) + EPS)
        qt = jnp.concatenate([qh, pad], axis=0).T
        kt = jnp.concatenate([kh, pad], axis=0).T
        for t in range(tb):
            eg = eg_all[t:t + 1, h:h + 1]
            bt = beta_all[t:t + 1, n_heads + h:n_heads + h + 1]
            kcol = kt[:, t:t + 1]
            qcol = qt[:, t:t + 1]
            s1 = s_in_ref[t, h] * eg
            ks = jnp.sum(s1 * kcol, axis=0, keepdims=True)
            delta = (vh[t:t + 1, :] - ks) * bt
            s2 = s1 + kcol * delta
            s_out_ref[t, h] = s2
            o_scr[t:t + 1, h * HEAD_DIM:(h + 1) * HEAD_DIM] = jnp.sum(s2 * qcol, axis=0, keepdims=True)

    gnw = gnw_ref[...]
    for h in range(n_heads):
        sl = slice(h * HEAD_DIM, (h + 1) * HEAD_DIM)
        o = _rms(o_scr[:, sl], gnw) * _silu(x[:, 3 * d + h * HEAD_DIM:3 * d + (h + 1) * HEAD_DIM])
        o_ref[:, sl] = o.astype(o_ref.dtype)


def _gdn_sample(proj, conv_state, ab, conv_w, alog, dtb, gnw, state, n_heads, tb):
    m = proj.shape[0]
    d = n_heads * HEAD_DIM
    vec = pl.BlockSpec((1, HEAD_DIM), lambda i: (0, 0))
    kern = functools.partial(_gdn_sample_kernel, n_heads=n_heads, tb=tb)
    return pl.pallas_call(
        kern,
        grid=(m // tb,),
        in_specs=[
            pl.BlockSpec((tb, 4 * d), lambda i: (i, 0)),
            pl.BlockSpec((tb, (CONV_W - 1) * 3 * d), lambda i: (i, 0)),
            pl.BlockSpec((tb, HEAD_DIM), lambda i: (i, 0)),
            pl.BlockSpec((CONV_W, 3 * d), lambda i: (0, 0)),
            vec, vec, vec,
            pl.BlockSpec((tb, n_heads, HEAD_DIM, HEAD_DIM), lambda i: (i, 0, 0, 0)),
        ],
        out_specs=[
            pl.BlockSpec((tb, d), lambda i: (i, 0)),
            pl.BlockSpec((tb, n_heads, HEAD_DIM, HEAD_DIM), lambda i: (i, 0, 0, 0)),
        ],
        out_shape=[
            jax.ShapeDtypeStruct((m, d), BF16),
            jax.ShapeDtypeStruct(state.shape, F32),
        ],
        scratch_shapes=[pltpu.VMEM((tb, d), F32)],
        compiler_params=_cparams(1),
        name="gdn_sample",
    )(proj, conv_state, ab, conv_w, alog, dtb, gnw, state)


def _mix_out_kernel(oa_ref, u_ref, hist_ref, x_ref, wp_ref, ps_ref, wo_ref, nw_ref, out_ref, *scratch,
                    tm, seq, pos0):
    d_pool = u_ref.shape[1]
    gd = d_pool // len(POOL_WINDOWS)
    u = u_ref[...]
    if seq > 1:
        (ext_ref,) = scratch
        start = (pl.program_id(0) * tm) % seq
        ext_ref[0:HALO_POOL, :] = jnp.where(start > 0, hist_ref[...], 0.0)
        ext_ref[HALO_POOL:, :] = u
        pos = pos0 + start + lax.broadcasted_iota(jnp.int32, (tm, gd), 0)
    else:
        pos = jnp.full((tm, gd), pos0, jnp.int32)

    parts = [oa_ref[...]]
    for gi, win in enumerate(POOL_WINDOWS):
        sl = slice(gi * gd, (gi + 1) * gd)
        cur = u[:, sl]
        wsum = cur
        for r in range(1, win):
            if seq > 1:
                wsum = wsum + ext_ref[pl.ds(HALO_POOL - r, tm), sl]
            else:
                off = (POOL_BUF - r) * d_pool
                wsum = wsum + hist_ref[:, off + gi * gd:off + (gi + 1) * gd]
        cnt = jnp.minimum(pos + 1, win).astype(F32)
        pooled = wsum / cnt - cur
        ob = _dot(pooled, wp_ref[gi]) * ps_ref[:, sl]
        parts.append(ob.astype(BF16))
    mix = jnp.dot(jnp.concatenate(parts, axis=1), wo_ref[...], preferred_element_type=F32)
    out_ref[...] = x_ref[...] + _rms(mix, nw_ref[...])


def _mix_out(o_a, u_src, u_col, hist, x, w_pool, pool_scale, w_out, norm_w, tm, seq, pos0):
    m, d = x.shape
    d_pool = pool_scale.shape[1]
    d_gdn = o_a.shape[1]
    if seq > 1:
        hb = tm // HALO_POOL
        hist_spec = pl.BlockSpec((HALO_POOL, d_pool), lambda i: (jnp.maximum(i * hb - 1, 0), u_col))
        scratch = [pltpu.VMEM((tm + HALO_POOL, d_pool), F32)]
    else:
        hist_spec = pl.BlockSpec((tm, POOL_BUF * d_pool), lambda i: (i, 0))
        scratch = []
    kern = functools.partial(_mix_out_kernel, tm=tm, seq=seq, pos0=pos0)
    return pl.pallas_call(
        kern,
        grid=(m // tm,),
        in_specs=[
            pl.BlockSpec((tm, d_gdn), lambda i: (i, 0)),
            pl.BlockSpec((tm, d_pool), lambda i: (i, u_col)),
            hist_spec,
            pl.BlockSpec((tm, d), lambda i: (i, 0)),
            pl.BlockSpec(w_pool.shape, lambda i: (0, 0, 0)),
            pl.BlockSpec((1, d_pool), lambda i: (0, 0)),
            pl.BlockSpec(w_out.shape, lambda i: (0, 0)),
            pl.BlockSpec((1, d), lambda i: (0, 0)),
        ],
        out_specs=pl.BlockSpec((tm, d), lambda i: (i, 0)),
        out_shape=jax.ShapeDtypeStruct((m, d), F32),
        scratch_shapes=scratch,
        compiler_params=_cparams(1),
        name="mix_out",
    )(o_a, u_src, hist, x, w_pool, pool_scale, w_out, norm_w)


def _mlp_kernel(x_ref, nw1_ref, wu_ref, wd_ref, nw2_ref, out_ref, h_ref):
    j = pl.program_id(1)

    @pl.when(j == 0)
    def _():
        h_ref[...] = _rms(x_ref[...], nw1_ref[...]).astype(BF16)

    up = jnp.dot(h_ref[...], wu_ref[...], preferred_element_type=F32)
    act = jnp.square(jnp.maximum(up, 0.0)).astype(BF16)
    part = jnp.dot(act, wd_ref[...], preferred_element_type=F32)

    @pl.when(j == 0)
    def _():
        out_ref[...] = part

    @pl.when(j > 0)
    def _():
        out_ref[...] += part

    @pl.when(j == pl.num_programs(1) - 1)
    def _():
        out_ref[...] = x_ref[...] + _rms(out_ref[...], nw2_ref[...])


def _mlp(x, nw1, w_up, w_down, nw2, tm, tf):
    m, d = x.shape
    f = w_up.shape[1]
    return pl.pallas_call(
        _mlp_kernel,
        grid=(m // tm, f // tf),
        in_specs=[
            pl.BlockSpec((tm, d), lambda i, j: (i, 0)),
            pl.BlockSpec((1, d), lambda i, j: (0, 0)),
            pl.BlockSpec((d, tf), lambda i, j: (0, j)),
            pl.BlockSpec((tf, d), lambda i, j: (j, 0)),
            pl.BlockSpec((1, d), lambda i, j: (0, 0)),
        ],
        out_specs=pl.BlockSpec((tm, d), lambda i, j: (i, 0)),
        out_shape=jax.ShapeDtypeStruct((m, d), F32),
        scratch_shapes=[pltpu.VMEM((tm, d), BF16)],
        compiler_params=_cparams(2),
        name="mlp",
    )(x, nw1, w_up, w_down, nw2)


def _pad_lanes(v, n=HEAD_DIM):
    return jnp.pad(v.reshape(1, -1), ((0, 0), (0, n - v.shape[-1])))


def kernel(x_prompt, x_sample, state_gdn, state_conv, state_pool, norm_pre_mix, w_in, conv_w, a_log, dt_bias,
           norm_gdn_out, w_pool, pool_scale, w_out, norm_post_mix, norm_pre_mlp, w_up, w_down, norm_post_mlp):
    batch, seq, d_model = x_prompt.shape
    dec_batch, dec_seq, _ = x_sample.shape
    assert dec_seq == 1
    depth = w_in.shape[0]
    n_heads = a_log.shape[1]
    d_gdn = n_heads * HEAD_DIM
    d_qkv = 3 * d_gdn
    d_pool = pool_scale.shape[1]
    o_gate = d_qkv + d_gdn
    o_pool = o_gate + 2 * n_heads
    u_col = o_gate // d_pool

    yp = x_prompt.reshape(batch * seq, d_model)
    ys = x_sample.reshape(dec_batch, d_model)
    outs = [[] for _ in range(6)]
    for l in range(depth):
        w_main = jnp.concatenate([w_in[l][:, :o_gate], w_in[l][:, o_pool:]], axis=1).astype(BF16)
        w_ab = jnp.pad(w_in[l][:, o_gate:o_pool], ((0, 0), (0, HEAD_DIM - 2 * n_heads))).astype(BF16)
        w_pool_b = w_pool[l].astype(BF16)
        w_out_b = w_out[l].astype(BF16)
        w_up_b = w_up[l].astype(BF16)
        w_down_b = w_down[l].astype(BF16)
        alog = _pad_lanes(a_log[l])
        dtb = _pad_lanes(dt_bias[l])
        gnw = norm_gdn_out[l].reshape(1, HEAD_DIM)
        npm = norm_pre_mix[l].reshape(1, d_model)
        npo = norm_post_mix[l].reshape(1, d_model)
        nm1 = norm_pre_mlp[l].reshape(1, d_model)
        nm2 = norm_post_mlp[l].reshape(1, d_model)
        ps = pool_scale[l].reshape(1, d_pool)

        proj_p, ab_p = _in_proj(yp, npm, w_main, w_ab, tm=512, tn=1024)
        oa_p, s_p = _gdn_prompt(proj_p, ab_p, conv_w[l], alog, dtb, gnw, batch, seq, n_heads, rows=512)
        x1_p = _mix_out(oa_p, proj_p, u_col, proj_p, yp, w_pool_b, ps, w_out_b, npo, tm=256, seq=seq, pos0=0)
        yp = _mlp(x1_p, nm1, w_up_b, w_down_b, nm2, tm=512, tf=512)
        proj_p3 = proj_p.reshape(batch, seq, -1)
        outs[0].append(s_p)
        outs[1].append(proj_p3[:, seq - (CONV_W - 1):, :d_qkv])
        outs[2].append(proj_p3[:, seq - POOL_BUF:, o_gate:])

        proj_s, ab_s = _in_proj(ys, npm, w_main, w_ab, tm=dec_batch, tn=1024)
        cs = state_conv[l].reshape(dec_batch, (CONV_W - 1) * d_qkv)
        hist = state_pool[l].reshape(dec_batch, POOL_BUF * d_pool)
        oa_s, s_s = _gdn_sample(proj_s, cs, ab_s, conv_w[l], alog, dtb, gnw, state_gdn[l], n_heads, tb=8)
        x1_s = _mix_out(oa_s, proj_s, u_col, hist, ys, w_pool_b, ps, w_out_b, npo, tm=dec_batch, seq=1,
                        pos0=PAST_LEN)
        ys = _mlp(x1_s, nm1, w_up_b, w_down_b, nm2, tm=dec_batch, tf=512)
        outs[3].append(s_s)
        outs[4].append(jnp.concatenate([state_conv[l][:, 1:], proj_s[:, None, :d_qkv]], axis=1))
        outs[5].append(jnp.concatenate([state_pool[l][:, 1:], proj_s[:, None, o_gate:]], axis=1))

    return (yp.reshape(batch, seq, d_model), ys.reshape(dec_batch, dec_seq, d_model),
            *[jnp.stack(o) for o in outs])
```

```python
import functools

import jax
import jax.numpy as jnp
from jax import lax
from jax.experimental import pallas as pl
from jax.experimental.pallas import tpu as pltpu

F32 = jnp.float32
BF16 = jnp.bfloat16

EPS = 1e-6
HEAD_DIM = 128
CONV_W = 4
POOL_WINDOWS = (2, 4, 8, 16)
POOL_BUF = max(POOL_WINDOWS) - 1
PAST_LEN = 16384
HALO_CONV = 8
HALO_POOL = 16
GDN_CHUNK = 128
VMEM_LIMIT = 56 * 1024 * 1024


def _cparams(n_axes):
    return pltpu.CompilerParams(
        dimension_semantics=("arbitrary",) * n_axes, vmem_limit_bytes=VMEM_LIMIT)


def _dot(a, b):
    return jnp.dot(a.astype(BF16), b.astype(BF16), preferred_element_type=F32)


def _dot_nt(a, b):
    return lax.dot_general(a.astype(BF16), b.astype(BF16), (((1,), (1,)), ((), ())),
                           preferred_element_type=F32)


def _rms(x, w):
    return x * lax.rsqrt(jnp.mean(x * x, axis=-1, keepdims=True) + EPS) * w


def _silu(x):
    return x * jax.nn.sigmoid(x)


def _softplus(x):
    return jnp.maximum(x, 0.0) + jnp.log(1.0 + jnp.exp(-jnp.abs(x)))


def _l2norm(x):
    return x * lax.rsqrt(jnp.sum(x * x, axis=-1, keepdims=True) + EPS)


def _inproj_kernel(x_ref, nw_ref, w_ref, wab_ref, alog_ref, dtb_ref, out_ref, gates_ref, h_ref, *, n_heads):
    @pl.when(pl.program_id(1) == 0)
    def _():
        h = _rms(x_ref[...], nw_ref[...]).astype(BF16)
        h_ref[...] = h
        ab = jnp.dot(h, wab_ref[...], preferred_element_type=F32)
        lane = lax.broadcasted_iota(jnp.int32, ab.shape, 1)
        g = -jnp.exp(alog_ref[...]) * _softplus(ab + dtb_ref[...])
        gates_ref[...] = jnp.where(lane < n_heads, g, jax.nn.sigmoid(ab))

    out_ref[...] = jnp.dot(h_ref[...], w_ref[...], preferred_element_type=F32)


def _in_proj(x, norm_w, w_main, w_ab, alog, dtb, n_heads, tm, tn):
    m, d = x.shape
    n = w_main.shape[1]
    vec = pl.BlockSpec((1, HEAD_DIM), lambda i, j: (0, 0))
    return pl.pallas_call(
        functools.partial(_inproj_kernel, n_heads=n_heads),
        grid=(m // tm, n // tn),
        in_specs=[
            pl.BlockSpec((tm, d), lambda i, j: (i, 0)),
            pl.BlockSpec((1, d), lambda i, j: (0, 0)),
            pl.BlockSpec((d, tn), lambda i, j: (0, j)),
            pl.BlockSpec((d, HEAD_DIM), lambda i, j: (0, 0)),
            vec, vec,
        ],
        out_specs=[
            pl.BlockSpec((tm, tn), lambda i, j: (i, j)),
            pl.BlockSpec((tm, HEAD_DIM), lambda i, j: (i, 0)),
        ],
        out_shape=[jax.ShapeDtypeStruct((m, n), F32), jax.ShapeDtypeStruct((m, HEAD_DIM), F32)],
        scratch_shapes=[pltpu.VMEM((tm, d), BF16)],
        compiler_params=_cparams(2),
        name="in_proj",
    )(x, norm_w, w_main, w_ab, alog, dtb)


def _lane_column(x, lane_idx):
    lane = lax.broadcasted_iota(jnp.int32, x.shape, 1)
    col = jnp.sum(jnp.where(lane == lane_idx, x, 0.0), axis=1, keepdims=True)
    return jnp.broadcast_to(col, x.shape)


def _unit_lower_inverses(lms):
    n = lms[0].shape[0]
    row = lax.broadcasted_iota(jnp.int32, (n, n), 0)
    col = lax.broadcasted_iota(jnp.int32, (n, n), 1)
    eye = jnp.where(row == col, 1.0, 0.0)
    nks = [-lm for lm in lms]
    qs = [eye + nk for nk in nks]
    nks = [_dot(nk, nk) for nk in nks]
    power = 2
    while 2 * power < n:
        res = [_dot(nk, jnp.concatenate([nk, q], axis=1)) for nk, q in zip(nks, qs)]
        nks = [r[:, :n] for r in res]
        qs = [q + r[:, n:] for q, r in zip(qs, res)]
        power *= 2
    return [q + _dot(nk, q) for nk, q in zip(nks, qs)]


def _gdn_prompt_kernel(q_ref, k_ref, v_ref, gate_ref, hq_ref, hk_ref, hv_ref, gates_ref,
                       cwq_ref, cwk_ref, cwv_ref, gnw_ref,
                       o_ref, sout_ref, s_ref, ext_ref, *, n_heads, hs, rows):
    hg = pl.program_id(1)
    blk = pl.program_id(2)
    c = GDN_CHUNK

    @pl.when(blk == 0)
    def _():
        s_ref[...] = jnp.zeros_like(s_ref)

    for a, (x_ref, halo_ref) in enumerate(((q_ref, hq_ref), (k_ref, hk_ref), (v_ref, hv_ref))):
        for hi in range(hs):
            lanes = slice(hi * HEAD_DIM, (hi + 1) * HEAD_DIM)
            ext_ref[a * hs + hi, 0:HALO_CONV, :] = jnp.where(blk > 0, halo_ref[:, lanes], 0.0)
            ext_ref[a * hs + hi, HALO_CONV:, :] = x_ref[:, lanes]

    def conv_silu(a, cw_ref, r0, hi):
        lanes = slice(hi * HEAD_DIM, (hi + 1) * HEAD_DIM)
        acc = None
        for j in range(CONV_W):
            term = cw_ref[j:j + 1, lanes] * ext_ref[a * hs + hi, pl.ds(HALO_CONV - (CONV_W - 1) + j + r0, c), :]
            acc = term if acc is None else acc + term
        return _silu(acc)

    row = lax.broadcasted_iota(jnp.int32, (c, c), 0)
    col = lax.broadcasted_iota(jnp.int32, (c, c), 1)
    tril = row >= col
    strict = row > col
    tril_f = jnp.where(tril, 1.0, 0.0)
    gnw = gnw_ref[...]
    heads = range(hs)

    s = [s_ref[hi] for hi in heads]
    for ci in range(rows // c):
        r0 = ci * c
        gates = gates_ref[r0:r0 + c, :]
        gcum = jnp.dot(tril_f, gates, preferred_element_type=F32, precision=lax.Precision.HIGHEST)
        qc = [_l2norm(conv_silu(0, cwq_ref, r0, hi)) * (HEAD_DIM ** -0.5) for hi in heads]
        kc = [_l2norm(conv_silu(1, cwk_ref, r0, hi)) for hi in heads]
        vc = [conv_silu(2, cwv_ref, r0, hi) for hi in heads]
        gc = [_lane_column(gcum, hg * hs + hi) for hi in heads]
        bc = [_lane_column(gates, hg * hs + hi + n_heads) for hi in heads]
        decay = [jnp.exp(jnp.where(tril, g - g.T, -1e30)) for g in gc]
        egc = [jnp.exp(g) for g in gc]
        g_last = [g[c - 1:c, :] for g in gc]
        kk = [_dot_nt(k, k) for k in kc]
        qk = [_dot_nt(q, k) for q, k in zip(qc, kc)]
        lm = [jnp.where(strict, kk[hi] * bc[hi] * decay[hi], 0.0) for hi in heads]
        t_inv = _unit_lower_inverses(lm)
        uw = [_dot(t_inv[hi], jnp.concatenate([vc[hi] * bc[hi], kc[hi] * (bc[hi] * egc[hi])], axis=1))
              for hi in heads]
        intra = [jnp.where(tril, qk[hi] * decay[hi], 0.0) for hi in heads]
        kdt = [(kc[hi] * jnp.exp(g_last[hi] - gc[hi])).T for hi in heads]
        r1 = [_dot(jnp.concatenate([uw[hi][:, HEAD_DIM:], qc[hi] * egc[hi]], axis=0), s[hi]) for hi in heads]
        v_new = [uw[hi][:, :HEAD_DIM] - r1[hi][:c] for hi in heads]
        r2 = [_dot(jnp.concatenate([intra[hi], kdt[hi]], axis=0), v_new[hi]) for hi in heads]
        s = [s[hi] * jnp.exp(g_last[hi]) + r2[hi][c:] for hi in heads]
        for hi in heads:
            lanes = slice(hi * HEAD_DIM, (hi + 1) * HEAD_DIM)
            o = _rms(r1[hi][c:] + r2[hi][:c], gnw) * _silu(gate_ref[r0:r0 + c, lanes])
            o_ref[r0:r0 + c, lanes] = o.astype(o_ref.dtype)

    for hi in range(hs):
        s_ref[hi] = s[hi]

    @pl.when(blk == pl.num_programs(2) - 1)
    def _():
        for hi in range(hs):
            sout_ref[0, hi] = s[hi]


def _gdn_prompt(proj, gates, conv_w, gnw, batch, seq, n_heads, hs, rows):
    nblk = seq // rows
    hb = rows // HALO_CONV
    ng = n_heads // hs
    width = hs * HEAD_DIM

    def main(off):
        return pl.BlockSpec((rows, width), lambda b, h, k: (b * nblk + k, off * ng + h))

    def halo(off):
        return pl.BlockSpec(
            (HALO_CONV, width), lambda b, h, k: (jnp.maximum((b * nblk + k) * hb - 1, 0), off * ng + h))

    def cw(off):
        return pl.BlockSpec((CONV_W, width), lambda b, h, k: (0, off * ng + h))

    kern = functools.partial(_gdn_prompt_kernel, n_heads=n_heads, hs=hs, rows=rows)
    return pl.pallas_call(
        kern,
        grid=(batch, ng, nblk),
        in_specs=[main(0), main(1), main(2), main(3), halo(0), halo(1), halo(2),
                  pl.BlockSpec((rows, HEAD_DIM), lambda b, h, k: (b * nblk + k, 0)),
                  cw(0), cw(1), cw(2),
                  pl.BlockSpec((1, HEAD_DIM), lambda b, h, k: (0, 0))],
        out_specs=[
            pl.BlockSpec((rows, width), lambda b, h, k: (b * nblk + k, h)),
            pl.BlockSpec((1, hs, HEAD_DIM, HEAD_DIM), lambda b, h, k: (b, h, 0, 0)),
        ],
        out_shape=[
            jax.ShapeDtypeStruct((batch * seq, n_heads * HEAD_DIM), BF16),
            jax.ShapeDtypeStruct((batch, n_heads, HEAD_DIM, HEAD_DIM), F32),
        ],
        scratch_shapes=[pltpu.VMEM((hs, HEAD_DIM, HEAD_DIM), F32),
                        pltpu.VMEM((3 * hs, rows + HALO_CONV, HEAD_DIM), F32)],
        compiler_params=_cparams(3),
        name="gdn_prompt",
    )(proj, proj, proj, proj, proj, proj, proj, gates, conv_w, conv_w, conv_w, gnw)


def _gdn_sample_kernel(x_ref, cs_ref, gates_ref, cw_ref, gnw_ref, s_in_ref,
                       o_ref, s_out_ref, o_scr, *, n_heads, tb):
    d = n_heads * HEAD_DIM
    x = x_ref[...]
    cw = cw_ref[...]
    conv = cw[CONV_W - 1:CONV_W, :] * x[:, :3 * d]
    for j in range(CONV_W - 1):
        conv = conv + cw[j:j + 1, :] * cs_ref[:, j * 3 * d:(j + 1) * 3 * d]
    qkv = _silu(conv)

    gates = gates_ref[...]
    eg_all = jnp.exp(gates)
    pad = jnp.zeros((HEAD_DIM - tb, HEAD_DIM), F32)

    for h in range(n_heads):
        qh = _l2norm(qkv[:, h * HEAD_DIM:(h + 1) * HEAD_DIM]) * (HEAD_DIM ** -0.5)
        kh = _l2norm(qkv[:, d + h * HEAD_DIM:d + (h + 1) * HEAD_DIM])
        vh = qkv[:, 2 * d + h * HEAD_DIM:2 * d + (h + 1) * HEAD_DIM]
        qt = jnp.concatenate([qh, pad], axis=0).T
        kt = jnp.concatenate([kh, pad], axis=0).T
        for t in range(tb):
            eg = eg_all[t:t + 1, h:h + 1]
            bt = gates[t:t + 1, n_heads + h:n_heads + h + 1]
            kcol = kt[:, t:t + 1]
            qcol = qt[:, t:t + 1]
            s1 = s_in_ref[t, h] * eg
            ks = jnp.sum(s1 * kcol, axis=0, keepdims=True)
            delta = (vh[t:t + 1, :] - ks) * bt
            s2 = s1 + kcol * delta
            s_out_ref[t, h] = s2
            o_scr[t:t + 1, h * HEAD_DIM:(h + 1) * HEAD_DIM] = jnp.sum(s2 * qcol, axis=0, keepdims=True)

    gnw = gnw_ref[...]
    for h in range(n_heads):
        sl = slice(h * HEAD_DIM, (h + 1) * HEAD_DIM)
        o = _rms(o_scr[:, sl], gnw) * _silu(x[:, 3 * d + h * HEAD_DIM:3 * d + (h + 1) * HEAD_DIM])
        o_ref[:, sl] = o.astype(o_ref.dtype)


def _gdn_sample(proj, conv_state, gates, conv_w, gnw, state, n_heads, tb):
    m = proj.shape[0]
    d = n_heads * HEAD_DIM
    kern = functools.partial(_gdn_sample_kernel, n_heads=n_heads, tb=tb)
    return pl.pallas_call(
        kern,
        grid=(m // tb,),
        in_specs=[
            pl.BlockSpec((tb, 4 * d), lambda i: (i, 0)),
            pl.BlockSpec((tb, (CONV_W - 1) * 3 * d), lambda i: (i, 0)),
            pl.BlockSpec((tb, HEAD_DIM), lambda i: (i, 0)),
            pl.BlockSpec((CONV_W, 3 * d), lambda i: (0, 0)),
            pl.BlockSpec((1, HEAD_DIM), lambda i: (0, 0)),
            pl.BlockSpec((tb, n_heads, HEAD_DIM, HEAD_DIM), lambda i: (i, 0, 0, 0)),
        ],
        out_specs=[
            pl.BlockSpec((tb, d), lambda i: (i, 0)),
            pl.BlockSpec((tb, n_heads, HEAD_DIM, HEAD_DIM), lambda i: (i, 0, 0, 0)),
        ],
        out_shape=[
            jax.ShapeDtypeStruct((m, d), BF16),
            jax.ShapeDtypeStruct(state.shape, F32),
        ],
        scratch_shapes=[pltpu.VMEM((tb, d), F32)],
        compiler_params=_cparams(1),
        name="gdn_sample",
    )(proj, conv_state, gates, conv_w, gnw, state)


def _mix_out_kernel(oa_ref, u_ref, hist_ref, x_ref, wp_ref, ps_ref, wo_ref, nw_ref, out_ref, *scratch,
                    tm, seq, pos0):
    d_pool = u_ref.shape[1]
    gd = d_pool // len(POOL_WINDOWS)
    u = u_ref[...]
    if seq > 1:
        (ext_ref,) = scratch
        start = (pl.program_id(0) * tm) % seq
        ext_ref[0:HALO_POOL, :] = jnp.where(start > 0, hist_ref[...], 0.0)
        ext_ref[HALO_POOL:, :] = u
        pos = pos0 + start + lax.broadcasted_iota(jnp.int32, (tm, gd), 0)
    else:
        pos = jnp.full((tm, gd), pos0, jnp.int32)

    parts = [oa_ref[...]]
    for gi, win in enumerate(POOL_WINDOWS):
        sl = slice(gi * gd, (gi + 1) * gd)
        cur = u[:, sl]
        wsum = cur
        for r in range(1, win):
            if seq > 1:
                wsum = wsum + ext_ref[pl.ds(HALO_POOL - r, tm), sl]
            else:
                off = (POOL_BUF - r) * d_pool
                wsum = wsum + hist_ref[:, off + gi * gd:off + (gi + 1) * gd]
        cnt = jnp.minimum(pos + 1, win).astype(F32)
        pooled = wsum / cnt - cur
        ob = _dot(pooled, wp_ref[gi]) * ps_ref[:, sl]
        parts.append(ob.astype(BF16))
    mix = jnp.dot(jnp.concatenate(parts, axis=1), wo_ref[...], preferred_element_type=F32)
    out_ref[...] = x_ref[...] + _rms(mix, nw_ref[...])


def _mix_out(o_a, u_src, u_col, hist, x, w_pool, pool_scale, w_out, norm_w, tm, seq, pos0):
    m, d = x.shape
    d_pool = pool_scale.shape[1]
    d_gdn = o_a.shape[1]
    if seq > 1:
        hb = tm // HALO_POOL
        hist_spec = pl.BlockSpec((HALO_POOL, d_pool), lambda i: (jnp.maximum(i * hb - 1, 0), u_col))
        scratch = [pltpu.VMEM((tm + HALO_POOL, d_pool), F32)]
    else:
        hist_spec = pl.BlockSpec((tm, POOL_BUF * d_pool), lambda i: (i, 0))
        scratch = []
    kern = functools.partial(_mix_out_kernel, tm=tm, seq=seq, pos0=pos0)
    return pl.pallas_call(
        kern,
        grid=(m // tm,),
        in_specs=[
            pl.BlockSpec((tm, d_gdn), lambda i: (i, 0)),
            pl.BlockSpec((tm, d_pool), lambda i: (i, u_col)),
            hist_spec,
            pl.BlockSpec((tm, d), lambda i: (i, 0)),
            pl.BlockSpec(w_pool.shape, lambda i: (0, 0, 0)),
            pl.BlockSpec((1, d_pool), lambda i: (0, 0)),
            pl.BlockSpec(w_out.shape, lambda i: (0, 0)),
            pl.BlockSpec((1, d), lambda i: (0, 0)),
        ],
        out_specs=pl.BlockSpec((tm, d), lambda i: (i, 0)),
        out_shape=jax.ShapeDtypeStruct((m, d), F32),
        scratch_shapes=scratch,
        compiler_params=_cparams(1),
        name="mix_out",
    )(o_a, u_src, hist, x, w_pool, pool_scale, w_out, norm_w)


def _mlp_kernel(x_ref, nw1_ref, wu_ref, wd_ref, nw2_ref, out_ref, h_ref):
    j = pl.program_id(1)

    @pl.when(j == 0)
    def _():
        h_ref[...] = _rms(x_ref[...], nw1_ref[...]).astype(BF16)

    up = jnp.dot(h_ref[...], wu_ref[...], preferred_element_type=F32)
    act = jnp.square(jnp.maximum(up, 0.0)).astype(BF16)
    part = jnp.dot(act, wd_ref[...], preferred_element_type=F32)

    @pl.when(j == 0)
    def _():
        out_ref[...] = part

    @pl.when(j > 0)
    def _():
        out_ref[...] += part

    @pl.when(j == pl.num_programs(1) - 1)
    def _():
        out_ref[...] = x_ref[...] + _rms(out_ref[...], nw2_ref[...])


def _mlp(x, nw1, w_up, w_down, nw2, tm, tf):
    m, d = x.shape
    f = w_up.shape[1]
    return pl.pallas_call(
        _mlp_kernel,
        grid=(m // tm, f // tf),
        in_specs=[
            pl.BlockSpec((tm, d), lambda i, j: (i, 0)),
            pl.BlockSpec((1, d), lambda i, j: (0, 0)),
            pl.BlockSpec((d, tf), lambda i, j: (0, j)),
            pl.BlockSpec((tf, d), lambda i, j: (j, 0)),
            pl.BlockSpec((1, d), lambda i, j: (0, 0)),
        ],
        out_specs=pl.BlockSpec((tm, d), lambda i, j: (i, 0)),
        out_shape=jax.ShapeDtypeStruct((m, d), F32),
        scratch_shapes=[pltpu.VMEM((tm, d), BF16)],
        compiler_params=_cparams(2),
        name="mlp",
    )(x, nw1, w_up, w_down, nw2)


def _pad_lanes(v, n=HEAD_DIM):
    return jnp.pad(v.reshape(1, -1), ((0, 0), (0, n - v.shape[-1])))


def kernel(x_prompt, x_sample, state_gdn, state_conv, state_pool, norm_pre_mix, w_in, conv_w, a_log, dt_bias,
           norm_gdn_out, w_pool, pool_scale, w_out, norm_post_mix, norm_pre_mlp, w_up, w_down, norm_post_mlp):
    batch, seq, d_model = x_prompt.shape
    dec_batch, dec_seq, _ = x_sample.shape
    assert dec_seq == 1
    depth = w_in.shape[0]
    n_heads = a_log.shape[1]
    d_gdn = n_heads * HEAD_DIM
    d_qkv = 3 * d_gdn
    d_pool = pool_scale.shape[1]
    o_gate = d_qkv + d_gdn
    o_pool = o_gate + 2 * n_heads
    u_col = o_gate // d_pool

    yp = x_prompt.reshape(batch * seq, d_model)
    ys = x_sample.reshape(dec_batch, d_model)
    outs = [[] for _ in range(6)]
    for l in range(depth):
        w_main = jnp.concatenate([w_in[l][:, :o_gate], w_in[l][:, o_pool:]], axis=1).astype(BF16)
        w_ab = jnp.pad(w_in[l][:, o_gate:o_pool], ((0, 0), (0, HEAD_DIM - 2 * n_heads))).astype(BF16)
        w_pool_b = w_pool[l].astype(BF16)
        w_out_b = w_out[l].astype(BF16)
        w_up_b = w_up[l].astype(BF16)
        w_down_b = w_down[l].astype(BF16)
        alog = _pad_lanes(a_log[l])
        dtb = _pad_lanes(dt_bias[l])
        gnw = norm_gdn_out[l].reshape(1, HEAD_DIM)
        npm = norm_pre_mix[l].reshape(1, d_model)
        npo = norm_post_mix[l].reshape(1, d_model)
        nm1 = norm_pre_mlp[l].reshape(1, d_model)
        nm2 = norm_post_mlp[l].reshape(1, d_model)
        ps = pool_scale[l].reshape(1, d_pool)

        proj_p, gates_p = _in_proj(yp, npm, w_main, w_ab, alog, dtb, n_heads, tm=512, tn=1024)
        oa_p, s_p = _gdn_prompt(proj_p, gates_p, conv_w[l], gnw, batch, seq, n_heads, hs=8, rows=256)
        x1_p = _mix_out(oa_p, proj_p, u_col, proj_p, yp, w_pool_b, ps, w_out_b, npo, tm=256, seq=seq, pos0=0)
        yp = _mlp(x1_p, nm1, w_up_b, w_down_b, nm2, tm=512, tf=512)
        proj_p3 = proj_p.reshape(batch, seq, -1)
        outs[0].append(s_p)
        outs[1].append(proj_p3[:, seq - (CONV_W - 1):, :d_qkv])
        outs[2].append(proj_p3[:, seq - POOL_BUF:, o_gate:])

        proj_s, gates_s = _in_proj(ys, npm, w_main, w_ab, alog, dtb, n_heads, tm=dec_batch, tn=1024)
        cs = state_conv[l].reshape(dec_batch, (CONV_W - 1) * d_qkv)
        hist = state_pool[l].reshape(dec_batch, POOL_BUF * d_pool)
        oa_s, s_s = _gdn_sample(proj_s, cs, gates_s, conv_w[l], gnw, state_gdn[l], n_heads, tb=8)
        x1_s = _mix_out(oa_s, proj_s, u_col, hist, ys, w_pool_b, ps, w_out_b, npo, tm=dec_batch, seq=1,
                        pos0=PAST_LEN)
        ys = _mlp(x1_s, nm1, w_up_b, w_down_b, nm2, tm=dec_batch, tf=512)
        outs[3].append(s_s)
        outs[4].append(jnp.concatenate([state_conv[l][:, 1:], proj_s[:, None, :d_qkv]], axis=1))
        outs[5].append(jnp.concatenate([state_pool[l][:, 1:], proj_s[:, None, o_gate:]], axis=1))

    return (yp.reshape(batch, seq, d_model), ys.reshape(dec_batch, dec_seq, d_model),
            *[jnp.stack(o) for o in outs])
```

```python
import functools

import jax
import jax.numpy as jnp
from jax import lax
from jax.experimental import pallas as pl
from jax.experimental.pallas import tpu as pltpu

F32 = jnp.float32
BF16 = jnp.bfloat16

EPS = 1e-6
HEAD_DIM = 128
CONV_W = 4
POOL_WINDOWS = (2, 4, 8, 16)
POOL_BUF = max(POOL_WINDOWS) - 1
PAST_LEN = 16384
HALO_CONV = 8
HALO_POOL = 16
GDN_CHUNK = 128
VMEM_LIMIT = 56 * 1024 * 1024


def _cparams(n_axes):
    return pltpu.CompilerParams(
        dimension_semantics=("arbitrary",) * n_axes, vmem_limit_bytes=VMEM_LIMIT)


def _dot(a, b):
    return jnp.dot(a.astype(BF16), b.astype(BF16), preferred_element_type=F32)


def _dot_nt(a, b):
    return lax.dot_general(a.astype(BF16), b.astype(BF16), (((1,), (1,)), ((), ())),
                           preferred_element_type=F32)


def _rms(x, w):
    return x * lax.rsqrt(jnp.mean(x * x, axis=-1, keepdims=True) + EPS) * w


def _silu(x):
    return x * jax.nn.sigmoid(x)


def _softplus(x):
    return jnp.maximum(x, 0.0) + jnp.log(1.0 + jnp.exp(-jnp.abs(x)))


def _l2norm(x):
    return x * lax.rsqrt(jnp.sum(x * x, axis=-1, keepdims=True) + EPS)


def _inproj_kernel(x_ref, nw_ref, w_ref, wab_ref, alog_ref, dtb_ref, out_ref, gates_ref, h_ref, *, n_heads):
    @pl.when(pl.program_id(1) == 0)
    def _():
        h = _rms(x_ref[...], nw_ref[...]).astype(BF16)
        h_ref[...] = h
        ab = jnp.dot(h, wab_ref[...], preferred_element_type=F32)
        lane = lax.broadcasted_iota(jnp.int32, ab.shape, 1)
        g = -jnp.exp(alog_ref[...]) * _softplus(ab + dtb_ref[...])
        gates_ref[...] = jnp.where(lane < n_heads, g, jax.nn.sigmoid(ab))

    out_ref[...] = jnp.dot(h_ref[...], w_ref[...], preferred_element_type=F32)


def _in_proj(x, norm_w, w_main, w_ab, alog, dtb, n_heads, tm, tn):
    m, d = x.shape
    n = w_main.shape[1]
    vec = pl.BlockSpec((1, HEAD_DIM), lambda i, j: (0, 0))
    return pl.pallas_call(
        functools.partial(_inproj_kernel, n_heads=n_heads),
        grid=(m // tm, n // tn),
        in_specs=[
            pl.BlockSpec((tm, d), lambda i, j: (i, 0)),
            pl.BlockSpec((1, d), lambda i, j: (0, 0)),
            pl.BlockSpec((d, tn), lambda i, j: (0, j)),
            pl.BlockSpec((d, HEAD_DIM), lambda i, j: (0, 0)),
            vec, vec,
        ],
        out_specs=[
            pl.BlockSpec((tm, tn), lambda i, j: (i, j)),
            pl.BlockSpec((tm, HEAD_DIM), lambda i, j: (i, 0)),
        ],
        out_shape=[jax.ShapeDtypeStruct((m, n), F32), jax.ShapeDtypeStruct((m, HEAD_DIM), F32)],
        scratch_shapes=[pltpu.VMEM((tm, d), BF16)],
        compiler_params=_cparams(2),
        name="in_proj",
    )(x, norm_w, w_main, w_ab, alog, dtb)


def _lane_column(x, lane_idx):
    lane = lax.broadcasted_iota(jnp.int32, x.shape, 1)
    col = jnp.sum(jnp.where(lane == lane_idx, x, 0.0), axis=1, keepdims=True)
    return jnp.broadcast_to(col, x.shape)


def _unit_lower_inverses(lms):
    n = lms[0].shape[0]
    row = lax.broadcasted_iota(jnp.int32, (n, n), 0)
    col = lax.broadcasted_iota(jnp.int32, (n, n), 1)
    eye = jnp.where(row == col, 1.0, 0.0)
    nks = [-lm for lm in lms]
    qs = [eye + nk for nk in nks]
    nks = [_dot(nk, nk) for nk in nks]
    power = 2
    while 2 * power < n:
        res = [_dot(nk, jnp.concatenate([nk, q], axis=1)) for nk, q in zip(nks, qs)]
        nks = [r[:, :n] for r in res]
        qs = [q + r[:, n:] for q, r in zip(qs, res)]
        power *= 2
    return [q + _dot(nk, q) for nk, q in zip(nks, qs)]


def _gdn_prompt_kernel(q_ref, k_ref, v_ref, gate_ref, hq_ref, hk_ref, hv_ref, gates_ref,
                       cwq_ref, cwk_ref, cwv_ref, gnw_ref,
                       o_ref, sout_ref, s_ref, ext_ref, *, n_heads, hs, rows):
    hg = pl.program_id(1)
    blk = pl.program_id(2)
    c = GDN_CHUNK

    @pl.when(blk == 0)
    def _():
        s_ref[...] = jnp.zeros_like(s_ref)

    for a, (x_ref, halo_ref) in enumerate(((q_ref, hq_ref), (k_ref, hk_ref), (v_ref, hv_ref))):
        for hi in range(hs):
            lanes = slice(hi * HEAD_DIM, (hi + 1) * HEAD_DIM)
            ext_ref[a * hs + hi, 0:HALO_CONV, :] = jnp.where(blk > 0, halo_ref[:, lanes], 0.0)
            ext_ref[a * hs + hi, HALO_CONV:, :] = x_ref[:, lanes]

    def conv_silu(a, cw_ref, r0, hi):
        lanes = slice(hi * HEAD_DIM, (hi + 1) * HEAD_DIM)
        acc = None
        for j in range(CONV_W):
            term = cw_ref[j:j + 1, lanes] * ext_ref[a * hs + hi, pl.ds(HALO_CONV - (CONV_W - 1) + j + r0, c), :]
            acc = term if acc is None else acc + term
        return _silu(acc)

    row = lax.broadcasted_iota(jnp.int32, (c, c), 0)
    col = lax.broadcasted_iota(jnp.int32, (c, c), 1)
    tril = row >= col
    strict = row > col
    tril_f = jnp.where(tril, 1.0, 0.0)
    gnw = gnw_ref[...]
    heads = range(hs)

    s = [s_ref[hi] for hi in heads]
    for ci in range(rows // c):
        r0 = ci * c
        gates = gates_ref[r0:r0 + c, :]
        gcum = jnp.dot(tril_f, gates, preferred_element_type=F32, precision=lax.Precision.HIGHEST)
        qc = [_l2norm(conv_silu(0, cwq_ref, r0, hi)) * (HEAD_DIM ** -0.5) for hi in heads]
        kc = [_l2norm(conv_silu(1, cwk_ref, r0, hi)) for hi in heads]
        vc = [conv_silu(2, cwv_ref, r0, hi) for hi in heads]
        gc = [_lane_column(gcum, hg * hs + hi) for hi in heads]
        bc = [_lane_column(gates, hg * hs + hi + n_heads) for hi in heads]
        decay = [jnp.exp(jnp.where(tril, g - g.T, -1e30)) for g in gc]
        egc = [jnp.exp(g) for g in gc]
        g_last = [g[c - 1:c, :] for g in gc]
        kk = [_dot_nt(k, k) for k in kc]
        qk = [_dot_nt(q, k) for q, k in zip(qc, kc)]
        lm = [jnp.where(strict, kk[hi] * bc[hi] * decay[hi], 0.0) for hi in heads]
        t_inv = _unit_lower_inverses(lm)
        uw = [_dot(t_inv[hi], jnp.concatenate([vc[hi] * bc[hi], kc[hi] * (bc[hi] * egc[hi])], axis=1))
              for hi in heads]
        intra = [jnp.where(tril, qk[hi] * decay[hi], 0.0) for hi in heads]
        kdt = [(kc[hi] * jnp.exp(g_last[hi] - gc[hi])).T for hi in heads]
        r1 = [_dot(jnp.concatenate([uw[hi][:, HEAD_DIM:], qc[hi] * egc[hi]], axis=0), s[hi]) for hi in heads]
        v_new = [uw[hi][:, :HEAD_DIM] - r1[hi][:c] for hi in heads]
        r2 = [_dot(jnp.concatenate([intra[hi], kdt[hi]], axis=0), v_new[hi]) for hi in heads]
        s = [s[hi] * jnp.exp(g_last[hi]) + r2[hi][c:] for hi in heads]
        for hi in heads:
            lanes = slice(hi * HEAD_DIM, (hi + 1) * HEAD_DIM)
            o = _rms(r1[hi][c:] + r2[hi][:c], gnw) * _silu(gate_ref[r0:r0 + c, lanes])
            o_ref[r0:r0 + c, lanes] = o.astype(o_ref.dtype)

    for hi in range(hs):
        s_ref[hi] = s[hi]

    @pl.when(blk == pl.num_programs(2) - 1)
    def _():
        for hi in range(hs):
            sout_ref[0, hi] = s[hi]


def _gdn_prompt(proj, gates, conv_w, gnw, batch, seq, n_heads, hs, rows):
    nblk = seq // rows
    hb = rows // HALO_CONV
    ng = n_heads // hs
    width = hs * HEAD_DIM

    def main(off):
        return pl.BlockSpec((rows, width), lambda b, h, k: (b * nblk + k, off * ng + h))

    def halo(off):
        return pl.BlockSpec(
            (HALO_CONV, width), lambda b, h, k: (jnp.maximum((b * nblk + k) * hb - 1, 0), off * ng + h))

    def cw(off):
        return pl.BlockSpec((CONV_W, width), lambda b, h, k: (0, off * ng + h))

    kern = functools.partial(_gdn_prompt_kernel, n_heads=n_heads, hs=hs, rows=rows)
    return pl.pallas_call(
        kern,
        grid=(batch, ng, nblk),
        in_specs=[main(0), main(1), main(2), main(3), halo(0), halo(1), halo(2),
                  pl.BlockSpec((rows, HEAD_DIM), lambda b, h, k: (b * nblk + k, 0)),
                  cw(0), cw(1), cw(2),
                  pl.BlockSpec((1, HEAD_DIM), lambda b, h, k: (0, 0))],
        out_specs=[
            pl.BlockSpec((rows, width), lambda b, h, k: (b * nblk + k, h)),
            pl.BlockSpec((1, hs, HEAD_DIM, HEAD_DIM), lambda b, h, k: (b, h, 0, 0)),
        ],
        out_shape=[
            jax.ShapeDtypeStruct((batch * seq, n_heads * HEAD_DIM), BF16),
            jax.ShapeDtypeStruct((batch, n_heads, HEAD_DIM, HEAD_DIM), F32),
        ],
        scratch_shapes=[pltpu.VMEM((hs, HEAD_DIM, HEAD_DIM), F32),
                        pltpu.VMEM((3 * hs, rows + HALO_CONV, HEAD_DIM), F32)],
        compiler_params=_cparams(3),
        name="gdn_prompt",
    )(proj, proj, proj, proj, proj, proj, proj, gates, conv_w, conv_w, conv_w, gnw)


def _gdn_sample_kernel(x_ref, cs_ref, gates_ref, cw_ref, gnw_ref, s_in_ref,
                       o_ref, s_out_ref, o_scr, *, n_heads, tb):
    d = n_heads * HEAD_DIM
    x = x_ref[...]
    cw = cw_ref[...]
    conv = cw[CONV_W - 1:CONV_W, :] * x[:, :3 * d]
    for j in range(CONV_W - 1):
        conv = conv + cw[j:j + 1, :] * cs_ref[:, j * 3 * d:(j + 1) * 3 * d]
    qkv = _silu(conv)

    gates = gates_ref[...]
    eg_all = jnp.exp(gates)
    pad = jnp.zeros((HEAD_DIM - tb, HEAD_DIM), F32)

    for h in range(n_heads):
        qh = _l2norm(qkv[:, h * HEAD_DIM:(h + 1) * HEAD_DIM]) * (HEAD_DIM ** -0.5)
        kh = _l2norm(qkv[:, d + h * HEAD_DIM:d + (h + 1) * HEAD_DIM])
        vh = qkv[:, 2 * d + h * HEAD_DIM:2 * d + (h + 1) * HEAD_DIM]
        qt = jnp.concatenate([qh, pad], axis=0).T
        kt = jnp.concatenate([kh, pad], axis=0).T
        for t in range(tb):
            eg = eg_all[t:t + 1, h:h + 1]
            bt = gates[t:t + 1, n_heads + h:n_heads + h + 1]
            kcol = kt[:, t:t + 1]
            qcol = qt[:, t:t + 1]
            s1 = s_in_ref[t, h] * eg
            ks = jnp.sum(s1 * kcol, axis=0, keepdims=True)
            delta = (vh[t:t + 1, :] - ks) * bt
            s2 = s1 + kcol * delta
            s_out_ref[t, h] = s2
            o_scr[t:t + 1, h * HEAD_DIM:(h + 1) * HEAD_DIM] = jnp.sum(s2 * qcol, axis=0, keepdims=True)

    gnw = gnw_ref[...]
    for h in range(n_heads):
        sl = slice(h * HEAD_DIM, (h + 1) * HEAD_DIM)
        o = _rms(o_scr[:, sl], gnw) * _silu(x[:, 3 * d + h * HEAD_DIM:3 * d + (h + 1) * HEAD_DIM])
        o_ref[:, sl] = o.astype(o_ref.dtype)


def _gdn_sample(proj, conv_state, gates, conv_w, gnw, state, n_heads, tb):
    m = proj.shape[0]
    d = n_heads * HEAD_DIM
    kern = functools.partial(_gdn_sample_kernel, n_heads=n_heads, tb=tb)
    return pl.pallas_call(
        kern,
        grid=(m // tb,),
        in_specs=[
            pl.BlockSpec((tb, 4 * d), lambda i: (i, 0)),
            pl.BlockSpec((tb, (CONV_W - 1) * 3 * d), lambda i: (i, 0)),
            pl.BlockSpec((tb, HEAD_DIM), lambda i: (i, 0)),
            pl.BlockSpec((CONV_W, 3 * d), lambda i: (0, 0)),
            pl.BlockSpec((1, HEAD_DIM), lambda i: (0, 0)),
            pl.BlockSpec((tb, n_heads, HEAD_DIM, HEAD_DIM), lambda i: (i, 0, 0, 0)),
        ],
        out_specs=[
            pl.BlockSpec((tb, d), lambda i: (i, 0)),
            pl.BlockSpec((tb, n_heads, HEAD_DIM, HEAD_DIM), lambda i: (i, 0, 0, 0)),
        ],
        out_shape=[
            jax.ShapeDtypeStruct((m, d), BF16),
            jax.ShapeDtypeStruct(state.shape, F32),
        ],
        scratch_shapes=[pltpu.VMEM((tb, d), F32)],
        compiler_params=_cparams(1),
        name="gdn_sample",
    )(proj, conv_state, gates, conv_w, gnw, state)


def _mix_out_kernel(oa_ref, u_ref, hist_ref, x_ref, wp_ref, ps_ref, wo_ref, nw_ref, out_ref, *scratch,
                    tm, seq, pos0):
    d_pool = u_ref.shape[1]
    gd = d_pool // len(POOL_WINDOWS)
    u = u_ref[...]
    if seq > 1:
        (ext_ref,) = scratch
        start = (pl.program_id(0) * tm) % seq
        ext_ref[0:HALO_POOL, :] = jnp.where(start > 0, hist_ref[...], 0.0)
        ext_ref[HALO_POOL:, :] = u
        pos = pos0 + start + lax.broadcasted_iota(jnp.int32, (tm, gd), 0)
    else:
        pos = jnp.full((tm, gd), pos0, jnp.int32)

    parts = [oa_ref[...]]
    for gi, win in enumerate(POOL_WINDOWS):
        sl = slice(gi * gd, (gi + 1) * gd)
        cur = u[:, sl]
        wsum = cur
        for r in range(1, win):
            if seq > 1:
                wsum = wsum + ext_ref[pl.ds(HALO_POOL - r, tm), sl]
            else:
                off = (POOL_BUF - r) * d_pool
                wsum = wsum + hist_ref[:, off + gi * gd:off + (gi + 1) * gd]
        cnt = jnp.minimum(pos + 1, win).astype(F32)
        pooled = wsum / cnt - cur
        ob = _dot(pooled, wp_ref[gi]) * ps_ref[:, sl]
        parts.append(ob.astype(BF16))
    mix = jnp.dot(jnp.concatenate(parts, axis=1), wo_ref[...], preferred_element_type=F32)
    out_ref[...] = x_ref[...] + _rms(mix, nw_ref[...])


def _mix_out(o_a, u_src, u_col, hist, x, w_pool, pool_scale, w_out, norm_w, tm, seq, pos0):
    m, d = x.shape
    d_pool = pool_scale.shape[1]
    d_gdn = o_a.shape[1]
    if seq > 1:
        hb = tm // HALO_POOL
        hist_spec = pl.BlockSpec((HALO_POOL, d_pool), lambda i: (jnp.maximum(i * hb - 1, 0), u_col))
        scratch = [pltpu.VMEM((tm + HALO_POOL, d_pool), F32)]
    else:
        hist_spec = pl.BlockSpec((tm, POOL_BUF * d_pool), lambda i: (i, 0))
        scratch = []
    kern = functools.partial(_mix_out_kernel, tm=tm, seq=seq, pos0=pos0)
    return pl.pallas_call(
        kern,
        grid=(m // tm,),
        in_specs=[
            pl.BlockSpec((tm, d_gdn), lambda i: (i, 0)),
            pl.BlockSpec((tm, d_pool), lambda i: (i, u_col)),
            hist_spec,
            pl.BlockSpec((tm, d), lambda i: (i, 0)),
            pl.BlockSpec(w_pool.shape, lambda i: (0, 0, 0)),
            pl.BlockSpec((1, d_pool), lambda i: (0, 0)),
            pl.BlockSpec(w_out.shape, lambda i: (0, 0)),
            pl.BlockSpec((1, d), lambda i: (0, 0)),
        ],
        out_specs=pl.BlockSpec((tm, d), lambda i: (i, 0)),
        out_shape=jax.ShapeDtypeStruct((m, d), F32),
        scratch_shapes=scratch,
        compiler_params=_cparams(1),
        name="mix_out",
    )(o_a, u_src, hist, x, w_pool, pool_scale, w_out, norm_w)


def _mlp_kernel(x_ref, nw1_ref, wu_ref, wd_ref, nw2_ref, out_ref, h_ref):
    j = pl.program_id(1)

    @pl.when(j == 0)
    def _():
        h_ref[...] = _rms(x_ref[...], nw1_ref[...]).astype(BF16)
        out_ref[...] = jnp.zeros_like(out_ref)

    up = jnp.dot(h_ref[...], wu_ref[...], preferred_element_type=F32)
    act = jnp.square(jnp.maximum(up, 0.0)).astype(BF16)
    out_ref[...] += jnp.dot(act, wd_ref[...], preferred_element_type=F32)

    @pl.when(j == pl.num_programs(1) - 1)
    def _():
        out_ref[...] = x_ref[...] + _rms(out_ref[...], nw2_ref[...])


def _mlp(x, nw1, w_up, w_down, nw2, tm, tf):
    m, d = x.shape
    f = w_up.shape[1]
    return pl.pallas_call(
        _mlp_kernel,
        grid=(m // tm, f // tf),
        in_specs=[
            pl.BlockSpec((tm, d), lambda i, j: (i, 0)),
            pl.BlockSpec((1, d), lambda i, j: (0, 0)),
            pl.BlockSpec((d, tf), lambda i, j: (0, j)),
            pl.BlockSpec((tf, d), lambda i, j: (j, 0)),
            pl.BlockSpec((1, d), lambda i, j: (0, 0)),
        ],
        out_specs=pl.BlockSpec((tm, d), lambda i, j: (i, 0)),
        out_shape=jax.ShapeDtypeStruct((m, d), F32),
        scratch_shapes=[pltpu.VMEM((tm, d), BF16)],
        compiler_params=_cparams(2),
        name="mlp",
    )(x, nw1, w_up, w_down, nw2)


def _pad_lanes(v, n=HEAD_DIM):
    return jnp.pad(v.reshape(1, -1), ((0, 0), (0, n - v.shape[-1])))


def kernel(x_prompt, x_sample, state_gdn, state_conv, state_pool, norm_pre_mix, w_in, conv_w, a_log, dt_bias,
           norm_gdn_out, w_pool, pool_scale, w_out, norm_post_mix, norm_pre_mlp, w_up, w_down, norm_post_mlp):
    batch, seq, d_model = x_prompt.shape
    dec_batch, dec_seq, _ = x_sample.shape
    assert dec_seq == 1
    depth = w_in.shape[0]
    n_heads = a_log.shape[1]
    d_gdn = n_heads * HEAD_DIM
    d_qkv = 3 * d_gdn
    d_pool = pool_scale.shape[1]
    o_gate = d_qkv + d_gdn
    o_pool = o_gate + 2 * n_heads
    u_col = o_gate // d_pool

    yp = x_prompt.reshape(batch * seq, d_model)
    ys = x_sample.reshape(dec_batch, d_model)
    outs = [[] for _ in range(6)]
    for l in range(depth):
        w_main = jnp.concatenate([w_in[l][:, :o_gate], w_in[l][:, o_pool:]], axis=1).astype(BF16)
        w_ab = jnp.pad(w_in[l][:, o_gate:o_pool], ((0, 0), (0, HEAD_DIM - 2 * n_heads))).astype(BF16)
        w_pool_b = w_pool[l].astype(BF16)
        w_out_b = w_out[l].astype(BF16)
        w_up_b = w_up[l].astype(BF16)
        w_down_b = w_down[l].astype(BF16)
        alog = _pad_lanes(a_log[l])
        dtb = _pad_lanes(dt_bias[l])
        gnw = norm_gdn_out[l].reshape(1, HEAD_DIM)
        npm = norm_pre_mix[l].reshape(1, d_model)
        npo = norm_post_mix[l].reshape(1, d_model)
        nm1 = norm_pre_mlp[l].reshape(1, d_model)
        nm2 = norm_post_mlp[l].reshape(1, d_model)
        ps = pool_scale[l].reshape(1, d_pool)

        proj_p, gates_p = _in_proj(yp, npm, w_main, w_ab, alog, dtb, n_heads, tm=1024, tn=1024)
        oa_p, s_p = _gdn_prompt(proj_p, gates_p, conv_w[l], gnw, batch, seq, n_heads, hs=8, rows=256)
        x1_p = _mix_out(oa_p, proj_p, u_col, proj_p, yp, w_pool_b, ps, w_out_b, npo, tm=256, seq=seq, pos0=0)
        yp = _mlp(x1_p, nm1, w_up_b, w_down_b, nm2, tm=512, tf=1024)
        proj_p3 = proj_p.reshape(batch, seq, -1)
        outs[0].append(s_p)
        outs[1].append(proj_p3[:, seq - (CONV_W - 1):, :d_qkv])
        outs[2].append(proj_p3[:, seq - POOL_BUF:, o_gate:])

        proj_s, gates_s = _in_proj(ys, npm, w_main, w_ab, alog, dtb, n_heads, tm=dec_batch, tn=1024)
        cs = state_conv[l].reshape(dec_batch, (CONV_W - 1) * d_qkv)
        hist = state_pool[l].reshape(dec_batch, POOL_BUF * d_pool)
        oa_s, s_s = _gdn_sample(proj_s, cs, gates_s, conv_w[l], gnw, state_gdn[l], n_heads, tb=8)
        x1_s = _mix_out(oa_s, proj_s, u_col, hist, ys, w_pool_b, ps, w_out_b, npo, tm=dec_batch, seq=1,
                        pos0=PAST_LEN)
        ys = _mlp(x1_s, nm1, w_up_b, w_down_b, nm2, tm=dec_batch, tf=1024)
        outs[3].append(s_s)
        outs[4].append(jnp.concatenate([state_conv[l][:, 1:], proj_s[:, None, :d_qkv]], axis=1))
        outs[5].append(jnp.concatenate([state_pool[l][:, 1:], proj_s[:, None, o_gate:]], axis=1))

    return (yp.reshape(batch, seq, d_model), ys.reshape(dec_batch, dec_seq, d_model),
            *[jnp.stack(o) for o in outs])
```

```python
import functools

import jax
import jax.numpy as jnp
from jax import lax
from jax.experimental import pallas as pl
from jax.experimental.pallas import tpu as pltpu

F32 = jnp.float32
BF16 = jnp.bfloat16

EPS = 1e-6
HEAD_DIM = 128
CONV_W = 4
POOL_WINDOWS = (2, 4, 8, 16)
POOL_BUF = max(POOL_WINDOWS) - 1
PAST_LEN = 16384
HALO_CONV = 8
HALO_POOL = 16
GDN_CHUNK = 128
VMEM_LIMIT = 56 * 1024 * 1024


def _cparams(n_axes):
    return pltpu.CompilerParams(
        dimension_semantics=("arbitrary",) * n_axes, vmem_limit_bytes=VMEM_LIMIT)


def _dot(a, b):
    return jnp.dot(a.astype(BF16), b.astype(BF16), preferred_element_type=F32)


def _dot_nt(a, b):
    return lax.dot_general(a.astype(BF16), b.astype(BF16), (((1,), (1,)), ((), ())),
                           preferred_element_type=F32)


def _rms(x, w):
    return x * lax.rsqrt(jnp.mean(x * x, axis=-1, keepdims=True) + EPS) * w


def _silu(x):
    return x * jax.nn.sigmoid(x)


def _softplus(x):
    return jnp.maximum(x, 0.0) + jnp.log(1.0 + jnp.exp(-jnp.abs(x)))


def _l2norm(x):
    return x * lax.rsqrt(jnp.sum(x * x, axis=-1, keepdims=True) + EPS)


def _inproj_kernel(x_ref, nw_ref, wa_ref, wb_ref, alog_ref, dtb_ref, out_ref, u_ref, gates_ref, h_ref, *,
                   n_heads, n_main):
    j = pl.program_id(1)

    @pl.when(j == 0)
    def _():
        h_ref[...] = _rms(x_ref[...], nw_ref[...]).astype(BF16)

    @pl.when(j < n_main)
    def _():
        out_ref[...] = jnp.dot(h_ref[...], wa_ref[...], preferred_element_type=F32)

    @pl.when(j == n_main)
    def _():
        d_pool = u_ref.shape[1]
        u_ref[...] = jnp.dot(h_ref[...], wb_ref[:, :d_pool], preferred_element_type=F32)
        ab = jnp.dot(h_ref[...], wb_ref[:, d_pool:], preferred_element_type=F32)
        lane = lax.broadcasted_iota(jnp.int32, ab.shape, 1)
        g = -jnp.exp(alog_ref[...]) * _softplus(ab + dtb_ref[...])
        gates_ref[...] = jnp.where(lane < n_heads, g, jax.nn.sigmoid(ab))


def _in_proj(x, norm_w, w_a, w_b, alog, dtb, n_heads, tm, tn):
    m, d = x.shape
    n = w_a.shape[1]
    n_main = n // tn
    d_pool = w_b.shape[1] - HEAD_DIM
    vec = pl.BlockSpec((1, HEAD_DIM), lambda i, j: (0, 0))
    return pl.pallas_call(
        functools.partial(_inproj_kernel, n_heads=n_heads, n_main=n_main),
        grid=(m // tm, n_main + 1),
        in_specs=[
            pl.BlockSpec((tm, d), lambda i, j: (i, 0)),
            pl.BlockSpec((1, d), lambda i, j: (0, 0)),
            pl.BlockSpec((d, tn), lambda i, j: (0, jnp.minimum(j, n_main - 1))),
            pl.BlockSpec(w_b.shape, lambda i, j: (0, 0), pipeline_mode=pl.Buffered(1)),
            vec, vec,
        ],
        out_specs=[
            pl.BlockSpec((tm, tn), lambda i, j: (i, jnp.minimum(j, n_main - 1))),
            pl.BlockSpec((tm, d_pool), lambda i, j: (i, 0)),
            pl.BlockSpec((tm, HEAD_DIM), lambda i, j: (i, 0)),
        ],
        out_shape=[jax.ShapeDtypeStruct((m, n), F32), jax.ShapeDtypeStruct((m, d_pool), F32),
                   jax.ShapeDtypeStruct((m, HEAD_DIM), F32)],
        scratch_shapes=[pltpu.VMEM((tm, d), BF16)],
        compiler_params=_cparams(2),
        name="in_proj",
    )(x, norm_w, w_a, w_b, alog, dtb)


def _lane_column(x, lane_idx):
    lane = lax.broadcasted_iota(jnp.int32, x.shape, 1)
    col = jnp.sum(jnp.where(lane == lane_idx, x, 0.0), axis=1, keepdims=True)
    return jnp.broadcast_to(col, x.shape)


def _unit_lower_inverses(lms):
    n = lms[0].shape[0]
    row = lax.broadcasted_iota(jnp.int32, (n, n), 0)
    col = lax.broadcasted_iota(jnp.int32, (n, n), 1)
    eye = jnp.where(row == col, 1.0, 0.0)
    nks = [-lm for lm in lms]
    qs = [eye + nk for nk in nks]
    nks = [_dot(nk, nk) for nk in nks]
    power = 2
    while 2 * power < n:
        res = [_dot(nk, jnp.concatenate([nk, q], axis=1)) for nk, q in zip(nks, qs)]
        nks = [r[:, :n] for r in res]
        qs = [q + r[:, n:] for q, r in zip(qs, res)]
        power *= 2
    return [q + _dot(nk, q) for nk, q in zip(nks, qs)]


def _gdn_prompt_kernel(q_ref, k_ref, v_ref, gate_ref, hq_ref, hk_ref, hv_ref, gates_ref,
                       cwq_ref, cwk_ref, cwv_ref, gnw_ref,
                       o_ref, sout_ref, s_ref, ext_ref, *, n_heads, hs, rows):
    hg = pl.program_id(1)
    blk = pl.program_id(2)
    c = GDN_CHUNK

    @pl.when(blk == 0)
    def _():
        s_ref[...] = jnp.zeros_like(s_ref)

    for a, (x_ref, halo_ref) in enumerate(((q_ref, hq_ref), (k_ref, hk_ref), (v_ref, hv_ref))):
        for hi in range(hs):
            lanes = slice(hi * HEAD_DIM, (hi + 1) * HEAD_DIM)
            ext_ref[a * hs + hi, 0:HALO_CONV, :] = jnp.where(blk > 0, halo_ref[:, lanes], 0.0)
            ext_ref[a * hs + hi, HALO_CONV:, :] = x_ref[:, lanes]

    def conv_silu(a, cw_ref, r0, hi):
        lanes = slice(hi * HEAD_DIM, (hi + 1) * HEAD_DIM)
        acc = None
        for j in range(CONV_W):
            term = cw_ref[j:j + 1, lanes] * ext_ref[a * hs + hi, pl.ds(HALO_CONV - (CONV_W - 1) + j + r0, c), :]
            acc = term if acc is None else acc + term
        return _silu(acc)

    row = lax.broadcasted_iota(jnp.int32, (c, c), 0)
    col = lax.broadcasted_iota(jnp.int32, (c, c), 1)
    tril = row >= col
    strict = row > col
    tril_f = jnp.where(tril, 1.0, 0.0)
    gnw = gnw_ref[...]
    heads = range(hs)

    s = [s_ref[hi] for hi in heads]
    for ci in range(rows // c):
        r0 = ci * c
        gates = gates_ref[r0:r0 + c, :]
        gcum = jnp.dot(tril_f, gates, preferred_element_type=F32, precision=lax.Precision.HIGHEST)
        qc = [_l2norm(conv_silu(0, cwq_ref, r0, hi)) * (HEAD_DIM ** -0.5) for hi in heads]
        kc = [_l2norm(conv_silu(1, cwk_ref, r0, hi)) for hi in heads]
        vc = [conv_silu(2, cwv_ref, r0, hi) for hi in heads]
        gc = [_lane_column(gcum, hg * hs + hi) for hi in heads]
        bc = [_lane_column(gates, hg * hs + hi + n_heads) for hi in heads]
        decay = [jnp.exp(jnp.where(tril, g - g.T, -1e30)) for g in gc]
        egc = [jnp.exp(g) for g in gc]
        g_last = [g[c - 1:c, :] for g in gc]
        kk = [_dot_nt(k, k) for k in kc]
        qk = [_dot_nt(q, k) for q, k in zip(qc, kc)]
        lm = [jnp.where(strict, kk[hi] * bc[hi] * decay[hi], 0.0) for hi in heads]
        t_inv = _unit_lower_inverses(lm)
        uw = [_dot(t_inv[hi], jnp.concatenate([vc[hi] * bc[hi], kc[hi] * (bc[hi] * egc[hi])], axis=1))
              for hi in heads]
        intra = [jnp.where(tril, qk[hi] * decay[hi], 0.0) for hi in heads]
        kdt = [(kc[hi] * jnp.exp(g_last[hi] - gc[hi])).T for hi in heads]
        r1 = [_dot(jnp.concatenate([uw[hi][:, HEAD_DIM:], qc[hi] * egc[hi]], axis=0), s[hi]) for hi in heads]
        v_new = [uw[hi][:, :HEAD_DIM] - r1[hi][:c] for hi in heads]
        r2 = [_dot(jnp.concatenate([intra[hi], kdt[hi]], axis=0), v_new[hi]) for hi in heads]
        s = [s[hi] * jnp.exp(g_last[hi]) + r2[hi][c:] for hi in heads]
        for hi in heads:
            lanes = slice(hi * HEAD_DIM, (hi + 1) * HEAD_DIM)
            o = _rms(r1[hi][c:] + r2[hi][:c], gnw) * _silu(gate_ref[r0:r0 + c, lanes])
            o_ref[r0:r0 + c, lanes] = o.astype(o_ref.dtype)

    for hi in range(hs):
        s_ref[hi] = s[hi]

    @pl.when(blk == pl.num_programs(2) - 1)
    def _():
        for hi in range(hs):
            sout_ref[0, hi] = s[hi]


def _gdn_prompt(proj, gates, conv_w, gnw, batch, seq, n_heads, hs, rows):
    nblk = seq // rows
    hb = rows // HALO_CONV
    ng = n_heads // hs
    width = hs * HEAD_DIM

    def main(off):
        return pl.BlockSpec((rows, width), lambda b, h, k: (b * nblk + k, off * ng + h))

    def halo(off):
        return pl.BlockSpec(
            (HALO_CONV, width), lambda b, h, k: (jnp.maximum((b * nblk + k) * hb - 1, 0), off * ng + h))

    def cw(off):
        return pl.BlockSpec((CONV_W, width), lambda b, h, k: (0, off * ng + h))

    kern = functools.partial(_gdn_prompt_kernel, n_heads=n_heads, hs=hs, rows=rows)
    return pl.pallas_call(
        kern,
        grid=(batch, ng, nblk),
        in_specs=[main(0), main(1), main(2), main(3), halo(0), halo(1), halo(2),
                  pl.BlockSpec((rows, HEAD_DIM), lambda b, h, k: (b * nblk + k, 0)),
                  cw(0), cw(1), cw(2),
                  pl.BlockSpec((1, HEAD_DIM), lambda b, h, k: (0, 0))],
        out_specs=[
            pl.BlockSpec((rows, width), lambda b, h, k: (b * nblk + k, h)),
            pl.BlockSpec((1, hs, HEAD_DIM, HEAD_DIM), lambda b, h, k: (b, h, 0, 0)),
        ],
        out_shape=[
            jax.ShapeDtypeStruct((batch * seq, n_heads * HEAD_DIM), BF16),
            jax.ShapeDtypeStruct((batch, n_heads, HEAD_DIM, HEAD_DIM), F32),
        ],
        scratch_shapes=[pltpu.VMEM((hs, HEAD_DIM, HEAD_DIM), F32),
                        pltpu.VMEM((3 * hs, rows + HALO_CONV, HEAD_DIM), F32)],
        compiler_params=_cparams(3),
        name="gdn_prompt",
    )(proj, proj, proj, proj, proj, proj, proj, gates, conv_w, conv_w, conv_w, gnw)


def _gdn_sample_kernel(x_ref, cs_ref, gates_ref, cw_ref, gnw_ref, s_in_ref,
                       o_ref, s_out_ref, o_scr, *, n_heads, tb):
    d = n_heads * HEAD_DIM
    x = x_ref[...]
    cw = cw_ref[...]
    conv = cw[CONV_W - 1:CONV_W, :] * x[:, :3 * d]
    for j in range(CONV_W - 1):
        conv = conv + cw[j:j + 1, :] * cs_ref[:, j * 3 * d:(j + 1) * 3 * d]
    qkv = _silu(conv)

    gates = gates_ref[...]
    eg_all = jnp.exp(gates)
    pad = jnp.zeros((HEAD_DIM - tb, HEAD_DIM), F32)

    for h in range(n_heads):
        qh = _l2norm(qkv[:, h * HEAD_DIM:(h + 1) * HEAD_DIM]) * (HEAD_DIM ** -0.5)
        kh = _l2norm(qkv[:, d + h * HEAD_DIM:d + (h + 1) * HEAD_DIM])
        vh = qkv[:, 2 * d + h * HEAD_DIM:2 * d + (h + 1) * HEAD_DIM]
        qt = jnp.concatenate([qh, pad], axis=0).T
        kt = jnp.concatenate([kh, pad], axis=0).T
        toks = range(tb)
        sq = (HEAD_DIM, HEAD_DIM)
        kb = [jnp.broadcast_to(kt[:, t:t + 1], sq) for t in toks]
        qb = [jnp.broadcast_to(qt[:, t:t + 1], sq) for t in toks]
        s1 = [s_in_ref[t, h] * eg_all[t:t + 1, h:h + 1] for t in toks]
        ks = [jnp.sum(s1[t] * kb[t], axis=0, keepdims=True) for t in toks]
        delta = [(vh[t:t + 1, :] - ks[t]) * gates[t:t + 1, n_heads + h:n_heads + h + 1] for t in toks]
        s2 = [s1[t] + kb[t] * delta[t] for t in toks]
        for t in toks:
            s_out_ref[t, h] = s2[t]
            o_scr[t:t + 1, h * HEAD_DIM:(h + 1) * HEAD_DIM] = jnp.sum(s2[t] * qb[t], axis=0, keepdims=True)

    gnw = gnw_ref[...]
    for h in range(n_heads):
        sl = slice(h * HEAD_DIM, (h + 1) * HEAD_DIM)
        o = _rms(o_scr[:, sl], gnw) * _silu(x[:, 3 * d + h * HEAD_DIM:3 * d + (h + 1) * HEAD_DIM])
        o_ref[:, sl] = o.astype(o_ref.dtype)


def _gdn_sample(proj, conv_state, gates, conv_w, gnw, state, n_heads, tb):
    m = proj.shape[0]
    d = n_heads * HEAD_DIM
    kern = functools.partial(_gdn_sample_kernel, n_heads=n_heads, tb=tb)
    return pl.pallas_call(
        kern,
        grid=(m // tb,),
        in_specs=[
            pl.BlockSpec((tb, 4 * d), lambda i: (i, 0)),
            pl.BlockSpec((tb, (CONV_W - 1) * 3 * d), lambda i: (i, 0)),
            pl.BlockSpec((tb, HEAD_DIM), lambda i: (i, 0)),
            pl.BlockSpec((CONV_W, 3 * d), lambda i: (0, 0)),
            pl.BlockSpec((1, HEAD_DIM), lambda i: (0, 0)),
            pl.BlockSpec((tb, n_heads, HEAD_DIM, HEAD_DIM), lambda i: (i, 0, 0, 0)),
        ],
        out_specs=[
            pl.BlockSpec((tb, d), lambda i: (i, 0)),
            pl.BlockSpec((tb, n_heads, HEAD_DIM, HEAD_DIM), lambda i: (i, 0, 0, 0)),
        ],
        out_shape=[
            jax.ShapeDtypeStruct((m, d), BF16),
            jax.ShapeDtypeStruct(state.shape, F32),
        ],
        scratch_shapes=[pltpu.VMEM((tb, d), F32)],
        compiler_params=_cparams(1),
        name="gdn_sample",
    )(proj, conv_state, gates, conv_w, gnw, state)


def _mix_out_kernel(oa_ref, u_ref, hist_ref, x_ref, wp_ref, ps_ref, wo_ref, nw_ref, out_ref, *scratch,
                    tm, seq, pos0):
    d_pool = u_ref.shape[1]
    gd = d_pool // len(POOL_WINDOWS)
    u = u_ref[...]
    if seq > 1:
        (ext_ref,) = scratch
        start = (pl.program_id(0) * tm) % seq
        ext_ref[0:HALO_POOL, :] = jnp.where(start > 0, hist_ref[...], 0.0)
        ext_ref[HALO_POOL:, :] = u
        pos = pos0 + start + lax.broadcasted_iota(jnp.int32, (tm, gd), 0)
    else:
        pos = jnp.full((tm, gd), pos0, jnp.int32)

    d_gdn = oa_ref.shape[1]
    mix = jnp.dot(oa_ref[...], wo_ref[:d_gdn, :], preferred_element_type=F32)
    parts = []
    for gi, win in enumerate(POOL_WINDOWS):
        sl = slice(gi * gd, (gi + 1) * gd)
        cur = u[:, sl]
        wsum = cur
        for r in range(1, win):
            if seq > 1:
                wsum = wsum + ext_ref[pl.ds(HALO_POOL - r, tm), sl]
            else:
                off = (POOL_BUF - r) * d_pool
                wsum = wsum + hist_ref[:, off + gi * gd:off + (gi + 1) * gd]
        cnt = jnp.minimum(pos + 1, win).astype(F32)
        pooled = wsum / cnt - cur
        ob = _dot(pooled, wp_ref[gi]) * ps_ref[:, sl]
        parts.append(ob.astype(BF16))
    mix = mix + jnp.dot(jnp.concatenate(parts, axis=1), wo_ref[d_gdn:, :], preferred_element_type=F32)
    out_ref[...] = x_ref[...] + _rms(mix, nw_ref[...])


def _mix_out(o_a, u, hist, x, w_pool, pool_scale, w_out, norm_w, tm, seq, pos0):
    m, d = x.shape
    d_pool = pool_scale.shape[1]
    d_gdn = o_a.shape[1]
    if seq > 1:
        hb = tm // HALO_POOL
        hist_spec = pl.BlockSpec((HALO_POOL, d_pool), lambda i: (jnp.maximum(i * hb - 1, 0), 0))
        scratch = [pltpu.VMEM((tm + HALO_POOL, d_pool), F32)]
    else:
        hist_spec = pl.BlockSpec((tm, POOL_BUF * d_pool), lambda i: (i, 0))
        scratch = []
    kern = functools.partial(_mix_out_kernel, tm=tm, seq=seq, pos0=pos0)
    return pl.pallas_call(
        kern,
        grid=(m // tm,),
        in_specs=[
            pl.BlockSpec((tm, d_gdn), lambda i: (i, 0)),
            pl.BlockSpec((tm, d_pool), lambda i: (i, 0)),
            hist_spec,
            pl.BlockSpec((tm, d), lambda i: (i, 0)),
            pl.BlockSpec(w_pool.shape, lambda i: (0, 0, 0), pipeline_mode=pl.Buffered(1)),
            pl.BlockSpec((1, d_pool), lambda i: (0, 0)),
            pl.BlockSpec(w_out.shape, lambda i: (0, 0), pipeline_mode=pl.Buffered(1)),
            pl.BlockSpec((1, d), lambda i: (0, 0)),
        ],
        out_specs=pl.BlockSpec((tm, d), lambda i: (i, 0)),
        out_shape=jax.ShapeDtypeStruct((m, d), F32),
        scratch_shapes=scratch,
        compiler_params=_cparams(1),
        name="mix_out",
    )(o_a, u, hist, x, w_pool, pool_scale, w_out, norm_w)


def _mlp_kernel(x_ref, nw1_ref, wu_ref, wd_ref, nw2_ref, out_ref, h_ref):
    j = pl.program_id(1)

    @pl.when(j == 0)
    def _():
        h_ref[...] = _rms(x_ref[...], nw1_ref[...]).astype(BF16)
        out_ref[...] = jnp.zeros_like(out_ref)

    up = jnp.dot(h_ref[...], wu_ref[...], preferred_element_type=F32)
    act = jnp.square(jnp.maximum(up, 0.0)).astype(BF16)
    out_ref[...] += jnp.dot(act, wd_ref[...], preferred_element_type=F32)

    @pl.when(j == pl.num_programs(1) - 1)
    def _():
        out_ref[...] = x_ref[...] + _rms(out_ref[...], nw2_ref[...])


def _mlp(x, nw1, w_up, w_down, nw2, tm, tf):
    m, d = x.shape
    f = w_up.shape[1]
    return pl.pallas_call(
        _mlp_kernel,
        grid=(m // tm, f // tf),
        in_specs=[
            pl.BlockSpec((tm, d), lambda i, j: (i, 0)),
            pl.BlockSpec((1, d), lambda i, j: (0, 0)),
            pl.BlockSpec((d, tf), lambda i, j: (0, j)),
            pl.BlockSpec((tf, d), lambda i, j: (j, 0)),
            pl.BlockSpec((1, d), lambda i, j: (0, 0)),
        ],
        out_specs=pl.BlockSpec((tm, d), lambda i, j: (i, 0)),
        out_shape=jax.ShapeDtypeStruct((m, d), F32),
        scratch_shapes=[pltpu.VMEM((tm, d), BF16)],
        compiler_params=_cparams(2),
        name="mlp",
    )(x, nw1, w_up, w_down, nw2)


def _pad_lanes(v, n=HEAD_DIM):
    return jnp.pad(v.reshape(1, -1), ((0, 0), (0, n - v.shape[-1])))


def kernel(x_prompt, x_sample, state_gdn, state_conv, state_pool, norm_pre_mix, w_in, conv_w, a_log, dt_bias,
           norm_gdn_out, w_pool, pool_scale, w_out, norm_post_mix, norm_pre_mlp, w_up, w_down, norm_post_mlp):
    batch, seq, d_model = x_prompt.shape
    dec_batch, dec_seq, _ = x_sample.shape
    assert dec_seq == 1
    depth = w_in.shape[0]
    n_heads = a_log.shape[1]
    d_gdn = n_heads * HEAD_DIM
    d_qkv = 3 * d_gdn
    d_pool = pool_scale.shape[1]
    o_gate = d_qkv + d_gdn
    o_pool = o_gate + 2 * n_heads
    u_col = o_gate // d_pool

    yp = x_prompt.reshape(batch * seq, d_model)
    ys = x_sample.reshape(dec_batch, d_model)
    outs = [[] for _ in range(6)]
    for l in range(depth):
        w_a = w_in[l][:, :o_gate].astype(BF16)
        w_b = jnp.concatenate(
            [w_in[l][:, o_pool:], w_in[l][:, o_gate:o_pool],
             jnp.zeros((d_model, HEAD_DIM - 2 * n_heads), F32)], axis=1).astype(BF16)
        w_pool_b = w_pool[l].astype(BF16)
        w_out_b = w_out[l].astype(BF16)
        w_up_b = w_up[l].astype(BF16)
        w_down_b = w_down[l].astype(BF16)
        alog = _pad_lanes(a_log[l])
        dtb = _pad_lanes(dt_bias[l])
        gnw = norm_gdn_out[l].reshape(1, HEAD_DIM)
        npm = norm_pre_mix[l].reshape(1, d_model)
        npo = norm_post_mix[l].reshape(1, d_model)
        nm1 = norm_pre_mlp[l].reshape(1, d_model)
        nm2 = norm_post_mlp[l].reshape(1, d_model)
        ps = pool_scale[l].reshape(1, d_pool)

        proj_p, u_p, gates_p = _in_proj(yp, npm, w_a, w_b, alog, dtb, n_heads, tm=1024, tn=512)
        oa_p, s_p = _gdn_prompt(proj_p, gates_p, conv_w[l], gnw, batch, seq, n_heads, hs=8, rows=256)
        x1_p = _mix_out(oa_p, u_p, u_p, yp, w_pool_b, ps, w_out_b, npo, tm=512, seq=seq, pos0=0)
        yp = _mlp(x1_p, nm1, w_up_b, w_down_b, nm2, tm=512, tf=1024)
        outs[0].append(s_p)
        outs[1].append(proj_p.reshape(batch, seq, -1)[:, seq - (CONV_W - 1):, :d_qkv])
        outs[2].append(u_p.reshape(batch, seq, -1)[:, seq - POOL_BUF:])

        proj_s, u_s, gates_s = _in_proj(ys, npm, w_a, w_b, alog, dtb, n_heads, tm=dec_batch, tn=512)
        cs = state_conv[l].reshape(dec_batch, (CONV_W - 1) * d_qkv)
        hist = state_pool[l].reshape(dec_batch, POOL_BUF * d_pool)
        oa_s, s_s = _gdn_sample(proj_s, cs, gates_s, conv_w[l], gnw, state_gdn[l], n_heads, tb=8)
        x1_s = _mix_out(oa_s, u_s, hist, ys, w_pool_b, ps, w_out_b, npo, tm=dec_batch, seq=1, pos0=PAST_LEN)
        ys = _mlp(x1_s, nm1, w_up_b, w_down_b, nm2, tm=dec_batch, tf=1024)
        outs[3].append(s_s)
        outs[4].append(jnp.concatenate([state_conv[l][:, 1:], proj_s[:, None, :d_qkv]], axis=1))
        outs[5].append(jnp.concatenate([state_pool[l][:, 1:], u_s[:, None]], axis=1))

    return (yp.reshape(batch, seq, d_model), ys.reshape(dec_batch, dec_seq, d_model),
            *[jnp.stack(o) for o in outs])
```

```python
import functools

import jax
import jax.numpy as jnp
from jax import lax
from jax.experimental import pallas as pl
from jax.experimental.pallas import tpu as pltpu

F32 = jnp.float32
BF16 = jnp.bfloat16

EPS = 1e-6
HEAD_DIM = 128
CONV_W = 4
POOL_WINDOWS = (2, 4, 8, 16)
POOL_BUF = max(POOL_WINDOWS) - 1
PAST_LEN = 16384
HALO_CONV = 8
HALO_POOL = 16
GDN_CHUNK = 128
VMEM_LIMIT = 56 * 1024 * 1024


def _cparams(n_axes):
    return pltpu.CompilerParams(
        dimension_semantics=("arbitrary",) * n_axes, vmem_limit_bytes=VMEM_LIMIT)


def _dot(a, b):
    return jnp.dot(a.astype(BF16), b.astype(BF16), preferred_element_type=F32)


def _dot_nt(a, b):
    return lax.dot_general(a.astype(BF16), b.astype(BF16), (((1,), (1,)), ((), ())),
                           preferred_element_type=F32)


def _rms(x, w):
    return x * lax.rsqrt(jnp.mean(x * x, axis=-1, keepdims=True) + EPS) * w


def _silu(x):
    return x * (0.5 * jnp.tanh(0.5 * x) + 0.5)


def _softplus(x):
    return jnp.maximum(x, 0.0) + jnp.log(1.0 + jnp.exp(-jnp.abs(x)))


def _l2norm(x):
    return x * lax.rsqrt(jnp.sum(x * x, axis=-1, keepdims=True) + EPS)


def _inproj_kernel(x_ref, nw_ref, wa_ref, wb_ref, alog_ref, dtb_ref, out_ref, u_ref, gates_ref, h_ref, *,
                   n_heads, n_main):
    j = pl.program_id(1)

    @pl.when(j == 0)
    def _():
        h_ref[...] = _rms(x_ref[...], nw_ref[...]).astype(BF16)

    @pl.when(j < n_main)
    def _():
        out_ref[...] = jnp.dot(h_ref[...], wa_ref[...], preferred_element_type=F32)

    @pl.when(j == n_main)
    def _():
        d_pool = u_ref.shape[1]
        u_ref[...] = jnp.dot(h_ref[...], wb_ref[:, :d_pool], preferred_element_type=F32)
        ab = jnp.dot(h_ref[...], wb_ref[:, d_pool:], preferred_element_type=F32)
        lane = lax.broadcasted_iota(jnp.int32, ab.shape, 1)
        g = -jnp.exp(alog_ref[...]) * _softplus(ab + dtb_ref[...])
        gates_ref[...] = jnp.where(lane < n_heads, g, jax.nn.sigmoid(ab))


def _in_proj(x, norm_w, w_a, n, w_b, alog, dtb, n_heads, tm, tn):
    m, d = x.shape
    n_main = n // tn
    d_pool = w_b.shape[1] - HEAD_DIM
    vec = pl.BlockSpec((1, HEAD_DIM), lambda i, j: (0, 0))
    return pl.pallas_call(
        functools.partial(_inproj_kernel, n_heads=n_heads, n_main=n_main),
        grid=(m // tm, n_main + 1),
        in_specs=[
            pl.BlockSpec((tm, d), lambda i, j: (i, 0)),
            pl.BlockSpec((1, d), lambda i, j: (0, 0)),
            pl.BlockSpec((d, tn), lambda i, j: (0, jnp.minimum(j, n_main - 1))),
            pl.BlockSpec(w_b.shape, lambda i, j: (0, 0), pipeline_mode=pl.Buffered(1)),
            vec, vec,
        ],
        out_specs=[
            pl.BlockSpec((tm, tn), lambda i, j: (i, jnp.minimum(j, n_main - 1))),
            pl.BlockSpec((tm, d_pool), lambda i, j: (i, 0)),
            pl.BlockSpec((tm, HEAD_DIM), lambda i, j: (i, 0)),
        ],
        out_shape=[jax.ShapeDtypeStruct((m, n), F32), jax.ShapeDtypeStruct((m, d_pool), F32),
                   jax.ShapeDtypeStruct((m, HEAD_DIM), F32)],
        scratch_shapes=[pltpu.VMEM((tm, d), BF16)],
        compiler_params=_cparams(2),
        name="in_proj",
    )(x, norm_w, w_a, w_b, alog, dtb)


def _lane_column(x, lane_idx):
    lane = lax.broadcasted_iota(jnp.int32, x.shape, 1)
    col = jnp.sum(jnp.where(lane == lane_idx, x, 0.0), axis=1, keepdims=True)
    return jnp.broadcast_to(col, x.shape)


def _unit_lower_inverses(lms):
    n = lms[0].shape[0]
    row = lax.broadcasted_iota(jnp.int32, (n, n), 0)
    col = lax.broadcasted_iota(jnp.int32, (n, n), 1)
    eye = jnp.where(row == col, 1.0, 0.0)
    nks = [-lm for lm in lms]
    qs = [eye + nk for nk in nks]
    nks = [_dot(nk, nk) for nk in nks]
    power = 2
    while 2 * power < n:
        res = [_dot(nk, jnp.concatenate([nk, q], axis=1)) for nk, q in zip(nks, qs)]
        nks = [r[:, :n] for r in res]
        qs = [q + r[:, n:] for q, r in zip(qs, res)]
        power *= 2
    return [q + _dot(nk, q) for nk, q in zip(nks, qs)]


def _gdn_prompt_kernel(q_ref, k_ref, v_ref, gate_ref, hq_ref, hk_ref, hv_ref, gates_ref,
                       cwq_ref, cwk_ref, cwv_ref, gnw_ref, *rest, n_heads, hs, rows, n_cast):
    cast_in = rest[:n_cast]
    o_ref, sout_ref = rest[n_cast:n_cast + 2]
    cast_out = rest[n_cast + 2:2 * n_cast + 2]
    s_ref, ext_ref = rest[2 * n_cast + 2:]
    hg = pl.program_id(1)
    blk = pl.program_id(2)
    c = GDN_CHUNK

    for src, dst in zip(cast_in, cast_out):
        dst[...] = src[...].astype(dst.dtype)

    @pl.when(blk == 0)
    def _():
        s_ref[...] = jnp.zeros_like(s_ref)

    for a, (x_ref, halo_ref) in enumerate(((q_ref, hq_ref), (k_ref, hk_ref), (v_ref, hv_ref))):
        for hi in range(hs):
            lanes = slice(hi * HEAD_DIM, (hi + 1) * HEAD_DIM)
            ext_ref[a * hs + hi, 0:HALO_CONV, :] = jnp.where(blk > 0, halo_ref[:, lanes], 0.0)
            ext_ref[a * hs + hi, HALO_CONV:, :] = x_ref[:, lanes]

    def conv_silu(a, cw_ref, r0, hi):
        lanes = slice(hi * HEAD_DIM, (hi + 1) * HEAD_DIM)
        acc = None
        for j in range(CONV_W):
            term = cw_ref[j:j + 1, lanes] * ext_ref[a * hs + hi, pl.ds(HALO_CONV - (CONV_W - 1) + j + r0, c), :]
            acc = term if acc is None else acc + term
        return _silu(acc)

    row = lax.broadcasted_iota(jnp.int32, (c, c), 0)
    col = lax.broadcasted_iota(jnp.int32, (c, c), 1)
    tril = row >= col
    strict = row > col
    tril_f = jnp.where(tril, 1.0, 0.0)
    gnw = gnw_ref[...]
    heads = range(hs)

    s = [s_ref[hi] for hi in heads]
    for ci in range(rows // c):
        r0 = ci * c
        gates = gates_ref[r0:r0 + c, :]
        gcum = jnp.dot(tril_f, gates, preferred_element_type=F32, precision=lax.Precision.HIGHEST)
        qc = [_l2norm(conv_silu(0, cwq_ref, r0, hi)) * (HEAD_DIM ** -0.5) for hi in heads]
        kc = [_l2norm(conv_silu(1, cwk_ref, r0, hi)) for hi in heads]
        vc = [conv_silu(2, cwv_ref, r0, hi) for hi in heads]
        gc = [_lane_column(gcum, hg * hs + hi) for hi in heads]
        bc = [_lane_column(gates, hg * hs + hi + n_heads) for hi in heads]
        decay = [jnp.exp(jnp.where(tril, g - g.T, -1e30)) for g in gc]
        egc = [jnp.exp(g) for g in gc]
        g_last = [g[c - 1:c, :] for g in gc]
        kk = [_dot_nt(k, k) for k in kc]
        qk = [_dot_nt(q, k) for q, k in zip(qc, kc)]
        lm = [jnp.where(strict, kk[hi] * bc[hi] * decay[hi], 0.0) for hi in heads]
        t_inv = _unit_lower_inverses(lm)
        uw = [_dot(t_inv[hi], jnp.concatenate([vc[hi] * bc[hi], kc[hi] * (bc[hi] * egc[hi])], axis=1))
              for hi in heads]
        intra = [jnp.where(tril, qk[hi] * decay[hi], 0.0) for hi in heads]
        kdt = [(kc[hi] * jnp.exp(g_last[hi] - gc[hi])).T for hi in heads]
        r1 = [_dot(jnp.concatenate([uw[hi][:, HEAD_DIM:], qc[hi] * egc[hi]], axis=0), s[hi]) for hi in heads]
        v_new = [uw[hi][:, :HEAD_DIM] - r1[hi][:c] for hi in heads]
        r2 = [_dot(jnp.concatenate([intra[hi], kdt[hi]], axis=0), v_new[hi]) for hi in heads]
        s = [s[hi] * jnp.exp(g_last[hi]) + r2[hi][c:] for hi in heads]
        for hi in heads:
            lanes = slice(hi * HEAD_DIM, (hi + 1) * HEAD_DIM)
            o = _rms(r1[hi][c:] + r2[hi][:c], gnw) * _silu(gate_ref[r0:r0 + c, lanes])
            o_ref[r0:r0 + c, lanes] = o.astype(o_ref.dtype)

    for hi in range(hs):
        s_ref[hi] = s[hi]

    @pl.when(blk == pl.num_programs(2) - 1)
    def _():
        for hi in range(hs):
            sout_ref[0, hi] = s[hi]


def _gdn_prompt(proj, gates, conv_w, gnw, to_cast, batch, seq, n_heads, hs, rows):
    nblk = seq // rows
    hb = rows // HALO_CONV
    ng = n_heads // hs
    width = hs * HEAD_DIM
    steps = batch * ng * nblk

    def slab(w):
        return pl.BlockSpec((w.shape[0] // steps, w.shape[1]), lambda b, h, k: ((b * ng + h) * nblk + k, 0))

    def main(off):
        return pl.BlockSpec((rows, width), lambda b, h, k: (b * nblk + k, off * ng + h))

    def halo(off):
        return pl.BlockSpec(
            (HALO_CONV, width), lambda b, h, k: (jnp.maximum((b * nblk + k) * hb - 1, 0), off * ng + h))

    def cw(off):
        return pl.BlockSpec((CONV_W, width), lambda b, h, k: (0, off * ng + h))

    kern = functools.partial(_gdn_prompt_kernel, n_heads=n_heads, hs=hs, rows=rows, n_cast=len(to_cast))
    return pl.pallas_call(
        kern,
        grid=(batch, ng, nblk),
        in_specs=[main(0), main(1), main(2), main(3), halo(0), halo(1), halo(2),
                  pl.BlockSpec((rows, HEAD_DIM), lambda b, h, k: (b * nblk + k, 0)),
                  cw(0), cw(1), cw(2),
                  pl.BlockSpec((1, HEAD_DIM), lambda b, h, k: (0, 0)),
                  *[slab(w) for w in to_cast]],
        out_specs=[
            pl.BlockSpec((rows, width), lambda b, h, k: (b * nblk + k, h)),
            pl.BlockSpec((1, hs, HEAD_DIM, HEAD_DIM), lambda b, h, k: (b, h, 0, 0)),
            *[slab(w) for w in to_cast],
        ],
        out_shape=[
            jax.ShapeDtypeStruct((batch * seq, n_heads * HEAD_DIM), BF16),
            jax.ShapeDtypeStruct((batch, n_heads, HEAD_DIM, HEAD_DIM), F32),
            *[jax.ShapeDtypeStruct(w.shape, BF16) for w in to_cast],
        ],
        scratch_shapes=[pltpu.VMEM((hs, HEAD_DIM, HEAD_DIM), F32),
                        pltpu.VMEM((3 * hs, rows + HALO_CONV, HEAD_DIM), F32)],
        compiler_params=_cparams(3),
        name="gdn_prompt",
    )(proj, proj, proj, proj, proj, proj, proj, gates, conv_w, conv_w, conv_w, gnw, *to_cast)


def _gdn_sample_kernel(x_ref, cs_ref, gates_ref, cw_ref, gnw_ref, s_in_ref,
                       o_ref, s_out_ref, o_scr, *, n_heads, tb):
    d = n_heads * HEAD_DIM
    x = x_ref[...]
    cw = cw_ref[...]
    conv = cw[CONV_W - 1:CONV_W, :] * x[:, :3 * d]
    for j in range(CONV_W - 1):
        conv = conv + cw[j:j + 1, :] * cs_ref[:, j * 3 * d:(j + 1) * 3 * d]
    qkv = _silu(conv)

    gates = gates_ref[...]
    eg_all = jnp.exp(gates)
    pad = jnp.zeros((HEAD_DIM - tb, HEAD_DIM), F32)

    for h in range(n_heads):
        qh = _l2norm(qkv[:, h * HEAD_DIM:(h + 1) * HEAD_DIM]) * (HEAD_DIM ** -0.5)
        kh = _l2norm(qkv[:, d + h * HEAD_DIM:d + (h + 1) * HEAD_DIM])
        vh = qkv[:, 2 * d + h * HEAD_DIM:2 * d + (h + 1) * HEAD_DIM]
        qt = jnp.concatenate([qh, pad], axis=0).T
        kt = jnp.concatenate([kh, pad], axis=0).T
        toks = range(tb)
        sq = (HEAD_DIM, HEAD_DIM)
        kb = [jnp.broadcast_to(kt[:, t:t + 1], sq) for t in toks]
        qb = [jnp.broadcast_to(qt[:, t:t + 1], sq) for t in toks]
        s1 = [s_in_ref[t, h] * eg_all[t:t + 1, h:h + 1] for t in toks]
        ks = [jnp.sum(s1[t] * kb[t], axis=0, keepdims=True) for t in toks]
        delta = [(vh[t:t + 1, :] - ks[t]) * gates[t:t + 1, n_heads + h:n_heads + h + 1] for t in toks]
        s2 = [s1[t] + kb[t] * delta[t] for t in toks]
        for t in toks:
            s_out_ref[t, h] = s2[t]
            o_scr[t:t + 1, h * HEAD_DIM:(h + 1) * HEAD_DIM] = jnp.sum(s2[t] * qb[t], axis=0, keepdims=True)

    gnw = gnw_ref[...]
    for h in range(n_heads):
        sl = slice(h * HEAD_DIM, (h + 1) * HEAD_DIM)
        o = _rms(o_scr[:, sl], gnw) * _silu(x[:, 3 * d + h * HEAD_DIM:3 * d + (h + 1) * HEAD_DIM])
        o_ref[:, sl] = o.astype(o_ref.dtype)


def _gdn_sample(proj, conv_state, gates, conv_w, gnw, state, n_heads, tb):
    m = proj.shape[0]
    d = n_heads * HEAD_DIM
    kern = functools.partial(_gdn_sample_kernel, n_heads=n_heads, tb=tb)
    return pl.pallas_call(
        kern,
        grid=(m // tb,),
        in_specs=[
            pl.BlockSpec((tb, 4 * d), lambda i: (i, 0)),
            pl.BlockSpec((tb, (CONV_W - 1) * 3 * d), lambda i: (i, 0)),
            pl.BlockSpec((tb, HEAD_DIM), lambda i: (i, 0)),
            pl.BlockSpec((CONV_W, 3 * d), lambda i: (0, 0)),
            pl.BlockSpec((1, HEAD_DIM), lambda i: (0, 0)),
            pl.BlockSpec((tb, n_heads, HEAD_DIM, HEAD_DIM), lambda i: (i, 0, 0, 0)),
        ],
        out_specs=[
            pl.BlockSpec((tb, d), lambda i: (i, 0)),
            pl.BlockSpec((tb, n_heads, HEAD_DIM, HEAD_DIM), lambda i: (i, 0, 0, 0)),
        ],
        out_shape=[
            jax.ShapeDtypeStruct((m, d), BF16),
            jax.ShapeDtypeStruct(state.shape, F32),
        ],
        scratch_shapes=[pltpu.VMEM((tb, d), F32)],
        compiler_params=_cparams(1),
        name="gdn_sample",
    )(proj, conv_state, gates, conv_w, gnw, state)


def _mix_out_kernel(oa_ref, u_ref, hist_ref, x_ref, wp_ref, ps_ref, wo_ref, nw_ref, out_ref, *scratch,
                    tm, seq, pos0):
    d_pool = u_ref.shape[1]
    gd = d_pool // len(POOL_WINDOWS)
    u = u_ref[...]
    if seq > 1:
        (ext_ref,) = scratch
        start = (pl.program_id(0) * tm) % seq
        ext_ref[0:HALO_POOL, :] = jnp.where(start > 0, hist_ref[...], 0.0)
        ext_ref[HALO_POOL:, :] = u
        pos = pos0 + start + lax.broadcasted_iota(jnp.int32, (tm, gd), 0)
    else:
        pos = jnp.full((tm, gd), pos0, jnp.int32)

    d_gdn = oa_ref.shape[1]
    mix = jnp.dot(oa_ref[...], wo_ref[:d_gdn, :], preferred_element_type=F32)
    parts = []
    for gi, win in enumerate(POOL_WINDOWS):
        sl = slice(gi * gd, (gi + 1) * gd)
        cur = u[:, sl]
        wsum = cur
        for r in range(1, win):
            if seq > 1:
                wsum = wsum + ext_ref[pl.ds(HALO_POOL - r, tm), sl]
            else:
                off = (POOL_BUF - r) * d_pool
                wsum = wsum + hist_ref[:, off + gi * gd:off + (gi + 1) * gd]
        cnt = jnp.minimum(pos + 1, win).astype(F32)
        pooled = wsum / cnt - cur
        ob = _dot(pooled, wp_ref[gi]) * ps_ref[:, sl]
        parts.append(ob.astype(BF16))
    mix = mix + jnp.dot(jnp.concatenate(parts, axis=1), wo_ref[d_gdn:, :], preferred_element_type=F32)
    out_ref[...] = x_ref[...] + _rms(mix, nw_ref[...])


def _mix_out(o_a, u, hist, x, w_pool, pool_scale, w_out, norm_w, tm, seq, pos0):
    m, d = x.shape
    d_pool = pool_scale.shape[1]
    d_gdn = o_a.shape[1]
    if seq > 1:
        hb = tm // HALO_POOL
        hist_spec = pl.BlockSpec((HALO_POOL, d_pool), lambda i: (jnp.maximum(i * hb - 1, 0), 0))
        scratch = [pltpu.VMEM((tm + HALO_POOL, d_pool), F32)]
    else:
        hist_spec = pl.BlockSpec((tm, POOL_BUF * d_pool), lambda i: (i, 0))
        scratch = []
    kern = functools.partial(_mix_out_kernel, tm=tm, seq=seq, pos0=pos0)
    return pl.pallas_call(
        kern,
        grid=(m // tm,),
        in_specs=[
            pl.BlockSpec((tm, d_gdn), lambda i: (i, 0)),
            pl.BlockSpec((tm, d_pool), lambda i: (i, 0)),
            hist_spec,
            pl.BlockSpec((tm, d), lambda i: (i, 0)),
            pl.BlockSpec(w_pool.shape, lambda i: (0, 0, 0), pipeline_mode=pl.Buffered(1)),
            pl.BlockSpec((1, d_pool), lambda i: (0, 0)),
            pl.BlockSpec(w_out.shape, lambda i: (0, 0), pipeline_mode=pl.Buffered(1)),
            pl.BlockSpec((1, d), lambda i: (0, 0)),
        ],
        out_specs=pl.BlockSpec((tm, d), lambda i: (i, 0)),
        out_shape=jax.ShapeDtypeStruct((m, d), F32),
        scratch_shapes=scratch,
        compiler_params=_cparams(1),
        name="mix_out",
    )(o_a, u, hist, x, w_pool, pool_scale, w_out, norm_w)


def _mlp_kernel(x_ref, nw1_ref, wu_ref, wd_ref, nw2_ref, out_ref, h_ref):
    j = pl.program_id(1)

    @pl.when(j == 0)
    def _():
        h_ref[...] = _rms(x_ref[...], nw1_ref[...]).astype(BF16)
        out_ref[...] = jnp.zeros_like(out_ref)

    up = jnp.dot(h_ref[...], wu_ref[...], preferred_element_type=F32)
    act = jnp.square(jnp.maximum(up, 0.0)).astype(BF16)
    out_ref[...] += jnp.dot(act, wd_ref[...], preferred_element_type=F32)

    @pl.when(j == pl.num_programs(1) - 1)
    def _():
        out_ref[...] = x_ref[...] + _rms(out_ref[...], nw2_ref[...])


def _mlp(x, nw1, w_up, w_down, nw2, tm, tf):
    m, d = x.shape
    f = w_up.shape[1]
    return pl.pallas_call(
        _mlp_kernel,
        grid=(m // tm, f // tf),
        in_specs=[
            pl.BlockSpec((tm, d), lambda i, j: (i, 0)),
            pl.BlockSpec((1, d), lambda i, j: (0, 0)),
            pl.BlockSpec((d, tf), lambda i, j: (0, j)),
            pl.BlockSpec((tf, d), lambda i, j: (j, 0)),
            pl.BlockSpec((1, d), lambda i, j: (0, 0)),
        ],
        out_specs=pl.BlockSpec((tm, d), lambda i, j: (i, 0)),
        out_shape=jax.ShapeDtypeStruct((m, d), F32),
        scratch_shapes=[pltpu.VMEM((tm, d), BF16)],
        compiler_params=_cparams(2),
        name="mlp",
    )(x, nw1, w_up, w_down, nw2)


def _pad_lanes(v, n=HEAD_DIM):
    return jnp.pad(v.reshape(1, -1), ((0, 0), (0, n - v.shape[-1])))


def kernel(x_prompt, x_sample, state_gdn, state_conv, state_pool, norm_pre_mix, w_in, conv_w, a_log, dt_bias,
           norm_gdn_out, w_pool, pool_scale, w_out, norm_post_mix, norm_pre_mlp, w_up, w_down, norm_post_mlp):
    batch, seq, d_model = x_prompt.shape
    dec_batch, dec_seq, _ = x_sample.shape
    assert dec_seq == 1
    depth = w_in.shape[0]
    n_heads = a_log.shape[1]
    d_gdn = n_heads * HEAD_DIM
    d_qkv = 3 * d_gdn
    d_pool = pool_scale.shape[1]
    o_gate = d_qkv + d_gdn
    o_pool = o_gate + 2 * n_heads

    yp = x_prompt.reshape(batch * seq, d_model)
    ys = x_sample.reshape(dec_batch, d_model)
    outs = [[] for _ in range(6)]
    for l in range(depth):
        w_a = w_in[l].astype(BF16)
        w_b = jnp.concatenate(
            [w_a[:, o_pool:], w_a[:, o_gate:o_pool],
             jnp.zeros((d_model, HEAD_DIM - 2 * n_heads), BF16)], axis=1)
        alog = _pad_lanes(a_log[l])
        dtb = _pad_lanes(dt_bias[l])
        gnw = norm_gdn_out[l].reshape(1, HEAD_DIM)
        npm = norm_pre_mix[l].reshape(1, d_model)
        npo = norm_post_mix[l].reshape(1, d_model)
        nm1 = norm_pre_mlp[l].reshape(1, d_model)
        nm2 = norm_post_mlp[l].reshape(1, d_model)
        ps = pool_scale[l].reshape(1, d_pool)

        proj_p, u_p, gates_p = _in_proj(yp, npm, w_a, o_gate, w_b, alog, dtb, n_heads, tm=1024, tn=512)
        to_cast = (w_up[l], w_down[l], w_out[l], w_pool[l].reshape(-1, w_pool.shape[-1]))
        oa_p, s_p, w_up_b, w_down_b, w_out_b, w_pool_b = _gdn_prompt(
            proj_p, gates_p, conv_w[l], gnw, to_cast, batch, seq, n_heads, hs=8, rows=256)
        w_pool_b = w_pool_b.reshape(w_pool.shape[1:])
        x1_p = _mix_out(oa_p, u_p, u_p, yp, w_pool_b, ps, w_out_b, npo, tm=512, seq=seq, pos0=0)
        yp = _mlp(x1_p, nm1, w_up_b, w_down_b, nm2, tm=512, tf=1024)
        outs[0].append(s_p)
        outs[1].append(proj_p.reshape(batch, seq, -1)[:, seq - (CONV_W - 1):, :d_qkv])
        outs[2].append(u_p.reshape(batch, seq, -1)[:, seq - POOL_BUF:])

        proj_s, u_s, gates_s = _in_proj(ys, npm, w_a, o_gate, w_b, alog, dtb, n_heads, tm=dec_batch, tn=512)
        cs = state_conv[l].reshape(dec_batch, (CONV_W - 1) * d_qkv)
        hist = state_pool[l].reshape(dec_batch, POOL_BUF * d_pool)
        oa_s, s_s = _gdn_sample(proj_s, cs, gates_s, conv_w[l], gnw, state_gdn[l], n_heads, tb=8)
        x1_s = _mix_out(oa_s, u_s, hist, ys, w_pool_b, ps, w_out_b, npo, tm=dec_batch, seq=1, pos0=PAST_LEN)
        ys = _mlp(x1_s, nm1, w_up_b, w_down_b, nm2, tm=dec_batch, tf=1024)
        outs[3].append(s_s)
        outs[4].append(jnp.concatenate([state_conv[l][:, 1:], proj_s[:, None, :d_qkv]], axis=1))
        outs[5].append(jnp.concatenate([state_pool[l][:, 1:], u_s[:, None]], axis=1))

    return (yp.reshape(batch, seq, d_model), ys.reshape(dec_batch, dec_seq, d_model),
            *[jnp.stack(o) for o in outs])
```

```python
import functools

import jax
import jax.numpy as jnp
from jax import lax
from jax.experimental import pallas as pl
from jax.experimental.pallas import tpu as pltpu

F32 = jnp.float32
BF16 = jnp.bfloat16

EPS = 1e-6
HEAD_DIM = 128
CONV_W = 4
POOL_WINDOWS = (2, 4, 8, 16)
POOL_BUF = max(POOL_WINDOWS) - 1
PAST_LEN = 16384
HALO_CONV = 8
HALO_POOL = 16
GDN_CHUNK = 128
VMEM_LIMIT = 56 * 1024 * 1024


def _cparams(n_axes):
    return pltpu.CompilerParams(
        dimension_semantics=("arbitrary",) * n_axes, vmem_limit_bytes=VMEM_LIMIT)


def _dot(a, b):
    return jnp.dot(a.astype(BF16), b.astype(BF16), preferred_element_type=F32)


def _dot_nt(a, b):
    return lax.dot_general(a.astype(BF16), b.astype(BF16), (((1,), (1,)), ((), ())),
                           preferred_element_type=F32)


def _rms(x, w):
    return x * lax.rsqrt(jnp.mean(x * x, axis=-1, keepdims=True) + EPS) * w


def _silu(x):
    return x * (0.5 * jnp.tanh(0.5 * x) + 0.5)


def _softplus(x):
    return jnp.maximum(x, 0.0) + jnp.log(1.0 + jnp.exp(-jnp.abs(x)))


def _l2norm(x):
    return x * lax.rsqrt(jnp.sum(x * x, axis=-1, keepdims=True) + EPS)


def _inproj_kernel(x_ref, nw_ref, wa_ref, wb_ref, alog_ref, dtb_ref, out_ref, u_ref, gates_ref, h_ref, *,
                   n_heads, n_main):
    j = pl.program_id(1)

    @pl.when(j == 0)
    def _():
        h_ref[...] = _rms(x_ref[...], nw_ref[...]).astype(BF16)

    @pl.when(j < n_main)
    def _():
        out_ref[...] = jnp.dot(h_ref[...], wa_ref[...], preferred_element_type=F32)

    @pl.when(j == n_main)
    def _():
        d_pool = u_ref.shape[1]
        u_ref[...] = jnp.dot(h_ref[...], wb_ref[:, :d_pool], preferred_element_type=F32)
        ab = jnp.dot(h_ref[...], wb_ref[:, d_pool:], preferred_element_type=F32)
        lane = lax.broadcasted_iota(jnp.int32, ab.shape, 1)
        g = -jnp.exp(alog_ref[...]) * _softplus(ab + dtb_ref[...])
        gates_ref[...] = jnp.where(lane < n_heads, g, jax.nn.sigmoid(ab))


def _in_proj(x, norm_w, w_a, n, w_b, alog, dtb, n_heads, tm, tn):
    m, d = x.shape
    n_main = n // tn
    d_pool = w_b.shape[1] - HEAD_DIM
    vec = pl.BlockSpec((1, HEAD_DIM), lambda i, j: (0, 0))
    return pl.pallas_call(
        functools.partial(_inproj_kernel, n_heads=n_heads, n_main=n_main),
        grid=(m // tm, n_main + 1),
        in_specs=[
            pl.BlockSpec((tm, d), lambda i, j: (i, 0)),
            pl.BlockSpec((1, d), lambda i, j: (0, 0)),
            pl.BlockSpec((d, tn), lambda i, j: (0, jnp.minimum(j, n_main - 1))),
            pl.BlockSpec(w_b.shape, lambda i, j: (0, 0), pipeline_mode=pl.Buffered(1)),
            vec, vec,
        ],
        out_specs=[
            pl.BlockSpec((tm, tn), lambda i, j: (i, jnp.minimum(j, n_main - 1))),
            pl.BlockSpec((tm, d_pool), lambda i, j: (i, 0)),
            pl.BlockSpec((tm, HEAD_DIM), lambda i, j: (i, 0)),
        ],
        out_shape=[jax.ShapeDtypeStruct((m, n), F32), jax.ShapeDtypeStruct((m, d_pool), F32),
                   jax.ShapeDtypeStruct((m, HEAD_DIM), F32)],
        scratch_shapes=[pltpu.VMEM((tm, d), BF16)],
        compiler_params=_cparams(2),
        name="in_proj",
    )(x, norm_w, w_a, w_b, alog, dtb)


def _lane_column(x, lane_idx):
    lane = lax.broadcasted_iota(jnp.int32, x.shape, 1)
    col = jnp.sum(jnp.where(lane == lane_idx, x, 0.0), axis=1, keepdims=True)
    return jnp.broadcast_to(col, x.shape)


def _unit_lower_inverses(lms):
    n = lms[0].shape[0]
    row = lax.broadcasted_iota(jnp.int32, (n, n), 0)
    col = lax.broadcasted_iota(jnp.int32, (n, n), 1)
    eye = jnp.where(row == col, 1.0, 0.0)
    nks = [-lm for lm in lms]
    qs = [eye + nk for nk in nks]
    nks = [_dot(nk, nk) for nk in nks]
    power = 2
    while 2 * power < n:
        res = [_dot(nk, jnp.concatenate([nk, q], axis=1)) for nk, q in zip(nks, qs)]
        nks = [r[:, :n] for r in res]
        qs = [q + r[:, n:] for q, r in zip(qs, res)]
        power *= 2
    return [q + _dot(nk, q) for nk, q in zip(nks, qs)]


def _gdn_prompt_kernel(q_ref, k_ref, v_ref, gate_ref, hq_ref, hk_ref, hv_ref, gates_ref,
                       cwq_ref, cwk_ref, cwv_ref, gnw_ref, *rest, n_heads, hs, rows, n_cast):
    cast_in = rest[:n_cast]
    o_ref, sout_ref = rest[n_cast:n_cast + 2]
    cast_out = rest[n_cast + 2:2 * n_cast + 2]
    s_ref, ext_ref = rest[2 * n_cast + 2:]
    hg = pl.program_id(1)
    blk = pl.program_id(2)
    c = GDN_CHUNK

    for src, dst in zip(cast_in, cast_out):
        dst[...] = src[...].astype(dst.dtype)

    @pl.when(blk == 0)
    def _():
        s_ref[...] = jnp.zeros_like(s_ref)

    for a, (x_ref, halo_ref) in enumerate(((q_ref, hq_ref), (k_ref, hk_ref), (v_ref, hv_ref))):
        for hi in range(hs):
            lanes = slice(hi * HEAD_DIM, (hi + 1) * HEAD_DIM)
            ext_ref[a * hs + hi, 0:HALO_CONV, :] = jnp.where(blk > 0, halo_ref[:, lanes], 0.0)
            ext_ref[a * hs + hi, HALO_CONV:, :] = x_ref[:, lanes]

    def conv_silu(a, cw_ref, r0, hi):
        lanes = slice(hi * HEAD_DIM, (hi + 1) * HEAD_DIM)
        acc = None
        for j in range(CONV_W):
            term = cw_ref[j:j + 1, lanes] * ext_ref[a * hs + hi, pl.ds(HALO_CONV - (CONV_W - 1) + j + r0, c), :]
            acc = term if acc is None else acc + term
        return _silu(acc)

    row = lax.broadcasted_iota(jnp.int32, (c, c), 0)
    col = lax.broadcasted_iota(jnp.int32, (c, c), 1)
    tril = row >= col
    strict = row > col
    tril_f = jnp.where(tril, 1.0, 0.0)
    gnw = gnw_ref[...]
    heads = range(hs)

    s = [s_ref[hi] for hi in heads]
    for ci in range(rows // c):
        r0 = ci * c
        gates = gates_ref[r0:r0 + c, :]
        gcum = jnp.dot(tril_f, gates, preferred_element_type=F32, precision=lax.Precision.HIGHEST)
        qc = [_l2norm(conv_silu(0, cwq_ref, r0, hi)) * (HEAD_DIM ** -0.5) for hi in heads]
        kc = [_l2norm(conv_silu(1, cwk_ref, r0, hi)) for hi in heads]
        vc = [conv_silu(2, cwv_ref, r0, hi) for hi in heads]
        gc = [_lane_column(gcum, hg * hs + hi) for hi in heads]
        bc = [_lane_column(gates, hg * hs + hi + n_heads) for hi in heads]
        decay = [jnp.exp(jnp.where(tril, g - g.T, -1e30)) for g in gc]
        egc = [jnp.exp(g) for g in gc]
        g_last = [g[c - 1:c, :] for g in gc]
        kk = [_dot_nt(k, k) for k in kc]
        qk = [_dot_nt(q, k) for q, k in zip(qc, kc)]
        lm = [jnp.where(strict, kk[hi] * bc[hi] * decay[hi], 0.0) for hi in heads]
        t_inv = _unit_lower_inverses(lm)
        uw = [_dot(t_inv[hi], jnp.concatenate([vc[hi] * bc[hi], kc[hi] * (bc[hi] * egc[hi])], axis=1))
              for hi in heads]
        intra = [jnp.where(tril, qk[hi] * decay[hi], 0.0) for hi in heads]
        kdt = [(kc[hi] * jnp.exp(g_last[hi] - gc[hi])).T for hi in heads]
        r1 = [_dot(jnp.concatenate([uw[hi][:, HEAD_DIM:], qc[hi] * egc[hi]], axis=0), s[hi]) for hi in heads]
        v_new = [uw[hi][:, :HEAD_DIM] - r1[hi][:c] for hi in heads]
        r2 = [_dot(jnp.concatenate([intra[hi], kdt[hi]], axis=0), v_new[hi]) for hi in heads]
        s = [s[hi] * jnp.exp(g_last[hi]) + r2[hi][c:] for hi in heads]
        for hi in heads:
            lanes = slice(hi * HEAD_DIM, (hi + 1) * HEAD_DIM)
            o = _rms(r1[hi][c:] + r2[hi][:c], gnw) * _silu(gate_ref[r0:r0 + c, lanes])
            o_ref[r0:r0 + c, lanes] = o.astype(o_ref.dtype)

    for hi in range(hs):
        s_ref[hi] = s[hi]

    @pl.when(blk == pl.num_programs(2) - 1)
    def _():
        for hi in range(hs):
            sout_ref[0, hi] = s[hi]


def _gdn_prompt(proj, gates, conv_w, gnw, to_cast, batch, seq, n_heads, hs, rows):
    nblk = seq // rows
    hb = rows // HALO_CONV
    ng = n_heads // hs
    width = hs * HEAD_DIM
    steps = batch * ng * nblk

    def slab(w):
        return pl.BlockSpec((w.shape[0] // steps, w.shape[1]), lambda b, h, k: ((b * ng + h) * nblk + k, 0))

    def main(off):
        return pl.BlockSpec((rows, width), lambda b, h, k: (b * nblk + k, off * ng + h))

    def halo(off):
        return pl.BlockSpec(
            (HALO_CONV, width), lambda b, h, k: (jnp.maximum((b * nblk + k) * hb - 1, 0), off * ng + h))

    def cw(off):
        return pl.BlockSpec((CONV_W, width), lambda b, h, k: (0, off * ng + h))

    kern = functools.partial(_gdn_prompt_kernel, n_heads=n_heads, hs=hs, rows=rows, n_cast=len(to_cast))
    return pl.pallas_call(
        kern,
        grid=(batch, ng, nblk),
        in_specs=[main(0), main(1), main(2), main(3), halo(0), halo(1), halo(2),
                  pl.BlockSpec((rows, HEAD_DIM), lambda b, h, k: (b * nblk + k, 0)),
                  cw(0), cw(1), cw(2),
                  pl.BlockSpec((1, HEAD_DIM), lambda b, h, k: (0, 0)),
                  *[slab(w) for w in to_cast]],
        out_specs=[
            pl.BlockSpec((rows, width), lambda b, h, k: (b * nblk + k, h)),
            pl.BlockSpec((1, hs, HEAD_DIM, HEAD_DIM), lambda b, h, k: (b, h, 0, 0)),
            *[slab(w) for w in to_cast],
        ],
        out_shape=[
            jax.ShapeDtypeStruct((batch * seq, n_heads * HEAD_DIM), BF16),
            jax.ShapeDtypeStruct((batch, n_heads, HEAD_DIM, HEAD_DIM), F32),
            *[jax.ShapeDtypeStruct(w.shape, BF16) for w in to_cast],
        ],
        scratch_shapes=[pltpu.VMEM((hs, HEAD_DIM, HEAD_DIM), F32),
                        pltpu.VMEM((3 * hs, rows + HALO_CONV, HEAD_DIM), F32)],
        compiler_params=_cparams(3),
        name="gdn_prompt",
    )(proj, proj, proj, proj, proj, proj, proj, gates, conv_w, conv_w, conv_w, gnw, *to_cast)


def _gdn_sample_kernel(x_ref, cs_ref, gates_ref, cw_ref, gnw_ref, s_in_ref,
                       o_ref, s_out_ref, o_scr, *, n_heads, tb):
    d = n_heads * HEAD_DIM
    x = x_ref[...]
    cw = cw_ref[...]
    conv = cw[CONV_W - 1:CONV_W, :] * x[:, :3 * d]
    for j in range(CONV_W - 1):
        conv = conv + cw[j:j + 1, :] * cs_ref[:, j * 3 * d:(j + 1) * 3 * d]
    qkv = _silu(conv)

    gates = gates_ref[...]
    eg_all = jnp.exp(gates)
    pad = jnp.zeros((HEAD_DIM - tb, HEAD_DIM), F32)

    for h in range(n_heads):
        qh = _l2norm(qkv[:, h * HEAD_DIM:(h + 1) * HEAD_DIM]) * (HEAD_DIM ** -0.5)
        kh = _l2norm(qkv[:, d + h * HEAD_DIM:d + (h + 1) * HEAD_DIM])
        vh = qkv[:, 2 * d + h * HEAD_DIM:2 * d + (h + 1) * HEAD_DIM]
        kt = jnp.concatenate([kh, pad], axis=0).T
        toks = range(tb)
        kb = [jnp.broadcast_to(kt[:, t:t + 1], (HEAD_DIM, HEAD_DIM)) for t in toks]
        s1 = [s_in_ref[t, h] * eg_all[t:t + 1, h:h + 1] for t in toks]
        ks = [_dot(kh, s1[t])[t:t + 1, :] for t in toks]
        delta = [(vh[t:t + 1, :] - ks[t]) * gates[t:t + 1, n_heads + h:n_heads + h + 1] for t in toks]
        s2 = [s1[t] + kb[t] * delta[t] for t in toks]
        for t in toks:
            s_out_ref[t, h] = s2[t]
            o_scr[t:t + 1, h * HEAD_DIM:(h + 1) * HEAD_DIM] = _dot(qh, s2[t])[t:t + 1, :]

    gnw = gnw_ref[...]
    for h in range(n_heads):
        sl = slice(h * HEAD_DIM, (h + 1) * HEAD_DIM)
        o = _rms(o_scr[:, sl], gnw) * _silu(x[:, 3 * d + h * HEAD_DIM:3 * d + (h + 1) * HEAD_DIM])
        o_ref[:, sl] = o.astype(o_ref.dtype)


def _gdn_sample(proj, conv_state, gates, conv_w, gnw, state, n_heads, tb):
    m = proj.shape[0]
    d = n_heads * HEAD_DIM
    kern = functools.partial(_gdn_sample_kernel, n_heads=n_heads, tb=tb)
    return pl.pallas_call(
        kern,
        grid=(m // tb,),
        in_specs=[
            pl.BlockSpec((tb, 4 * d), lambda i: (i, 0)),
            pl.BlockSpec((tb, (CONV_W - 1) * 3 * d), lambda i: (i, 0)),
            pl.BlockSpec((tb, HEAD_DIM), lambda i: (i, 0)),
            pl.BlockSpec((CONV_W, 3 * d), lambda i: (0, 0)),
            pl.BlockSpec((1, HEAD_DIM), lambda i: (0, 0)),
            pl.BlockSpec((tb, n_heads, HEAD_DIM, HEAD_DIM), lambda i: (i, 0, 0, 0)),
        ],
        out_specs=[
            pl.BlockSpec((tb, d), lambda i: (i, 0)),
            pl.BlockSpec((tb, n_heads, HEAD_DIM, HEAD_DIM), lambda i: (i, 0, 0, 0)),
        ],
        out_shape=[
            jax.ShapeDtypeStruct((m, d), BF16),
            jax.ShapeDtypeStruct(state.shape, F32),
        ],
        scratch_shapes=[pltpu.VMEM((tb, d), F32)],
        compiler_params=_cparams(1),
        name="gdn_sample",
    )(proj, conv_state, gates, conv_w, gnw, state)


def _mix_out_kernel(oa_ref, u_ref, hist_ref, x_ref, wp_ref, ps_ref, wo_ref, nw_ref, out_ref, *scratch,
                    tm, seq, pos0):
    d_pool = u_ref.shape[1]
    d_gdn = oa_ref.shape[1]
    d = out_ref.shape[1]
    n_groups = len(POOL_WINDOWS)
    gd = d_pool // n_groups
    strips = gd // HEAD_DIM
    if seq > 1:
        (ext_ref,) = scratch
        start = (pl.program_id(0) * tm) % seq
        for si in range(d_pool // HEAD_DIM):
            lanes = slice(si * HEAD_DIM, (si + 1) * HEAD_DIM)
            ext_ref[si, 0:HALO_POOL, :] = jnp.where(start > 0, hist_ref[:, lanes], 0.0)
            ext_ref[si, HALO_POOL:, :] = u_ref[:, lanes]
        pos = pos0 + start + lax.broadcasted_iota(jnp.int32, (tm, HEAD_DIM), 0)
    else:
        pos = jnp.full((tm, HEAD_DIM), pos0, jnp.int32)

    tn = d // n_groups
    oa = oa_ref[...]
    acc = []
    parts = []
    for gi, win in enumerate(POOL_WINDOWS):
        acc.append(jnp.dot(oa, wo_ref[:d_gdn, gi * tn:(gi + 1) * tn], preferred_element_type=F32))
        cnt = jnp.minimum(pos + 1, win).astype(F32)
        pooled = []
        for si in range(gi * strips, (gi + 1) * strips):
            lanes = slice(si * HEAD_DIM, (si + 1) * HEAD_DIM)
            cur = u_ref[:, lanes]
            wsum = cur
            for r in range(1, win):
                if seq > 1:
                    wsum = wsum + ext_ref[si, pl.ds(HALO_POOL - r, tm), :]
                else:
                    off = (POOL_BUF - r) * d_pool + si * HEAD_DIM
                    wsum = wsum + hist_ref[:, off:off + HEAD_DIM]
            pooled.append(wsum / cnt - cur)
        ob = _dot(jnp.concatenate(pooled, axis=1), wp_ref[gi]) * ps_ref[:, gi * gd:(gi + 1) * gd]
        parts.append(ob.astype(BF16))
    ob_all = jnp.concatenate(parts, axis=1)
    mix = [acc[ci] + jnp.dot(ob_all, wo_ref[d_gdn:, ci * tn:(ci + 1) * tn], preferred_element_type=F32)
           for ci in range(n_groups)]
    ss = sum(jnp.sum(m * m, axis=-1, keepdims=True) for m in mix)
    inv = lax.rsqrt(ss * (1.0 / d) + EPS)
    for ci in range(n_groups):
        cols = slice(ci * tn, (ci + 1) * tn)
        out_ref[:, cols] = x_ref[:, cols] + mix[ci] * inv * nw_ref[:, cols]


def _mix_out(o_a, u, hist, x, w_pool, pool_scale, w_out, norm_w, tm, seq, pos0):
    m, d = x.shape
    d_pool = pool_scale.shape[1]
    d_gdn = o_a.shape[1]
    if seq > 1:
        hb = tm // HALO_POOL
        hist_spec = pl.BlockSpec((HALO_POOL, d_pool), lambda i: (jnp.maximum(i * hb - 1, 0), 0))
        scratch = [pltpu.VMEM((d_pool // HEAD_DIM, tm + HALO_POOL, HEAD_DIM), F32)]
    else:
        hist_spec = pl.BlockSpec((tm, POOL_BUF * d_pool), lambda i: (i, 0))
        scratch = []
    kern = functools.partial(_mix_out_kernel, tm=tm, seq=seq, pos0=pos0)
    return pl.pallas_call(
        kern,
        grid=(m // tm,),
        in_specs=[
            pl.BlockSpec((tm, d_gdn), lambda i: (i, 0)),
            pl.BlockSpec((tm, d_pool), lambda i: (i, 0)),
            hist_spec,
            pl.BlockSpec((tm, d), lambda i: (i, 0)),
            pl.BlockSpec(w_pool.shape, lambda i: (0, 0, 0), pipeline_mode=pl.Buffered(1)),
            pl.BlockSpec((1, d_pool), lambda i: (0, 0)),
            pl.BlockSpec(w_out.shape, lambda i: (0, 0), pipeline_mode=pl.Buffered(1)),
            pl.BlockSpec((1, d), lambda i: (0, 0)),
        ],
        out_specs=pl.BlockSpec((tm, d), lambda i: (i, 0)),
        out_shape=jax.ShapeDtypeStruct((m, d), F32),
        scratch_shapes=scratch,
        compiler_params=_cparams(1),
        name="mix_out",
    )(o_a, u, hist, x, w_pool, pool_scale, w_out, norm_w)


def _mlp_kernel(x_ref, nw1_ref, wu_ref, wd_ref, nw2_ref, out_ref, h_ref):
    j = pl.program_id(1)

    @pl.when(j == 0)
    def _():
        h_ref[...] = _rms(x_ref[...], nw1_ref[...]).astype(BF16)
        out_ref[...] = jnp.zeros_like(out_ref)

    up = jnp.dot(h_ref[...], wu_ref[...], preferred_element_type=F32)
    act = jnp.square(jnp.maximum(up, 0.0)).astype(BF16)
    out_ref[...] += jnp.dot(act, wd_ref[...], preferred_element_type=F32)

    @pl.when(j == pl.num_programs(1) - 1)
    def _():
        out_ref[...] = x_ref[...] + _rms(out_ref[...], nw2_ref[...])


def _mlp(x, nw1, w_up, w_down, nw2, tm, tf):
    m, d = x.shape
    f = w_up.shape[1]
    return pl.pallas_call(
        _mlp_kernel,
        grid=(m // tm, f // tf),
        in_specs=[
            pl.BlockSpec((tm, d), lambda i, j: (i, 0)),
            pl.BlockSpec((1, d), lambda i, j: (0, 0)),
            pl.BlockSpec((d, tf), lambda i, j: (0, j)),
            pl.BlockSpec((tf, d), lambda i, j: (j, 0)),
            pl.BlockSpec((1, d), lambda i, j: (0, 0)),
        ],
        out_specs=pl.BlockSpec((tm, d), lambda i, j: (i, 0)),
        out_shape=jax.ShapeDtypeStruct((m, d), F32),
        scratch_shapes=[pltpu.VMEM((tm, d), BF16)],
        compiler_params=_cparams(2),
        name="mlp",
    )(x, nw1, w_up, w_down, nw2)


def _pad_lanes(v, n=HEAD_DIM):
    return jnp.pad(v.reshape(1, -1), ((0, 0), (0, n - v.shape[-1])))


def kernel(x_prompt, x_sample, state_gdn, state_conv, state_pool, norm_pre_mix, w_in, conv_w, a_log, dt_bias,
           norm_gdn_out, w_pool, pool_scale, w_out, norm_post_mix, norm_pre_mlp, w_up, w_down, norm_post_mlp):
    batch, seq, d_model = x_prompt.shape
    dec_batch, dec_seq, _ = x_sample.shape
    assert dec_seq == 1
    depth = w_in.shape[0]
    n_heads = a_log.shape[1]
    d_gdn = n_heads * HEAD_DIM
    d_qkv = 3 * d_gdn
    d_pool = pool_scale.shape[1]
    o_gate = d_qkv + d_gdn
    o_pool = o_gate + 2 * n_heads

    yp = x_prompt.reshape(batch * seq, d_model)
    ys = x_sample.reshape(dec_batch, d_model)
    outs = [[] for _ in range(6)]
    for l in range(depth):
        w_a = w_in[l].astype(BF16)
        w_b = jnp.concatenate(
            [w_a[:, o_pool:], w_a[:, o_gate:o_pool],
             jnp.zeros((d_model, HEAD_DIM - 2 * n_heads), BF16)], axis=1)
        alog = _pad_lanes(a_log[l])
        dtb = _pad_lanes(dt_bias[l])
        gnw = norm_gdn_out[l].reshape(1, HEAD_DIM)
        npm = norm_pre_mix[l].reshape(1, d_model)
        npo = norm_post_mix[l].reshape(1, d_model)
        nm1 = norm_pre_mlp[l].reshape(1, d_model)
        nm2 = norm_post_mlp[l].reshape(1, d_model)
        ps = pool_scale[l].reshape(1, d_pool)

        proj_p, u_p, gates_p = _in_proj(yp, npm, w_a, o_gate, w_b, alog, dtb, n_heads, tm=1024, tn=512)
        to_cast = (w_up[l], w_down[l], w_out[l], w_pool[l].reshape(-1, w_pool.shape[-1]))
        oa_p, s_p, w_up_b, w_down_b, w_out_b, w_pool_b = _gdn_prompt(
            proj_p, gates_p, conv_w[l], gnw, to_cast, batch, seq, n_heads, hs=8, rows=256)
        w_pool_b = w_pool_b.reshape(w_pool.shape[1:])
        x1_p = _mix_out(oa_p, u_p, u_p, yp, w_pool_b, ps, w_out_b, npo, tm=512, seq=seq, pos0=0)
        yp = _mlp(x1_p, nm1, w_up_b, w_down_b, nm2, tm=512, tf=1024)
        outs[0].append(s_p)
        outs[1].append(proj_p.reshape(batch, seq, -1)[:, seq - (CONV_W - 1):, :d_qkv])
        outs[2].append(u_p.reshape(batch, seq, -1)[:, seq - POOL_BUF:])

        proj_s, u_s, gates_s = _in_proj(ys, npm, w_a, o_gate, w_b, alog, dtb, n_heads, tm=dec_batch, tn=512)
        cs = state_conv[l].reshape(dec_batch, (CONV_W - 1) * d_qkv)
        hist = state_pool[l].reshape(dec_batch, POOL_BUF * d_pool)
        oa_s, s_s = _gdn_sample(proj_s, cs, gates_s, conv_w[l], gnw, state_gdn[l], n_heads, tb=8)
        x1_s = _mix_out(oa_s, u_s, hist, ys, w_pool_b, ps, w_out_b, npo, tm=dec_batch, seq=1, pos0=PAST_LEN)
        ys = _mlp(x1_s, nm1, w_up_b, w_down_b, nm2, tm=dec_batch, tf=1024)
        outs[3].append(s_s)
        outs[4].append(jnp.concatenate([state_conv[l][:, 1:], proj_s[:, None, :d_qkv]], axis=1))
        outs[5].append(jnp.concatenate([state_pool[l][:, 1:], u_s[:, None]], axis=1))

    return (yp.reshape(batch, seq, d_model), ys.reshape(dec_batch, dec_seq, d_model),
            *[jnp.stack(o) for o in outs])
```

```python
import functools

import jax
import jax.numpy as jnp
from jax import lax
from jax.experimental import pallas as pl
from jax.experimental.pallas import tpu as pltpu

F32 = jnp.float32
BF16 = jnp.bfloat16

EPS = 1e-6
HEAD_DIM = 128
CONV_W = 4
POOL_WINDOWS = (2, 4, 8, 16)
POOL_BUF = max(POOL_WINDOWS) - 1
PAST_LEN = 16384
HALO_CONV = 8
HALO_POOL = 16
GDN_CHUNK = 128
VMEM_LIMIT = 56 * 1024 * 1024


def _cparams(n_axes):
    return pltpu.CompilerParams(
        dimension_semantics=("arbitrary",) * n_axes, vmem_limit_bytes=VMEM_LIMIT)


def _dot(a, b):
    return jnp.dot(a.astype(BF16), b.astype(BF16), preferred_element_type=F32)


def _dot_nt(a, b):
    return lax.dot_general(a.astype(BF16), b.astype(BF16), (((1,), (1,)), ((), ())),
                           preferred_element_type=F32)


def _rms(x, w):
    return x * lax.rsqrt(jnp.mean(x * x, axis=-1, keepdims=True) + EPS) * w


def _silu(x):
    return x * (0.5 * jnp.tanh(0.5 * x) + 0.5)


def _softplus(x):
    return jnp.maximum(x, 0.0) + jnp.log(1.0 + jnp.exp(-jnp.abs(x)))


def _l2norm(x):
    return x * lax.rsqrt(jnp.sum(x * x, axis=-1, keepdims=True) + EPS)


def _inproj_kernel(x_ref, nw_ref, wa_ref, wt_ref, wc_ref, alog_ref, dtb_ref, *rest, n_heads, n_main, emit):
    if emit:
        out_ref, u_ref, gates_ref, wa_out, wt_out, h_ref = rest
    else:
        out_ref, u_ref, gates_ref, h_ref = rest
    j = pl.program_id(1)

    @pl.when(j == 0)
    def _():
        h_ref[...] = _rms(x_ref[...], nw_ref[...]).astype(BF16)

    @pl.when(j < n_main)
    def _():
        w = wa_ref[...].astype(BF16)
        if emit:
            wa_out[...] = w
        out_ref[...] = jnp.dot(h_ref[...], w, preferred_element_type=F32)

    @pl.when(j == n_main)
    def _():
        w = wt_ref[...].astype(BF16)
        if emit:
            wt_out[...] = w
        h = h_ref[...]
        tail = jnp.concatenate([jnp.dot(h, w, preferred_element_type=F32),
                                jnp.dot(h, wc_ref[...], preferred_element_type=F32)], axis=1)
        n_gate = 2 * n_heads
        u_ref[...] = tail[:, n_gate:n_gate + u_ref.shape[1]]
        ab = tail[:, :HEAD_DIM]
        lane = lax.broadcasted_iota(jnp.int32, ab.shape, 1)
        g = -jnp.exp(alog_ref[...]) * _softplus(ab + dtb_ref[...])
        gates_ref[...] = jnp.where(lane < n_heads, g, jax.nn.sigmoid(ab))


def _in_proj(x, norm_w, w_a, n, w_t, t_col, w_c, d_pool, alog, dtb, n_heads, tm, tn, emit):
    m, d = x.shape
    n_main = n // tn
    vec = pl.BlockSpec((1, HEAD_DIM), lambda i, j: (0, 0))
    out_specs = [
        pl.BlockSpec((tm, tn), lambda i, j: (i, jnp.minimum(j, n_main - 1))),
        pl.BlockSpec((tm, d_pool), lambda i, j: (i, 0)),
        pl.BlockSpec((tm, HEAD_DIM), lambda i, j: (i, 0)),
    ]
    out_shape = [jax.ShapeDtypeStruct((m, n), F32), jax.ShapeDtypeStruct((m, d_pool), F32),
                 jax.ShapeDtypeStruct((m, HEAD_DIM), F32)]
    if emit:
        out_specs += [pl.BlockSpec((d, tn), lambda i, j: (0, jnp.minimum(j, n_main - 1))),
                      pl.BlockSpec((d, d_pool), lambda i, j: (0, 0))]
        out_shape += [jax.ShapeDtypeStruct((d, n), BF16), jax.ShapeDtypeStruct((d, d_pool), BF16)]
    return pl.pallas_call(
        functools.partial(_inproj_kernel, n_heads=n_heads, n_main=n_main, emit=emit),
        grid=(m // tm, n_main + 1),
        in_specs=[
            pl.BlockSpec((tm, d), lambda i, j: (i, 0)),
            pl.BlockSpec((1, d), lambda i, j: (0, 0)),
            pl.BlockSpec((d, tn), lambda i, j: (0, jnp.minimum(j, n_main - 1))),
            pl.BlockSpec((d, d_pool), lambda i, j: (0, t_col), pipeline_mode=pl.Buffered(1)),
            pl.BlockSpec((d, HEAD_DIM), lambda i, j: (0, 0)),
            vec, vec,
        ],
        out_specs=out_specs,
        out_shape=out_shape,
        scratch_shapes=[pltpu.VMEM((tm, d), BF16)],
        compiler_params=_cparams(2),
        name="in_proj",
    )(x, norm_w, w_a, w_t, w_c, alog, dtb)


def _lane_column(x, lane_idx):
    lane = lax.broadcasted_iota(jnp.int32, x.shape, 1)
    col = jnp.sum(jnp.where(lane == lane_idx, x, 0.0), axis=1, keepdims=True)
    return jnp.broadcast_to(col, x.shape)


def _unit_lower_inverses(lms):
    n = lms[0].shape[0]
    row = lax.broadcasted_iota(jnp.int32, (n, n), 0)
    col = lax.broadcasted_iota(jnp.int32, (n, n), 1)
    eye = jnp.where(row == col, 1.0, 0.0)
    nks = [-lm for lm in lms]
    qs = [eye + nk for nk in nks]
    nks = [_dot(nk, nk) for nk in nks]
    power = 2
    while 2 * power < n:
        res = [_dot(nk, jnp.concatenate([nk, q], axis=1)) for nk, q in zip(nks, qs)]
        nks = [r[:, :n] for r in res]
        qs = [q + r[:, n:] for q, r in zip(qs, res)]
        power *= 2
    return [q + _dot(nk, q) for nk, q in zip(nks, qs)]


def _gdn_prompt_kernel(q_ref, k_ref, v_ref, gate_ref, hq_ref, hk_ref, hv_ref, gates_ref,
                       cwq_ref, cwk_ref, cwv_ref, gnw_ref, *rest, n_heads, hs, rows, n_cast):
    cast_in = rest[:n_cast]
    o_ref, sout_ref = rest[n_cast:n_cast + 2]
    cast_out = rest[n_cast + 2:2 * n_cast + 2]
    s_ref, ext_ref = rest[2 * n_cast + 2:]
    hg = pl.program_id(1)
    blk = pl.program_id(2)
    c = GDN_CHUNK

    for src, dst in zip(cast_in, cast_out):
        dst[...] = src[...].astype(dst.dtype)

    @pl.when(blk == 0)
    def _():
        s_ref[...] = jnp.zeros_like(s_ref)

    for a, (x_ref, halo_ref) in enumerate(((q_ref, hq_ref), (k_ref, hk_ref), (v_ref, hv_ref))):
        for hi in range(hs):
            lanes = slice(hi * HEAD_DIM, (hi + 1) * HEAD_DIM)
            ext_ref[a * hs + hi, 0:HALO_CONV, :] = jnp.where(blk > 0, halo_ref[:, lanes], 0.0)
            ext_ref[a * hs + hi, HALO_CONV:, :] = x_ref[:, lanes]

    def conv_silu(a, cw_ref, r0, hi):
        lanes = slice(hi * HEAD_DIM, (hi + 1) * HEAD_DIM)
        acc = None
        for j in range(CONV_W):
            term = cw_ref[j:j + 1, lanes] * ext_ref[a * hs + hi, pl.ds(HALO_CONV - (CONV_W - 1) + j + r0, c), :]
            acc = term if acc is None else acc + term
        return _silu(acc)

    row = lax.broadcasted_iota(jnp.int32, (c, c), 0)
    col = lax.broadcasted_iota(jnp.int32, (c, c), 1)
    tril = row >= col
    strict = row > col
    tril_f = jnp.where(tril, 1.0, 0.0)
    gnw = gnw_ref[...]
    heads = range(hs)

    s = [s_ref[hi] for hi in heads]
    for ci in range(rows // c):
        r0 = ci * c
        gates = gates_ref[r0:r0 + c, :]
        gcum = jnp.dot(tril_f, gates, preferred_element_type=F32, precision=lax.Precision.HIGHEST)
        qc = [_l2norm(conv_silu(0, cwq_ref, r0, hi)) * (HEAD_DIM ** -0.5) for hi in heads]
        kc = [_l2norm(conv_silu(1, cwk_ref, r0, hi)) for hi in heads]
        vc = [conv_silu(2, cwv_ref, r0, hi) for hi in heads]
        gc = [_lane_column(gcum, hg * hs + hi) for hi in heads]
        bc = [_lane_column(gates, hg * hs + hi + n_heads) for hi in heads]
        decay = [jnp.exp(jnp.where(tril, g - g.T, -1e30)) for g in gc]
        egc = [jnp.exp(g) for g in gc]
        g_last = [g[c - 1:c, :] for g in gc]
        kk = [_dot_nt(k, k) for k in kc]
        qk = [_dot_nt(q, k) for q, k in zip(qc, kc)]
        lm = [jnp.where(strict, kk[hi] * bc[hi] * decay[hi], 0.0) for hi in heads]
        t_inv = _unit_lower_inverses(lm)
        uw = [_dot(t_inv[hi], jnp.concatenate([vc[hi] * bc[hi], kc[hi] * (bc[hi] * egc[hi])], axis=1))
              for hi in heads]
        intra = [jnp.where(tril, qk[hi] * decay[hi], 0.0) for hi in heads]
        kdt = [(kc[hi] * jnp.exp(g_last[hi] - gc[hi])).T for hi in heads]
        r1 = [_dot(jnp.concatenate([uw[hi][:, HEAD_DIM:], qc[hi] * egc[hi]], axis=0), s[hi]) for hi in heads]
        v_new = [uw[hi][:, :HEAD_DIM] - r1[hi][:c] for hi in heads]
        r2 = [_dot(jnp.concatenate([intra[hi], kdt[hi]], axis=0), v_new[hi]) for hi in heads]
        s = [s[hi] * jnp.exp(g_last[hi]) + r2[hi][c:] for hi in heads]
        for hi in heads:
            lanes = slice(hi * HEAD_DIM, (hi + 1) * HEAD_DIM)
            o = _rms(r1[hi][c:] + r2[hi][:c], gnw) * _silu(gate_ref[r0:r0 + c, lanes])
            o_ref[r0:r0 + c, lanes] = o.astype(o_ref.dtype)

    for hi in range(hs):
        s_ref[hi] = s[hi]

    @pl.when(blk == pl.num_programs(2) - 1)
    def _():
        for hi in range(hs):
            sout_ref[0, hi] = s[hi]


def _gdn_prompt(proj, gates, conv_w, gnw, to_cast, batch, seq, n_heads, hs, rows):
    nblk = seq // rows
    hb = rows // HALO_CONV
    ng = n_heads // hs
    width = hs * HEAD_DIM
    steps = batch * ng * nblk

    def slab(w):
        return pl.BlockSpec((w.shape[0] // steps, w.shape[1]), lambda b, h, k: ((b * ng + h) * nblk + k, 0))

    def main(off):
        return pl.BlockSpec((rows, width), lambda b, h, k: (b * nblk + k, off * ng + h))

    def halo(off):
        return pl.BlockSpec(
            (HALO_CONV, width), lambda b, h, k: (jnp.maximum((b * nblk + k) * hb - 1, 0), off * ng + h))

    def cw(off):
        return pl.BlockSpec((CONV_W, width), lambda b, h, k: (0, off * ng + h))

    kern = functools.partial(_gdn_prompt_kernel, n_heads=n_heads, hs=hs, rows=rows, n_cast=len(to_cast))
    return pl.pallas_call(
        kern,
        grid=(batch, ng, nblk),
        in_specs=[main(0), main(1), main(2), main(3), halo(0), halo(1), halo(2),
                  pl.BlockSpec((rows, HEAD_DIM), lambda b, h, k: (b * nblk + k, 0)),
                  cw(0), cw(1), cw(2),
                  pl.BlockSpec((1, HEAD_DIM), lambda b, h, k: (0, 0)),
                  *[slab(w) for w in to_cast]],
        out_specs=[
            pl.BlockSpec((rows, width), lambda b, h, k: (b * nblk + k, h)),
            pl.BlockSpec((1, hs, HEAD_DIM, HEAD_DIM), lambda b, h, k: (b, h, 0, 0)),
            *[slab(w) for w in to_cast],
        ],
        out_shape=[
            jax.ShapeDtypeStruct((batch * seq, n_heads * HEAD_DIM), BF16),
            jax.ShapeDtypeStruct((batch, n_heads, HEAD_DIM, HEAD_DIM), F32),
            *[jax.ShapeDtypeStruct(w.shape, BF16) for w in to_cast],
        ],
        scratch_shapes=[pltpu.VMEM((hs, HEAD_DIM, HEAD_DIM), F32),
                        pltpu.VMEM((3 * hs, rows + HALO_CONV, HEAD_DIM), F32)],
        compiler_params=_cparams(3),
        name="gdn_prompt",
    )(proj, proj, proj, proj, proj, proj, proj, gates, conv_w, conv_w, conv_w, gnw, *to_cast)


def _gdn_sample_kernel(x_ref, cs_ref, gates_ref, cw_ref, gnw_ref, s_in_ref,
                       o_ref, s_out_ref, o_scr, *, n_heads, tb):
    d = n_heads * HEAD_DIM
    x = x_ref[...]
    cw = cw_ref[...]
    conv = cw[CONV_W - 1:CONV_W, :] * x[:, :3 * d]
    for j in range(CONV_W - 1):
        conv = conv + cw[j:j + 1, :] * cs_ref[:, j * 3 * d:(j + 1) * 3 * d]
    qkv = _silu(conv)

    gates = gates_ref[...]
    eg_all = jnp.exp(gates)
    pad = jnp.zeros((HEAD_DIM - tb, HEAD_DIM), F32)

    for h in range(n_heads):
        qh = _l2norm(qkv[:, h * HEAD_DIM:(h + 1) * HEAD_DIM]) * (HEAD_DIM ** -0.5)
        kh = _l2norm(qkv[:, d + h * HEAD_DIM:d + (h + 1) * HEAD_DIM])
        vh = qkv[:, 2 * d + h * HEAD_DIM:2 * d + (h + 1) * HEAD_DIM]
        kt = jnp.concatenate([kh, pad], axis=0).T
        toks = range(tb)
        kb = [jnp.broadcast_to(kt[:, t:t + 1], (HEAD_DIM, HEAD_DIM)) for t in toks]
        s1 = [s_in_ref[t, h] * eg_all[t:t + 1, h:h + 1] for t in toks]
        ks = [_dot(kh, s1[t])[t:t + 1, :] for t in toks]
        delta = [(vh[t:t + 1, :] - ks[t]) * gates[t:t + 1, n_heads + h:n_heads + h + 1] for t in toks]
        s2 = [s1[t] + kb[t] * delta[t] for t in toks]
        for t in toks:
            s_out_ref[t, h] = s2[t]
            o_scr[t:t + 1, h * HEAD_DIM:(h + 1) * HEAD_DIM] = _dot(qh, s2[t])[t:t + 1, :]

    gnw = gnw_ref[...]
    for h in range(n_heads):
        sl = slice(h * HEAD_DIM, (h + 1) * HEAD_DIM)
        o = _rms(o_scr[:, sl], gnw) * _silu(x[:, 3 * d + h * HEAD_DIM:3 * d + (h + 1) * HEAD_DIM])
        o_ref[:, sl] = o.astype(o_ref.dtype)


def _gdn_sample(proj, conv_state, gates, conv_w, gnw, state, n_heads, tb):
    m = proj.shape[0]
    d = n_heads * HEAD_DIM
    kern = functools.partial(_gdn_sample_kernel, n_heads=n_heads, tb=tb)
    return pl.pallas_call(
        kern,
        grid=(m // tb,),
        in_specs=[
            pl.BlockSpec((tb, 4 * d), lambda i: (i, 0)),
            pl.BlockSpec((tb, (CONV_W - 1) * 3 * d), lambda i: (i, 0)),
            pl.BlockSpec((tb, HEAD_DIM), lambda i: (i, 0)),
            pl.BlockSpec((CONV_W, 3 * d), lambda i: (0, 0)),
            pl.BlockSpec((1, HEAD_DIM), lambda i: (0, 0)),
            pl.BlockSpec((tb, n_heads, HEAD_DIM, HEAD_DIM), lambda i: (i, 0, 0, 0)),
        ],
        out_specs=[
            pl.BlockSpec((tb, d), lambda i: (i, 0)),
            pl.BlockSpec((tb, n_heads, HEAD_DIM, HEAD_DIM), lambda i: (i, 0, 0, 0)),
        ],
        out_shape=[
            jax.ShapeDtypeStruct((m, d), BF16),
            jax.ShapeDtypeStruct(state.shape, F32),
        ],
        scratch_shapes=[pltpu.VMEM((tb, d), F32)],
        compiler_params=_cparams(1),
        name="gdn_sample",
    )(proj, conv_state, gates, conv_w, gnw, state)


def _mix_out_kernel(oa_ref, u_ref, hist_ref, x_ref, wp_ref, ps_ref, wo_ref, nw_ref, out_ref, *scratch,
                    tm, seq, pos0):
    d_pool = u_ref.shape[1]
    d_gdn = oa_ref.shape[1]
    d = out_ref.shape[1]
    n_groups = len(POOL_WINDOWS)
    gd = d_pool // n_groups
    strips = gd // HEAD_DIM
    if seq > 1:
        (ext_ref,) = scratch
        start = (pl.program_id(0) * tm) % seq
        for si in range(d_pool // HEAD_DIM):
            lanes = slice(si * HEAD_DIM, (si + 1) * HEAD_DIM)
            ext_ref[si, 0:HALO_POOL, :] = jnp.where(start > 0, hist_ref[:, lanes], 0.0)
            ext_ref[si, HALO_POOL:, :] = u_ref[:, lanes]
        pos = pos0 + start + lax.broadcasted_iota(jnp.int32, (tm, HEAD_DIM), 0)
    else:
        pos = jnp.full((tm, HEAD_DIM), pos0, jnp.int32)

    tn = d // n_groups
    oa = oa_ref[...]
    acc = []
    parts = []
    for gi, win in enumerate(POOL_WINDOWS):
        acc.append(jnp.dot(oa, wo_ref[:d_gdn, gi * tn:(gi + 1) * tn], preferred_element_type=F32))
        cnt = jnp.minimum(pos + 1, win).astype(F32)
        pooled = []
        for si in range(gi * strips, (gi + 1) * strips):
            lanes = slice(si * HEAD_DIM, (si + 1) * HEAD_DIM)
            cur = u_ref[:, lanes]
            wsum = cur
            for r in range(1, win):
                if seq > 1:
                    wsum = wsum + ext_ref[si, pl.ds(HALO_POOL - r, tm), :]
                else:
                    off = (POOL_BUF - r) * d_pool + si * HEAD_DIM
                    wsum = wsum + hist_ref[:, off:off + HEAD_DIM]
            pooled.append(wsum / cnt - cur)
        ob = _dot(jnp.concatenate(pooled, axis=1), wp_ref[gi]) * ps_ref[:, gi * gd:(gi + 1) * gd]
        parts.append(ob.astype(BF16))
    ob_all = jnp.concatenate(parts, axis=1)
    mix = [acc[ci] + jnp.dot(ob_all, wo_ref[d_gdn:, ci * tn:(ci + 1) * tn], preferred_element_type=F32)
           for ci in range(n_groups)]
    ss = sum(jnp.sum(m * m, axis=-1, keepdims=True) for m in mix)
    inv = lax.rsqrt(ss * (1.0 / d) + EPS)
    for ci in range(n_groups):
        cols = slice(ci * tn, (ci + 1) * tn)
        out_ref[:, cols] = x_ref[:, cols] + mix[ci] * inv * nw_ref[:, cols]


def _mix_out(o_a, u, hist, x, w_pool, pool_scale, w_out, norm_w, tm, seq, pos0):
    m, d = x.shape
    d_pool = pool_scale.shape[1]
    d_gdn = o_a.shape[1]
    if seq > 1:
        hb = tm // HALO_POOL
        hist_spec = pl.BlockSpec((HALO_POOL, d_pool), lambda i: (jnp.maximum(i * hb - 1, 0), 0))
        scratch = [pltpu.VMEM((d_pool // HEAD_DIM, tm + HALO_POOL, HEAD_DIM), F32)]
    else:
        hist_spec = pl.BlockSpec((tm, POOL_BUF * d_pool), lambda i: (i, 0))
        scratch = []
    kern = functools.partial(_mix_out_kernel, tm=tm, seq=seq, pos0=pos0)
    return pl.pallas_call(
        kern,
        grid=(m // tm,),
        in_specs=[
            pl.BlockSpec((tm, d_gdn), lambda i: (i, 0)),
            pl.BlockSpec((tm, d_pool), lambda i: (i, 0)),
            hist_spec,
            pl.BlockSpec((tm, d), lambda i: (i, 0)),
            pl.BlockSpec(w_pool.shape, lambda i: (0, 0, 0), pipeline_mode=pl.Buffered(1)),
            pl.BlockSpec((1, d_pool), lambda i: (0, 0)),
            pl.BlockSpec(w_out.shape, lambda i: (0, 0), pipeline_mode=pl.Buffered(1)),
            pl.BlockSpec((1, d), lambda i: (0, 0)),
        ],
        out_specs=pl.BlockSpec((tm, d), lambda i: (i, 0)),
        out_shape=jax.ShapeDtypeStruct((m, d), F32),
        scratch_shapes=scratch,
        compiler_params=_cparams(1),
        name="mix_out",
    )(o_a, u, hist, x, w_pool, pool_scale, w_out, norm_w)


def _mlp_kernel(x_ref, nw1_ref, wu_ref, wd_ref, nw2_ref, out_ref, h_ref):
    j = pl.program_id(1)

    @pl.when(j == 0)
    def _():
        h_ref[...] = _rms(x_ref[...], nw1_ref[...]).astype(BF16)
        out_ref[...] = jnp.zeros_like(out_ref)

    up = jnp.dot(h_ref[...], wu_ref[...], preferred_element_type=F32)
    act = jnp.square(jnp.maximum(up, 0.0)).astype(BF16)
    out_ref[...] += jnp.dot(act, wd_ref[...], preferred_element_type=F32)

    @pl.when(j == pl.num_programs(1) - 1)
    def _():
        out_ref[...] = x_ref[...] + _rms(out_ref[...], nw2_ref[...])


def _mlp(x, nw1, w_up, w_down, nw2, tm, tf):
    m, d = x.shape
    f = w_up.shape[1]
    return pl.pallas_call(
        _mlp_kernel,
        grid=(m // tm, f // tf),
        in_specs=[
            pl.BlockSpec((tm, d), lambda i, j: (i, 0)),
            pl.BlockSpec((1, d), lambda i, j: (0, 0)),
            pl.BlockSpec((d, tf), lambda i, j: (0, j)),
            pl.BlockSpec((tf, d), lambda i, j: (j, 0)),
            pl.BlockSpec((1, d), lambda i, j: (0, 0)),
        ],
        out_specs=pl.BlockSpec((tm, d), lambda i, j: (i, 0)),
        out_shape=jax.ShapeDtypeStruct((m, d), F32),
        scratch_shapes=[pltpu.VMEM((tm, d), BF16)],
        compiler_params=_cparams(2),
        name="mlp",
    )(x, nw1, w_up, w_down, nw2)


def _pad_lanes(v, n=HEAD_DIM):
    return jnp.pad(v.reshape(1, -1), ((0, 0), (0, n - v.shape[-1])))


def kernel(x_prompt, x_sample, state_gdn, state_conv, state_pool, norm_pre_mix, w_in, conv_w, a_log, dt_bias,
           norm_gdn_out, w_pool, pool_scale, w_out, norm_post_mix, norm_pre_mlp, w_up, w_down, norm_post_mlp):
    batch, seq, d_model = x_prompt.shape
    dec_batch, dec_seq, _ = x_sample.shape
    assert dec_seq == 1
    depth = w_in.shape[0]
    n_heads = a_log.shape[1]
    d_gdn = n_heads * HEAD_DIM
    d_qkv = 3 * d_gdn
    d_pool = pool_scale.shape[1]
    o_gate = d_qkv + d_gdn

    yp = x_prompt.reshape(batch * seq, d_model)
    ys = x_sample.reshape(dec_batch, d_model)
    outs = [[] for _ in range(6)]
    for l in range(depth):
        w_c = jnp.pad(w_in[l][:, o_gate + d_pool:], ((0, 0), (0, HEAD_DIM - 2 * n_heads))).astype(BF16)
        alog = _pad_lanes(a_log[l])
        dtb = _pad_lanes(dt_bias[l])
        gnw = norm_gdn_out[l].reshape(1, HEAD_DIM)
        npm = norm_pre_mix[l].reshape(1, d_model)
        npo = norm_post_mix[l].reshape(1, d_model)
        nm1 = norm_pre_mlp[l].reshape(1, d_model)
        nm2 = norm_post_mlp[l].reshape(1, d_model)
        ps = pool_scale[l].reshape(1, d_pool)

        assert o_gate % d_pool == 0
        proj_s, u_s, gates_s, w_a, w_t = _in_proj(
            ys, npm, w_in[l], o_gate, w_in[l], o_gate // d_pool, w_c, d_pool, alog, dtb, n_heads,
            tm=dec_batch, tn=512, emit=True)

        proj_p, u_p, gates_p = _in_proj(
            yp, npm, w_a, o_gate, w_t, 0, w_c, d_pool, alog, dtb, n_heads, tm=1024, tn=512, emit=False)
        to_cast = (w_up[l], w_down[l], w_out[l], w_pool[l].reshape(-1, w_pool.shape[-1]))
        oa_p, s_p, w_up_b, w_down_b, w_out_b, w_pool_b = _gdn_prompt(
            proj_p, gates_p, conv_w[l], gnw, to_cast, batch, seq, n_heads, hs=8, rows=256)
        w_pool_b = w_pool_b.reshape(w_pool.shape[1:])
        x1_p = _mix_out(oa_p, u_p, u_p, yp, w_pool_b, ps, w_out_b, npo, tm=512, seq=seq, pos0=0)
        yp = _mlp(x1_p, nm1, w_up_b, w_down_b, nm2, tm=512, tf=1024)
        outs[0].append(s_p)
        outs[1].append(proj_p.reshape(batch, seq, -1)[:, seq - (CONV_W - 1):, :d_qkv])
        outs[2].append(u_p.reshape(batch, seq, -1)[:, seq - POOL_BUF:])

        cs =state_conv[l].reshape(dec_batch, (CONV_W - 1) * d_qkv)
        hist = state_pool[l].reshape(dec_batch, POOL_BUF * d_pool)
        oa_s, s_s = _gdn_sample(proj_s, cs, gates_s, conv_w[l], gnw, state_gdn[l], n_heads, tb=8)
        x1_s = _mix_out(oa_s, u_s, hist, ys, w_pool_b, ps, w_out_b, npo, tm=dec_batch, seq=1, pos0=PAST_LEN)
        ys = _mlp(x1_s, nm1, w_up_b, w_down_b, nm2, tm=dec_batch, tf=1024)
        outs[3].append(s_s)
        outs[4].append(jnp.concatenate([state_conv[l][:, 1:], proj_s[:, None, :d_qkv]], axis=1))
        outs[5].append(jnp.concatenate([state_pool[l][:, 1:], u_s[:, None]], axis=1))

    return (yp.reshape(batch, seq, d_model), ys.reshape(dec_batch, dec_seq, d_model),
            *[jnp.stack(o) for o in outs])
```

```python
import functools

import jax
import jax.numpy as jnp
from jax import lax
from jax.experimental import pallas as pl
from jax.experimental.pallas import tpu as pltpu

F32 = jnp.float32
BF16 = jnp.bfloat16

EPS = 1e-6
HEAD_DIM = 128
CONV_W = 4
POOL_WINDOWS = (2, 4, 8, 16)
POOL_BUF = max(POOL_WINDOWS) - 1
PAST_LEN = 16384
HALO_CONV = 8
HALO_POOL = 16
GDN_CHUNK = 128
VMEM_LIMIT = 56 * 1024 * 1024


def _cparams(n_axes):
    return pltpu.CompilerParams(
        dimension_semantics=("arbitrary",) * n_axes, vmem_limit_bytes=VMEM_LIMIT)


def _dot(a, b):
    return jnp.dot(a.astype(BF16), b.astype(BF16), preferred_element_type=F32)


def _dot_nt(a, b):
    return lax.dot_general(a.astype(BF16), b.astype(BF16), (((1,), (1,)), ((), ())),
                           preferred_element_type=F32)


def _rms(x, w):
    return x * lax.rsqrt(jnp.mean(x * x, axis=-1, keepdims=True) + EPS) * w


def _silu(x):
    return x * (0.5 * jnp.tanh(0.5 * x) + 0.5)


def _softplus(x):
    return jnp.maximum(x, 0.0) + jnp.log(1.0 + jnp.exp(-jnp.abs(x)))


def _l2norm(x):
    return x * lax.rsqrt(jnp.sum(x * x, axis=-1, keepdims=True) + EPS)


def _inproj_kernel(x_ref, nw_ref, wa_ref, wt_ref, wc_ref, alog_ref, dtb_ref, *rest, n_heads, n_main, emit):
    if emit:
        out_ref, u_ref, gates_ref, wa_out, wt_out, wc_out, h_ref = rest
    else:
        out_ref, u_ref, gates_ref, h_ref = rest
    j = pl.program_id(1)

    @pl.when(j == 0)
    def _():
        h_ref[...] = _rms(x_ref[...], nw_ref[...]).astype(BF16)

    @pl.when(j < n_main)
    def _():
        w = wa_ref[...].T.astype(BF16) if emit else wa_ref[...]
        if emit:
            wa_out[...] = w
        out_ref[...] = jnp.dot(h_ref[...], w, preferred_element_type=F32)

    @pl.when(j == n_main)
    def _():
        if emit:
            n_gate = 2 * n_heads
            blk = wt_ref[...]
            pool_rows = jnp.concatenate([blk[n_gate:], wc_ref[...]], axis=0)
            gate_rows = jnp.concatenate([blk[:n_gate], jnp.zeros((HEAD_DIM - n_gate, blk.shape[1]), F32)], axis=0)
            wt = pool_rows.T.astype(BF16)
            wc = gate_rows.T.astype(BF16)
            wt_out[...] = wt
            wc_out[...] = wc
        else:
            wt = wt_ref[...]
            wc = wc_ref[...]
        h = h_ref[...]
        u_ref[...] = jnp.dot(h, wt, preferred_element_type=F32)
        ab = jnp.dot(h, wc, preferred_element_type=F32)
        lane = lax.broadcasted_iota(jnp.int32, ab.shape, 1)
        g = -jnp.exp(alog_ref[...]) * _softplus(ab + dtb_ref[...])
        gates_ref[...] = jnp.where(lane < n_heads, g, jax.nn.sigmoid(ab))


def _in_proj(x, norm_w, weights, n, d_pool, alog, dtb, n_heads, tm, tn, emit):
    m, d = x.shape
    n_main = n // tn
    n_gate = 2 * n_heads
    vec = pl.BlockSpec((1, HEAD_DIM), lambda i, j: (0, 0))
    out_specs = [
        pl.BlockSpec((tm, tn), lambda i, j: (i, jnp.minimum(j, n_main - 1))),
        pl.BlockSpec((tm, d_pool), lambda i, j: (i, 0)),
        pl.BlockSpec((tm, HEAD_DIM), lambda i, j: (i, 0)),
    ]
    out_shape = [jax.ShapeDtypeStruct((m, n), F32), jax.ShapeDtypeStruct((m, d_pool), F32),
                 jax.ShapeDtypeStruct((m, HEAD_DIM), F32)]
    wa_bf_spec = pl.BlockSpec((d, tn), lambda i, j: (0, jnp.minimum(j, n_main - 1)))
    wt_bf_spec = pl.BlockSpec((d, d_pool), lambda i, j: (0, 0), pipeline_mode=pl.Buffered(1))
    wc_bf_spec = pl.BlockSpec((d, HEAD_DIM), lambda i, j: (0, 0))
    if emit:
        assert n % d_pool == 0 and (n + d_pool) % n_gate == 0 and n_gate % 8 == 0
        operands = (weights, weights, weights)
        w_specs = [
            pl.BlockSpec((tn, d), lambda i, j: (jnp.minimum(j, n_main - 1), 0)),
            pl.BlockSpec((d_pool, d), lambda i, j: (n // d_pool, 0), pipeline_mode=pl.Buffered(1)),
            pl.BlockSpec((n_gate, d), lambda i, j: ((n + d_pool) // n_gate, 0)),
        ]
        out_specs += [wa_bf_spec, pl.BlockSpec((d, d_pool), lambda i, j: (0, 0)), wc_bf_spec]
        out_shape += [jax.ShapeDtypeStruct((d, n), BF16), jax.ShapeDtypeStruct((d, d_pool), BF16),
                      jax.ShapeDtypeStruct((d, HEAD_DIM), BF16)]
    else:
        operands = weights
        w_specs = [wa_bf_spec, wt_bf_spec, wc_bf_spec]
    return pl.pallas_call(
        functools.partial(_inproj_kernel, n_heads=n_heads, n_main=n_main, emit=emit),
        grid=(m // tm, n_main + 1),
        in_specs=[
            pl.BlockSpec((tm, d), lambda i, j: (i, 0)),
            pl.BlockSpec((1, d), lambda i, j: (0, 0)),
            *w_specs,
            vec, vec,
        ],
        out_specs=out_specs,
        out_shape=out_shape,
        scratch_shapes=[pltpu.VMEM((tm, d), BF16)],
        compiler_params=_cparams(2),
        name="in_proj",
    )(x, norm_w, *operands, alog, dtb)


def _lane_column(x, lane_idx):
    lane = lax.broadcasted_iota(jnp.int32, x.shape, 1)
    col = jnp.sum(jnp.where(lane == lane_idx, x, 0.0), axis=1, keepdims=True)
    return jnp.broadcast_to(col, x.shape)


def _unit_lower_inverses(lms):
    n = lms[0].shape[0]
    row = lax.broadcasted_iota(jnp.int32, (n, n), 0)
    col = lax.broadcasted_iota(jnp.int32, (n, n), 1)
    eye = jnp.where(row == col, 1.0, 0.0)
    nks = [-lm for lm in lms]
    qs = [eye + nk for nk in nks]
    nks = [_dot(nk, nk) for nk in nks]
    power = 2
    while 2 * power < n:
        res = [_dot(nk, jnp.concatenate([nk, q], axis=1)) for nk, q in zip(nks, qs)]
        nks = [r[:, :n] for r in res]
        qs = [q + r[:, n:] for q, r in zip(qs, res)]
        power *= 2
    return [q + _dot(nk, q) for nk, q in zip(nks, qs)]


def _gdn_prompt_kernel(q_ref, k_ref, v_ref, gate_ref, hq_ref, hk_ref, hv_ref, gates_ref,
                       cwq_ref, cwk_ref, cwv_ref, gnw_ref, *rest, n_heads, hs, rows, n_cast):
    cast_in = rest[:n_cast]
    o_ref, sout_ref = rest[n_cast:n_cast + 2]
    cast_out = rest[n_cast + 2:2 * n_cast + 2]
    s_ref, ext_ref = rest[2 * n_cast + 2:]
    hg = pl.program_id(1)
    blk = pl.program_id(2)
    c = GDN_CHUNK

    for src, dst in zip(cast_in, cast_out):
        dst[...] = src[...].astype(dst.dtype)

    @pl.when(blk == 0)
    def _():
        s_ref[...] = jnp.zeros_like(s_ref)

    for a, (x_ref, halo_ref) in enumerate(((q_ref, hq_ref), (k_ref, hk_ref), (v_ref, hv_ref))):
        for hi in range(hs):
            lanes = slice(hi * HEAD_DIM, (hi + 1) * HEAD_DIM)
            ext_ref[a * hs + hi, 0:HALO_CONV, :] = jnp.where(blk > 0, halo_ref[:, lanes], 0.0)
            ext_ref[a * hs + hi, HALO_CONV:, :] = x_ref[:, lanes]

    def conv_silu(a, cw_ref, r0, hi):
        lanes = slice(hi * HEAD_DIM, (hi + 1) * HEAD_DIM)
        acc = None
        for j in range(CONV_W):
            term = cw_ref[j:j + 1, lanes] * ext_ref[a * hs + hi, pl.ds(HALO_CONV - (CONV_W - 1) + j + r0, c), :]
            acc = term if acc is None else acc + term
        return _silu(acc)

    row = lax.broadcasted_iota(jnp.int32, (c, c), 0)
    col = lax.broadcasted_iota(jnp.int32, (c, c), 1)
    tril = row >= col
    strict = row > col
    tril_f = jnp.where(tril, 1.0, 0.0)
    gnw = gnw_ref[...]
    heads = range(hs)

    s = [s_ref[hi] for hi in heads]
    for ci in range(rows // c):
        r0 = ci * c
        gates = gates_ref[r0:r0 + c, :]
        gcum = jnp.dot(tril_f, gates, preferred_element_type=F32, precision=lax.Precision.HIGHEST)
        qc = [_l2norm(conv_silu(0, cwq_ref, r0, hi)) * (HEAD_DIM ** -0.5) for hi in heads]
        kc = [_l2norm(conv_silu(1, cwk_ref, r0, hi)) for hi in heads]
        vc = [conv_silu(2, cwv_ref, r0, hi) for hi in heads]
        gc = [_lane_column(gcum, hg * hs + hi) for hi in heads]
        bc = [_lane_column(gates, hg * hs + hi + n_heads) for hi in heads]
        decay = [jnp.exp(jnp.where(tril, g - g.T, -1e30)) for g in gc]
        egc = [jnp.exp(g) for g in gc]
        g_last = [g[c - 1:c, :] for g in gc]
        kk = [_dot_nt(k, k) for k in kc]
        qk = [_dot_nt(q, k) for q, k in zip(qc, kc)]
        lm = [jnp.where(strict, kk[hi] * bc[hi] * decay[hi], 0.0) for hi in heads]
        t_inv = _unit_lower_inverses(lm)
        uw = [_dot(t_inv[hi], jnp.concatenate([vc[hi] * bc[hi], kc[hi] * (bc[hi] * egc[hi])], axis=1))
              for hi in heads]
        intra = [jnp.where(tril, qk[hi] * decay[hi], 0.0) for hi in heads]
        kdt = [(kc[hi] * jnp.exp(g_last[hi] - gc[hi])).T for hi in heads]
        r1 = [_dot(jnp.concatenate([uw[hi][:, HEAD_DIM:], qc[hi] * egc[hi]], axis=0), s[hi]) for hi in heads]
        v_new = [uw[hi][:, :HEAD_DIM] - r1[hi][:c] for hi in heads]
        r2 = [_dot(jnp.concatenate([intra[hi], kdt[hi]], axis=0), v_new[hi]) for hi in heads]
        s = [s[hi] * jnp.exp(g_last[hi]) + r2[hi][c:] for hi in heads]
        for hi in heads:
            lanes = slice(hi * HEAD_DIM, (hi + 1) * HEAD_DIM)
            o = _rms(r1[hi][c:] + r2[hi][:c], gnw) * _silu(gate_ref[r0:r0 + c, lanes])
            o_ref[r0:r0 + c, lanes] = o.astype(o_ref.dtype)

    for hi in range(hs):
        s_ref[hi] = s[hi]

    @pl.when(blk == pl.num_programs(2) - 1)
    def _():
        for hi in range(hs):
            sout_ref[0, hi] = s[hi]


def _gdn_prompt(proj, gates, conv_w, gnw, to_cast, batch, seq, n_heads, hs, rows):
    nblk = seq // rows
    hb = rows // HALO_CONV
    ng = n_heads // hs
    width = hs * HEAD_DIM
    steps = batch * ng * nblk

    def slab(w):
        return pl.BlockSpec((w.shape[0] // steps, w.shape[1]), lambda b, h, k: ((b * ng + h) * nblk + k, 0))

    def main(off):
        return pl.BlockSpec((rows, width), lambda b, h, k: (b * nblk + k, off * ng + h))

    def halo(off):
        return pl.BlockSpec(
            (HALO_CONV, width), lambda b, h, k: (jnp.maximum((b * nblk + k) * hb - 1, 0), off * ng + h))

    def cw(off):
        return pl.BlockSpec((CONV_W, width), lambda b, h, k: (0, off * ng + h))

    kern = functools.partial(_gdn_prompt_kernel, n_heads=n_heads, hs=hs, rows=rows, n_cast=len(to_cast))
    return pl.pallas_call(
        kern,
        grid=(batch, ng, nblk),
        in_specs=[main(0), main(1), main(2), main(3), halo(0), halo(1), halo(2),
                  pl.BlockSpec((rows, HEAD_DIM), lambda b, h, k: (b * nblk + k, 0)),
                  cw(0), cw(1), cw(2),
                  pl.BlockSpec((1, HEAD_DIM), lambda b, h, k: (0, 0)),
                  *[slab(w) for w in to_cast]],
        out_specs=[
            pl.BlockSpec((rows, width), lambda b, h, k: (b * nblk + k, h)),
            pl.BlockSpec((1, hs, HEAD_DIM, HEAD_DIM), lambda b, h, k: (b, h, 0, 0)),
            *[slab(w) for w in to_cast],
        ],
        out_shape=[
            jax.ShapeDtypeStruct((batch * seq, n_heads * HEAD_DIM), BF16),
            jax.ShapeDtypeStruct((batch, n_heads, HEAD_DIM, HEAD_DIM), F32),
            *[jax.ShapeDtypeStruct(w.shape, BF16) for w in to_cast],
        ],
        scratch_shapes=[pltpu.VMEM((hs, HEAD_DIM, HEAD_DIM), F32),
                        pltpu.VMEM((3 * hs, rows + HALO_CONV, HEAD_DIM), F32)],
        compiler_params=_cparams(3),
        name="gdn_prompt",
    )(proj, proj, proj, proj, proj, proj, proj, gates, conv_w, conv_w, conv_w, gnw, *to_cast)


def _gdn_sample_kernel(x_ref, cs_ref, gates_ref, cw_ref, gnw_ref, s_in_ref,
                       o_ref, s_out_ref, cs_out_ref, o_scr, *, n_heads, tb):
    d = n_heads * HEAD_DIM
    x = x_ref[...]
    cw = cw_ref[...]
    conv = cw[CONV_W - 1:CONV_W, :] * x[:, :3 * d]
    for j in range(CONV_W - 1):
        conv = conv + cw[j:j + 1, :] * cs_ref[j]
    qkv = _silu(conv)
    for j in range(CONV_W - 2):
        cs_out_ref[j] = cs_ref[j + 1]
    cs_out_ref[CONV_W - 2] = x[:, :3 * d]

    gates = gates_ref[...]
    eg_all = jnp.exp(gates)
    pad = jnp.zeros((HEAD_DIM - tb, HEAD_DIM), F32)

    for h in range(n_heads):
        qh = _l2norm(qkv[:, h * HEAD_DIM:(h + 1) * HEAD_DIM]) * (HEAD_DIM ** -0.5)
        kh = _l2norm(qkv[:, d + h * HEAD_DIM:d + (h + 1) * HEAD_DIM])
        vh = qkv[:, 2 * d + h * HEAD_DIM:2 * d + (h + 1) * HEAD_DIM]
        kt = jnp.concatenate([kh, pad], axis=0).T
        toks = range(tb)
        kb = [jnp.broadcast_to(kt[:, t:t + 1], (HEAD_DIM, HEAD_DIM)) for t in toks]
        s1 = [s_in_ref[t, h] * eg_all[t:t + 1, h:h + 1] for t in toks]
        ks = [_dot(kh, s1[t])[t:t + 1, :] for t in toks]
        delta = [(vh[t:t + 1, :] - ks[t]) * gates[t:t + 1, n_heads + h:n_heads + h + 1] for t in toks]
        s2 = [s1[t] + kb[t] * delta[t] for t in toks]
        for t in toks:
            s_out_ref[t, h] = s2[t]
            o_scr[t:t + 1, h * HEAD_DIM:(h + 1) * HEAD_DIM] = _dot(qh, s2[t])[t:t + 1, :]

    gnw = gnw_ref[...]
    for h in range(n_heads):
        sl = slice(h * HEAD_DIM, (h + 1) * HEAD_DIM)
        o = _rms(o_scr[:, sl], gnw) * _silu(x[:, 3 * d + h * HEAD_DIM:3 * d + (h + 1) * HEAD_DIM])
        o_ref[:, sl] = o.astype(o_ref.dtype)


def _gdn_sample(proj, conv_state, gates, conv_w, gnw, state, n_heads, tb):
    m = proj.shape[0]
    d = n_heads * HEAD_DIM
    cs_spec = pl.BlockSpec((CONV_W - 1, tb, 3 * d), lambda i: (0, i, 0))
    kern = functools.partial(_gdn_sample_kernel, n_heads=n_heads, tb=tb)
    return pl.pallas_call(
        kern,
        grid=(m // tb,),
        in_specs=[
            pl.BlockSpec((tb, 4 * d), lambda i: (i, 0)),
            cs_spec,
            pl.BlockSpec((tb, HEAD_DIM), lambda i: (i, 0)),
            pl.BlockSpec((CONV_W, 3 * d), lambda i: (0, 0)),
            pl.BlockSpec((1, HEAD_DIM), lambda i: (0, 0)),
            pl.BlockSpec((tb, n_heads, HEAD_DIM, HEAD_DIM), lambda i: (i, 0, 0, 0)),
        ],
        out_specs=[
            pl.BlockSpec((tb, d), lambda i: (i, 0)),
            pl.BlockSpec((tb, n_heads, HEAD_DIM, HEAD_DIM), lambda i: (i, 0, 0, 0)),
            cs_spec,
        ],
        out_shape=[
            jax.ShapeDtypeStruct((m, d), BF16),
            jax.ShapeDtypeStruct(state.shape, F32),
            jax.ShapeDtypeStruct(conv_state.shape, F32),
        ],
        scratch_shapes=[pltpu.VMEM((tb, d), F32)],
        compiler_params=_cparams(1),
        name="gdn_sample",
    )(proj, conv_state, gates, conv_w, gnw, state)


def _mix_out_kernel(oa_ref, u_ref, hist_ref, x_ref, wp_ref, ps_ref, wo_ref, nw_ref, out_ref, *rest,
                    tm, seq, pos0):
    d_pool = u_ref.shape[1]
    d_gdn = oa_ref.shape[1]
    d = out_ref.shape[1]
    n_groups = len(POOL_WINDOWS)
    gd = d_pool // n_groups
    strips = gd // HEAD_DIM
    if seq > 1:
        (ext_ref,) = rest
        start = (pl.program_id(0) * tm) % seq
        for si in range(d_pool // HEAD_DIM):
            lanes = slice(si * HEAD_DIM, (si + 1) * HEAD_DIM)
            ext_ref[si, 0:HALO_POOL, :] = jnp.where(start > 0, hist_ref[:, lanes], 0.0)
            ext_ref[si, HALO_POOL:, :] = u_ref[:, lanes]
        pos = pos0 + start + lax.broadcasted_iota(jnp.int32, (tm, HEAD_DIM), 0)
    else:
        (hist_out_ref,) = rest
        for r in range(POOL_BUF - 1):
            hist_out_ref[r] = hist_ref[r + 1]
        hist_out_ref[POOL_BUF - 1] = u_ref[...]
        pos = jnp.full((tm, HEAD_DIM), pos0, jnp.int32)

    tn = d // n_groups
    oa = oa_ref[...]
    acc = []
    parts = []
    for gi, win in enumerate(POOL_WINDOWS):
        acc.append(jnp.dot(oa, wo_ref[:d_gdn, gi * tn:(gi + 1) * tn], preferred_element_type=F32))
        cnt = jnp.minimum(pos + 1, win).astype(F32)
        pooled = []
        for si in range(gi * strips, (gi + 1) * strips):
            lanes = slice(si * HEAD_DIM, (si + 1) * HEAD_DIM)
            cur = u_ref[:, lanes]
            wsum = cur
            for r in range(1, win):
                if seq > 1:
                    wsum = wsum + ext_ref[si, pl.ds(HALO_POOL - r, tm), :]
                else:
                    wsum = wsum + hist_ref[POOL_BUF - r, :, lanes]
            pooled.append(wsum / cnt - cur)
        ob = _dot(jnp.concatenate(pooled, axis=1), wp_ref[gi]) * ps_ref[:, gi * gd:(gi + 1) * gd]
        parts.append(ob.astype(BF16))
    ob_all = jnp.concatenate(parts, axis=1)
    mix = [acc[ci] + jnp.dot(ob_all, wo_ref[d_gdn:, ci * tn:(ci + 1) * tn], preferred_element_type=F32)
           for ci in range(n_groups)]
    ss = sum(jnp.sum(m * m, axis=-1, keepdims=True) for m in mix)
    inv = lax.rsqrt(ss * (1.0 / d) + EPS)
    for ci in range(n_groups):
        cols = slice(ci * tn, (ci + 1) * tn)
        out_ref[:, cols] = x_ref[:, cols] + mix[ci] * inv * nw_ref[:, cols]


def _mix_out(o_a, u, hist, x, w_pool, pool_scale, w_out, norm_w, tm, seq, pos0):
    m, d = x.shape
    d_pool = pool_scale.shape[1]
    d_gdn = o_a.shape[1]
    out_specs = [pl.BlockSpec((tm, d), lambda i: (i, 0))]
    out_shape = [jax.ShapeDtypeStruct((m, d), F32)]
    if seq > 1:
        hb = tm // HALO_POOL
        hist_spec = pl.BlockSpec((HALO_POOL, d_pool), lambda i: (jnp.maximum(i * hb - 1, 0), 0))
        scratch = [pltpu.VMEM((d_pool // HEAD_DIM, tm + HALO_POOL, HEAD_DIM), F32)]
    else:
        hist_spec = pl.BlockSpec((POOL_BUF, tm, d_pool), lambda i: (0, i, 0))
        out_specs.append(hist_spec)
        out_shape.append(jax.ShapeDtypeStruct(hist.shape, F32))
        scratch = []
    kern = functools.partial(_mix_out_kernel, tm=tm, seq=seq, pos0=pos0)
    return pl.pallas_call(
        kern,
        grid=(m // tm,),
        in_specs=[
            pl.BlockSpec((tm, d_gdn), lambda i: (i, 0)),
            pl.BlockSpec((tm, d_pool), lambda i: (i, 0)),
            hist_spec,
            pl.BlockSpec((tm, d), lambda i: (i, 0)),
            pl.BlockSpec(w_pool.shape, lambda i: (0, 0, 0), pipeline_mode=pl.Buffered(1)),
            pl.BlockSpec((1, d_pool), lambda i: (0, 0)),
            pl.BlockSpec(w_out.shape, lambda i: (0, 0), pipeline_mode=pl.Buffered(1)),
            pl.BlockSpec((1, d), lambda i: (0, 0)),
        ],
        out_specs=out_specs,
        out_shape=out_shape,
        scratch_shapes=scratch,
        compiler_params=_cparams(1),
        name="mix_out",
    )(o_a, u, hist, x, w_pool, pool_scale, w_out, norm_w)


def _mlp_kernel(x_ref, nw1_ref, wu_ref, wd_ref, nw2_ref, out_ref, h_ref):
    j = pl.program_id(1)

    @pl.when(j == 0)
    def _():
        h_ref[...] = _rms(x_ref[...], nw1_ref[...]).astype(BF16)
        out_ref[...] = jnp.zeros_like(out_ref)

    up = jnp.dot(h_ref[...], wu_ref[...], preferred_element_type=F32)
    act = jnp.square(jnp.maximum(up, 0.0)).astype(BF16)
    out_ref[...] += jnp.dot(act, wd_ref[...], preferred_element_type=F32)

    @pl.when(j == pl.num_programs(1) - 1)
    def _():
        out_ref[...] = x_ref[...] + _rms(out_ref[...], nw2_ref[...])


def _mlp(x, nw1, w_up, w_down, nw2, tm, tf):
    m, d = x.shape
    f = w_up.shape[1]
    return pl.pallas_call(
        _mlp_kernel,
        grid=(m // tm, f // tf),
        in_specs=[
            pl.BlockSpec((tm, d), lambda i, j: (i, 0)),
            pl.BlockSpec((1, d), lambda i, j: (0, 0)),
            pl.BlockSpec((d, tf), lambda i, j: (0, j)),
            pl.BlockSpec((tf, d), lambda i, j: (j, 0)),
            pl.BlockSpec((1, d), lambda i, j: (0, 0)),
        ],
        out_specs=pl.BlockSpec((tm, d), lambda i, j: (i, 0)),
        out_shape=jax.ShapeDtypeStruct((m, d), F32),
        scratch_shapes=[pltpu.VMEM((tm, d), BF16)],
        compiler_params=_cparams(2),
        name="mlp",
    )(x, nw1, w_up, w_down, nw2)


def _pad_lanes(v, n=HEAD_DIM):
    return jnp.pad(v.reshape(1, -1), ((0, 0), (0, n - v.shape[-1])))


def kernel(x_prompt, x_sample, state_gdn, state_conv, state_pool, norm_pre_mix, w_in, conv_w, a_log, dt_bias,
           norm_gdn_out, w_pool, pool_scale, w_out, norm_post_mix, norm_pre_mlp, w_up, w_down, norm_post_mlp):
    batch, seq, d_model = x_prompt.shape
    dec_batch, dec_seq, _ = x_sample.shape
    assert dec_seq == 1
    depth = w_in.shape[0]
    n_heads = a_log.shape[1]
    d_gdn = n_heads * HEAD_DIM
    d_qkv = 3 * d_gdn
    d_pool = pool_scale.shape[1]
    o_gate = d_qkv + d_gdn

    yp = x_prompt.reshape(batch * seq, d_model)
    ys = x_sample.reshape(dec_batch, d_model)
    outs = [[] for _ in range(6)]
    for l in range(depth):
        alog = _pad_lanes(a_log[l])
        dtb = _pad_lanes(dt_bias[l])
        gnw = norm_gdn_out[l].reshape(1, HEAD_DIM)
        npm = norm_pre_mix[l].reshape(1, d_model)
        npo = norm_post_mix[l].reshape(1, d_model)
        nm1 = norm_pre_mlp[l].reshape(1, d_model)
        nm2 = norm_post_mlp[l].reshape(1, d_model)
        ps = pool_scale[l].reshape(1, d_pool)

        proj_s, u_s, gates_s, *w_in_b = _in_proj(
            ys, npm, jnp.swapaxes(w_in[l], 0, 1), o_gate, d_pool, alog, dtb, n_heads,
            tm=dec_batch, tn=512, emit=True)

        proj_p, u_p, gates_p = _in_proj(
            yp, npm, w_in_b, o_gate, d_pool, alog, dtb, n_heads, tm=1024, tn=512, emit=False)
        to_cast = (w_up[l], w_down[l], w_out[l], w_pool[l].reshape(-1, w_pool.shape[-1]))
        oa_p, s_p, w_up_b, w_down_b, w_out_b, w_pool_b = _gdn_prompt(
            proj_p, gates_p, conv_w[l], gnw, to_cast, batch, seq, n_heads, hs=8, rows=256)
        w_pool_b = w_pool_b.reshape(w_pool.shape[1:])
        (x1_p,) = _mix_out(oa_p, u_p, u_p, yp, w_pool_b, ps, w_out_b, npo, tm=512, seq=seq, pos0=0)
        yp = _mlp(x1_p, nm1, w_up_b, w_down_b, nm2, tm=512, tf=1024)
        outs[0].append(s_p)
        outs[1].append(proj_p.reshape(batch, seq, -1)[:, seq - (CONV_W - 1):, :d_qkv])
        outs[2].append(u_p.reshape(batch, seq, -1)[:, seq - POOL_BUF:])

        oa_s, s_s, cs_new = _gdn_sample(
            proj_s, jnp.swapaxes(state_conv[l], 0, 1), gates_s, conv_w[l], gnw, state_gdn[l], n_heads, tb=8)
        x1_s, hist_new = _mix_out(oa_s, u_s, jnp.swapaxes(state_pool[l], 0, 1), ys, w_pool_b, ps, w_out_b, npo,
                                  tm=dec_batch, seq=1, pos0=PAST_LEN)
        ys = _mlp(x1_s, nm1, w_up_b, w_down_b, nm2, tm=dec_batch, tf=1024)
        outs[3].append(s_s)
        outs[4].append(jnp.swapaxes(cs_new, 0, 1))
        outs[5].append(jnp.swapaxes(hist_new, 0, 1))

    return (yp.reshape(batch, seq, d_model), ys.reshape(dec_batch, dec_seq, d_model),
            *[jnp.stack(o) for o in outs])
```

```python
import functools

import jax
import jax.numpy as jnp
from jax import lax
from jax.experimental import pallas as pl
from jax.experimental.pallas import tpu as pltpu

F32 = jnp.float32
BF16 = jnp.bfloat16

EPS = 1e-6
HEAD_DIM = 128
CONV_W = 4
POOL_WINDOWS = (2, 4, 8, 16)
POOL_BUF = max(POOL_WINDOWS) - 1
PAST_LEN = 16384
HALO_CONV = 8
HALO_POOL = 16
GDN_CHUNK = 128
VMEM_LIMIT = 56 * 1024 * 1024


def _cparams(n_axes):
    return pltpu.CompilerParams(
        dimension_semantics=("arbitrary",) * n_axes, vmem_limit_bytes=VMEM_LIMIT)


def _dot(a, b):
    return jnp.dot(a.astype(BF16), b.astype(BF16), preferred_element_type=F32)


def _dot_nt(a, b):
    return lax.dot_general(a.astype(BF16), b.astype(BF16), (((1,), (1,)), ((), ())),
                           preferred_element_type=F32)


def _rms(x, w):
    return x * lax.rsqrt(jnp.mean(x * x, axis=-1, keepdims=True) + EPS) * w


def _silu(x):
    return x * (0.5 * jnp.tanh(0.5 * x) + 0.5)


def _softplus(x):
    return jnp.maximum(x, 0.0) + jnp.log(1.0 + jnp.exp(-jnp.abs(x)))


def _l2norm(x):
    return x * lax.rsqrt(jnp.sum(x * x, axis=-1, keepdims=True) + EPS)


def _inproj_kernel(x_ref, nw_ref, wa_ref, wt_ref, wc_ref, alog_ref, dtb_ref, *rest, n_heads, n_main, emit):
    if emit:
        out_ref, u_ref, gates_ref, wa_out, wt_out, wc_out, h_ref = rest
    else:
        out_ref, u_ref, gates_ref, h_ref = rest
    j = pl.program_id(1)

    @pl.when(j == 0)
    def _():
        h_ref[...] = _rms(x_ref[...], nw_ref[...]).astype(BF16)

    @pl.when(j < n_main)
    def _():
        w = wa_ref[...].T.astype(BF16) if emit else wa_ref[...]
        if emit:
            wa_out[...] = w
        out_ref[...] = jnp.dot(h_ref[...], w, preferred_element_type=F32)

    @pl.when(j == n_main)
    def _():
        if emit:
            n_gate = 2 * n_heads
            blk = wt_ref[...]
            pool_rows = jnp.concatenate([blk[n_gate:], wc_ref[...]], axis=0)
            gate_rows = jnp.concatenate([blk[:n_gate], jnp.zeros((HEAD_DIM - n_gate, blk.shape[1]), F32)], axis=0)
            wt = pool_rows.T.astype(BF16)
            wc = gate_rows.T.astype(BF16)
            wt_out[...] = wt
            wc_out[...] = wc
        else:
            wt = wt_ref[...]
            wc = wc_ref[...]
        h = h_ref[...]
        u_ref[...] = jnp.dot(h, wt, preferred_element_type=F32)
        ab = jnp.dot(h, wc, preferred_element_type=F32)
        lane = lax.broadcasted_iota(jnp.int32, ab.shape, 1)
        g = -jnp.exp(alog_ref[...]) * _softplus(ab + dtb_ref[...])
        gates_ref[...] = jnp.where(lane < n_heads, g, jax.nn.sigmoid(ab))


def _in_proj(x, norm_w, weights, n, d_pool, alog, dtb, n_heads, tm, tn, emit):
    m, d = x.shape
    n_main = n // tn
    n_gate = 2 * n_heads
    vec = pl.BlockSpec((1, HEAD_DIM), lambda i, j: (0, 0))
    out_specs = [
        pl.BlockSpec((tm, tn), lambda i, j: (i, jnp.minimum(j, n_main - 1))),
        pl.BlockSpec((tm, d_pool), lambda i, j: (i, 0)),
        pl.BlockSpec((tm, HEAD_DIM), lambda i, j: (i, 0)),
    ]
    out_shape = [jax.ShapeDtypeStruct((m, n), F32), jax.ShapeDtypeStruct((m, d_pool), F32),
                 jax.ShapeDtypeStruct((m, HEAD_DIM), F32)]
    wa_bf_spec = pl.BlockSpec((d, tn), lambda i, j: (0, jnp.minimum(j, n_main - 1)))
    wt_bf_spec = pl.BlockSpec((d, d_pool), lambda i, j: (0, 0), pipeline_mode=pl.Buffered(1))
    wc_bf_spec = pl.BlockSpec((d, HEAD_DIM), lambda i, j: (0, 0))
    if emit:
        assert n % d_pool == 0 and (n + d_pool) % n_gate == 0 and n_gate % 8 == 0
        operands = (weights, weights, weights)
        w_specs = [
            pl.BlockSpec((tn, d), lambda i, j: (jnp.minimum(j, n_main - 1), 0)),
            pl.BlockSpec((d_pool, d), lambda i, j: (n // d_pool, 0), pipeline_mode=pl.Buffered(1)),
            pl.BlockSpec((n_gate, d), lambda i, j: ((n + d_pool) // n_gate, 0)),
        ]
        out_specs += [wa_bf_spec, pl.BlockSpec((d, d_pool), lambda i, j: (0, 0)), wc_bf_spec]
        out_shape += [jax.ShapeDtypeStruct((d, n), BF16), jax.ShapeDtypeStruct((d, d_pool), BF16),
                      jax.ShapeDtypeStruct((d, HEAD_DIM), BF16)]
    else:
        operands = weights
        w_specs = [wa_bf_spec, wt_bf_spec, wc_bf_spec]
    return pl.pallas_call(
        functools.partial(_inproj_kernel, n_heads=n_heads, n_main=n_main, emit=emit),
        grid=(m // tm, n_main + 1),
        in_specs=[
            pl.BlockSpec((tm, d), lambda i, j: (i, 0)),
            pl.BlockSpec((1, d), lambda i, j: (0, 0)),
            *w_specs,
            vec, vec,
        ],
        out_specs=out_specs,
        out_shape=out_shape,
        scratch_shapes=[pltpu.VMEM((tm, d), BF16)],
        compiler_params=_cparams(2),
        name="in_proj",
    )(x, norm_w, *operands, alog, dtb)


def _lane_column(x, lane_idx):
    lane = lax.broadcasted_iota(jnp.int32, x.shape, 1)
    col = jnp.sum(jnp.where(lane == lane_idx, x, 0.0), axis=1, keepdims=True)
    return jnp.broadcast_to(col, x.shape)


def _unit_lower_inverses(lms):
    n = lms[0].shape[0]
    row = lax.broadcasted_iota(jnp.int32, (n, n), 0)
    col = lax.broadcasted_iota(jnp.int32, (n, n), 1)
    eye = jnp.where(row == col, 1.0, 0.0)
    nks = [-lm for lm in lms]
    qs = [eye + nk for nk in nks]
    nks = [_dot(nk, nk) for nk in nks]
    power = 2
    while 2 * power < n:
        res = [_dot(nk, jnp.concatenate([nk, q], axis=1)) for nk, q in zip(nks, qs)]
        nks = [r[:, :n] for r in res]
        qs = [q + r[:, n:] for q, r in zip(qs, res)]
        power *= 2
    return [q + _dot(nk, q) for nk, q in zip(nks, qs)]


def _gdn_prompt_kernel(q_ref, k_ref, v_ref, gate_ref, hq_ref, hk_ref, hv_ref, gates_ref,
                       cwq_ref, cwk_ref, cwv_ref, gnw_ref, *rest, n_heads, hs, rows, n_cast):
    cast_in = rest[:n_cast]
    o_ref, sout_ref = rest[n_cast:n_cast + 2]
    cast_out = rest[n_cast + 2:2 * n_cast + 2]
    s_ref, ext_ref = rest[2 * n_cast + 2:]
    hg = pl.program_id(1)
    blk = pl.program_id(2)
    c = GDN_CHUNK

    for src, dst in zip(cast_in, cast_out):
        dst[...] = src[...].astype(dst.dtype)

    @pl.when(blk == 0)
    def _():
        s_ref[...] = jnp.zeros_like(s_ref)

    for a, (x_ref, halo_ref) in enumerate(((q_ref, hq_ref), (k_ref, hk_ref), (v_ref, hv_ref))):
        for hi in range(hs):
            lanes = slice(hi * HEAD_DIM, (hi + 1) * HEAD_DIM)
            ext_ref[a * hs + hi, 0:HALO_CONV, :] = jnp.where(blk > 0, halo_ref[:, lanes], 0.0)
            ext_ref[a * hs + hi, HALO_CONV:, :] = x_ref[:, lanes]

    def conv_silu(a, cw_ref, r0, hi):
        lanes = slice(hi * HEAD_DIM, (hi + 1) * HEAD_DIM)
        acc = None
        for j in range(CONV_W):
            term = cw_ref[j:j + 1, lanes] * ext_ref[a * hs + hi, pl.ds(HALO_CONV - (CONV_W - 1) + j + r0, c), :]
            acc = term if acc is None else acc + term
        return _silu(acc)

    row = lax.broadcasted_iota(jnp.int32, (c, c), 0)
    col = lax.broadcasted_iota(jnp.int32, (c, c), 1)
    tril = row >= col
    strict = row > col
    tril_f = jnp.where(tril, 1.0, 0.0)
    gnw = gnw_ref[...]
    heads = range(hs)

    s = [s_ref[hi] for hi in heads]
    for ci in range(rows // c):
        r0 = ci * c
        gates = gates_ref[r0:r0 + c, :]
        gcum = jnp.dot(tril_f, gates, preferred_element_type=F32, precision=lax.Precision.HIGHEST)
        qc = [_l2norm(conv_silu(0, cwq_ref, r0, hi)) * (HEAD_DIM ** -0.5) for hi in heads]
        kc = [_l2norm(conv_silu(1, cwk_ref, r0, hi)) for hi in heads]
        vc = [conv_silu(2, cwv_ref, r0, hi) for hi in heads]
        gc = [_lane_column(gcum, hg * hs + hi) for hi in heads]
        bc = [_lane_column(gates, hg * hs + hi + n_heads) for hi in heads]
        decay = [jnp.exp(jnp.where(tril, g - g.T, -1e30)) for g in gc]
        egc = [jnp.exp(g) for g in gc]
        g_last = [g[c - 1:c, :] for g in gc]
        kk = [_dot_nt(k, k) for k in kc]
        qk = [_dot_nt(q, k) for q, k in zip(qc, kc)]
        lm = [jnp.where(strict, kk[hi] * bc[hi] * decay[hi], 0.0) for hi in heads]
        t_inv = _unit_lower_inverses(lm)
        uw = [_dot(t_inv[hi], jnp.concatenate([vc[hi] * bc[hi], kc[hi] * (bc[hi] * egc[hi])], axis=1))
              for hi in heads]
        intra = [jnp.where(tril, qk[hi] * decay[hi], 0.0) for hi in heads]
        kdt = [(kc[hi] * jnp.exp(g_last[hi] - gc[hi])).T for hi in heads]
        r1 = [_dot(jnp.concatenate([uw[hi][:, HEAD_DIM:], qc[hi] * egc[hi]], axis=0), s[hi]) for hi in heads]
        v_new = [uw[hi][:, :HEAD_DIM] - r1[hi][:c] for hi in heads]
        r2 = [_dot(jnp.concatenate([intra[hi], kdt[hi]], axis=0), v_new[hi]) for hi in heads]
        s = [s[hi] * jnp.exp(g_last[hi]) + r2[hi][c:] for hi in heads]
        for hi in heads:
            lanes = slice(hi * HEAD_DIM, (hi + 1) * HEAD_DIM)
            o = _rms(r1[hi][c:] + r2[hi][:c], gnw) * _silu(gate_ref[r0:r0 + c, lanes])
            o_ref[r0:r0 + c, lanes] = o.astype(o_ref.dtype)

    for hi in range(hs):
        s_ref[hi] = s[hi]

    @pl.when(blk == pl.num_programs(2) - 1)
    def _():
        for hi in range(hs):
            sout_ref[0, hi] = s[hi]


def _gdn_prompt(proj, gates, conv_w, gnw, to_cast, batch, seq, n_heads, hs, rows):
    nblk = seq // rows
    hb = rows // HALO_CONV
    ng = n_heads // hs
    width = hs * HEAD_DIM
    steps = batch * ng * nblk

    def slab(w):
        return pl.BlockSpec((w.shape[0] // steps, w.shape[1]), lambda b, h, k: ((b * ng + h) * nblk + k, 0))

    def main(off):
        return pl.BlockSpec((rows, width), lambda b, h, k: (b * nblk + k, off * ng + h))

    def halo(off):
        return pl.BlockSpec(
            (HALO_CONV, width), lambda b, h, k: (jnp.maximum((b * nblk + k) * hb - 1, 0), off * ng + h))

    def cw(off):
        return pl.BlockSpec((CONV_W, width), lambda b, h, k: (0, off * ng + h))

    kern = functools.partial(_gdn_prompt_kernel, n_heads=n_heads, hs=hs, rows=rows, n_cast=len(to_cast))
    return pl.pallas_call(
        kern,
        grid=(batch, ng, nblk),
        in_specs=[main(0), main(1), main(2), main(3), halo(0), halo(1), halo(2),
                  pl.BlockSpec((rows, HEAD_DIM), lambda b, h, k: (b * nblk + k, 0)),
                  cw(0), cw(1), cw(2),
                  pl.BlockSpec((1, HEAD_DIM), lambda b, h, k: (0, 0)),
                  *[slab(w) for w in to_cast]],
        out_specs=[
            pl.BlockSpec((rows, width), lambda b, h, k: (b * nblk + k, h)),
            pl.BlockSpec((1, hs, HEAD_DIM, HEAD_DIM), lambda b, h, k: (b, h, 0, 0)),
            *[slab(w) for w in to_cast],
        ],
        out_shape=[
            jax.ShapeDtypeStruct((batch * seq, n_heads * HEAD_DIM), BF16),
            jax.ShapeDtypeStruct((batch, n_heads, HEAD_DIM, HEAD_DIM), F32),
            *[jax.ShapeDtypeStruct(w.shape, BF16) for w in to_cast],
        ],
        scratch_shapes=[pltpu.VMEM((hs, HEAD_DIM, HEAD_DIM), F32),
                        pltpu.VMEM((3 * hs, rows + HALO_CONV, HEAD_DIM), F32)],
        compiler_params=_cparams(3),
        name="gdn_prompt",
    )(proj, proj, proj, proj, proj, proj, proj, gates, conv_w, conv_w, conv_w, gnw, *to_cast)


def _gdn_sample_kernel(x_ref, cs_ref, gates_ref, cw_ref, gnw_ref, s_in_ref,
                       o_ref, s_out_ref, cs_out_ref, o_scr, *, n_heads, tb):
    d = n_heads * HEAD_DIM
    x = x_ref[...]
    cw = cw_ref[...]
    conv = cw[CONV_W - 1:CONV_W, :] * x[:, :3 * d]
    for j in range(CONV_W - 1):
        conv = conv + cw[j:j + 1, :] * cs_ref[j]
    qkv = _silu(conv)
    for j in range(CONV_W - 2):
        cs_out_ref[j] = cs_ref[j + 1]
    cs_out_ref[CONV_W - 2] = x[:, :3 * d]

    gates = gates_ref[...]
    eg_all = jnp.exp(gates)
    pad = jnp.zeros((HEAD_DIM - tb, HEAD_DIM), F32)

    for h in range(n_heads):
        qh = _l2norm(qkv[:, h * HEAD_DIM:(h + 1) * HEAD_DIM]) * (HEAD_DIM ** -0.5)
        kh = _l2norm(qkv[:, d + h * HEAD_DIM:d + (h + 1) * HEAD_DIM])
        vh = qkv[:, 2 * d + h * HEAD_DIM:2 * d + (h + 1) * HEAD_DIM]
        kt = jnp.concatenate([kh, pad], axis=0).T
        toks = range(tb)
        kb = [jnp.broadcast_to(kt[:, t:t + 1], (HEAD_DIM, HEAD_DIM)) for t in toks]
        s1 = [s_in_ref[t, h] * eg_all[t:t + 1, h:h + 1] for t in toks]
        ks = [_dot(kh, s1[t])[t:t + 1, :] for t in toks]
        delta = [(vh[t:t + 1, :] - ks[t]) * gates[t:t + 1, n_heads + h:n_heads + h + 1] for t in toks]
        s2 = [s1[t] + kb[t] * delta[t] for t in toks]
        for t in toks:
            s_out_ref[t, h] = s2[t]
            o_scr[t:t + 1, h * HEAD_DIM:(h + 1) * HEAD_DIM] = _dot(qh, s2[t])[t:t + 1, :]

    gnw = gnw_ref[...]
    for h in range(n_heads):
        sl = slice(h * HEAD_DIM, (h + 1) * HEAD_DIM)
        o = _rms(o_scr[:, sl], gnw) * _silu(x[:, 3 * d + h * HEAD_DIM:3 * d + (h + 1) * HEAD_DIM])
        o_ref[:, sl] = o.astype(o_ref.dtype)


def _gdn_sample(proj, conv_state, gates, conv_w, gnw, state, n_heads, tb):
    m = proj.shape[0]
    d = n_heads * HEAD_DIM
    cs_spec = pl.BlockSpec((CONV_W - 1, tb, 3 * d), lambda i: (0, i, 0))
    kern = functools.partial(_gdn_sample_kernel, n_heads=n_heads, tb=tb)
    return pl.pallas_call(
        kern,
        grid=(m // tb,),
        in_specs=[
            pl.BlockSpec((tb, 4 * d), lambda i: (i, 0)),
            cs_spec,
            pl.BlockSpec((tb, HEAD_DIM), lambda i: (i, 0)),
            pl.BlockSpec((CONV_W, 3 * d), lambda i: (0, 0)),
            pl.BlockSpec((1, HEAD_DIM), lambda i: (0, 0)),
            pl.BlockSpec((tb, n_heads, HEAD_DIM, HEAD_DIM), lambda i: (i, 0, 0, 0)),
        ],
        out_specs=[
            pl.BlockSpec((tb, d), lambda i: (i, 0)),
            pl.BlockSpec((tb, n_heads, HEAD_DIM, HEAD_DIM), lambda i: (i, 0, 0, 0)),
            cs_spec,
        ],
        out_shape=[
            jax.ShapeDtypeStruct((m, d), BF16),
            jax.ShapeDtypeStruct(state.shape, F32),
            jax.ShapeDtypeStruct(conv_state.shape, F32),
        ],
        scratch_shapes=[pltpu.VMEM((tb, d), F32)],
        compiler_params=_cparams(1),
        name="gdn_sample",
    )(proj, conv_state, gates, conv_w, gnw, state)


def _mix_out_kernel(oa_ref, u_ref, hist_ref, x_ref, wp_ref, ps_ref, wo_ref, nw_ref, out_ref, *rest,
                    tm, seq, pos0):
    d_pool = u_ref.shape[1]
    d_gdn = oa_ref.shape[1]
    d = out_ref.shape[1]
    n_groups = len(POOL_WINDOWS)
    gd = d_pool // n_groups
    strips = gd // HEAD_DIM
    if seq > 1:
        (ext_ref,) = rest
        start = (pl.program_id(0) * tm) % seq
        for si in range(d_pool // HEAD_DIM):
            lanes = slice(si * HEAD_DIM, (si + 1) * HEAD_DIM)
            ext_ref[si, 0:HALO_POOL, :] = jnp.where(start > 0, hist_ref[:, lanes], 0.0)
            ext_ref[si, HALO_POOL:, :] = u_ref[:, lanes]
        pos = pos0 + start + lax.broadcasted_iota(jnp.int32, (tm, HEAD_DIM), 0)
    else:
        (hist_out_ref,) = rest
        for r in range(POOL_BUF - 1):
            hist_out_ref[r] = hist_ref[r + 1]
        hist_out_ref[POOL_BUF - 1] = u_ref[...]
        pos = jnp.full((tm, HEAD_DIM), pos0, jnp.int32)

    tn = d // n_groups
    oa = oa_ref[...]
    parts = []
    for gi, win in enumerate(POOL_WINDOWS):
        cols = slice(gi * tn, (gi + 1) * tn)
        out_ref[:, cols] = jnp.dot(oa, wo_ref[:d_gdn, cols], preferred_element_type=F32)
        cnt = jnp.minimum(pos + 1, win).astype(F32)
        pooled = []
        for si in range(gi * strips, (gi + 1) * strips):
            lanes = slice(si * HEAD_DIM, (si + 1) * HEAD_DIM)
            cur = u_ref[:, lanes]
            wsum = cur
            for r in range(1, win):
                if seq > 1:
                    wsum = wsum + ext_ref[si, pl.ds(HALO_POOL - r, tm), :]
                else:
                    wsum = wsum + hist_ref[POOL_BUF - r, :, lanes]
            pooled.append(wsum / cnt - cur)
        ob = _dot(jnp.concatenate(pooled, axis=1), wp_ref[gi]) * ps_ref[:, gi * gd:(gi + 1) * gd]
        parts.append(ob.astype(BF16))
    ob_all = jnp.concatenate(parts, axis=1)
    mix = [out_ref[:, ci * tn:(ci + 1) * tn]
           + jnp.dot(ob_all, wo_ref[d_gdn:, ci * tn:(ci + 1) * tn], preferred_element_type=F32)
           for ci in range(n_groups)]
    ss = sum(jnp.sum(m * m, axis=-1, keepdims=True) for m in mix)
    inv = lax.rsqrt(ss * (1.0 / d) + EPS)
    for ci in range(n_groups):
        cols = slice(ci * tn, (ci + 1) * tn)
        out_ref[:, cols] = x_ref[:, cols] + mix[ci] * inv * nw_ref[:, cols]


def _mix_out(o_a, u, hist, x, w_pool, pool_scale, w_out, norm_w, tm, seq, pos0):
    m, d = x.shape
    d_pool = pool_scale.shape[1]
    d_gdn = o_a.shape[1]
    out_specs = [pl.BlockSpec((tm, d), lambda i: (i, 0))]
    out_shape = [jax.ShapeDtypeStruct((m, d), F32)]
    if seq > 1:
        hb = tm // HALO_POOL
        hist_spec = pl.BlockSpec((HALO_POOL, d_pool), lambda i: (jnp.maximum(i * hb - 1, 0), 0))
        scratch = [pltpu.VMEM((d_pool // HEAD_DIM, tm + HALO_POOL, HEAD_DIM), F32)]
    else:
        hist_spec = pl.BlockSpec((POOL_BUF, tm, d_pool), lambda i: (0, i, 0))
        out_specs.append(hist_spec)
        out_shape.append(jax.ShapeDtypeStruct(hist.shape, F32))
        scratch = []
    kern = functools.partial(_mix_out_kernel, tm=tm, seq=seq, pos0=pos0)
    return pl.pallas_call(
        kern,
        grid=(m // tm,),
        in_specs=[
            pl.BlockSpec((tm, d_gdn), lambda i: (i, 0)),
            pl.BlockSpec((tm, d_pool), lambda i: (i, 0)),
            hist_spec,
            pl.BlockSpec((tm, d), lambda i: (i, 0)),
            pl.BlockSpec(w_pool.shape, lambda i: (0, 0, 0), pipeline_mode=pl.Buffered(1)),
            pl.BlockSpec((1, d_pool), lambda i: (0, 0)),
            pl.BlockSpec(w_out.shape, lambda i: (0, 0), pipeline_mode=pl.Buffered(1)),
            pl.BlockSpec((1, d), lambda i: (0, 0)),
        ],
        out_specs=out_specs,
        out_shape=out_shape,
        scratch_shapes=scratch,
        compiler_params=_cparams(1),
        name="mix_out",
    )(o_a, u, hist, x, w_pool, pool_scale, w_out, norm_w)


def _mlp_kernel(x_ref, nw1_ref, wu_ref, wd_ref, nw2_ref, out_ref, h_ref):
    j = pl.program_id(1)

    @pl.when(j == 0)
    def _():
        h_ref[...] = _rms(x_ref[...], nw1_ref[...]).astype(BF16)
        out_ref[...] = jnp.zeros_like(out_ref)

    up = jnp.dot(h_ref[...], wu_ref[...], preferred_element_type=F32)
    act = jnp.square(jnp.maximum(up, 0.0)).astype(BF16)
    out_ref[...] += jnp.dot(act, wd_ref[...], preferred_element_type=F32)

    @pl.when(j == pl.num_programs(1) - 1)
    def _():
        out_ref[...] = x_ref[...] + _rms(out_ref[...], nw2_ref[...])


def _mlp(x, nw1, w_up, w_down, nw2, tm, tf):
    m, d = x.shape
    f = w_up.shape[1]
    return pl.pallas_call(
        _mlp_kernel,
        grid=(m // tm, f // tf),
        in_specs=[
            pl.BlockSpec((tm, d), lambda i, j: (i, 0)),
            pl.BlockSpec((1, d), lambda i, j: (0, 0)),
            pl.BlockSpec((d, tf), lambda i, j: (0, j)),
            pl.BlockSpec((tf, d), lambda i, j: (j, 0)),
            pl.BlockSpec((1, d), lambda i, j: (0, 0)),
        ],
        out_specs=pl.BlockSpec((tm, d), lambda i, j: (i, 0)),
        out_shape=jax.ShapeDtypeStruct((m, d), F32),
        scratch_shapes=[pltpu.VMEM((tm, d), BF16)],
        compiler_params=_cparams(2),
        name="mlp",
    )(x, nw1, w_up, w_down, nw2)


def _pad_lanes(v, n=HEAD_DIM):
    return jnp.pad(v.reshape(1, -1), ((0, 0), (0, n - v.shape[-1])))


def kernel(x_prompt, x_sample, state_gdn, state_conv, state_pool, norm_pre_mix, w_in, conv_w, a_log, dt_bias,
           norm_gdn_out, w_pool, pool_scale, w_out, norm_post_mix, norm_pre_mlp, w_up, w_down, norm_post_mlp):
    batch, seq, d_model = x_prompt.shape
    dec_batch, dec_seq, _ = x_sample.shape
    assert dec_seq == 1
    depth = w_in.shape[0]
    n_heads = a_log.shape[1]
    d_gdn = n_heads * HEAD_DIM
    d_qkv = 3 * d_gdn
    d_pool = pool_scale.shape[1]
    o_gate = d_qkv + d_gdn

    yp = x_prompt.reshape(batch * seq, d_model)
    ys = x_sample.reshape(dec_batch, d_model)
    outs = [[] for _ in range(6)]
    for l in range(depth):
        alog = _pad_lanes(a_log[l])
        dtb = _pad_lanes(dt_bias[l])
        gnw = norm_gdn_out[l].reshape(1, HEAD_DIM)
        npm = norm_pre_mix[l].reshape(1, d_model)
        npo = norm_post_mix[l].reshape(1, d_model)
        nm1 = norm_pre_mlp[l].reshape(1, d_model)
        nm2 = norm_post_mlp[l].reshape(1, d_model)
        ps = pool_scale[l].reshape(1, d_pool)

        proj_s, u_s, gates_s, *w_in_b = _in_proj(
            ys, npm, jnp.swapaxes(w_in[l], 0, 1), o_gate, d_pool, alog, dtb, n_heads,
            tm=dec_batch, tn=512, emit=True)

        proj_p, u_p, gates_p = _in_proj(
            yp, npm, w_in_b, o_gate, d_pool, alog, dtb, n_heads, tm=1024, tn=1024, emit=False)
        to_cast = (w_up[l], w_down[l], w_out[l], w_pool[l].reshape(-1, w_pool.shape[-1]))
        oa_p, s_p, w_up_b, w_down_b, w_out_b, w_pool_b = _gdn_prompt(
            proj_p, gates_p, conv_w[l], gnw, to_cast, batch, seq, n_heads, hs=8, rows=256)
        w_pool_b = w_pool_b.reshape(w_pool.shape[1:])
        (x1_p,) = _mix_out(oa_p, u_p, u_p, yp, w_pool_b, ps, w_out_b, npo, tm=512, seq=seq, pos0=0)
        yp = _mlp(x1_p, nm1, w_up_b, w_down_b, nm2, tm=512, tf=1024)
        outs[0].append(s_p)
        outs[1].append(proj_p.reshape(batch, seq, -1)[:, seq - (CONV_W - 1):, :d_qkv])
        outs[2].append(u_p.reshape(batch, seq, -1)[:, seq - POOL_BUF:])

        oa_s, s_s, cs_new = _gdn_sample(
            proj_s, jnp.swapaxes(state_conv[l], 0, 1), gates_s, conv_w[l], gnw, state_gdn[l], n_heads, tb=8)
        x1_s, hist_new = _mix_out(oa_s, u_s, jnp.swapaxes(state_pool[l], 0, 1), ys, w_pool_b, ps, w_out_b, npo,
                                  tm=dec_batch, seq=1, pos0=PAST_LEN)
        ys = _mlp(x1_s, nm1, w_up_b, w_down_b, nm2, tm=dec_batch, tf=1024)
        outs[3].append(s_s)
        outs[4].append(jnp.swapaxes(cs_new, 0, 1))
        outs[5].append(jnp.swapaxes(hist_new, 0, 1))

    return (yp.reshape(batch, seq, d_model), ys.reshape(dec_batch, dec_seq, d_model),
            *[jnp.stack(o) for o in outs])
```

```python
import functools

import jax
import jax.numpy as jnp
from jax import lax
from jax.experimental import pallas as pl
from jax.experimental.pallas import tpu as pltpu

F32 = jnp.float32
BF16 = jnp.bfloat16

EPS = 1e-6
HEAD_DIM = 128
CONV_W = 4
POOL_WINDOWS = (2, 4, 8, 16)
POOL_BUF = max(POOL_WINDOWS) - 1
PAST_LEN = 16384
HALO_CONV = 8
HALO_POOL = 16
GDN_CHUNK = 128
VMEM_LIMIT = 56 * 1024 * 1024


def _cparams(n_axes):
    return pltpu.CompilerParams(
        dimension_semantics=("arbitrary",) * n_axes, vmem_limit_bytes=VMEM_LIMIT)


def _dot(a, b):
    return jnp.dot(a.astype(BF16), b.astype(BF16), preferred_element_type=F32)


def _dot_nt(a, b):
    return lax.dot_general(a.astype(BF16), b.astype(BF16), (((1,), (1,)), ((), ())),
                           preferred_element_type=F32)


def _rms(x, w):
    return x * lax.rsqrt(jnp.mean(x * x, axis=-1, keepdims=True) + EPS) * w


def _silu(x):
    return x * (0.5 * jnp.tanh(0.5 * x) + 0.5)


def _softplus(x):
    return jnp.maximum(x, 0.0) + jnp.log(1.0 + jnp.exp(-jnp.abs(x)))


def _l2norm(x):
    return x * lax.rsqrt(jnp.sum(x * x, axis=-1, keepdims=True) + EPS)


def _inproj_kernel(x_ref, nw_ref, wa_ref, wt_ref, wc_ref, alog_ref, dtb_ref, *rest, n_heads, n_main, emit):
    if emit:
        out_ref, u_ref, gates_ref, wa_out, wt_out, wc_out, h_ref = rest
    else:
        out_ref, u_ref, gates_ref, h_ref = rest
    j = pl.program_id(1)

    @pl.when(j == 0)
    def _():
        h_ref[...] = _rms(x_ref[...], nw_ref[...]).astype(BF16)

    @pl.when(j < n_main)
    def _():
        w = wa_ref[...].T.astype(BF16) if emit else wa_ref[...]
        if emit:
            wa_out[...] = w
        out_ref[...] = jnp.dot(h_ref[...], w, preferred_element_type=F32)

    @pl.when(j == n_main)
    def _():
        if emit:
            n_gate = 2 * n_heads
            blk = wt_ref[...]
            pool_rows = jnp.concatenate([blk[n_gate:], wc_ref[...]], axis=0)
            gate_rows = jnp.concatenate([blk[:n_gate], jnp.zeros((HEAD_DIM - n_gate, blk.shape[1]), F32)], axis=0)
            wt = pool_rows.T.astype(BF16)
            wc = gate_rows.T.astype(BF16)
            wt_out[...] = wt
            wc_out[...] = wc
        else:
            wt = wt_ref[...]
            wc = wc_ref[...]
        h = h_ref[...]
        u_ref[...] = jnp.dot(h, wt, preferred_element_type=F32)
        ab = jnp.dot(h, wc, preferred_element_type=F32)
        lane = lax.broadcasted_iota(jnp.int32, ab.shape, 1)
        g = -jnp.exp(alog_ref[...]) * _softplus(ab + dtb_ref[...])
        gates_ref[...] = jnp.where(lane < n_heads, g, jax.nn.sigmoid(ab))


def _in_proj(x, norm_w, weights, n, d_pool, alog, dtb, n_heads, tm, tn, emit):
    m, d = x.shape
    n_main = n // tn
    n_gate = 2 * n_heads
    vec = pl.BlockSpec((1, HEAD_DIM), lambda i, j: (0, 0))
    out_specs = [
        pl.BlockSpec((tm, tn), lambda i, j: (i, jnp.minimum(j, n_main - 1))),
        pl.BlockSpec((tm, d_pool), lambda i, j: (i, 0)),
        pl.BlockSpec((tm, HEAD_DIM), lambda i, j: (i, 0)),
    ]
    out_shape = [jax.ShapeDtypeStruct((m, n), F32), jax.ShapeDtypeStruct((m, d_pool), F32),
                 jax.ShapeDtypeStruct((m, HEAD_DIM), F32)]
    wa_bf_spec = pl.BlockSpec((d, tn), lambda i, j: (0, jnp.minimum(j, n_main - 1)))
    wt_bf_spec = pl.BlockSpec((d, d_pool), lambda i, j: (0, 0), pipeline_mode=pl.Buffered(1))
    wc_bf_spec = pl.BlockSpec((d, HEAD_DIM), lambda i, j: (0, 0))
    if emit:
        assert n % d_pool == 0 and (n + d_pool) % n_gate == 0 and n_gate % 8 == 0
        operands = (weights, weights, weights)
        w_specs = [
            pl.BlockSpec((tn, d), lambda i, j: (jnp.minimum(j, n_main - 1), 0)),
            pl.BlockSpec((d_pool, d), lambda i, j: (n // d_pool, 0), pipeline_mode=pl.Buffered(1)),
            pl.BlockSpec((n_gate, d), lambda i, j: ((n + d_pool) // n_gate, 0)),
        ]
        out_specs += [wa_bf_spec, pl.BlockSpec((d, d_pool), lambda i, j: (0, 0)), wc_bf_spec]
        out_shape += [jax.ShapeDtypeStruct((d, n), BF16), jax.ShapeDtypeStruct((d, d_pool), BF16),
                      jax.ShapeDtypeStruct((d, HEAD_DIM), BF16)]
    else:
        operands = weights
        w_specs = [wa_bf_spec, wt_bf_spec, wc_bf_spec]
    return pl.pallas_call(
        functools.partial(_inproj_kernel, n_heads=n_heads, n_main=n_main, emit=emit),
        grid=(m // tm, n_main + 1),
        in_specs=[
            pl.BlockSpec((tm, d), lambda i, j: (i, 0)),
            pl.BlockSpec((1, d), lambda i, j: (0, 0)),
            *w_specs,
            vec, vec,
        ],
        out_specs=out_specs,
        out_shape=out_shape,
        scratch_shapes=[pltpu.VMEM((tm, d), BF16)],
        compiler_params=_cparams(2),
        name="in_proj",
    )(x, norm_w, *operands, alog, dtb)


def _lane_column(x, lane_idx):
    lane = lax.broadcasted_iota(jnp.int32, x.shape, 1)
    col = jnp.sum(jnp.where(lane == lane_idx, x, 0.0), axis=1, keepdims=True)
    return jnp.broadcast_to(col, x.shape)


def _unit_lower_inverses(lms):
    n = lms[0].shape[0]
    row = lax.broadcasted_iota(jnp.int32, (n, n), 0)
    col = lax.broadcasted_iota(jnp.int32, (n, n), 1)
    eye = jnp.where(row == col, 1.0, 0.0)
    nks = [-lm for lm in lms]
    qs = [eye + nk for nk in nks]
    nks = [_dot(nk, nk) for nk in nks]
    power = 2
    while 2 * power < n:
        res = [_dot(nk, jnp.concatenate([nk, q], axis=1)) for nk, q in zip(nks, qs)]
        nks = [r[:, :n] for r in res]
        qs = [q + r[:, n:] for q, r in zip(qs, res)]
        power *= 2
    return [q + _dot(nk, q) for nk, q in zip(nks, qs)]


def _gdn_prompt_kernel(q_ref, k_ref, v_ref, gate_ref, hq_ref, hk_ref, hv_ref, gates_ref,
                       cwq_ref, cwk_ref, cwv_ref, gnw_ref, *rest, n_heads, hs, rows, n_cast):
    cast_in = rest[:n_cast]
    o_ref, sout_ref = rest[n_cast:n_cast + 2]
    cast_out = rest[n_cast + 2:2 * n_cast + 2]
    s_ref, ext_ref = rest[2 * n_cast + 2:]
    hg = pl.program_id(1)
    blk = pl.program_id(2)
    c = GDN_CHUNK

    for src, dst in zip(cast_in, cast_out):
        dst[...] = src[...].astype(dst.dtype)

    @pl.when(blk == 0)
    def _():
        s_ref[...] = jnp.zeros_like(s_ref)

    for a, (x_ref, halo_ref) in enumerate(((q_ref, hq_ref), (k_ref, hk_ref), (v_ref, hv_ref))):
        for hi in range(hs):
            lanes = slice(hi * HEAD_DIM, (hi + 1) * HEAD_DIM)
            ext_ref[a * hs + hi, 0:HALO_CONV, :] = jnp.where(blk > 0, halo_ref[:, lanes], 0.0)
            ext_ref[a * hs + hi, HALO_CONV:, :] = x_ref[:, lanes]

    def conv_silu(a, cw_ref, p):
        ci, hi = p
        lanes = slice(hi * HEAD_DIM, (hi + 1) * HEAD_DIM)
        acc = None
        for j in range(CONV_W):
            term = (cw_ref[j:j + 1, lanes]
                    * ext_ref[a * hs + hi, pl.ds(HALO_CONV - (CONV_W - 1) + j + ci * c, c), :])
            acc = term if acc is None else acc + term
        return _silu(acc)

    row = lax.broadcasted_iota(jnp.int32, (c, c), 0)
    col = lax.broadcasted_iota(jnp.int32, (c, c), 1)
    tril = row >= col
    strict = row > col
    tril_f = jnp.where(tril, 1.0, 0.0)
    gnw = gnw_ref[...]
    heads = range(hs)

    n_chunks = rows // c
    probs = [(ci, hi) for ci in range(n_chunks) for hi in heads]
    n_p = range(len(probs))
    gates = [gates_ref[ci * c:(ci + 1) * c, :] for ci in range(n_chunks)]
    gcum = [jnp.dot(tril_f, g, preferred_element_type=F32, precision=lax.Precision.HIGHEST) for g in gates]
    qc = [_l2norm(conv_silu(0, cwq_ref, p)) * (HEAD_DIM ** -0.5) for p in probs]
    kc = [_l2norm(conv_silu(1, cwk_ref, p)) for p in probs]
    vc = [conv_silu(2, cwv_ref, p) for p in probs]
    gc = [_lane_column(gcum[ci], hg * hs + hi) for ci, hi in probs]
    bc = [_lane_column(gates[ci], hg * hs + hi + n_heads) for ci, hi in probs]
    decay = [jnp.exp(jnp.where(tril, g - g.T, -1e30)) for g in gc]
    egc = [jnp.exp(g) for g in gc]
    g_last = [g[c - 1:c, :] for g in gc]
    kk = [_dot_nt(k, k) for k in kc]
    qk = [_dot_nt(q, k) for q, k in zip(qc, kc)]
    lm = [jnp.where(strict, kk[p] * bc[p] * decay[p], 0.0) for p in n_p]
    t_inv = _unit_lower_inverses(lm)
    uw = [_dot(t_inv[p], jnp.concatenate([vc[p] * bc[p], kc[p] * (bc[p] * egc[p])], axis=1)) for p in n_p]
    intra = [jnp.where(tril, qk[p] * decay[p], 0.0) for p in n_p]
    kdt = [(kc[p] * jnp.exp(g_last[p] - gc[p])).T for p in n_p]
    wq = [jnp.concatenate([uw[p][:, HEAD_DIM:], qc[p] * egc[p]], axis=0) for p in n_p]
    ik = [jnp.concatenate([intra[p], kdt[p]], axis=0) for p in n_p]

    s = [s_ref[hi] for hi in heads]
    for ci in range(n_chunks):
        ps = [ci * hs + hi for hi in heads]
        r1 = [_dot(wq[p], s[hi]) for hi, p in zip(heads, ps)]
        v_new = [uw[p][:, :HEAD_DIM] - r1[hi][:c] for hi, p in zip(heads, ps)]
        r2 = [_dot(ik[p], v_new[hi]) for hi, p in zip(heads, ps)]
        s = [s[hi] * jnp.exp(g_last[p]) + r2[hi][c:] for hi, p in zip(heads, ps)]
        for hi in heads:
            lanes = slice(hi * HEAD_DIM, (hi + 1) * HEAD_DIM)
            o = _rms(r1[hi][c:] + r2[hi][:c], gnw) * _silu(gate_ref[ci * c:(ci + 1) * c, lanes])
            o_ref[ci * c:(ci + 1) * c, lanes] = o.astype(o_ref.dtype)

    for hi in range(hs):
        s_ref[hi] = s[hi]

    @pl.when(blk == pl.num_programs(2) - 1)
    def _():
        for hi in range(hs):
            sout_ref[0, hi] = s[hi]


def _gdn_prompt(proj, gates, conv_w, gnw, to_cast, batch, seq, n_heads, hs, rows):
    nblk = seq // rows
    hb = rows // HALO_CONV
    ng = n_heads // hs
    width = hs * HEAD_DIM
    steps = batch * ng * nblk

    def slab(w):
        return pl.BlockSpec((w.shape[0] // steps, w.shape[1]), lambda b, h, k: ((b * ng + h) * nblk + k, 0))

    def main(off):
        return pl.BlockSpec((rows, width), lambda b, h, k: (b * nblk + k, off * ng + h))

    def halo(off):
        return pl.BlockSpec(
            (HALO_CONV, width), lambda b, h, k: (jnp.maximum((b * nblk + k) * hb - 1, 0), off * ng + h))

    def cw(off):
        return pl.BlockSpec((CONV_W, width), lambda b, h, k: (0, off * ng + h))

    kern = functools.partial(_gdn_prompt_kernel, n_heads=n_heads, hs=hs, rows=rows, n_cast=len(to_cast))
    return pl.pallas_call(
        kern,
        grid=(batch, ng, nblk),
        in_specs=[main(0), main(1), main(2), main(3), halo(0), halo(1), halo(2),
                  pl.BlockSpec((rows, HEAD_DIM), lambda b, h, k: (b * nblk + k, 0)),
                  cw(0), cw(1), cw(2),
                  pl.BlockSpec((1, HEAD_DIM), lambda b, h, k: (0, 0)),
                  *[slab(w) for w in to_cast]],
        out_specs=[
            pl.BlockSpec((rows, width), lambda b, h, k: (b * nblk + k, h)),
            pl.BlockSpec((1, hs, HEAD_DIM, HEAD_DIM), lambda b, h, k: (b, h, 0, 0)),
            *[slab(w) for w in to_cast],
        ],
        out_shape=[
            jax.ShapeDtypeStruct((batch * seq, n_heads * HEAD_DIM), BF16),
            jax.ShapeDtypeStruct((batch, n_heads, HEAD_DIM, HEAD_DIM), F32),
            *[jax.ShapeDtypeStruct(w.shape, BF16) for w in to_cast],
        ],
        scratch_shapes=[pltpu.VMEM((hs, HEAD_DIM, HEAD_DIM), F32),
                        pltpu.VMEM((3 * hs, rows + HALO_CONV, HEAD_DIM), F32)],
        compiler_params=_cparams(3),
        name="gdn_prompt",
    )(proj, proj, proj, proj, proj, proj, proj, gates, conv_w, conv_w, conv_w, gnw, *to_cast)


def _gdn_token_step(r, x_ref, cs_ref, gates_ref, cw_ref, gnw_ref, s_in_ref, o_ref, s_out_ref, cs_out_ref, n_heads):
    d = n_heads * HEAD_DIM
    sq = (HEAD_DIM, HEAD_DIM)
    x = x_ref[pl.ds(r, 1), :]
    cw = cw_ref[...]
    conv = cw[CONV_W - 1:CONV_W, :] * x[:, :3 * d]
    for j in range(CONV_W - 1):
        conv = conv + cw[j:j + 1, :] * cs_ref[j, pl.ds(r, 1), :]
    qkv = _silu(conv)

    def put_row(ref, idx, value):
        old = ref[idx]
        rows = lax.broadcasted_iota(jnp.int32, old.shape, 0)
        ref[idx] = jnp.where(rows == r, jnp.broadcast_to(value, old.shape), old)

    for j in range(CONV_W - 2):
        put_row(cs_out_ref, j, cs_ref[j + 1, pl.ds(r, 1), :])
    put_row(cs_out_ref, CONV_W - 2, x[:, :3 * d])

    gates = gates_ref[pl.ds(r, 1), :]
    eg = jnp.exp(gates)
    gnw = gnw_ref[...]
    heads = range(n_heads)
    q = [_l2norm(qkv[:, h * HEAD_DIM:(h + 1) * HEAD_DIM]) * (HEAD_DIM ** -0.5) for h in heads]
    k = [_l2norm(qkv[:, d + h * HEAD_DIM:d + (h + 1) * HEAD_DIM]) for h in heads]
    v = [qkv[:, 2 * d + h * HEAD_DIM:2 * d + (h + 1) * HEAD_DIM] for h in heads]
    kb = [jnp.broadcast_to(k[h], sq).T for h in heads]
    qb = [jnp.broadcast_to(q[h], sq).T for h in heads]
    s1 = [s_in_ref[0, h] * eg[:, h:h + 1] for h in heads]
    ks = [jnp.sum(s1[h] * kb[h], axis=0, keepdims=True) for h in heads]
    delta = [(v[h] - ks[h]) * gates[:, n_heads + h:n_heads + h + 1] for h in heads]
    s2 = [s1[h] + kb[h] * delta[h] for h in heads]
    o_rows = []
    for h in heads:
        s_out_ref[0, h] = s2[h]
        o = jnp.sum(s2[h] * qb[h], axis=0, keepdims=True)
        o_rows.append(_rms(o, gnw) * _silu(x[:, 3 * d + h * HEAD_DIM:3 * d + (h + 1) * HEAD_DIM]))
    put_row(o_ref, slice(None), jnp.concatenate(o_rows, axis=1))


def _mix_out_kernel(oa_ref, u_ref, hist_ref, x_ref, wp_ref, ps_ref, wo_ref, nw_ref, out_ref, *rest,
                    tm, seq, pos0):
    d_pool = u_ref.shape[1]
    d_gdn = oa_ref.shape[1]
    d = out_ref.shape[1]
    n_groups = len(POOL_WINDOWS)
    gd = d_pool // n_groups
    strips = gd // HEAD_DIM
    if seq > 1:
        (ext_ref,) = rest
        start = (pl.program_id(0) * tm) % seq
        for si in range(d_pool // HEAD_DIM):
            lanes = slice(si * HEAD_DIM, (si + 1) * HEAD_DIM)
            ext_ref[si, 0:HALO_POOL, :] = jnp.where(start > 0, hist_ref[:, lanes], 0.0)
            ext_ref[si, HALO_POOL:, :] = u_ref[:, lanes]
        pos = pos0 + start + lax.broadcasted_iota(jnp.int32, (tm, HEAD_DIM), 0)
    else:
        (hist_out_ref,) = rest
        for r in range(POOL_BUF - 1):
            hist_out_ref[r] = hist_ref[r + 1]
        hist_out_ref[POOL_BUF - 1] = u_ref[...]
        pos = jnp.full((tm, HEAD_DIM), pos0, jnp.int32)

    tn = d // n_groups
    oa = oa_ref[...].astype(BF16)
    parts = []
    for gi, win in enumerate(POOL_WINDOWS):
        cols = slice(gi * tn, (gi + 1) * tn)
        out_ref[:, cols] = jnp.dot(oa, wo_ref[:d_gdn, cols], preferred_element_type=F32)
        cnt = jnp.minimum(pos + 1, win).astype(F32)
        pooled = []
        for si in range(gi * strips, (gi + 1) * strips):
            lanes = slice(si * HEAD_DIM, (si + 1) * HEAD_DIM)
            cur = u_ref[:, lanes]
            wsum = cur
            for r in range(1, win):
                if seq > 1:
                    wsum = wsum + ext_ref[si, pl.ds(HALO_POOL - r, tm), :]
                else:
                    wsum = wsum + hist_ref[POOL_BUF - r, :, lanes]
            pooled.append(wsum / cnt - cur)
        ob = _dot(jnp.concatenate(pooled, axis=1), wp_ref[gi]) * ps_ref[:, gi * gd:(gi + 1) * gd]
        parts.append(ob.astype(BF16))
    ob_all = jnp.concatenate(parts, axis=1)
    mix = [out_ref[:, ci * tn:(ci + 1) * tn]
           + jnp.dot(ob_all, wo_ref[d_gdn:, ci * tn:(ci + 1) * tn], preferred_element_type=F32)
           for ci in range(n_groups)]
    ss = sum(jnp.sum(m * m, axis=-1, keepdims=True) for m in mix)
    inv = lax.rsqrt(ss * (1.0 / d) + EPS)
    for ci in range(n_groups):
        cols = slice(ci * tn, (ci + 1) * tn)
        out_ref[:, cols] = x_ref[:, cols] + mix[ci] * inv * nw_ref[:, cols]


def _mix_out(o_a, u, hist, x, w_pool, pool_scale, w_out, norm_w, tm, seq, pos0):
    m, d = x.shape
    d_pool = pool_scale.shape[1]
    d_gdn = o_a.shape[1]
    out_specs = [pl.BlockSpec((tm, d), lambda i: (i, 0))]
    out_shape = [jax.ShapeDtypeStruct((m, d), F32)]
    if seq > 1:
        hb = tm // HALO_POOL
        hist_spec = pl.BlockSpec((HALO_POOL, d_pool), lambda i: (jnp.maximum(i * hb - 1, 0), 0))
        scratch = [pltpu.VMEM((d_pool // HEAD_DIM, tm + HALO_POOL, HEAD_DIM), F32)]
    else:
        hist_spec = pl.BlockSpec((POOL_BUF, tm, d_pool), lambda i: (0, i, 0))
        out_specs.append(hist_spec)
        out_shape.append(jax.ShapeDtypeStruct(hist.shape, F32))
        scratch = []
    kern = functools.partial(_mix_out_kernel, tm=tm, seq=seq, pos0=pos0)
    return pl.pallas_call(
        kern,
        grid=(m // tm,),
        in_specs=[
            pl.BlockSpec((tm, d_gdn), lambda i: (i, 0)),
            pl.BlockSpec((tm, d_pool), lambda i: (i, 0)),
            hist_spec,
            pl.BlockSpec((tm, d), lambda i: (i, 0)),
            pl.BlockSpec(w_pool.shape, lambda i: (0, 0, 0), pipeline_mode=pl.Buffered(1)),
            pl.BlockSpec((1, d_pool), lambda i: (0, 0)),
            pl.BlockSpec(w_out.shape, lambda i: (0, 0), pipeline_mode=pl.Buffered(1)),
            pl.BlockSpec((1, d), lambda i: (0, 0)),
        ],
        out_specs=out_specs,
        out_shape=out_shape,
        scratch_shapes=scratch,
        compiler_params=_cparams(1),
        name="mix_out",
    )(o_a, u, hist, x, w_pool, pool_scale, w_out, norm_w)


def _mlp_kernel(x_ref, nw1_ref, wu_ref, wd_ref, nw2_ref, *rest, n_heads, hosted):
    if hosted:
        tok_in, (out_ref, *tok_out), (h_ref,) = rest[:6], rest[6:10], rest[10:]
    else:
        out_ref, h_ref = rest
    j = pl.program_id(1)

    @pl.when(j == 0)
    def _():
        h_ref[...] = _rms(x_ref[...], nw1_ref[...]).astype(BF16)
        out_ref[...] = jnp.zeros_like(out_ref)
        if hosted:
            tok_out[0][...] = jnp.zeros_like(tok_out[0])
            tok_out[2][...] = jnp.zeros_like(tok_out[2])

    up = jnp.dot(h_ref[...], wu_ref[...], preferred_element_type=F32)
    act = jnp.square(jnp.maximum(up, 0.0)).astype(BF16)
    out_ref[...] += jnp.dot(act, wd_ref[...], preferred_element_type=F32)
    if hosted:
        _gdn_token_step(j, *tok_in, *tok_out, n_heads)

    @pl.when(j == pl.num_programs(1) - 1)
    def _():
        out_ref[...] = x_ref[...] + _rms(out_ref[...], nw2_ref[...])


def _mlp(x, nw1, w_up, w_down, nw2, tm, tf, token_job=None):
    m, d = x.shape
    f = w_up.shape[1]
    ni, nj = m // tm, f // tf
    in_specs = [
        pl.BlockSpec((tm, d), lambda i, j: (i, 0)),
        pl.BlockSpec((1, d), lambda i, j: (0, 0)),
        pl.BlockSpec((d, tf), lambda i, j: (0, j)),
        pl.BlockSpec((tf, d), lambda i, j: (j, 0)),
        pl.BlockSpec((1, d), lambda i, j: (0, 0)),
    ]
    out_specs = [pl.BlockSpec((tm, d), lambda i, j: (i, 0))]
    out_shape = [jax.ShapeDtypeStruct((m, d), F32)]
    operands = [x, nw1, w_up, w_down, nw2]
    n_heads = 0
    if token_job is not None:
        proj, conv_state, gates, conv_w, gnw, state, n_heads = token_job
        tokens, dg = proj.shape[0], n_heads * HEAD_DIM
        assert tokens == ni * nj and nj % 8 == 0
        cs_spec = pl.BlockSpec((CONV_W - 1, nj, 3 * dg), lambda i, j: (0, i, 0))
        st_spec = pl.BlockSpec((1, n_heads, HEAD_DIM, HEAD_DIM), lambda i, j: (i * nj + j, 0, 0, 0))
        in_specs += [
            pl.BlockSpec((nj, 4 * dg), lambda i, j: (i, 0)),
            cs_spec,
            pl.BlockSpec((nj, HEAD_DIM), lambda i, j: (i, 0)),
            pl.BlockSpec((CONV_W, 3 * dg), lambda i, j: (0, 0)),
            pl.BlockSpec((1, HEAD_DIM), lambda i, j: (0, 0)),
            st_spec,
        ]
        out_specs += [pl.BlockSpec((nj, dg), lambda i, j: (i, 0)), st_spec, cs_spec]
        out_shape += [jax.ShapeDtypeStruct((tokens, dg), F32), jax.ShapeDtypeStruct(state.shape, F32),
                      jax.ShapeDtypeStruct(conv_state.shape, F32)]
        operands += [proj, conv_state, gates, conv_w, gnw, state]
    res = pl.pallas_call(
        functools.partial(_mlp_kernel, n_heads=n_heads, hosted=token_job is not None),
        grid=(ni, nj),
        in_specs=in_specs,
        out_specs=out_specs,
        out_shape=out_shape,
        scratch_shapes=[pltpu.VMEM((tm, d), BF16)],
        compiler_params=_cparams(2),
        name="mlp",
    )(*operands)
    return res if token_job is not None else res[0]


def _pad_lanes(v, n=HEAD_DIM):
    return jnp.pad(v.reshape(1, -1), ((0, 0), (0, n - v.shape[-1])))


def kernel(x_prompt, x_sample, state_gdn, state_conv, state_pool, norm_pre_mix, w_in, conv_w, a_log, dt_bias,
           norm_gdn_out, w_pool, pool_scale, w_out, norm_post_mix, norm_pre_mlp, w_up, w_down, norm_post_mlp):
    batch, seq, d_model = x_prompt.shape
    dec_batch, dec_seq, _ = x_sample.shape
    assert dec_seq == 1
    depth = w_in.shape[0]
    n_heads = a_log.shape[1]
    d_gdn = n_heads * HEAD_DIM
    d_qkv = 3 * d_gdn
    d_pool = pool_scale.shape[1]
    o_gate = d_qkv + d_gdn

    yp = x_prompt.reshape(batch * seq, d_model)
    ys = x_sample.reshape(dec_batch, d_model)
    outs = [[] for _ in range(6)]
    for l in range(depth):
        alog = _pad_lanes(a_log[l])
        dtb = _pad_lanes(dt_bias[l])
        gnw = norm_gdn_out[l].reshape(1, HEAD_DIM)
        npm = norm_pre_mix[l].reshape(1, d_model)
        npo = norm_post_mix[l].reshape(1, d_model)
        nm1 = norm_pre_mlp[l].reshape(1, d_model)
        nm2 = norm_post_mlp[l].reshape(1, d_model)
        ps = pool_scale[l].reshape(1, d_pool)

        proj_s, u_s, gates_s, *w_in_b = _in_proj(
            ys, npm, jnp.swapaxes(w_in[l], 0, 1), o_gate, d_pool, alog, dtb, n_heads,
            tm=dec_batch, tn=512, emit=True)

        proj_p, u_p, gates_p = _in_proj(
            yp, npm, w_in_b, o_gate, d_pool, alog, dtb, n_heads, tm=1024, tn=1024, emit=False)
        to_cast = (w_up[l], w_down[l], w_out[l], w_pool[l].reshape(-1, w_pool.shape[-1]))
        oa_p, s_p, w_up_b, w_down_b, w_out_b, w_pool_b = _gdn_prompt(
            proj_p, gates_p, conv_w[l], gnw, to_cast, batch, seq, n_heads, hs=8, rows=256)
        w_pool_b = w_pool_b.reshape(w_pool.shape[1:])
        (x1_p,) = _mix_out(oa_p, u_p, u_p, yp, w_pool_b, ps, w_out_b, npo, tm=512, seq=seq, pos0=0)
        token_job = (proj_s, jnp.swapaxes(state_conv[l], 0, 1), gates_s, conv_w[l], gnw, state_gdn[l], n_heads)
        yp, oa_s, s_s, cs_new = _mlp(x1_p, nm1, w_up_b, w_down_b, nm2, tm=512, tf=1024, token_job=token_job)
        outs[0].append(s_p)
        outs[1].append(proj_p.reshape(batch, seq, -1)[:, seq - (CONV_W - 1):, :d_qkv])
        outs[2].append(u_p.reshape(batch, seq, -1)[:, seq - POOL_BUF:])

        x1_s, hist_new = _mix_out(oa_s, u_s, jnp.swapaxes(state_pool[l], 0, 1), ys, w_pool_b, ps, w_out_b, npo,
                                  tm=dec_batch, seq=1, pos0=PAST_LEN)
        ys = _mlp(x1_s, nm1, w_up_b, w_down_b, nm2, tm=dec_batch, tf=1024)
        outs[3].append(s_s)
        outs[4].append(jnp.swapaxes(cs_new, 0, 1))
        outs[5].append(jnp.swapaxes(hist_new, 0, 1))

    return (yp.reshape(batch, seq, d_model), ys.reshape(dec_batch, dec_seq, d_model),
            *[jnp.stack(o) for o in outs])
```

```python
import functools

import jax
import jax.numpy as jnp
from jax import lax
from jax.experimental import pallas as pl
from jax.experimental.pallas import tpu as pltpu

F32 = jnp.float32
BF16 = jnp.bfloat16

EPS = 1e-6
HEAD_DIM = 128
CONV_W = 4
POOL_WINDOWS = (2, 4, 8, 16)
POOL_BUF = max(POOL_WINDOWS) - 1
PAST_LEN = 16384
HALO_CONV = 8
HALO_POOL = 16
GDN_CHUNK = 128
VMEM_LIMIT = 56 * 1024 * 1024


def _cparams(n_axes):
    return pltpu.CompilerParams(
        dimension_semantics=("arbitrary",) * n_axes, vmem_limit_bytes=VMEM_LIMIT)


def _dot(a, b):
    return jnp.dot(a.astype(BF16), b.astype(BF16), preferred_element_type=F32)


def _dot_nt(a, b):
    return lax.dot_general(a.astype(BF16), b.astype(BF16), (((1,), (1,)), ((), ())),
                           preferred_element_type=F32)


def _rms(x, w):
    return x * lax.rsqrt(jnp.mean(x * x, axis=-1, keepdims=True) + EPS) * w


def _silu(x):
    return x * (0.5 * jnp.tanh(0.5 * x) + 0.5)


def _softplus(x):
    return jnp.maximum(x, 0.0) + jnp.log(1.0 + jnp.exp(-jnp.abs(x)))


def _l2norm(x):
    return x * lax.rsqrt(jnp.sum(x * x, axis=-1, keepdims=True) + EPS)


def _inproj_kernel(x_ref, nw_ref, wa_ref, wt_ref, wc_ref, alog_ref, dtb_ref, *rest, n_heads, n_main, emit):
    if emit:
        out_ref, u_ref, gates_ref, wa_out, wt_out, wc_out, h_ref = rest
    else:
        out_ref, u_ref, gates_ref, h_ref = rest
    j = pl.program_id(1)

    def main_step(first):
        if first:
            h = _rms(x_ref[...], nw_ref[...]).astype(BF16)
            h_ref[...] = h
        else:
            h = h_ref[...]
        w = wa_ref[...].T.astype(BF16) if emit else wa_ref[...]
        if emit:
            wa_out[...] = w
        out_ref[...] = jnp.dot(h, w, preferred_element_type=F32)

    pl.when(j == 0)(lambda: main_step(True))
    pl.when((j > 0) & (j < n_main))(lambda: main_step(False))

    @pl.when(j == n_main)
    def _():
        if emit:
            n_gate = 2 * n_heads
            blk = wt_ref[...]
            pool_rows = jnp.concatenate([blk[n_gate:], wc_ref[...]], axis=0)
            gate_rows = jnp.concatenate([blk[:n_gate], jnp.zeros((HEAD_DIM - n_gate, blk.shape[1]), F32)], axis=0)
            wt = pool_rows.T.astype(BF16)
            wc = gate_rows.T.astype(BF16)
            wt_out[...] = wt
            wc_out[...] = wc
        else:
            wt = wt_ref[...]
            wc = wc_ref[...]
        h = h_ref[...]
        u_ref[...] = jnp.dot(h, wt, preferred_element_type=F32)
        ab = jnp.dot(h, wc, preferred_element_type=F32)
        lane = lax.broadcasted_iota(jnp.int32, ab.shape, 1)
        g = -jnp.exp(alog_ref[...]) * _softplus(ab + dtb_ref[...])
        gates_ref[...] = jnp.where(lane < n_heads, g, jax.nn.sigmoid(ab))


def _in_proj(x, norm_w, weights, n, d_pool, alog, dtb, n_heads, tm, tn, emit):
    m, d = x.shape
    n_main = n // tn
    n_gate = 2 * n_heads
    vec = pl.BlockSpec((1, HEAD_DIM), lambda i, j: (0, 0))
    out_specs = [
        pl.BlockSpec((tm, tn), lambda i, j: (i, jnp.minimum(j, n_main - 1))),
        pl.BlockSpec((tm, d_pool), lambda i, j: (i, 0)),
        pl.BlockSpec((tm, HEAD_DIM), lambda i, j: (i, 0)),
    ]
    out_shape = [jax.ShapeDtypeStruct((m, n), F32), jax.ShapeDtypeStruct((m, d_pool), F32),
                 jax.ShapeDtypeStruct((m, HEAD_DIM), F32)]
    wa_bf_spec = pl.BlockSpec((d, tn), lambda i, j: (0, jnp.minimum(j, n_main - 1)))
    wt_bf_spec = pl.BlockSpec((d, d_pool), lambda i, j: (0, 0), pipeline_mode=pl.Buffered(1))
    wc_bf_spec = pl.BlockSpec((d, HEAD_DIM), lambda i, j: (0, 0))
    if emit:
        assert n % d_pool == 0 and (n + d_pool) % n_gate == 0 and n_gate % 8 == 0
        operands = (weights, weights, weights)
        w_specs = [
            pl.BlockSpec((tn, d), lambda i, j: (jnp.minimum(j, n_main - 1), 0)),
            pl.BlockSpec((d_pool, d), lambda i, j: (n // d_pool, 0), pipeline_mode=pl.Buffered(1)),
            pl.BlockSpec((n_gate, d), lambda i, j: ((n + d_pool) // n_gate, 0)),
        ]
        out_specs += [wa_bf_spec, pl.BlockSpec((d, d_pool), lambda i, j: (0, 0)), wc_bf_spec]
        out_shape += [jax.ShapeDtypeStruct((d, n), BF16), jax.ShapeDtypeStruct((d, d_pool), BF16),
                      jax.ShapeDtypeStruct((d, HEAD_DIM), BF16)]
    else:
        operands = weights
        w_specs = [wa_bf_spec, wt_bf_spec, wc_bf_spec]
    return pl.pallas_call(
        functools.partial(_inproj_kernel, n_heads=n_heads, n_main=n_main, emit=emit),
        grid=(m // tm, n_main + 1),
        in_specs=[
            pl.BlockSpec((tm, d), lambda i, j: (i, 0)),
            pl.BlockSpec((1, d), lambda i, j: (0, 0)),
            *w_specs,
            vec, vec,
        ],
        out_specs=out_specs,
        out_shape=out_shape,
        scratch_shapes=[pltpu.VMEM((tm, d), BF16)],
        compiler_params=_cparams(2),
        name="in_proj",
    )(x, norm_w, *operands, alog, dtb)


def _lane_column(x, lane_idx):
    lane = lax.broadcasted_iota(jnp.int32, x.shape, 1)
    col = jnp.sum(jnp.where(lane == lane_idx, x, 0.0), axis=1, keepdims=True)
    return jnp.broadcast_to(col, x.shape)


def _unit_lower_inverses(lms):
    n = lms[0].shape[0]
    row = lax.broadcasted_iota(jnp.int32, (n, n), 0)
    col = lax.broadcasted_iota(jnp.int32, (n, n), 1)
    eye = jnp.where(row == col, 1.0, 0.0)
    nks = [-lm for lm in lms]
    qs = [eye + nk for nk in nks]
    nks = [_dot(nk, nk) for nk in nks]
    power = 2
    while 2 * power < n:
        res = [_dot(nk, jnp.concatenate([nk, q], axis=1)) for nk, q in zip(nks, qs)]
        nks = [r[:, :n] for r in res]
        qs = [q + r[:, n:] for q, r in zip(qs, res)]
        power *= 2
    return [q + _dot(nk, q) for nk, q in zip(nks, qs)]


def _gdn_prompt_kernel(q_ref, k_ref, v_ref, gate_ref, hq_ref, hk_ref, hv_ref, gates_ref,
                       cwq_ref, cwk_ref, cwv_ref, gnw_ref, *rest, n_heads, hs, rows, n_cast):
    cast_in = rest[:n_cast]
    o_ref, sout_ref = rest[n_cast:n_cast + 2]
    cast_out = rest[n_cast + 2:2 * n_cast + 2]
    s_ref, ext_ref = rest[2 * n_cast + 2:]
    hg = pl.program_id(1)
    blk = pl.program_id(2)
    c = GDN_CHUNK

    for src, dst in zip(cast_in, cast_out):
        dst[...] = src[...].astype(dst.dtype)

    @pl.when(blk == 0)
    def _():
        s_ref[...] = jnp.zeros_like(s_ref)

    for a, (x_ref, halo_ref) in enumerate(((q_ref, hq_ref), (k_ref, hk_ref), (v_ref, hv_ref))):
        for hi in range(hs):
            lanes = slice(hi * HEAD_DIM, (hi + 1) * HEAD_DIM)
            ext_ref[a * hs + hi, 0:HALO_CONV, :] = jnp.where(blk > 0, halo_ref[:, lanes], 0.0)
            ext_ref[a * hs + hi, HALO_CONV:, :] = x_ref[:, lanes]

    def conv_silu(a, cw_ref, p):
        ci, hi = p
        lanes = slice(hi * HEAD_DIM, (hi + 1) * HEAD_DIM)
        acc = None
        for j in range(CONV_W):
            term = (cw_ref[j:j + 1, lanes]
                    * ext_ref[a * hs + hi, pl.ds(HALO_CONV - (CONV_W - 1) + j + ci * c, c), :])
            acc = term if acc is None else acc + term
        return _silu(acc)

    row = lax.broadcasted_iota(jnp.int32, (c, c), 0)
    col = lax.broadcasted_iota(jnp.int32, (c, c), 1)
    tril = row >= col
    strict = row > col
    tril_f = jnp.where(tril, 1.0, 0.0)
    gnw = gnw_ref[...]
    heads = range(hs)

    n_chunks = rows // c
    probs = [(ci, hi) for ci in range(n_chunks) for hi in heads]
    n_p = range(len(probs))
    gates = [gates_ref[ci * c:(ci + 1) * c, :] for ci in range(n_chunks)]
    gcum = [jnp.dot(tril_f, g, preferred_element_type=F32, precision=lax.Precision.HIGHEST) for g in gates]
    qc = [_l2norm(conv_silu(0, cwq_ref, p)) * (HEAD_DIM ** -0.5) for p in probs]
    kc = [_l2norm(conv_silu(1, cwk_ref, p)) for p in probs]
    vc = [conv_silu(2, cwv_ref, p) for p in probs]
    gc = [_lane_column(gcum[ci], hg * hs + hi) for ci, hi in probs]
    bc = [_lane_column(gates[ci], hg * hs + hi + n_heads) for ci, hi in probs]
    decay = [jnp.exp(jnp.where(tril, g - g.T, -1e30)) for g in gc]
    egc = [jnp.exp(g) for g in gc]
    g_last = [g[c - 1:c, :] for g in gc]
    kk = [_dot_nt(k, k) for k in kc]
    qk = [_dot_nt(q, k) for q, k in zip(qc, kc)]
    lm = [jnp.where(strict, kk[p] * bc[p] * decay[p], 0.0) for p in n_p]
    t_inv = _unit_lower_inverses(lm)
    uw = [_dot(t_inv[p], jnp.concatenate([vc[p] * bc[p], kc[p] * (bc[p] * egc[p])], axis=1)) for p in n_p]
    intra = [jnp.where(tril, qk[p] * decay[p], 0.0) for p in n_p]
    kdt = [(kc[p] * jnp.exp(g_last[p] - gc[p])).T for p in n_p]
    wq = [jnp.concatenate([uw[p][:, HEAD_DIM:], qc[p] * egc[p]], axis=0) for p in n_p]
    ik = [jnp.concatenate([intra[p], kdt[p]], axis=0) for p in n_p]

    s = [s_ref[hi] for hi in heads]
    for ci in range(n_chunks):
        ps = [ci * hs + hi for hi in heads]
        r1 = [_dot(wq[p], s[hi]) for hi, p in zip(heads, ps)]
        v_new = [uw[p][:, :HEAD_DIM] - r1[hi][:c] for hi, p in zip(heads, ps)]
        r2 = [_dot(ik[p], v_new[hi]) for hi, p in zip(heads, ps)]
        s = [s[hi] * jnp.exp(g_last[p]) + r2[hi][c:] for hi, p in zip(heads, ps)]
        for hi in heads:
            lanes = slice(hi * HEAD_DIM, (hi + 1) * HEAD_DIM)
            o = _rms(r1[hi][c:] + r2[hi][:c], gnw) * _silu(gate_ref[ci * c:(ci + 1) * c, lanes])
            o_ref[ci * c:(ci + 1) * c, lanes] = o.astype(o_ref.dtype)

    for hi in range(hs):
        s_ref[hi] = s[hi]

    @pl.when(blk == pl.num_programs(2) - 1)
    def _():
        for hi in range(hs):
            sout_ref[0, hi] = s[hi]


def _gdn_prompt(proj, gates, conv_w, gnw, to_cast, batch, seq, n_heads, hs, rows):
    nblk = seq // rows
    hb = rows // HALO_CONV
    ng = n_heads // hs
    width = hs * HEAD_DIM
    steps = batch * ng * nblk

    def slab(w):
        return pl.BlockSpec((w.shape[0] // steps, w.shape[1]), lambda b, h, k: ((b * ng + h) * nblk + k, 0))

    def main(off):
        return pl.BlockSpec((rows, width), lambda b, h, k: (b * nblk + k, off * ng + h))

    def halo(off):
        return pl.BlockSpec(
            (HALO_CONV, width), lambda b, h, k: (jnp.maximum((b * nblk + k) * hb - 1, 0), off * ng + h))

    def cw(off):
        return pl.BlockSpec((CONV_W, width), lambda b, h, k: (0, off * ng + h))

    kern = functools.partial(_gdn_prompt_kernel, n_heads=n_heads, hs=hs, rows=rows, n_cast=len(to_cast))
    return pl.pallas_call(
        kern,
        grid=(batch, ng, nblk),
        in_specs=[main(0), main(1), main(2), main(3), halo(0), halo(1), halo(2),
                  pl.BlockSpec((rows, HEAD_DIM), lambda b, h, k: (b * nblk + k, 0)),
                  cw(0), cw(1), cw(2),
                  pl.BlockSpec((1, HEAD_DIM), lambda b, h, k: (0, 0)),
                  *[slab(w) for w in to_cast]],
        out_specs=[
            pl.BlockSpec((rows, width), lambda b, h, k: (b * nblk + k, h)),
            pl.BlockSpec((1, hs, HEAD_DIM, HEAD_DIM), lambda b, h, k: (b, h, 0, 0)),
            *[slab(w) for w in to_cast],
        ],
        out_shape=[
            jax.ShapeDtypeStruct((batch * seq, n_heads * HEAD_DIM), BF16),
            jax.ShapeDtypeStruct((batch, n_heads, HEAD_DIM, HEAD_DIM), F32),
            *[jax.ShapeDtypeStruct(w.shape, BF16) for w in to_cast],
        ],
        scratch_shapes=[pltpu.VMEM((hs, HEAD_DIM, HEAD_DIM), F32),
                        pltpu.VMEM((3 * hs, rows + HALO_CONV, HEAD_DIM), F32)],
        compiler_params=_cparams(3),
        name="gdn_prompt",
    )(proj, proj, proj, proj, proj, proj, proj, gates, conv_w, conv_w, conv_w, gnw, *to_cast)


def _gdn_token_step(r, x_ref, cs_ref, gates_ref, cw_ref, gnw_ref, s_in_ref, o_ref, s_out_ref, cs_out_ref, n_heads):
    d = n_heads * HEAD_DIM
    sq = (HEAD_DIM, HEAD_DIM)
    x = x_ref[pl.ds(r, 1), :]
    cw = cw_ref[...]
    conv = cw[CONV_W - 1:CONV_W, :] * x[:, :3 * d]
    for j in range(CONV_W - 1):
        conv = conv + cw[j:j + 1, :] * cs_ref[j, pl.ds(r, 1), :]
    qkv = _silu(conv)

    def put_row(ref, idx, value):
        old = ref[idx]
        rows = lax.broadcasted_iota(jnp.int32, old.shape, 0)
        ref[idx] = jnp.where(rows == r, jnp.broadcast_to(value, old.shape), old)

    for j in range(CONV_W - 2):
        put_row(cs_out_ref, j, cs_ref[j + 1, pl.ds(r, 1), :])
    put_row(cs_out_ref, CONV_W - 2, x[:, :3 * d])

    gates = gates_ref[pl.ds(r, 1), :]
    eg = jnp.exp(gates)
    gnw = gnw_ref[...]
    heads = range(n_heads)
    q = [_l2norm(qkv[:, h * HEAD_DIM:(h + 1) * HEAD_DIM]) * (HEAD_DIM ** -0.5) for h in heads]
    k = [_l2norm(qkv[:, d + h * HEAD_DIM:d + (h + 1) * HEAD_DIM]) for h in heads]
    v = [qkv[:, 2 * d + h * HEAD_DIM:2 * d + (h + 1) * HEAD_DIM] for h in heads]
    kb = [jnp.broadcast_to(k[h], sq).T for h in heads]
    qb = [jnp.broadcast_to(q[h], sq).T for h in heads]
    s1 = [s_in_ref[0, h] * eg[:, h:h + 1] for h in heads]
    ks = [jnp.sum(s1[h] * kb[h], axis=0, keepdims=True) for h in heads]
    delta = [(v[h] - ks[h]) * gates[:, n_heads + h:n_heads + h + 1] for h in heads]
    s2 = [s1[h] + kb[h] * delta[h] for h in heads]
    o_rows = []
    for h in heads:
        s_out_ref[0, h] = s2[h]
        o = jnp.sum(s2[h] * qb[h], axis=0, keepdims=True)
        o_rows.append(_rms(o, gnw) * _silu(x[:, 3 * d + h * HEAD_DIM:3 * d + (h + 1) * HEAD_DIM]))
    put_row(o_ref, slice(None), jnp.concatenate(o_rows, axis=1))


def _mix_out_kernel(oa_ref, u_ref, hist_ref, x_ref, wp_ref, ps_ref, wo_ref, nw_ref, out_ref, *rest,
                    tm, seq, pos0):
    d_pool = u_ref.shape[1]
    d_gdn = oa_ref.shape[1]
    d = out_ref.shape[1]
    n_groups = len(POOL_WINDOWS)
    gd = d_pool // n_groups
    strips = gd // HEAD_DIM
    if seq > 1:
        (ext_ref,) = rest
        start = (pl.program_id(0) * tm) % seq
        for si in range(d_pool // HEAD_DIM):
            lanes = slice(si * HEAD_DIM, (si + 1) * HEAD_DIM)
            ext_ref[si, 0:HALO_POOL, :] = jnp.where(start > 0, hist_ref[:, lanes], 0.0)
            ext_ref[si, HALO_POOL:, :] = u_ref[:, lanes]
        pos = pos0 + start + lax.broadcasted_iota(jnp.int32, (tm, HEAD_DIM), 0)
    else:
        (hist_out_ref,) = rest
        for r in range(POOL_BUF - 1):
            hist_out_ref[r] = hist_ref[r + 1]
        hist_out_ref[POOL_BUF - 1] = u_ref[...]
        pos = jnp.full((tm, HEAD_DIM), pos0, jnp.int32)

    tn = d // n_groups
    oa = oa_ref[...].astype(BF16)
    parts = []
    for gi, win in enumerate(POOL_WINDOWS):
        cols = slice(gi * tn, (gi + 1) * tn)
        out_ref[:, cols] = jnp.dot(oa, wo_ref[:d_gdn, cols], preferred_element_type=F32)
        cnt = jnp.minimum(pos + 1, win).astype(F32)
        pooled = []
        for si in range(gi * strips, (gi + 1) * strips):
            lanes = slice(si * HEAD_DIM, (si + 1) * HEAD_DIM)
            cur = u_ref[:, lanes]
            wsum = cur
            for r in range(1, win):
                if seq > 1:
                    wsum = wsum + ext_ref[si, pl.ds(HALO_POOL - r, tm), :]
                else:
                    wsum = wsum + hist_ref[POOL_BUF - r, :, lanes]
            pooled.append(wsum / cnt - cur)
        ob = _dot(jnp.concatenate(pooled, axis=1), wp_ref[gi]) * ps_ref[:, gi * gd:(gi + 1) * gd]
        parts.append(ob.astype(BF16))
    ob_all = jnp.concatenate(parts, axis=1)
    mix = [out_ref[:, ci * tn:(ci + 1) * tn]
           + jnp.dot(ob_all, wo_ref[d_gdn:, ci * tn:(ci + 1) * tn], preferred_element_type=F32)
           for ci in range(n_groups)]
    ss = sum(jnp.sum(m * m, axis=-1, keepdims=True) for m in mix)
    inv = lax.rsqrt(ss * (1.0 / d) + EPS)
    for ci in range(n_groups):
        cols = slice(ci * tn, (ci + 1) * tn)
        out_ref[:, cols] = x_ref[:, cols] + mix[ci] * inv * nw_ref[:, cols]


def _mix_out(o_a, u, hist, x, w_pool, pool_scale, w_out, norm_w, tm, seq, pos0):
    m, d = x.shape
    d_pool = pool_scale.shape[1]
    d_gdn = o_a.shape[1]
    out_specs = [pl.BlockSpec((tm, d), lambda i: (i, 0))]
    out_shape = [jax.ShapeDtypeStruct((m, d), F32)]
    if seq > 1:
        hb = tm // HALO_POOL
        hist_spec = pl.BlockSpec((HALO_POOL, d_pool), lambda i: (jnp.maximum(i * hb - 1, 0), 0))
        scratch = [pltpu.VMEM((d_pool // HEAD_DIM, tm + HALO_POOL, HEAD_DIM), F32)]
    else:
        hist_spec = pl.BlockSpec((POOL_BUF, tm, d_pool), lambda i: (0, i, 0))
        out_specs.append(hist_spec)
        out_shape.append(jax.ShapeDtypeStruct(hist.shape, F32))
        scratch = []
    kern = functools.partial(_mix_out_kernel, tm=tm, seq=seq, pos0=pos0)
    return pl.pallas_call(
        kern,
        grid=(m // tm,),
        in_specs=[
            pl.BlockSpec((tm, d_gdn), lambda i: (i, 0)),
            pl.BlockSpec((tm, d_pool), lambda i: (i, 0)),
            hist_spec,
            pl.BlockSpec((tm, d), lambda i: (i, 0)),
            pl.BlockSpec(w_pool.shape, lambda i: (0, 0, 0), pipeline_mode=pl.Buffered(1)),
            pl.BlockSpec((1, d_pool), lambda i: (0, 0)),
            pl.BlockSpec(w_out.shape, lambda i: (0, 0), pipeline_mode=pl.Buffered(1)),
            pl.BlockSpec((1, d), lambda i: (0, 0)),
        ],
        out_specs=out_specs,
        out_shape=out_shape,
        scratch_shapes=scratch,
        compiler_params=_cparams(1),
        name="mix_out",
    )(o_a, u, hist, x, w_pool, pool_scale, w_out, norm_w)


def _mlp_kernel(x_ref, nw1_ref, wu_ref, wd_ref, nw2_ref, *rest, n_heads, hosted):
    if hosted:
        tok_in, (out_ref, *tok_out), (h_ref,) = rest[:6], rest[6:10], rest[10:]
    else:
        out_ref, h_ref = rest
    j = pl.program_id(1)

    @pl.when(j == 0)
    def _():
        if hosted:
            tok_out[0][...] = jnp.zeros_like(tok_out[0])
            tok_out[2][...] = jnp.zeros_like(tok_out[2])

    def step(first):
        if first:
            h = _rms(x_ref[...], nw1_ref[...]).astype(BF16)
            h_ref[...] = h
        else:
            h = h_ref[...]
        up = jnp.dot(h, wu_ref[...], preferred_element_type=F32)
        act = jnp.square(jnp.maximum(up, 0.0)).astype(BF16)
        part = jnp.dot(act, wd_ref[...], preferred_element_type=F32)
        out_ref[...] = part if first else out_ref[...] + part
        if hosted:
            _gdn_token_step(j, *tok_in, *tok_out, n_heads)

    pl.when(j == 0)(lambda: step(True))
    pl.when(j > 0)(lambda: step(False))

    @pl.when(j == pl.num_programs(1) - 1)
    def _():
        out_ref[...] = x_ref[...] + _rms(out_ref[...], nw2_ref[...])


def _mlp(x, nw1, w_up, w_down, nw2, tm, tf, token_job=None):
    m, d = x.shape
    f = w_up.shape[1]
    ni, nj = m // tm, f // tf
    in_specs = [
        pl.BlockSpec((tm, d), lambda i, j: (i, 0)),
        pl.BlockSpec((1, d), lambda i, j: (0, 0)),
        pl.BlockSpec((d, tf), lambda i, j: (0, j)),
        pl.BlockSpec((tf, d), lambda i, j: (j, 0)),
        pl.BlockSpec((1, d), lambda i, j: (0, 0)),
    ]
    out_specs = [pl.BlockSpec((tm, d), lambda i, j: (i, 0))]
    out_shape = [jax.ShapeDtypeStruct((m, d), F32)]
    operands = [x, nw1, w_up, w_down, nw2]
    n_heads = 0
    if token_job is not None:
        proj, conv_state, gates, conv_w, gnw, state, n_heads = token_job
        tokens, dg = proj.shape[0], n_heads * HEAD_DIM
        assert tokens == ni * nj and nj % 8 == 0
        cs_spec = pl.BlockSpec((CONV_W - 1, nj, 3 * dg), lambda i, j: (0, i, 0))
        st_spec = pl.BlockSpec((1, n_heads, HEAD_DIM, HEAD_DIM), lambda i, j: (i * nj + j, 0, 0, 0))
        in_specs += [
            pl.BlockSpec((nj, 4 * dg), lambda i, j: (i, 0)),
            cs_spec,
            pl.BlockSpec((nj, HEAD_DIM), lambda i, j: (i, 0)),
            pl.BlockSpec((CONV_W, 3 * dg), lambda i, j: (0, 0)),
            pl.BlockSpec((1, HEAD_DIM), lambda i, j: (0, 0)),
            st_spec,
        ]
        out_specs += [pl.BlockSpec((nj, dg), lambda i, j: (i, 0)), st_spec, cs_spec]
        out_shape += [jax.ShapeDtypeStruct((tokens, dg), F32), jax.ShapeDtypeStruct(state.shape, F32),
                      jax.ShapeDtypeStruct(conv_state.shape, F32)]
        operands += [proj, conv_state, gates, conv_w, gnw, state]
    res = pl.pallas_call(
        functools.partial(_mlp_kernel, n_heads=n_heads, hosted=token_job is not None),
        grid=(ni, nj),
        in_specs=in_specs,
        out_specs=out_specs,
        out_shape=out_shape,
        scratch_shapes=[pltpu.VMEM((tm, d), BF16)],
        compiler_params=_cparams(2),
        name="mlp",
    )(*operands)
    return res if token_job is not None else res[0]


def _pad_lanes(v, n=HEAD_DIM):
    return jnp.pad(v.reshape(1, -1), ((0, 0), (0, n - v.shape[-1])))


def kernel(x_prompt, x_sample, state_gdn, state_conv, state_pool, norm_pre_mix, w_in, conv_w, a_log, dt_bias,
           norm_gdn_out, w_pool, pool_scale, w_out, norm_post_mix, norm_pre_mlp, w_up, w_down, norm_post_mlp):
    batch, seq, d_model = x_prompt.shape
    dec_batch, dec_seq, _ = x_sample.shape
    assert dec_seq == 1
    depth = w_in.shape[0]
    n_heads = a_log.shape[1]
    d_gdn = n_heads * HEAD_DIM
    d_qkv = 3 * d_gdn
    d_pool = pool_scale.shape[1]
    o_gate = d_qkv + d_gdn

    yp = x_prompt.reshape(batch * seq, d_model)
    ys = x_sample.reshape(dec_batch, d_model)
    outs = [[] for _ in range(6)]
    for l in range(depth):
        alog = _pad_lanes(a_log[l])
        dtb = _pad_lanes(dt_bias[l])
        gnw = norm_gdn_out[l].reshape(1, HEAD_DIM)
        npm = norm_pre_mix[l].reshape(1, d_model)
        npo = norm_post_mix[l].reshape(1, d_model)
        nm1 = norm_pre_mlp[l].reshape(1, d_model)
        nm2 = norm_post_mlp[l].reshape(1, d_model)
        ps = pool_scale[l].reshape(1, d_pool)

        proj_s, u_s, gates_s, *w_in_b = _in_proj(
            ys, npm, jnp.swapaxes(w_in[l], 0, 1), o_gate, d_pool, alog, dtb, n_heads,
            tm=dec_batch, tn=512, emit=True)

        proj_p, u_p, gates_p = _in_proj(
            yp, npm, w_in_b, o_gate, d_pool, alog, dtb, n_heads, tm=1024, tn=1024, emit=False)
        to_cast = (w_up[l], w_down[l], w_out[l], w_pool[l].reshape(-1, w_pool.shape[-1]))
        oa_p, s_p, w_up_b, w_down_b, w_out_b, w_pool_b = _gdn_prompt(
            proj_p, gates_p, conv_w[l], gnw, to_cast, batch, seq, n_heads, hs=8, rows=256)
        w_pool_b = w_pool_b.reshape(w_pool.shape[1:])
        (x1_p,) = _mix_out(oa_p, u_p, u_p, yp, w_pool_b, ps, w_out_b, npo, tm=512, seq=seq, pos0=0)
        token_job = (proj_s, jnp.swapaxes(state_conv[l], 0, 1), gates_s, conv_w[l], gnw, state_gdn[l], n_heads)
        yp, oa_s, s_s, cs_new = _mlp(x1_p, nm1, w_up_b, w_down_b, nm2, tm=512, tf=1024, token_job=token_job)
        outs[0].append(s_p)
        outs[1].append(proj_p.reshape(batch, seq, -1)[:, seq - (CONV_W - 1):, :d_qkv])
        outs[2].append(u_p.reshape(batch, seq, -1)[:, seq - POOL_BUF:])

        x1_s, hist_new = _mix_out(oa_s, u_s, jnp.swapaxes(state_pool[l], 0, 1), ys, w_pool_b, ps, w_out_b, npo,
                                  tm=dec_batch, seq=1, pos0=PAST_LEN)
        ys = _mlp(x1_s, nm1, w_up_b, w_down_b, nm2, tm=dec_batch, tf=1024)
        outs[3].append(s_s)
        outs[4].append(jnp.swapaxes(cs_new, 0, 1))
        outs[5].append(jnp.swapaxes(hist_new, 0, 1))

    return (yp.reshape(batch, seq, d_model), ys.reshape(dec_batch, dec_seq, d_model),
            *[jnp.stack(o) for o in outs])
```

```python
import functools

import jax
import jax.numpy as jnp
from jax import lax
from jax.experimental import pallas as pl
from jax.experimental.pallas import tpu as pltpu

F32 = jnp.float32
BF16 = jnp.bfloat16

EPS = 1e-6
HEAD_DIM = 128
CONV_W = 4
POOL_WINDOWS = (2, 4, 8, 16)
POOL_BUF = max(POOL_WINDOWS) - 1
PAST_LEN = 16384
HALO_CONV = 8
HALO_POOL = 16
GDN_CHUNK = 128
VMEM_LIMIT = 56 * 1024 * 1024


def _cparams(n_axes):
    return pltpu.CompilerParams(
        dimension_semantics=("arbitrary",) * n_axes, vmem_limit_bytes=VMEM_LIMIT)


def _dot(a, b):
    return jnp.dot(a.astype(BF16), b.astype(BF16), preferred_element_type=F32)


def _dot_nt(a, b):
    return lax.dot_general(a.astype(BF16), b.astype(BF16), (((1,), (1,)), ((), ())),
                           preferred_element_type=F32)


def _rms(x, w):
    return x * lax.rsqrt(jnp.mean(x * x, axis=-1, keepdims=True) + EPS) * w


def _silu(x):
    h = 0.5 * x
    return h + h * jnp.tanh(h)


def _softplus(x):
    return jnp.maximum(x, 0.0) + jnp.log(1.0 + jnp.exp(-jnp.abs(x)))


def _l2norm(x, scale=1.0):
    return x * (lax.rsqrt(jnp.sum(x * x, axis=-1, keepdims=True) + EPS) * scale)


def _inproj_kernel(x_ref, nw_ref, wa_ref, wt_ref, wc_ref, alog_ref, dtb_ref, *rest, n_heads, n_main, emit):
    if emit:
        out_ref, u_ref, gates_ref, wa_out, wt_out, wc_out, h_ref = rest
    else:
        out_ref, u_ref, gates_ref, h_ref = rest
    j = pl.program_id(1)

    def main_step(first):
        if first:
            h = _rms(x_ref[...], nw_ref[...]).astype(BF16)
            h_ref[...] = h
        else:
            h = h_ref[...]
        w = wa_ref[...].T.astype(BF16) if emit else wa_ref[...]
        if emit:
            wa_out[...] = w
        out_ref[...] = jnp.dot(h, w, preferred_element_type=F32)

    pl.when(j == 0)(lambda: main_step(True))
    pl.when((j > 0) & (j < n_main))(lambda: main_step(False))

    @pl.when(j == n_main)
    def _():
        if emit:
            n_gate = 2 * n_heads
            blk = wt_ref[...]
            pool_rows = jnp.concatenate([blk[n_gate:], wc_ref[...]], axis=0)
            gate_rows = jnp.concatenate([blk[:n_gate], jnp.zeros((HEAD_DIM - n_gate, blk.shape[1]), F32)], axis=0)
            wt = pool_rows.T.astype(BF16)
            wc = gate_rows.T.astype(BF16)
            wt_out[...] = wt
            wc_out[...] = wc
        else:
            wt = wt_ref[...]
            wc = wc_ref[...]
        h = h_ref[...]
        u_ref[...] = jnp.dot(h, wt, preferred_element_type=F32)
        ab = jnp.dot(h, wc, preferred_element_type=F32)
        lane = lax.broadcasted_iota(jnp.int32, ab.shape, 1)
        g = -jnp.exp(alog_ref[...]) * _softplus(ab + dtb_ref[...])
        gates_ref[...] = jnp.where(lane < n_heads, g, jax.nn.sigmoid(ab))


def _in_proj(x, norm_w, weights, n, d_pool, alog, dtb, n_heads, tm, tn, emit):
    m, d = x.shape
    n_main = n // tn
    n_gate = 2 * n_heads
    vec = pl.BlockSpec((1, HEAD_DIM), lambda i, j: (0, 0))
    out_specs = [
        pl.BlockSpec((tm, tn), lambda i, j: (i, jnp.minimum(j, n_main - 1))),
        pl.BlockSpec((tm, d_pool), lambda i, j: (i, 0)),
        pl.BlockSpec((tm, HEAD_DIM), lambda i, j: (i, 0)),
    ]
    out_shape = [jax.ShapeDtypeStruct((m, n), F32), jax.ShapeDtypeStruct((m, d_pool), F32),
                 jax.ShapeDtypeStruct((m, HEAD_DIM), F32)]
    wa_bf_spec = pl.BlockSpec((d, tn), lambda i, j: (0, jnp.minimum(j, n_main - 1)))
    wt_bf_spec = pl.BlockSpec((d, d_pool), lambda i, j: (0, 0), pipeline_mode=pl.Buffered(1))
    wc_bf_spec = pl.BlockSpec((d, HEAD_DIM), lambda i, j: (0, 0))
    if emit:
        assert n % d_pool == 0 and (n + d_pool) % n_gate == 0 and n_gate % 8 == 0
        operands = (weights, weights, weights)
        w_specs = [
            pl.BlockSpec((tn, d), lambda i, j: (jnp.minimum(j, n_main - 1), 0)),
            pl.BlockSpec((d_pool, d), lambda i, j: (n // d_pool, 0), pipeline_mode=pl.Buffered(1)),
            pl.BlockSpec((n_gate, d), lambda i, j: ((n + d_pool) // n_gate, 0)),
        ]
        out_specs += [wa_bf_spec, pl.BlockSpec((d, d_pool), lambda i, j: (0, 0)), wc_bf_spec]
        out_shape += [jax.ShapeDtypeStruct((d, n), BF16), jax.ShapeDtypeStruct((d, d_pool), BF16),
                      jax.ShapeDtypeStruct((d, HEAD_DIM), BF16)]
    else:
        operands = weights
        w_specs = [wa_bf_spec, wt_bf_spec, wc_bf_spec]
    return pl.pallas_call(
        functools.partial(_inproj_kernel, n_heads=n_heads, n_main=n_main, emit=emit),
        grid=(m // tm, n_main + 1),
        in_specs=[
            pl.BlockSpec((tm, d), lambda i, j: (i, 0)),
            pl.BlockSpec((1, d), lambda i, j: (0, 0)),
            *w_specs,
            vec, vec,
        ],
        out_specs=out_specs,
        out_shape=out_shape,
        scratch_shapes=[pltpu.VMEM((tm, d), BF16)],
        compiler_params=_cparams(2),
        name="in_proj",
    )(x, norm_w, *operands, alog, dtb)


def _lane_column(x, lane_idx):
    lane = lax.broadcasted_iota(jnp.int32, x.shape, 1)
    col = jnp.sum(jnp.where(lane == lane_idx, x, 0.0), axis=1, keepdims=True)
    return jnp.broadcast_to(col, x.shape)


def _unit_lower_inverses(lms):
    n = lms[0].shape[0]
    row = lax.broadcasted_iota(jnp.int32, (n, n), 0)
    col = lax.broadcasted_iota(jnp.int32, (n, n), 1)
    eye = jnp.where(row == col, 1.0, 0.0)
    def mm(a, b):
        return jnp.dot(a, b, preferred_element_type=F32)

    qs = [eye - lm for lm in lms]
    lbs = [lm.astype(BF16) for lm in lms]
    nks = [mm(lb, lb) for lb in lbs]
    power = 2
    while 2 * power < n:
        nbs = [nk.astype(BF16) for nk in nks]
        res = [mm(nb, jnp.concatenate([nb, q.astype(BF16)], axis=1)) for nb, q in zip(nbs, qs)]
        nks = [r[:, :n] for r in res]
        qs = [q + r[:, n:] for q, r in zip(qs, res)]
        power *= 2
    return [q + _dot(nk, q) for nk, q in zip(nks, qs)]


def _gdn_prompt_kernel(q_ref, k_ref, v_ref, gate_ref, hq_ref, hk_ref, hv_ref, gates_ref,
                       cwq_ref, cwk_ref, cwv_ref, gnw_ref, *rest, n_heads, hs, rows, n_cast):
    cast_in = rest[:n_cast]
    o_ref, sout_ref = rest[n_cast:n_cast + 2]
    cast_out = rest[n_cast + 2:2 * n_cast + 2]
    s_ref, ext_ref = rest[2 * n_cast + 2:]
    hg = pl.program_id(1)
    blk = pl.program_id(2)
    c = GDN_CHUNK

    for src, dst in zip(cast_in, cast_out):
        dst[...] = src[...].astype(dst.dtype)

    @pl.when(blk == 0)
    def _():
        s_ref[...] = jnp.zeros_like(s_ref)

    for a, (x_ref, halo_ref) in enumerate(((q_ref, hq_ref), (k_ref, hk_ref), (v_ref, hv_ref))):
        for hi in range(hs):
            lanes = slice(hi * HEAD_DIM, (hi + 1) * HEAD_DIM)
            ext_ref[a * hs + hi, 0:HALO_CONV, :] = jnp.where(blk > 0, halo_ref[:, lanes], 0.0)
            ext_ref[a * hs + hi, HALO_CONV:, :] = x_ref[:, lanes]

    def conv_silu(a, cw_ref, p):
        ci, hi = p
        lanes = slice(hi * HEAD_DIM, (hi + 1) * HEAD_DIM)
        acc = None
        for j in range(CONV_W):
            term = (cw_ref[j:j + 1, lanes]
                    * ext_ref[a * hs + hi, pl.ds(HALO_CONV - (CONV_W - 1) + j + ci * c, c), :])
            acc = term if acc is None else acc + term
        return _silu(acc)

    row = lax.broadcasted_iota(jnp.int32, (c, c), 0)
    col = lax.broadcasted_iota(jnp.int32, (c, c), 1)
    tril = row >= col
    strict = row > col
    tril_f = jnp.where(tril, 1.0, 0.0)
    gnw = gnw_ref[...]
    heads = range(hs)

    n_chunks = rows // c
    probs = [(ci, hi) for ci in range(n_chunks) for hi in heads]
    n_p = range(len(probs))
    gates = [gates_ref[ci * c:(ci + 1) * c, :] for ci in range(n_chunks)]
    gcum = [jnp.dot(tril_f, g, preferred_element_type=F32, precision=lax.Precision.HIGHEST) for g in gates]
    qc = [_l2norm(conv_silu(0, cwq_ref, p), HEAD_DIM ** -0.5) for p in probs]
    kc = [_l2norm(conv_silu(1, cwk_ref, p)) for p in probs]
    vc = [conv_silu(2, cwv_ref, p) for p in probs]
    gc = [_lane_column(gcum[ci], hg * hs + hi) for ci, hi in probs]
    bc = [_lane_column(gates[ci], hg * hs + hi + n_heads) for ci, hi in probs]
    decay = [jnp.exp(jnp.where(tril, g - g.T, -1e30)) for g in gc]
    egc = [jnp.exp(g) for g in gc]
    g_last = [g[c - 1:c, :] for g in gc]
    kb = [k.astype(BF16) for k in kc]
    kk = [_dot_nt(k, k) for k in kb]
    qk = [_dot_nt(q, k) for q, k in zip(qc, kb)]
    lm = [jnp.where(strict, kk[p] * bc[p] * decay[p], 0.0) for p in n_p]
    t_inv = _unit_lower_inverses(lm)
    uw = [_dot(t_inv[p], jnp.concatenate([vc[p] * bc[p], kc[p] * (bc[p] * egc[p])], axis=1)) for p in n_p]
    intra = [qk[p] * decay[p] for p in n_p]
    kdt = [(kc[p] * jnp.exp(g_last[p] - gc[p])).T for p in n_p]
    wq = [jnp.concatenate([uw[p][:, HEAD_DIM:], qc[p] * egc[p]], axis=0) for p in n_p]
    ik = [jnp.concatenate([intra[p], kdt[p]], axis=0) for p in n_p]

    s = [s_ref[hi] for hi in heads]
    for ci in range(n_chunks):
        ps = [ci * hs + hi for hi in heads]
        r1 = [_dot(wq[p], s[hi]) for hi, p in zip(heads, ps)]
        v_new = [uw[p][:, :HEAD_DIM] - r1[hi][:c] for hi, p in zip(heads, ps)]
        r2 = [_dot(ik[p], v_new[hi]) for hi, p in zip(heads, ps)]
        s = [s[hi] * jnp.exp(g_last[p]) + r2[hi][c:] for hi, p in zip(heads, ps)]
        for hi in heads:
            lanes = slice(hi * HEAD_DIM, (hi + 1) * HEAD_DIM)
            o = _rms(r1[hi][c:] + r2[hi][:c], gnw) * _silu(gate_ref[ci * c:(ci + 1) * c, lanes])
            o_ref[ci * c:(ci + 1) * c, lanes] = o.astype(o_ref.dtype)

    for hi in range(hs):
        s_ref[hi] = s[hi]

    @pl.when(blk == pl.num_programs(2) - 1)
    def _():
        for hi in range(hs):
            sout_ref[0, hi] = s[hi]


def _gdn_prompt(proj, gates, conv_w, gnw, to_cast, batch, seq, n_heads, hs, rows):
    nblk = seq // rows
    hb = rows // HALO_CONV
    ng = n_heads // hs
    width = hs * HEAD_DIM
    steps = batch * ng * nblk

    def slab(w):
        return pl.BlockSpec((w.shape[0] // steps, w.shape[1]), lambda b, h, k: ((b * ng + h) * nblk + k, 0))

    def main(off):
        return pl.BlockSpec((rows, width), lambda b, h, k: (b * nblk + k, off * ng + h))

    def halo(off):
        return pl.BlockSpec(
            (HALO_CONV, width), lambda b, h, k: (jnp.maximum((b * nblk + k) * hb - 1, 0), off * ng + h))

    def cw(off):
        return pl.BlockSpec((CONV_W, width), lambda b, h, k: (0, off * ng + h))

    kern = functools.partial(_gdn_prompt_kernel, n_heads=n_heads, hs=hs, rows=rows, n_cast=len(to_cast))
    return pl.pallas_call(
        kern,
        grid=(batch, ng, nblk),
        in_specs=[main(0), main(1), main(2), main(3), halo(0), halo(1), halo(2),
                  pl.BlockSpec((rows, HEAD_DIM), lambda b, h, k: (b * nblk + k, 0)),
                  cw(0), cw(1), cw(2),
                  pl.BlockSpec((1, HEAD_DIM), lambda b, h, k: (0, 0)),
                  *[slab(w) for w in to_cast]],
        out_specs=[
            pl.BlockSpec((rows, width), lambda b, h, k: (b * nblk + k, h)),
            pl.BlockSpec((1, hs, HEAD_DIM, HEAD_DIM), lambda b, h, k: (b, h, 0, 0)),
            *[slab(w) for w in to_cast],
        ],
        out_shape=[
            jax.ShapeDtypeStruct((batch * seq, n_heads * HEAD_DIM), BF16),
            jax.ShapeDtypeStruct((batch, n_heads, HEAD_DIM, HEAD_DIM), F32),
            *[jax.ShapeDtypeStruct(w.shape, BF16) for w in to_cast],
        ],
        scratch_shapes=[pltpu.VMEM((hs, HEAD_DIM, HEAD_DIM), F32),
                        pltpu.VMEM((3 * hs, rows + HALO_CONV, HEAD_DIM), F32)],
        compiler_params=_cparams(3),
        name="gdn_prompt",
    )(proj, proj, proj, proj, proj, proj, proj, gates, conv_w, conv_w, conv_w, gnw, *to_cast)


def _gdn_token_step(r, x_ref, cs_ref, gates_ref, cw_ref, gnw_ref, s_in_ref, o_ref, s_out_ref, cs_out_ref, n_heads):
    d = n_heads * HEAD_DIM
    sq = (HEAD_DIM, HEAD_DIM)
    x = x_ref[pl.ds(r, 1), :]
    cw = cw_ref[...]
    conv = cw[CONV_W - 1:CONV_W, :] * x[:, :3 * d]
    for j in range(CONV_W - 1):
        conv = conv + cw[j:j + 1, :] * cs_ref[j, pl.ds(r, 1), :]
    qkv = _silu(conv)

    def put_row(ref, idx, value):
        old = ref[idx]
        rows = lax.broadcasted_iota(jnp.int32, old.shape, 0)
        ref[idx] = jnp.where(rows == r, jnp.broadcast_to(value, old.shape), old)

    for j in range(CONV_W - 2):
        put_row(cs_out_ref, j, cs_ref[j + 1, pl.ds(r, 1), :])
    put_row(cs_out_ref, CONV_W - 2, x[:, :3 * d])

    gates = gates_ref[pl.ds(r, 1), :]
    eg = jnp.exp(gates)
    gnw = gnw_ref[...]
    heads = range(n_heads)
    q = [_l2norm(qkv[:, h * HEAD_DIM:(h + 1) * HEAD_DIM], HEAD_DIM ** -0.5) for h in heads]
    k = [_l2norm(qkv[:, d + h * HEAD_DIM:d + (h + 1) * HEAD_DIM]) for h in heads]
    v = [qkv[:, 2 * d + h * HEAD_DIM:2 * d + (h + 1) * HEAD_DIM] for h in heads]
    kb = [jnp.broadcast_to(k[h], sq).T for h in heads]
    qb = [jnp.broadcast_to(q[h], sq).T for h in heads]
    s1 = [s_in_ref[0, h] * eg[:, h:h + 1] for h in heads]
    ks = [jnp.sum(s1[h] * kb[h], axis=0, keepdims=True) for h in heads]
    delta = [(v[h] - ks[h]) * gates[:, n_heads + h:n_heads + h + 1] for h in heads]
    s2 = [s1[h] + kb[h] * delta[h] for h in heads]
    o_rows = []
    for h in heads:
        s_out_ref[0, h] = s2[h]
        o = jnp.sum(s2[h] * qb[h], axis=0, keepdims=True)
        o_rows.append(_rms(o, gnw) * _silu(x[:, 3 * d + h * HEAD_DIM:3 * d + (h + 1) * HEAD_DIM]))
    put_row(o_ref, slice(None), jnp.concatenate(o_rows, axis=1))


def _mix_out_kernel(oa_ref, u_ref, hist_ref, x_ref, wp_ref, ps_ref, wo_ref, nw_ref, out_ref, *rest,
                    tm, seq, pos0):
    d_pool = u_ref.shape[1]
    d_gdn = oa_ref.shape[1]
    d = out_ref.shape[1]
    n_groups = len(POOL_WINDOWS)
    gd = d_pool // n_groups
    strips = gd // HEAD_DIM
    if seq > 1:
        (ext_ref,) = rest
        start = (pl.program_id(0) * tm) % seq
        for si in range(d_pool // HEAD_DIM):
            lanes = slice(si * HEAD_DIM, (si + 1) * HEAD_DIM)
            ext_ref[si, 0:HALO_POOL, :] = jnp.where(start > 0, hist_ref[:, lanes], 0.0)
            ext_ref[si, HALO_POOL:, :] = u_ref[:, lanes]
        pos = pos0 + start + lax.broadcasted_iota(jnp.int32, (tm, HEAD_DIM), 0)
    else:
        (hist_out_ref,) = rest
        for r in range(POOL_BUF - 1):
            hist_out_ref[r] = hist_ref[r + 1]
        hist_out_ref[POOL_BUF - 1] = u_ref[...]
        pos = jnp.full((tm, HEAD_DIM), pos0, jnp.int32)

    tn = d // n_groups
    oa = oa_ref[...].astype(BF16)
    parts = []
    for gi, win in enumerate(POOL_WINDOWS):
        cols = slice(gi * tn, (gi + 1) * tn)
        out_ref[:, cols] = jnp.dot(oa, wo_ref[:d_gdn, cols], preferred_element_type=F32)
        cnt = jnp.minimum(pos + 1, win).astype(F32)
        pooled = []
        for si in range(gi * strips, (gi + 1) * strips):
            lanes = slice(si * HEAD_DIM, (si + 1) * HEAD_DIM)
            cur = u_ref[:, lanes]
            wsum = cur
            for r in range(1, win):
                if seq > 1:
                    wsum = wsum + ext_ref[si, pl.ds(HALO_POOL - r, tm), :]
                else:
                    wsum = wsum + hist_ref[POOL_BUF - r, :, lanes]
            pooled.append(wsum / cnt - cur)
        ob = _dot(jnp.concatenate(pooled, axis=1), wp_ref[gi]) * ps_ref[:, gi * gd:(gi + 1) * gd]
        parts.append(ob.astype(BF16))
    ob_all = jnp.concatenate(parts, axis=1)
    mix = [out_ref[:, ci * tn:(ci + 1) * tn]
           + jnp.dot(ob_all, wo_ref[d_gdn:, ci * tn:(ci + 1) * tn], preferred_element_type=F32)
           for ci in range(n_groups)]
    ss = sum(jnp.sum(m * m, axis=-1, keepdims=True) for m in mix)
    inv = lax.rsqrt(ss * (1.0 / d) + EPS)
    for ci in range(n_groups):
        cols = slice(ci * tn, (ci + 1) * tn)
        out_ref[:, cols] = x_ref[:, cols] + mix[ci] * inv * nw_ref[:, cols]


def _mix_out(o_a, u, hist, x, w_pool, pool_scale, w_out, norm_w, tm, seq, pos0):
    m, d = x.shape
    d_pool = pool_scale.shape[1]
    d_gdn = o_a.shape[1]
    out_specs = [pl.BlockSpec((tm, d), lambda i: (i, 0))]
    out_shape = [jax.ShapeDtypeStruct((m, d), F32)]
    if seq > 1:
        hb = tm // HALO_POOL
        hist_spec = pl.BlockSpec((HALO_POOL, d_pool), lambda i: (jnp.maximum(i * hb - 1, 0), 0))
        scratch = [pltpu.VMEM((d_pool // HEAD_DIM, tm + HALO_POOL, HEAD_DIM), F32)]
    else:
        hist_spec = pl.BlockSpec((POOL_BUF, tm, d_pool), lambda i: (0, i, 0))
        out_specs.append(hist_spec)
        out_shape.append(jax.ShapeDtypeStruct(hist.shape, F32))
        scratch = []
    kern = functools.partial(_mix_out_kernel, tm=tm, seq=seq, pos0=pos0)
    return pl.pallas_call(
        kern,
        grid=(m // tm,),
        in_specs=[
            pl.BlockSpec((tm, d_gdn), lambda i: (i, 0)),
            pl.BlockSpec((tm, d_pool), lambda i: (i, 0)),
            hist_spec,
            pl.BlockSpec((tm, d), lambda i: (i, 0)),
            pl.BlockSpec(w_pool.shape, lambda i: (0, 0, 0), pipeline_mode=pl.Buffered(1)),
            pl.BlockSpec((1, d_pool), lambda i: (0, 0)),
            pl.BlockSpec(w_out.shape, lambda i: (0, 0), pipeline_mode=pl.Buffered(1)),
            pl.BlockSpec((1, d), lambda i: (0, 0)),
        ],
        out_specs=out_specs,
        out_shape=out_shape,
        scratch_shapes=scratch,
        compiler_params=_cparams(1),
        name="mix_out",
    )(o_a, u, hist, x, w_pool, pool_scale, w_out, norm_w)


def _mlp_kernel(x_ref, nw1_ref, wu_ref, wd_ref, nw2_ref, *rest, n_heads, hosted):
    if hosted:
        tok_in, (out_ref, *tok_out), (h_ref,) = rest[:6], rest[6:10], rest[10:]
    else:
        out_ref, h_ref = rest
    j = pl.program_id(1)

    @pl.when(j == 0)
    def _():
        if hosted:
            tok_out[0][...] = jnp.zeros_like(tok_out[0])
            tok_out[2][...] = jnp.zeros_like(tok_out[2])

    def step(first):
        if first:
            h = _rms(x_ref[...], nw1_ref[...]).astype(BF16)
            h_ref[...] = h
        else:
            h = h_ref[...]
        up = jnp.dot(h, wu_ref[...], preferred_element_type=F32)
        act = jnp.square(jnp.maximum(up, 0.0)).astype(BF16)
        part = jnp.dot(act, wd_ref[...], preferred_element_type=F32)
        out_ref[...] = part if first else out_ref[...] + part
        if hosted:
            _gdn_token_step(j, *tok_in, *tok_out, n_heads)

    pl.when(j == 0)(lambda: step(True))
    pl.when(j > 0)(lambda: step(False))

    @pl.when(j == pl.num_programs(1) - 1)
    def _():
        out_ref[...] = x_ref[...] + _rms(out_ref[...], nw2_ref[...])


def _mlp(x, nw1, w_up, w_down, nw2, tm, tf, token_job=None):
    m, d = x.shape
    f = w_up.shape[1]
    ni, nj = m // tm, f // tf
    in_specs = [
        pl.BlockSpec((tm, d), lambda i, j: (i, 0)),
        pl.BlockSpec((1, d), lambda i, j: (0, 0)),
        pl.BlockSpec((d, tf), lambda i, j: (0, j)),
        pl.BlockSpec((tf, d), lambda i, j: (j, 0)),
        pl.BlockSpec((1, d), lambda i, j: (0, 0)),
    ]
    out_specs = [pl.BlockSpec((tm, d), lambda i, j: (i, 0))]
    out_shape = [jax.ShapeDtypeStruct((m, d), F32)]
    operands = [x, nw1, w_up, w_down, nw2]
    n_heads = 0
    if token_job is not None:
        proj, conv_state, gates, conv_w, gnw, state, n_heads = token_job
        tokens, dg = proj.shape[0], n_heads * HEAD_DIM
        assert tokens == ni * nj and nj % 8 == 0
        cs_spec = pl.BlockSpec((CONV_W - 1, nj, 3 * dg), lambda i, j: (0, i, 0))
        st_spec = pl.BlockSpec((1, n_heads, HEAD_DIM, HEAD_DIM), lambda i, j: (i * nj + j, 0, 0, 0))
        in_specs += [
            pl.BlockSpec((nj, 4 * dg), lambda i, j: (i, 0)),
            cs_spec,
            pl.BlockSpec((nj, HEAD_DIM), lambda i, j: (i, 0)),
            pl.BlockSpec((CONV_W, 3 * dg), lambda i, j: (0, 0)),
            pl.BlockSpec((1, HEAD_DIM), lambda i, j: (0, 0)),
            st_spec,
        ]
        out_specs += [pl.BlockSpec((nj, dg), lambda i, j: (i, 0)), st_spec, cs_spec]
        out_shape += [jax.ShapeDtypeStruct((tokens, dg), F32), jax.ShapeDtypeStruct(state.shape, F32),
                      jax.ShapeDtypeStruct(conv_state.shape, F32)]
        operands += [proj, conv_state, gates, conv_w, gnw, state]
    res = pl.pallas_call(
        functools.partial(_mlp_kernel, n_heads=n_heads, hosted=token_job is not None),
        grid=(ni, nj),
        in_specs=in_specs,
        out_specs=out_specs,
        out_shape=out_shape,
        scratch_shapes=[pltpu.VMEM((tm, d), BF16)],
        compiler_params=_cparams(2),
        name="mlp",
    )(*operands)
    return res if token_job is not None else res[0]


def _pad_lanes(v, n=HEAD_DIM):
    return jnp.pad(v.reshape(1, -1), ((0, 0), (0, n - v.shape[-1])))


def kernel(x_prompt, x_sample, state_gdn, state_conv, state_pool, norm_pre_mix, w_in, conv_w, a_log, dt_bias,
           norm_gdn_out, w_pool, pool_scale, w_out, norm_post_mix, norm_pre_mlp, w_up, w_down, norm_post_mlp):
    batch, seq, d_model = x_prompt.shape
    dec_batch, dec_seq, _ = x_sample.shape
    assert dec_seq == 1
    depth = w_in.shape[0]
    n_heads = a_log.shape[1]
    d_gdn = n_heads * HEAD_DIM
    d_qkv = 3 * d_gdn
    d_pool = pool_scale.shape[1]
    o_gate = d_qkv + d_gdn

    yp = x_prompt.reshape(batch * seq, d_model)
    ys = x_sample.reshape(dec_batch, d_model)
    outs = [[] for _ in range(6)]
    for l in range(depth):
        alog = _pad_lanes(a_log[l])
        dtb = _pad_lanes(dt_bias[l])
        gnw = norm_gdn_out[l].reshape(1, HEAD_DIM)
        npm = norm_pre_mix[l].reshape(1, d_model)
        npo = norm_post_mix[l].reshape(1, d_model)
        nm1 = norm_pre_mlp[l].reshape(1, d_model)
        nm2 = norm_post_mlp[l].reshape(1, d_model)
        ps = pool_scale[l].reshape(1, d_pool)

        proj_s, u_s, gates_s, *w_in_b = _in_proj(
            ys, npm, jnp.swapaxes(w_in[l], 0, 1), o_gate, d_pool, alog, dtb, n_heads,
            tm=dec_batch, tn=512, emit=True)

        proj_p, u_p, gates_p = _in_proj(
            yp, npm, w_in_b, o_gate, d_pool, alog, dtb, n_heads, tm=1024, tn=1024, emit=False)
        to_cast = (w_up[l], w_down[l], w_out[l], w_pool[l].reshape(-1, w_pool.shape[-1]))
        oa_p, s_p, w_up_b, w_down_b, w_out_b, w_pool_b = _gdn_prompt(
            proj_p, gates_p, conv_w[l], gnw, to_cast, batch, seq, n_heads, hs=8, rows=256)
        w_pool_b = w_pool_b.reshape(w_pool.shape[1:])
        (x1_p,) = _mix_out(oa_p, u_p, u_p, yp, w_pool_b, ps, w_out_b, npo, tm=512, seq=seq, pos0=0)
        token_job = (proj_s, jnp.swapaxes(state_conv[l], 0, 1), gates_s, conv_w[l], gnw, state_gdn[l], n_heads)
        yp, oa_s, s_s, cs_new = _mlp(x1_p, nm1, w_up_b, w_down_b, nm2, tm=512, tf=1024, token_job=token_job)
        outs[0].append(s_p)
        outs[1].append(proj_p.reshape(batch, seq, -1)[:, seq - (CONV_W - 1):, :d_qkv])
        outs[2].append(u_p.reshape(batch, seq, -1)[:, seq - POOL_BUF:])

        x1_s, hist_new = _mix_out(oa_s, u_s, jnp.swapaxes(state_pool[l], 0, 1), ys, w_pool_b, ps, w_out_b, npo,
                                  tm=dec_batch, seq=1, pos0=PAST_LEN)
        ys = _mlp(x1_s, nm1, w_up_b, w_down_b, nm2, tm=dec_batch, tf=1024)
        outs[3].append(s_s)
        outs[4].append(jnp.swapaxes(cs_new, 0, 1))
        outs[5].append(jnp.swapaxes(hist_new, 0, 1))

    return (yp.reshape(batch, seq, d_model), ys.reshape(dec_batch, dec_seq, d_model),
            *[jnp.stack(o) for o in outs])
```

```python
import functools

import jax
import jax.numpy as jnp
from jax import lax
from jax.experimental import pallas as pl
from jax.experimental.pallas import tpu as pltpu

F32 = jnp.float32
BF16 = jnp.bfloat16

EPS = 1e-6
HEAD_DIM = 128
CONV_W = 4
POOL_WINDOWS = (2, 4, 8, 16)
POOL_BUF = max(POOL_WINDOWS) - 1
PAST_LEN = 16384
HALO_CONV = 8
HALO_POOL = 16
GDN_CHUNK = 128
SUBLANES = 8
MASKED = -1e30
VMEM_LIMIT = 56 * 1024 * 1024

IN_PROJ_TM, IN_PROJ_TN, SAMPLE_TN = 1024, 1024, 512
GDN_HEADS_PER_STEP, GDN_ROWS = 8, 256
MIX_TM = 512
MLP_TM, MLP_TF = 512, 1024


def _cparams(n_axes):
    return pltpu.CompilerParams(
        dimension_semantics=("arbitrary",) * n_axes, vmem_limit_bytes=VMEM_LIMIT)


def _dot(a, b):
    return jnp.dot(a.astype(BF16), b.astype(BF16), preferred_element_type=F32)


def _dot_nt(a, b):
    return lax.dot_general(a.astype(BF16), b.astype(BF16), (((1,), (1,)), ((), ())),
                           preferred_element_type=F32)


def _rms(x, w):
    return x * lax.rsqrt(jnp.mean(x * x, axis=-1, keepdims=True) + EPS) * w


def _silu(x):
    h = 0.5 * x
    return h + h * jnp.tanh(h)


def _softplus(x):
    return jnp.maximum(x, 0.0) + jnp.log(1.0 + jnp.exp(-jnp.abs(x)))


def _l2norm(x, scale=1.0):
    return x * (lax.rsqrt(jnp.sum(x * x, axis=-1, keepdims=True) + EPS) * scale)


def _inproj_kernel(x_ref, nw_ref, wa_ref, wt_ref, wc_ref, alog_ref, dtb_ref, *rest, n_heads, n_main, emit):
    if emit:
        out_ref, u_ref, gates_ref, wa_out, wt_out, wc_out, h_ref = rest
    else:
        out_ref, u_ref, gates_ref, h_ref = rest
    j = pl.program_id(1)

    def main_step(first):
        if first:
            h = _rms(x_ref[...], nw_ref[...]).astype(BF16)
            h_ref[...] = h
        else:
            h = h_ref[...]
        w = wa_ref[...].T.astype(BF16) if emit else wa_ref[...]
        if emit:
            wa_out[...] = w
        out_ref[...] = jnp.dot(h, w, preferred_element_type=F32)

    pl.when(j == 0)(lambda: main_step(True))
    pl.when((j > 0) & (j < n_main))(lambda: main_step(False))

    @pl.when(j == n_main)
    def _():
        if emit:
            n_gate = 2 * n_heads
            blk = wt_ref[...]
            pool_rows = jnp.concatenate([blk[n_gate:], wc_ref[...]], axis=0)
            gate_rows = jnp.concatenate([blk[:n_gate], jnp.zeros((HEAD_DIM - n_gate, blk.shape[1]), F32)], axis=0)
            wt = pool_rows.T.astype(BF16)
            wc = gate_rows.T.astype(BF16)
            wt_out[...] = wt
            wc_out[...] = wc
        else:
            wt = wt_ref[...]
            wc = wc_ref[...]
        h = h_ref[...]
        u_ref[...] = jnp.dot(h, wt, preferred_element_type=F32)
        ab = jnp.dot(h, wc, preferred_element_type=F32)
        lane = lax.broadcasted_iota(jnp.int32, ab.shape, 1)
        g = -jnp.exp(alog_ref[...]) * _softplus(ab + dtb_ref[...])
        gates_ref[...] = jnp.where(lane < n_heads, g, jax.nn.sigmoid(ab))


def _in_proj(x, norm_w, weights, n, d_pool, alog, dtb, n_heads, tm, tn, emit):
    m, d = x.shape
    n_main = n // tn
    n_gate = 2 * n_heads
    vec = pl.BlockSpec((1, HEAD_DIM), lambda i, j: (0, 0))
    out_specs = [
        pl.BlockSpec((tm, tn), lambda i, j: (i, jnp.minimum(j, n_main - 1))),
        pl.BlockSpec((tm, d_pool), lambda i, j: (i, 0)),
        pl.BlockSpec((tm, HEAD_DIM), lambda i, j: (i, 0)),
    ]
    out_shape = [jax.ShapeDtypeStruct((m, n), F32), jax.ShapeDtypeStruct((m, d_pool), F32),
                 jax.ShapeDtypeStruct((m, HEAD_DIM), F32)]
    wa_bf_spec = pl.BlockSpec((d, tn), lambda i, j: (0, jnp.minimum(j, n_main - 1)))
    wt_bf_spec = pl.BlockSpec((d, d_pool), lambda i, j: (0, 0), pipeline_mode=pl.Buffered(1))
    wc_bf_spec = pl.BlockSpec((d, HEAD_DIM), lambda i, j: (0, 0))
    if emit:
        assert n % d_pool == 0 and (n + d_pool) % n_gate == 0 and n_gate % SUBLANES == 0
        operands = (weights, weights, weights)
        w_specs = [
            pl.BlockSpec((tn, d), lambda i, j: (jnp.minimum(j, n_main - 1), 0)),
            pl.BlockSpec((d_pool, d), lambda i, j: (n // d_pool, 0), pipeline_mode=pl.Buffered(1)),
            pl.BlockSpec((n_gate, d), lambda i, j: ((n + d_pool) // n_gate, 0)),
        ]
        out_specs += [wa_bf_spec, pl.BlockSpec((d, d_pool), lambda i, j: (0, 0)), wc_bf_spec]
        out_shape += [jax.ShapeDtypeStruct((d, n), BF16), jax.ShapeDtypeStruct((d, d_pool), BF16),
                      jax.ShapeDtypeStruct((d, HEAD_DIM), BF16)]
    else:
        operands = weights
        w_specs = [wa_bf_spec, wt_bf_spec, wc_bf_spec]
    return pl.pallas_call(
        functools.partial(_inproj_kernel, n_heads=n_heads, n_main=n_main, emit=emit),
        grid=(m // tm, n_main + 1),
        in_specs=[
            pl.BlockSpec((tm, d), lambda i, j: (i, 0)),
            pl.BlockSpec((1, d), lambda i, j: (0, 0)),
            *w_specs,
            vec, vec,
        ],
        out_specs=out_specs,
        out_shape=out_shape,
        scratch_shapes=[pltpu.VMEM((tm, d), BF16)],
        compiler_params=_cparams(2),
        name="in_proj",
    )(x, norm_w, *operands, alog, dtb)


def _lane_column(x, lane_idx):
    lane = lax.broadcasted_iota(jnp.int32, x.shape, 1)
    col = jnp.sum(jnp.where(lane == lane_idx, x, 0.0), axis=1, keepdims=True)
    return jnp.broadcast_to(col, x.shape)


def _unit_lower_inverses(lms):
    n = lms[0].shape[0]
    row = lax.broadcasted_iota(jnp.int32, (n, n), 0)
    col = lax.broadcasted_iota(jnp.int32, (n, n), 1)
    eye = jnp.where(row == col, 1.0, 0.0)
    def mm(a, b):
        return jnp.dot(a, b, preferred_element_type=F32)

    qs = [eye - lm for lm in lms]
    lbs = [lm.astype(BF16) for lm in lms]
    nks = [mm(lb, lb) for lb in lbs]
    power = 2
    while 2 * power < n:
        nbs = [nk.astype(BF16) for nk in nks]
        res = [mm(nb, jnp.concatenate([nb, q.astype(BF16)], axis=1)) for nb, q in zip(nbs, qs)]
        nks = [r[:, :n] for r in res]
        qs = [q + r[:, n:] for q, r in zip(qs, res)]
        power *= 2
    return [q + _dot(nk, q) for nk, q in zip(nks, qs)]


def _gdn_prompt_kernel(q_ref, k_ref, v_ref, gate_ref, hq_ref, hk_ref, hv_ref, gates_ref,
                       cwq_ref, cwk_ref, cwv_ref, gnw_ref, *rest, n_heads, hs, rows, n_cast):
    cast_in = rest[:n_cast]
    o_ref, sout_ref = rest[n_cast:n_cast + 2]
    cast_out = rest[n_cast + 2:2 * n_cast + 2]
    s_ref, ext_ref = rest[2 * n_cast + 2:]
    hg = pl.program_id(1)
    blk = pl.program_id(2)
    c = GDN_CHUNK

    for src, dst in zip(cast_in, cast_out):
        dst[...] = src[...].astype(dst.dtype)

    @pl.when(blk == 0)
    def _():
        s_ref[...] = jnp.zeros_like(s_ref)

    for a, (x_ref, halo_ref) in enumerate(((q_ref, hq_ref), (k_ref, hk_ref), (v_ref, hv_ref))):
        for hi in range(hs):
            lanes = slice(hi * HEAD_DIM, (hi + 1) * HEAD_DIM)
            ext_ref[a * hs + hi, 0:HALO_CONV, :] = jnp.where(blk > 0, halo_ref[:, lanes], 0.0)
            ext_ref[a * hs + hi, HALO_CONV:, :] = x_ref[:, lanes]

    def conv_silu(a, cw_ref, p):
        ci, hi = p
        lanes = slice(hi * HEAD_DIM, (hi + 1) * HEAD_DIM)
        acc = None
        for j in range(CONV_W):
            term = (cw_ref[j:j + 1, lanes]
                    * ext_ref[a * hs + hi, pl.ds(HALO_CONV - (CONV_W - 1) + j + ci * c, c), :])
            acc = term if acc is None else acc + term
        return _silu(acc)

    row = lax.broadcasted_iota(jnp.int32, (c, c), 0)
    col = lax.broadcasted_iota(jnp.int32, (c, c), 1)
    tril = row >= col
    strict = row > col
    tril_f = jnp.where(tril, 1.0, 0.0)
    gnw = gnw_ref[...]
    heads = range(hs)

    n_chunks = rows // c
    probs = [(ci, hi) for ci in range(n_chunks) for hi in heads]
    n_p = range(len(probs))
    gates = [gates_ref[ci * c:(ci + 1) * c, :] for ci in range(n_chunks)]
    gcum = [jnp.dot(tril_f, g, preferred_element_type=F32, precision=lax.Precision.HIGHEST) for g in gates]
    qc = [_l2norm(conv_silu(0, cwq_ref, p), HEAD_DIM ** -0.5) for p in probs]
    kc = [_l2norm(conv_silu(1, cwk_ref, p)) for p in probs]
    vc = [conv_silu(2, cwv_ref, p) for p in probs]
    gc = [_lane_column(gcum[ci], hg * hs + hi) for ci, hi in probs]
    bc = [_lane_column(gates[ci], hg * hs + hi + n_heads) for ci, hi in probs]
    decay = [jnp.exp(jnp.where(tril, g - g.T, MASKED)) for g in gc]
    egc = [jnp.exp(g) for g in gc]
    g_last = [g[c - 1:c, :] for g in gc]
    kb = [k.astype(BF16) for k in kc]
    kk = [_dot_nt(k, k) for k in kb]
    qk = [_dot_nt(q, k) for q, k in zip(qc, kb)]
    lm = [jnp.where(strict, kk[p] * bc[p] * decay[p], 0.0) for p in n_p]
    t_inv = _unit_lower_inverses(lm)
    uw = [_dot(t_inv[p], jnp.concatenate([vc[p] * bc[p], kc[p] * (bc[p] * egc[p])], axis=1)) for p in n_p]
    intra = [qk[p] * decay[p] for p in n_p]
    kdt = [(kc[p] * jnp.exp(g_last[p] - gc[p])).T for p in n_p]
    wq = [jnp.concatenate([uw[p][:, HEAD_DIM:], qc[p] * egc[p]], axis=0) for p in n_p]
    ik = [jnp.concatenate([intra[p], kdt[p]], axis=0) for p in n_p]

    s = [s_ref[hi] for hi in heads]
    for ci in range(n_chunks):
        ps = [ci * hs + hi for hi in heads]
        r1 = [_dot(wq[p], s[hi]) for hi, p in zip(heads, ps)]
        v_new = [uw[p][:, :HEAD_DIM] - r1[hi][:c] for hi, p in zip(heads, ps)]
        r2 = [_dot(ik[p], v_new[hi]) for hi, p in zip(heads, ps)]
        s = [s[hi] * jnp.exp(g_last[p]) + r2[hi][c:] for hi, p in zip(heads, ps)]
        for hi in heads:
            lanes = slice(hi * HEAD_DIM, (hi + 1) * HEAD_DIM)
            o = _rms(r1[hi][c:] + r2[hi][:c], gnw) * _silu(gate_ref[ci * c:(ci + 1) * c, lanes])
            o_ref[ci * c:(ci + 1) * c, lanes] = o.astype(o_ref.dtype)

    for hi in range(hs):
        s_ref[hi] = s[hi]

    @pl.when(blk == pl.num_programs(2) - 1)
    def _():
        for hi in range(hs):
            sout_ref[0, hi] = s[hi]


def _gdn_prompt(proj, gates, conv_w, gnw, to_cast, batch, seq, n_heads, hs, rows):
    nblk = seq // rows
    hb = rows // HALO_CONV
    ng = n_heads // hs
    width = hs * HEAD_DIM
    steps = batch * ng * nblk

    def slab(w):
        return pl.BlockSpec((w.shape[0] // steps, w.shape[1]), lambda b, h, k: ((b * ng + h) * nblk + k, 0))

    def main(off):
        return pl.BlockSpec((rows, width), lambda b, h, k: (b * nblk + k, off * ng + h))

    def halo(off):
        return pl.BlockSpec(
            (HALO_CONV, width), lambda b, h, k: (jnp.maximum((b * nblk + k) * hb - 1, 0), off * ng + h))

    def cw(off):
        return pl.BlockSpec((CONV_W, width), lambda b, h, k: (0, off * ng + h))

    kern = functools.partial(_gdn_prompt_kernel, n_heads=n_heads, hs=hs, rows=rows, n_cast=len(to_cast))
    return pl.pallas_call(
        kern,
        grid=(batch, ng, nblk),
        in_specs=[main(0), main(1), main(2), main(3), halo(0), halo(1), halo(2),
                  pl.BlockSpec((rows, HEAD_DIM), lambda b, h, k: (b * nblk + k, 0)),
                  cw(0), cw(1), cw(2),
                  pl.BlockSpec((1, HEAD_DIM), lambda b, h, k: (0, 0)),
                  *[slab(w) for w in to_cast]],
        out_specs=[
            pl.BlockSpec((rows, width), lambda b, h, k: (b * nblk + k, h)),
            pl.BlockSpec((1, hs, HEAD_DIM, HEAD_DIM), lambda b, h, k: (b, h, 0, 0)),
            *[slab(w) for w in to_cast],
        ],
        out_shape=[
            jax.ShapeDtypeStruct((batch * seq, n_heads * HEAD_DIM), BF16),
            jax.ShapeDtypeStruct((batch, n_heads, HEAD_DIM, HEAD_DIM), F32),
            *[jax.ShapeDtypeStruct(w.shape, BF16) for w in to_cast],
        ],
        scratch_shapes=[pltpu.VMEM((hs, HEAD_DIM, HEAD_DIM), F32),
                        pltpu.VMEM((3 * hs, rows + HALO_CONV, HEAD_DIM), F32)],
        compiler_params=_cparams(3),
        name="gdn_prompt",
    )(proj, proj, proj, proj, proj, proj, proj, gates, conv_w, conv_w, conv_w, gnw, *to_cast)


def _gdn_token_step(r, x_ref, cs_ref, gates_ref, cw_ref, gnw_ref, s_in_ref, o_ref, s_out_ref, cs_out_ref, n_heads):
    d = n_heads * HEAD_DIM
    sq = (HEAD_DIM, HEAD_DIM)
    x = x_ref[pl.ds(r, 1), :]
    cw = cw_ref[...]
    conv = cw[CONV_W - 1:CONV_W, :] * x[:, :3 * d]
    for j in range(CONV_W - 1):
        conv = conv + cw[j:j + 1, :] * cs_ref[j, pl.ds(r, 1), :]
    qkv = _silu(conv)

    def put_row(ref, idx, value):
        old = ref[idx]
        rows = lax.broadcasted_iota(jnp.int32, old.shape, 0)
        ref[idx] = jnp.where(rows == r, jnp.broadcast_to(value, old.shape), old)

    for j in range(CONV_W - 2):
        put_row(cs_out_ref, j, cs_ref[j + 1, pl.ds(r, 1), :])
    put_row(cs_out_ref, CONV_W - 2, x[:, :3 * d])

    gates = gates_ref[pl.ds(r, 1), :]
    eg = jnp.exp(gates)
    gnw = gnw_ref[...]
    heads = range(n_heads)
    q = [_l2norm(qkv[:, h * HEAD_DIM:(h + 1) * HEAD_DIM], HEAD_DIM ** -0.5) for h in heads]
    k = [_l2norm(qkv[:, d + h * HEAD_DIM:d + (h + 1) * HEAD_DIM]) for h in heads]
    v = [qkv[:, 2 * d + h * HEAD_DIM:2 * d + (h + 1) * HEAD_DIM] for h in heads]
    kb = [jnp.broadcast_to(k[h], sq).T for h in heads]
    qb = [jnp.broadcast_to(q[h], sq).T for h in heads]
    s1 = [s_in_ref[0, h] * eg[:, h:h + 1] for h in heads]
    ks = [jnp.sum(s1[h] * kb[h], axis=0, keepdims=True) for h in heads]
    delta = [(v[h] - ks[h]) * gates[:, n_heads + h:n_heads + h + 1] for h in heads]
    s2 = [s1[h] + kb[h] * delta[h] for h in heads]
    o_rows = []
    for h in heads:
        s_out_ref[0, h] = s2[h]
        o = jnp.sum(s2[h] * qb[h], axis=0, keepdims=True)
        o_rows.append(_rms(o, gnw) * _silu(x[:, 3 * d + h * HEAD_DIM:3 * d + (h + 1) * HEAD_DIM]))
    put_row(o_ref, slice(None), jnp.concatenate(o_rows, axis=1))


def _mix_out_kernel(oa_ref, u_ref, hist_ref, x_ref, wp_ref, ps_ref, wo_ref, nw_ref, out_ref, *rest,
                    tm, seq, pos0):
    d_pool = u_ref.shape[1]
    d_gdn = oa_ref.shape[1]
    d = out_ref.shape[1]
    n_groups = len(POOL_WINDOWS)
    gd = d_pool // n_groups
    strips = gd // HEAD_DIM
    if seq > 1:
        (ext_ref,) = rest
        start = (pl.program_id(0) * tm) % seq
        for si in range(d_pool // HEAD_DIM):
            lanes = slice(si * HEAD_DIM, (si + 1) * HEAD_DIM)
            ext_ref[si, 0:HALO_POOL, :] = jnp.where(start > 0, hist_ref[:, lanes], 0.0)
            ext_ref[si, HALO_POOL:, :] = u_ref[:, lanes]
        pos = pos0 + start + lax.broadcasted_iota(jnp.int32, (tm, HEAD_DIM), 0)
    else:
        (hist_out_ref,) = rest
        for r in range(POOL_BUF - 1):
            hist_out_ref[r] = hist_ref[r + 1]
        hist_out_ref[POOL_BUF - 1] = u_ref[...]
        pos = jnp.full((tm, HEAD_DIM), pos0, jnp.int32)

    tn = d // n_groups
    oa = oa_ref[...].astype(BF16)
    parts = []
    for gi, win in enumerate(POOL_WINDOWS):
        cols = slice(gi * tn, (gi + 1) * tn)
        out_ref[:, cols] = jnp.dot(oa, wo_ref[:d_gdn, cols], preferred_element_type=F32)
        cnt = jnp.minimum(pos + 1, win).astype(F32)
        pooled = []
        for si in range(gi * strips, (gi + 1) * strips):
            lanes = slice(si * HEAD_DIM, (si + 1) * HEAD_DIM)
            cur = u_ref[:, lanes]
            wsum = cur
            for r in range(1, win):
                if seq > 1:
                    wsum = wsum + ext_ref[si, pl.ds(HALO_POOL - r, tm), :]
                else:
                    wsum = wsum + hist_ref[POOL_BUF - r, :, lanes]
            pooled.append(wsum / cnt - cur)
        ob = _dot(jnp.concatenate(pooled, axis=1), wp_ref[gi]) * ps_ref[:, gi * gd:(gi + 1) * gd]
        parts.append(ob.astype(BF16))
    ob_all = jnp.concatenate(parts, axis=1)
    mix = [out_ref[:, ci * tn:(ci + 1) * tn]
           + jnp.dot(ob_all, wo_ref[d_gdn:, ci * tn:(ci + 1) * tn], preferred_element_type=F32)
           for ci in range(n_groups)]
    ss = sum(jnp.sum(m * m, axis=-1, keepdims=True) for m in mix)
    inv = lax.rsqrt(ss * (1.0 / d) + EPS)
    for ci in range(n_groups):
        cols = slice(ci * tn, (ci + 1) * tn)
        out_ref[:, cols] = x_ref[:, cols] + mix[ci] * inv * nw_ref[:, cols]


def _mix_out(o_a, u, hist, x, w_pool, pool_scale, w_out, norm_w, tm, seq, pos0):
    m, d = x.shape
    d_pool = pool_scale.shape[1]
    d_gdn = o_a.shape[1]
    out_specs = [pl.BlockSpec((tm, d), lambda i: (i, 0))]
    out_shape = [jax.ShapeDtypeStruct((m, d), F32)]
    if seq > 1:
        hb = tm // HALO_POOL
        hist_spec = pl.BlockSpec((HALO_POOL, d_pool), lambda i: (jnp.maximum(i * hb - 1, 0), 0))
        scratch = [pltpu.VMEM((d_pool // HEAD_DIM, tm + HALO_POOL, HEAD_DIM), F32)]
    else:
        hist_spec = pl.BlockSpec((POOL_BUF, tm, d_pool), lambda i: (0, i, 0))
        out_specs.append(hist_spec)
        out_shape.append(jax.ShapeDtypeStruct(hist.shape, F32))
        scratch = []
    kern = functools.partial(_mix_out_kernel, tm=tm, seq=seq, pos0=pos0)
    return pl.pallas_call(
        kern,
        grid=(m // tm,),
        in_specs=[
            pl.BlockSpec((tm, d_gdn), lambda i: (i, 0)),
            pl.BlockSpec((tm, d_pool), lambda i: (i, 0)),
            hist_spec,
            pl.BlockSpec((tm, d), lambda i: (i, 0)),
            pl.BlockSpec(w_pool.shape, lambda i: (0, 0, 0), pipeline_mode=pl.Buffered(1)),
            pl.BlockSpec((1, d_pool), lambda i: (0, 0)),
            pl.BlockSpec(w_out.shape, lambda i: (0, 0), pipeline_mode=pl.Buffered(1)),
            pl.BlockSpec((1, d), lambda i: (0, 0)),
        ],
        out_specs=out_specs,
        out_shape=out_shape,
        scratch_shapes=scratch,
        compiler_params=_cparams(1),
        name="mix_out",
    )(o_a, u, hist, x, w_pool, pool_scale, w_out, norm_w)


def _mlp_kernel(x_ref, nw1_ref, wu_ref, wd_ref, nw2_ref, *rest, n_heads, hosted, n_ff_tiles):
    if hosted:
        tok_in, (out_ref, *tok_out), (h_ref,) = rest[:6], rest[6:10], rest[10:]
    else:
        out_ref, h_ref = rest
    j = pl.program_id(1)

    @pl.when(j == 0)
    def _():
        if hosted:
            tok_out[0][...] = jnp.zeros_like(tok_out[0])
            tok_out[2][...] = jnp.zeros_like(tok_out[2])

    def step(first, last):
        if first:
            h = _rms(x_ref[...], nw1_ref[...]).astype(BF16)
            h_ref[...] = h
        else:
            h = h_ref[...]
        up = jnp.dot(h, wu_ref[...], preferred_element_type=F32)
        act = jnp.square(jnp.maximum(up, 0.0)).astype(BF16)
        part = jnp.dot(act, wd_ref[...], preferred_element_type=F32)
        acc = part if first else out_ref[...] + part
        out_ref[...] = x_ref[...] + _rms(acc, nw2_ref[...]) if last else acc
        if hosted:
            _gdn_token_step(j, *tok_in, *tok_out, n_heads)

    nj = n_ff_tiles
    if nj == 1:
        step(True, True)
    else:
        pl.when(j == 0)(lambda: step(True, False))
        pl.when((j > 0) & (j < nj - 1))(lambda: step(False, False))
        pl.when(j == nj - 1)(lambda: step(False, True))


def _mlp(x, nw1, w_up, w_down, nw2, tm, tf, token_job=None):
    m, d = x.shape
    f = w_up.shape[1]
    ni, nj = m // tm, f // tf
    in_specs = [
        pl.BlockSpec((tm, d), lambda i, j: (i, 0)),
        pl.BlockSpec((1, d), lambda i, j: (0, 0)),
        pl.BlockSpec((d, tf), lambda i, j: (0, j)),
        pl.BlockSpec((tf, d), lambda i, j: (j, 0)),
        pl.BlockSpec((1, d), lambda i, j: (0, 0)),
    ]
    out_specs = [pl.BlockSpec((tm, d), lambda i, j: (i, 0))]
    out_shape = [jax.ShapeDtypeStruct((m, d), F32)]
    operands = [x, nw1, w_up, w_down, nw2]
    n_heads = 0
    if token_job is not None:
        proj, conv_state, gates, conv_w, gnw, state, n_heads = token_job
        tokens, dg = proj.shape[0], n_heads * HEAD_DIM
        assert tokens == ni * nj and nj % SUBLANES == 0
        cs_spec = pl.BlockSpec((CONV_W - 1, nj, 3 * dg), lambda i, j: (0, i, 0))
        st_spec = pl.BlockSpec((1, n_heads, HEAD_DIM, HEAD_DIM), lambda i, j: (i * nj + j, 0, 0, 0))
        in_specs += [
            pl.BlockSpec((nj, 4 * dg), lambda i, j: (i, 0)),
            cs_spec,
            pl.BlockSpec((nj, HEAD_DIM), lambda i, j: (i, 0)),
            pl.BlockSpec((CONV_W, 3 * dg), lambda i, j: (0, 0)),
            pl.BlockSpec((1, HEAD_DIM), lambda i, j: (0, 0)),
            st_spec,
        ]
        out_specs += [pl.BlockSpec((nj, dg), lambda i, j: (i, 0)), st_spec, cs_spec]
        out_shape += [jax.ShapeDtypeStruct((tokens, dg), F32), jax.ShapeDtypeStruct(state.shape, F32),
                      jax.ShapeDtypeStruct(conv_state.shape, F32)]
        operands += [proj, conv_state, gates, conv_w, gnw, state]
    res = pl.pallas_call(
        functools.partial(_mlp_kernel, n_heads=n_heads, hosted=token_job is not None, n_ff_tiles=nj),
        grid=(ni, nj),
        in_specs=in_specs,
        out_specs=out_specs,
        out_shape=out_shape,
        scratch_shapes=[pltpu.VMEM((tm, d), BF16)],
        compiler_params=_cparams(2),
        name="mlp",
    )(*operands)
    return res if token_job is not None else res[0]


def _pad_lanes(v, n=HEAD_DIM):
    return jnp.pad(v.reshape(1, -1), ((0, 0), (0, n - v.shape[-1])))


def kernel(x_prompt, x_sample, state_gdn, state_conv, state_pool, norm_pre_mix, w_in, conv_w, a_log, dt_bias,
           norm_gdn_out, w_pool, pool_scale, w_out, norm_post_mix, norm_pre_mlp, w_up, w_down, norm_post_mlp):
    batch, seq, d_model = x_prompt.shape
    dec_batch, dec_seq, _ = x_sample.shape
    assert dec_seq == 1
    depth = w_in.shape[0]
    n_heads = a_log.shape[1]
    d_gdn = n_heads * HEAD_DIM
    d_qkv = 3 * d_gdn
    d_pool = pool_scale.shape[1]
    o_gate = d_qkv + d_gdn

    yp = x_prompt.reshape(batch * seq, d_model)
    ys = x_sample.reshape(dec_batch, d_model)
    outs = [[] for _ in range(6)]
    for l in range(depth):
        alog = _pad_lanes(a_log[l])
        dtb = _pad_lanes(dt_bias[l])
        gnw = norm_gdn_out[l].reshape(1, HEAD_DIM)
        npm = norm_pre_mix[l].reshape(1, d_model)
        npo = norm_post_mix[l].reshape(1, d_model)
        nm1 = norm_pre_mlp[l].reshape(1, d_model)
        nm2 = norm_post_mlp[l].reshape(1, d_model)
        ps = pool_scale[l].reshape(1, d_pool)

        proj_s, u_s, gates_s, *w_in_b = _in_proj(
            ys, npm, jnp.swapaxes(w_in[l], 0, 1), o_gate, d_pool, alog, dtb, n_heads,
            tm=dec_batch, tn=SAMPLE_TN, emit=True)

        proj_p, u_p, gates_p = _in_proj(
            yp, npm, w_in_b, o_gate, d_pool, alog, dtb, n_heads, tm=IN_PROJ_TM, tn=IN_PROJ_TN, emit=False)
        to_cast = (w_up[l], w_down[l], w_out[l], w_pool[l].reshape(-1, w_pool.shape[-1]))
        oa_p, s_p, w_up_b, w_down_b, w_out_b, w_pool_b = _gdn_prompt(
            proj_p, gates_p, conv_w[l], gnw, to_cast, batch, seq, n_heads, hs=GDN_HEADS_PER_STEP, rows=GDN_ROWS)
        w_pool_b = w_pool_b.reshape(w_pool.shape[1:])
        (x1_p,) = _mix_out(oa_p, u_p, u_p, yp, w_pool_b, ps, w_out_b, npo, tm=MIX_TM, seq=seq, pos0=0)
        token_job = (proj_s, jnp.swapaxes(state_conv[l], 0, 1), gates_s, conv_w[l], gnw, state_gdn[l], n_heads)
        yp, oa_s, s_s, cs_new = _mlp(x1_p, nm1, w_up_b, w_down_b, nm2, tm=MLP_TM, tf=MLP_TF, token_job=token_job)
        outs[0].append(s_p)
        outs[1].append(proj_p.reshape(batch, seq, -1)[:, seq - (CONV_W - 1):, :d_qkv])
        outs[2].append(u_p.reshape(batch, seq, -1)[:, seq - POOL_BUF:])

        x1_s, hist_new = _mix_out(oa_s, u_s, jnp.swapaxes(state_pool[l], 0, 1), ys, w_pool_b, ps, w_out_b, npo,
                                  tm=dec_batch, seq=1, pos0=PAST_LEN)
        ys = _mlp(x1_s, nm1, w_up_b, w_down_b, nm2, tm=dec_batch, tf=MLP_TF)
        outs[3].append(s_s)
        outs[4].append(jnp.swapaxes(cs_new, 0, 1))
        outs[5].append(jnp.swapaxes(hist_new, 0, 1))

    return (yp.reshape(batch, seq, d_model), ys.reshape(dec_batch, dec_seq, d_model),
            *[jnp.stack(o) for o in outs])
```

```python
import functools

import jax
import jax.numpy as jnp
from jax import lax
from jax.experimental import pallas as pl
from jax.experimental.pallas import tpu as pltpu

F32 = jnp.float32
BF16 = jnp.bfloat16

EPS = 1e-6
HEAD_DIM = 128
CONV_W = 4
POOL_WINDOWS = (2, 4, 8, 16)
POOL_BUF = max(POOL_WINDOWS) - 1
PAST_LEN = 16384
HALO_CONV = 8
HALO_POOL = 16
GDN_CHUNK = 128
INV_BASE = 16
INV_FAST_NORM = 0.75
SUBLANES = 8
MASKED = -1e30
VMEM_LIMIT = 56 * 1024 * 1024

IN_PROJ_TM, IN_PROJ_TN, SAMPLE_TN = 1024, 1024, 1024
GDN_HEADS_PER_STEP, GDN_ROWS = 8, 256
MIX_TM = 512
MLP_TM, MLP_TF = 512, 1024


def _cparams(n_axes):
    return pltpu.CompilerParams(
        dimension_semantics=("arbitrary",) * n_axes, vmem_limit_bytes=VMEM_LIMIT)


def _dot(a, b):
    return jnp.dot(a.astype(BF16), b.astype(BF16), preferred_element_type=F32)


def _dot_nt(a, b):
    return lax.dot_general(a.astype(BF16), b.astype(BF16), (((1,), (1,)), ((), ())),
                           preferred_element_type=F32)


def _rms(x, w):
    return x * lax.rsqrt(jnp.mean(x * x, axis=-1, keepdims=True) + EPS) * w


def _silu(x):
    h = 0.5 * x
    return h + h * jnp.tanh(h)


def _softplus(x):
    return jnp.maximum(x, 0.0) + jnp.log(1.0 + jnp.exp(-jnp.abs(x)))


def _l2norm(x, scale=1.0):
    return x * (lax.rsqrt(jnp.sum(x * x, axis=-1, keepdims=True) + EPS) * scale)


def _inproj_kernel(x_ref, nw_ref, wa_ref, wt_ref, wc_ref, alog_ref, dtb_ref, *rest, n_heads, n_main, emit):
    if emit:
        out_ref, u_ref, gates_ref, wa_out, wt_out, wc_out, h_ref = rest
    else:
        out_ref, u_ref, gates_ref, h_ref = rest
    j = pl.program_id(1)

    def main_step(first):
        if first:
            h = _rms(x_ref[...], nw_ref[...]).astype(BF16)
            h_ref[...] = h
        else:
            h = h_ref[...]
        w = wa_ref[...].T.astype(BF16) if emit else wa_ref[...]
        if emit:
            wa_out[...] = w
        out_ref[...] = jnp.dot(h, w, preferred_element_type=F32)

    pl.when(j == 0)(lambda: main_step(True))
    pl.when((j > 0) & (j < n_main))(lambda: main_step(False))

    @pl.when(j == n_main)
    def _():
        if emit:
            n_gate = 2 * n_heads
            blk = wt_ref[...]
            pool_rows = jnp.concatenate([blk[n_gate:], wc_ref[...]], axis=0)
            gate_rows = jnp.concatenate([blk[:n_gate], jnp.zeros((HEAD_DIM - n_gate, blk.shape[1]), F32)], axis=0)
            wt = pool_rows.T.astype(BF16)
            wc = gate_rows.T.astype(BF16)
            wt_out[...] = wt
            wc_out[...] = wc
        else:
            wt = wt_ref[...]
            wc = wc_ref[...]
        h = h_ref[...]
        u_ref[...] = jnp.dot(h, wt, preferred_element_type=F32)
        ab = jnp.dot(h, wc, preferred_element_type=F32)
        lane = lax.broadcasted_iota(jnp.int32, ab.shape, 1)
        g = -jnp.exp(alog_ref[...]) * _softplus(ab + dtb_ref[...])
        gates_ref[...] = jnp.where(lane < n_heads, g, jax.nn.sigmoid(ab))


def _in_proj(x, norm_w, weights, n, d_pool, alog, dtb, n_heads, tm, tn, emit):
    m, d = x.shape
    n_main = n // tn
    n_gate = 2 * n_heads
    vec = pl.BlockSpec((1, HEAD_DIM), lambda i, j: (0, 0))
    out_specs = [
        pl.BlockSpec((tm, tn), lambda i, j: (i, jnp.minimum(j, n_main - 1))),
        pl.BlockSpec((tm, d_pool), lambda i, j: (i, 0)),
        pl.BlockSpec((tm, HEAD_DIM), lambda i, j: (i, 0)),
    ]
    out_shape = [jax.ShapeDtypeStruct((m, n), F32), jax.ShapeDtypeStruct((m, d_pool), F32),
                 jax.ShapeDtypeStruct((m, HEAD_DIM), F32)]
    wa_bf_spec = pl.BlockSpec((d, tn), lambda i, j: (0, jnp.minimum(j, n_main - 1)))
    wt_bf_spec = pl.BlockSpec((d, d_pool), lambda i, j: (0, 0), pipeline_mode=pl.Buffered(1))
    wc_bf_spec = pl.BlockSpec((d, HEAD_DIM), lambda i, j: (0, 0))
    if emit:
        assert n % d_pool == 0 and (n + d_pool) % n_gate == 0 and n_gate % SUBLANES == 0
        operands = (weights, weights, weights)
        w_specs = [
            pl.BlockSpec((tn, d), lambda i, j: (jnp.minimum(j, n_main - 1), 0)),
            pl.BlockSpec((d_pool, d), lambda i, j: (n // d_pool, 0), pipeline_mode=pl.Buffered(1)),
            pl.BlockSpec((n_gate, d), lambda i, j: ((n + d_pool) // n_gate, 0)),
        ]
        out_specs += [wa_bf_spec, pl.BlockSpec((d, d_pool), lambda i, j: (0, 0)), wc_bf_spec]
        out_shape += [jax.ShapeDtypeStruct((d, n), BF16), jax.ShapeDtypeStruct((d, d_pool), BF16),
                      jax.ShapeDtypeStruct((d, HEAD_DIM), BF16)]
    else:
        operands = weights
        w_specs = [wa_bf_spec, wt_bf_spec, wc_bf_spec]
    return pl.pallas_call(
        functools.partial(_inproj_kernel, n_heads=n_heads, n_main=n_main, emit=emit),
        grid=(m // tm, n_main + 1),
        in_specs=[
            pl.BlockSpec((tm, d), lambda i, j: (i, 0)),
            pl.BlockSpec((1, d), lambda i, j: (0, 0)),
            *w_specs,
            vec, vec,
        ],
        out_specs=out_specs,
        out_shape=out_shape,
        scratch_shapes=[pltpu.VMEM((tm, d), BF16)],
        compiler_params=_cparams(2),
        name="in_proj",
    )(x, norm_w, *operands, alog, dtb)


def _lane_column(x, lane_idx):
    lane = lax.broadcasted_iota(jnp.int32, x.shape, 1)
    col = jnp.sum(jnp.where(lane == lane_idx, x, 0.0), axis=1, keepdims=True)
    return jnp.broadcast_to(col, x.shape)


def _unit_lower_inverses(lms, precise):
    n = lms[0].shape[0]
    row = lax.broadcasted_iota(jnp.int32, (n, n), 0)
    col = lax.broadcasted_iota(jnp.int32, (n, n), 1)
    eye = jnp.where(row == col, 1.0, 0.0)

    def same_block(size):
        shift = size.bit_length() - 1
        return lax.shift_right_logical(row, shift) == lax.shift_right_logical(col, shift)

    if precise:
        op = lambda x: x
        mm = functools.partial(jnp.dot, preferred_element_type=F32, precision=lax.Precision.HIGHEST)
    else:
        op = lambda x: x.astype(BF16)
        mm = functools.partial(jnp.dot, preferred_element_type=F32)

    in_base = same_block(INV_BASE)
    lbs = [jnp.where(in_base, lm, 0.0) for lm in lms]
    qs = [eye - lb for lb in lbs]
    lbs = [op(lb) for lb in lbs]
    nks = [mm(lb, lb) for lb in lbs]
    power = 2
    while 2 * power < INV_BASE:
        nbs = [op(nk) for nk in nks]
        res = [mm(nb, jnp.concatenate([nb, op(q)], axis=1)) for nb, q in zip(nbs, qs)]
        nks = [r[:, :n] for r in res]
        qs = [q + r[:, n:] for q, r in zip(qs, res)]
        power *= 2
    ps = [q + mm(op(nk), op(q)) for nk, q in zip(nks, qs)]
    size = INV_BASE
    while size < n:
        off = jnp.logical_and(same_block(2 * size), jnp.logical_not(same_block(size)))
        cs = [op(jnp.where(off, lm, 0.0)) for lm in lms]
        pbs = [op(p) for p in ps]
        ts = [op(mm(c, pb)) for c, pb in zip(cs, pbs)]
        ps = [p - mm(pb, t) for p, pb, t in zip(ps, pbs, ts)]
        size *= 2
    return ps


def _gdn_prompt_kernel(q_ref, k_ref, v_ref, gate_ref, hq_ref, hk_ref, hv_ref, gates_ref,
                       cwq_ref, cwk_ref, cwv_ref, gnw_ref, *rest, n_heads, hs, rows, n_cast, precise):
    cast_in = rest[:n_cast]
    o_ref, sout_ref, norm_ref = rest[n_cast:n_cast + 3]
    cast_out = rest[n_cast + 3:2 * n_cast + 3]
    s_ref, ext_ref = rest[2 * n_cast + 3:]
    hg = pl.program_id(1)
    blk = pl.program_id(2)
    c = GDN_CHUNK

    for src, dst in zip(cast_in, cast_out):
        dst[...] = src[...].astype(dst.dtype)

    @pl.when(blk == 0)
    def _():
        s_ref[...] = jnp.zeros_like(s_ref)

    for a, (x_ref, halo_ref) in enumerate(((q_ref, hq_ref), (k_ref, hk_ref), (v_ref, hv_ref))):
        for hi in range(hs):
            lanes = slice(hi * HEAD_DIM, (hi + 1) * HEAD_DIM)
            ext_ref[a * hs + hi, 0:HALO_CONV, :] = jnp.where(blk > 0, halo_ref[:, lanes], 0.0)
            ext_ref[a * hs + hi, HALO_CONV:, :] = x_ref[:, lanes]

    def conv_silu(a, cw_ref, p):
        ci, hi = p
        lanes = slice(hi * HEAD_DIM, (hi + 1) * HEAD_DIM)
        acc = None
        for j in range(CONV_W):
            term = (cw_ref[j:j + 1, lanes]
                    * ext_ref[a * hs + hi, pl.ds(HALO_CONV - (CONV_W - 1) + j + ci * c, c), :])
            acc = term if acc is None else acc + term
        return _silu(acc)

    row = lax.broadcasted_iota(jnp.int32, (c, c), 0)
    col = lax.broadcasted_iota(jnp.int32, (c, c), 1)
    tril = row >= col
    strict = row > col
    tril_f = jnp.where(tril, 1.0, 0.0)
    gnw = gnw_ref[...]
    heads = range(hs)

    n_chunks = rows // c
    probs = [(ci, hi) for ci in range(n_chunks) for hi in heads]
    n_p = range(len(probs))
    gates = [gates_ref[ci * c:(ci + 1) * c, :] for ci in range(n_chunks)]
    gcum = [jnp.dot(tril_f, g, preferred_element_type=F32, precision=lax.Precision.HIGHEST) for g in gates]
    qc = [_l2norm(conv_silu(0, cwq_ref, p), HEAD_DIM ** -0.5) for p in probs]
    kc = [_l2norm(conv_silu(1, cwk_ref, p)) for p in probs]
    vc = [conv_silu(2, cwv_ref, p) for p in probs]
    gc = [_lane_column(gcum[ci], hg * hs + hi) for ci, hi in probs]
    bc = [_lane_column(gates[ci], hg * hs + hi + n_heads) for ci, hi in probs]
    decay = [jnp.exp(jnp.where(tril, g - g.T, MASKED)) for g in gc]
    egc = [jnp.exp(g) for g in gc]
    g_last = [g[c - 1:c, :] for g in gc]
    kb = [k.astype(BF16) for k in kc]
    kk = [_dot_nt(k, k) for k in kb]
    qk = [_dot_nt(q, k) for q, k in zip(qc, kb)]
    lm = [jnp.where(strict, kk[p] * bc[p] * decay[p], 0.0) for p in n_p]
    row_sums = [jnp.sum(jnp.abs(x), axis=1, keepdims=True) for x in lm]
    worst = jnp.max(functools.reduce(jnp.maximum, row_sums), axis=0, keepdims=True)
    norm_ref[...] = jnp.broadcast_to(worst.reshape(1, 1, 1), norm_ref.shape)
    t_inv = _unit_lower_inverses(lm, precise)
    uw = [_dot(t_inv[p], jnp.concatenate([vc[p] * bc[p], kc[p] * (bc[p] * egc[p])], axis=1)) for p in n_p]
    intra = [qk[p] * decay[p] for p in n_p]
    kdt = [(kc[p] * jnp.exp(g_last[p] - gc[p])).T for p in n_p]
    wq = [jnp.concatenate([uw[p][:, HEAD_DIM:], qc[p] * egc[p]], axis=0) for p in n_p]
    ik = [jnp.concatenate([intra[p], kdt[p]], axis=0) for p in n_p]

    s = [s_ref[hi] for hi in heads]
    for ci in range(n_chunks):
        ps = [ci * hs + hi for hi in heads]
        r1 = [_dot(wq[p], s[hi]) for hi, p in zip(heads, ps)]
        v_new = [uw[p][:, :HEAD_DIM] - r1[hi][:c] for hi, p in zip(heads, ps)]
        r2 = [_dot(ik[p], v_new[hi]) for hi, p in zip(heads, ps)]
        s = [s[hi] * jnp.exp(g_last[p]) + r2[hi][c:] for hi, p in zip(heads, ps)]
        for hi in heads:
            lanes = slice(hi * HEAD_DIM, (hi + 1) * HEAD_DIM)
            o = _rms(r1[hi][c:] + r2[hi][:c], gnw) * _silu(gate_ref[ci * c:(ci + 1) * c, lanes])
            o_ref[ci * c:(ci + 1) * c, lanes] = o.astype(o_ref.dtype)

    for hi in range(hs):
        s_ref[hi] = s[hi]

    @pl.when(blk == pl.num_programs(2) - 1)
    def _():
        for hi in range(hs):
            sout_ref[0, hi] = s[hi]


def _gdn_prompt(proj, gates, conv_w, gnw, to_cast, batch, seq, n_heads, hs, rows, precise):
    nblk = seq // rows
    hb = rows // HALO_CONV
    ng = n_heads // hs
    width = hs * HEAD_DIM
    steps = batch * ng * nblk

    def slab(w):
        return pl.BlockSpec((w.shape[0] // steps, w.shape[1]), lambda b, h, k: ((b * ng + h) * nblk + k, 0))

    def main(off):
        return pl.BlockSpec((rows, width), lambda b, h, k: (b * nblk + k, off * ng + h))

    def halo(off):
        return pl.BlockSpec(
            (HALO_CONV, width), lambda b, h, k: (jnp.maximum((b * nblk + k) * hb - 1, 0), off * ng + h))

    def cw(off):
        return pl.BlockSpec((CONV_W, width), lambda b, h, k: (0, off * ng + h))

    kern = functools.partial(_gdn_prompt_kernel, n_heads=n_heads, hs=hs, rows=rows, n_cast=len(to_cast),
                             precise=precise)
    return pl.pallas_call(
        kern,
        grid=(batch, ng, nblk),
        in_specs=[main(0), main(1), main(2), main(3), halo(0), halo(1), halo(2),
                  pl.BlockSpec((rows, HEAD_DIM), lambda b, h, k: (b * nblk + k, 0)),
                  cw(0), cw(1), cw(2),
                  pl.BlockSpec((1, HEAD_DIM), lambda b, h, k: (0, 0)),
                  *[slab(w) for w in to_cast]],
        out_specs=[
            pl.BlockSpec((rows, width), lambda b, h, k: (b * nblk + k, h)),
            pl.BlockSpec((1, hs, HEAD_DIM, HEAD_DIM), lambda b, h, k: (b, h, 0, 0)),
            pl.BlockSpec((1, SUBLANES, HEAD_DIM), lambda b, h, k: ((b * ng + h) * nblk + k, 0, 0)),
            *[slab(w) for w in to_cast],
        ],
        out_shape=[
            jax.ShapeDtypeStruct((batch * seq, n_heads * HEAD_DIM), BF16),
            jax.ShapeDtypeStruct((batch, n_heads, HEAD_DIM, HEAD_DIM), F32),
            jax.ShapeDtypeStruct((steps, SUBLANES, HEAD_DIM), F32),
            *[jax.ShapeDtypeStruct(w.shape, BF16) for w in to_cast],
        ],
        scratch_shapes=[pltpu.VMEM((hs, HEAD_DIM, HEAD_DIM), F32),
                        pltpu.VMEM((3 * hs, rows + HALO_CONV, HEAD_DIM), F32)],
        compiler_params=_cparams(3),
        name="gdn_prompt",
    )(proj, proj, proj, proj, proj, proj, proj, gates, conv_w, conv_w, conv_w, gnw, *to_cast)


def _gdn_token_steps(x_ref, cs_ref, gates_ref, cw_ref, gnw_ref, s_in_ref, o_ref, s_out_ref, cs_out_ref, n_heads):
    d = n_heads * HEAD_DIM
    sq = (HEAD_DIM, HEAD_DIM)
    x = x_ref[...]
    cw = cw_ref[...]
    conv = cw[CONV_W - 1:CONV_W, :] * x[:, :3 * d]
    for j in range(CONV_W - 1):
        conv = conv + cw[j:j + 1, :] * cs_ref[j]
    qkv = _silu(conv)
    for j in range(CONV_W - 2):
        cs_out_ref[j] = cs_ref[j + 1]
    cs_out_ref[CONV_W - 2] = x[:, :3 * d]

    gates = gates_ref[...]
    eg = jnp.exp(gates)
    gnw = gnw_ref[...]
    heads = range(n_heads)
    qn = [_l2norm(qkv[:, h * HEAD_DIM:(h + 1) * HEAD_DIM], HEAD_DIM ** -0.5) for h in heads]
    kn = [_l2norm(qkv[:, d + h * HEAD_DIM:d + (h + 1) * HEAD_DIM]) for h in heads]
    for t in range(x.shape[0]):
        row = slice(t, t + 1)
        v = [qkv[row, 2 * d + h * HEAD_DIM:2 * d + (h + 1) * HEAD_DIM] for h in heads]
        kb = [jnp.broadcast_to(kn[h][row], sq).T for h in heads]
        qb = [jnp.broadcast_to(qn[h][row], sq).T for h in heads]
        s1 = [s_in_ref[t, h] * eg[row, h:h + 1] for h in heads]
        ks = [jnp.sum(s1[h] * kb[h], axis=0, keepdims=True) for h in heads]
        delta = [(v[h] - ks[h]) * gates[row, n_heads + h:n_heads + h + 1] for h in heads]
        s2 = [s1[h] + kb[h] * delta[h] for h in heads]
        for h in heads:
            lanes = slice(h * HEAD_DIM, (h + 1) * HEAD_DIM)
            s_out_ref[t, h] = s2[h]
            o = jnp.sum(s2[h] * qb[h], axis=0, keepdims=True)
            o_ref[row, lanes] = _rms(o, gnw) * _silu(x[row, 3 * d + h * HEAD_DIM:3 * d + (h + 1) * HEAD_DIM])


def _mix_out_kernel(oa_ref, u_ref, hist_ref, x_ref, wp_ref, ps_ref, wo_ref, nw_ref, *rest,
                    tm, seq, pos0, n_heads):
    if n_heads:
        tok_in, out_ref, tok_out, rest = rest[:6], rest[6], rest[7:10], rest[10:]
        _gdn_token_steps(*tok_in, *tok_out, n_heads)
    else:
        out_ref, rest = rest[0], rest[1:]
    d_pool = u_ref.shape[1]
    d_gdn = oa_ref.shape[1]
    d = out_ref.shape[1]
    n_groups = len(POOL_WINDOWS)
    gd = d_pool // n_groups
    strips = gd // HEAD_DIM
    if seq > 1:
        (ext_ref,) = rest
        start = (pl.program_id(0) * tm) % seq
        for si in range(d_pool // HEAD_DIM):
            lanes = slice(si * HEAD_DIM, (si + 1) * HEAD_DIM)
            ext_ref[si, 0:HALO_POOL, :] = jnp.where(start > 0, hist_ref[:, lanes], 0.0)
            ext_ref[si, HALO_POOL:, :] = u_ref[:, lanes]
        pos = pos0 + start + lax.broadcasted_iota(jnp.int32, (tm, HEAD_DIM), 0)
    else:
        (hist_out_ref,) = rest
        for r in range(POOL_BUF - 1):
            hist_out_ref[r] = hist_ref[r + 1]
        hist_out_ref[POOL_BUF - 1] = u_ref[...]
        pos = jnp.full((tm, HEAD_DIM), pos0, jnp.int32)

    tn = d // n_groups
    oa = oa_ref[...].astype(BF16)
    parts = []
    for gi, win in enumerate(POOL_WINDOWS):
        cols = slice(gi * tn, (gi + 1) * tn)
        out_ref[:, cols] = jnp.dot(oa, wo_ref[:d_gdn, cols], preferred_element_type=F32)
        cnt = jnp.minimum(pos + 1, win).astype(F32)
        pooled = []
        for si in range(gi * strips, (gi + 1) * strips):
            lanes = slice(si * HEAD_DIM, (si + 1) * HEAD_DIM)
            cur = u_ref[:, lanes]
            wsum = cur
            for r in range(1, win):
                if seq > 1:
                    wsum = wsum + ext_ref[si, pl.ds(HALO_POOL - r, tm), :]
                else:
                    wsum = wsum + hist_ref[POOL_BUF - r, :, lanes]
            pooled.append(wsum / cnt - cur)
        ob = _dot(jnp.concatenate(pooled, axis=1), wp_ref[gi]) * ps_ref[:, gi * gd:(gi + 1) * gd]
        parts.append(ob.astype(BF16))
    ob_all = jnp.concatenate(parts, axis=1)
    mix = [out_ref[:, ci * tn:(ci + 1) * tn]
           + jnp.dot(ob_all, wo_ref[d_gdn:, ci * tn:(ci + 1) * tn], preferred_element_type=F32)
           for ci in range(n_groups)]
    ss = sum(jnp.sum(m * m, axis=-1, keepdims=True) for m in mix)
    inv = lax.rsqrt(ss * (1.0 / d) + EPS)
    for ci in range(n_groups):
        cols = slice(ci * tn, (ci + 1) * tn)
        out_ref[:, cols] = x_ref[:, cols] + mix[ci] * inv * nw_ref[:, cols]


def _mix_out(o_a, u, hist, x, w_pool, pool_scale, w_out, norm_w, tm, seq, pos0, token_job=None):
    m, d = x.shape
    d_pool = pool_scale.shape[1]
    d_gdn = o_a.shape[1]
    out_specs = [pl.BlockSpec((tm, d), lambda i: (i, 0))]
    out_shape = [jax.ShapeDtypeStruct((m, d), F32)]
    tok_specs, tok_operands, n_heads = [], [], 0
    if token_job is not None:
        proj, conv_state, gates, conv_w, gnw, state, n_heads = token_job
        tokens, dg = proj.shape[0], n_heads * HEAD_DIM
        nt = tokens // (m // tm)
        assert seq > 1 and nt * (m // tm) == tokens and nt % SUBLANES == 0
        cs_spec = pl.BlockSpec((CONV_W - 1, nt, 3 * dg), lambda i: (0, i, 0))
        st_spec = pl.BlockSpec((nt, n_heads, HEAD_DIM, HEAD_DIM), lambda i: (i, 0, 0, 0))
        tok_specs = [
            pl.BlockSpec((nt, 4 * dg), lambda i: (i, 0)),
            cs_spec,
            pl.BlockSpec((nt, HEAD_DIM), lambda i: (i, 0)),
            pl.BlockSpec((CONV_W, 3 * dg), lambda i: (0, 0)),
            pl.BlockSpec((1, HEAD_DIM), lambda i: (0, 0)),
            st_spec,
        ]
        tok_operands = [proj, conv_state, gates, conv_w, gnw, state]
        out_specs += [pl.BlockSpec((nt, dg), lambda i: (i, 0)), st_spec, cs_spec]
        out_shape += [jax.ShapeDtypeStruct((tokens, dg), F32), jax.ShapeDtypeStruct(state.shape, F32),
                      jax.ShapeDtypeStruct(conv_state.shape, F32)]
    if seq > 1:
        hb = tm // HALO_POOL
        hist_spec = pl.BlockSpec((HALO_POOL, d_pool), lambda i: (jnp.maximum(i * hb - 1, 0), 0))
        scratch = [pltpu.VMEM((d_pool // HEAD_DIM, tm + HALO_POOL, HEAD_DIM), F32)]
    else:
        hist_spec = pl.BlockSpec((POOL_BUF, tm, d_pool), lambda i: (0, i, 0))
        out_specs.append(hist_spec)
        out_shape.append(jax.ShapeDtypeStruct(hist.shape, F32))
        scratch = []
    kern = functools.partial(_mix_out_kernel, tm=tm, seq=seq, pos0=pos0, n_heads=n_heads)
    return pl.pallas_call(
        kern,
        grid=(m // tm,),
        in_specs=[
            pl.BlockSpec((tm, d_gdn), lambda i: (i, 0)),
            pl.BlockSpec((tm, d_pool), lambda i: (i, 0)),
            hist_spec,
            pl.BlockSpec((tm, d), lambda i: (i, 0)),
            pl.BlockSpec(w_pool.shape, lambda i: (0, 0, 0), pipeline_mode=pl.Buffered(1)),
            pl.BlockSpec((1, d_pool), lambda i: (0, 0)),
            pl.BlockSpec(w_out.shape, lambda i: (0, 0), pipeline_mode=pl.Buffered(1)),
            pl.BlockSpec((1, d), lambda i: (0, 0)),
            *tok_specs,
        ],
        out_specs=out_specs,
        out_shape=out_shape,
        scratch_shapes=scratch,
        compiler_params=_cparams(1),
        name="mix_out",
    )(o_a, u, hist, x, w_pool, pool_scale, w_out, norm_w, *tok_operands)


def _mlp_kernel(x_ref, nw1_ref, wu_ref, wd_ref, nw2_ref, xe_ref, out_ref, oute_ref, h_ref, he_ref, *, n_ff_tiles):
    i = pl.program_id(0)
    j = pl.program_id(1)

    def rows_step(first, last, x_ref, h_ref, out_ref):
        if first:
            h = _rms(x_ref[...], nw1_ref[...]).astype(BF16)
            h_ref[...] = h
        else:
            h = h_ref[...]
        up = jnp.dot(h, wu_ref[...], preferred_element_type=F32)
        act = jnp.square(jnp.maximum(up, 0.0)).astype(BF16)
        part = jnp.dot(act, wd_ref[...], preferred_element_type=F32)
        acc = part if first else out_ref[...] + part
        out_ref[...] = x_ref[...] + _rms(acc, nw2_ref[...]) if last else acc

    def step(first, last):
        rows_step(first, last, x_ref, h_ref, out_ref)
        pl.when(i == 0)(lambda: rows_step(first, last, xe_ref, he_ref, oute_ref))

    nj = n_ff_tiles
    if nj == 1:
        step(True, True)
    else:
        pl.when(j == 0)(lambda: step(True, False))
        pl.when((j > 0) & (j < nj - 1))(lambda: step(False, False))
        pl.when(j == nj - 1)(lambda: step(False, True))


def _mlp(x, nw1, w_up, w_down, nw2, tm, tf, x_extra):
    m, d = x.shape
    me = x_extra.shape[0]
    f = w_up.shape[1]
    whole = lambda i, j: (0, 0)
    out, out_extra = pl.pallas_call(
        functools.partial(_mlp_kernel, n_ff_tiles=f // tf),
        grid=(m // tm, f // tf),
        in_specs=[
            pl.BlockSpec((tm, d), lambda i, j: (i, 0)),
            pl.BlockSpec((1, d), whole),
            pl.BlockSpec((d, tf), lambda i, j: (0, j)),
            pl.BlockSpec((tf, d), lambda i, j: (j, 0)),
            pl.BlockSpec((1, d), whole),
            pl.BlockSpec((me, d), whole),
        ],
        out_specs=[pl.BlockSpec((tm, d), lambda i, j: (i, 0)), pl.BlockSpec((me, d), whole)],
        out_shape=[jax.ShapeDtypeStruct((m, d), F32), jax.ShapeDtypeStruct((me, d), F32)],
        scratch_shapes=[pltpu.VMEM((tm, d), BF16), pltpu.VMEM((me, d), BF16)],
        compiler_params=_cparams(2),
        name="mlp",
    )(x, nw1, w_up, w_down, nw2, x_extra)
    return out, out_extra


def _pad_lanes(v, n=HEAD_DIM):
    return jnp.pad(v.reshape(1, -1), ((0, 0), (0, n - v.shape[-1])))


def kernel(x_prompt, x_sample, state_gdn, state_conv, state_pool, norm_pre_mix, w_in, conv_w, a_log, dt_bias,
           norm_gdn_out, w_pool, pool_scale, w_out, norm_post_mix, norm_pre_mlp, w_up, w_down, norm_post_mlp):
    batch, seq, d_model = x_prompt.shape
    dec_batch, dec_seq, _ = x_sample.shape
    assert dec_seq == 1
    depth = w_in.shape[0]
    n_heads = a_log.shape[1]
    d_gdn = n_heads * HEAD_DIM
    d_qkv = 3 * d_gdn
    d_pool = pool_scale.shape[1]
    o_gate = d_qkv + d_gdn

    yp = x_prompt.reshape(batch * seq, d_model)
    ys = x_sample.reshape(dec_batch, d_model)
    outs = [[] for _ in range(6)]
    for l in range(depth):
        alog = _pad_lanes(a_log[l])
        dtb = _pad_lanes(dt_bias[l])
        gnw = norm_gdn_out[l].reshape(1, HEAD_DIM)
        npm = norm_pre_mix[l].reshape(1, d_model)
        npo = norm_post_mix[l].reshape(1, d_model)
        nm1 = norm_pre_mlp[l].reshape(1, d_model)
        nm2 = norm_post_mlp[l].reshape(1, d_model)
        ps = pool_scale[l].reshape(1, d_pool)

        proj_s, u_s, gates_s, *w_in_b = _in_proj(
            ys, npm, jnp.swapaxes(w_in[l], 0, 1), o_gate, d_pool, alog, dtb, n_heads,
            tm=dec_batch, tn=SAMPLE_TN, emit=True)

        proj_p, u_p, gates_p = _in_proj(
            yp, npm, w_in_b, o_gate, d_pool, alog, dtb, n_heads, tm=IN_PROJ_TM, tn=IN_PROJ_TN, emit=False)
        to_cast = (w_up[l], w_down[l], w_out[l], w_pool[l].reshape(-1, w_pool.shape[-1]))
        gdn = functools.partial(_gdn_prompt, proj_p, gates_p, conv_w[l], gnw, batch=batch, seq=seq,
                                n_heads=n_heads, hs=GDN_HEADS_PER_STEP, rows=GDN_ROWS)
        oa_p, s_p, norm_p, w_up_b, w_down_b, w_out_b, w_pool_b = gdn(to_cast, precise=False)
        oa_p, s_p = lax.cond(jnp.max(norm_p) <= INV_FAST_NORM, lambda: (oa_p, s_p),
                             lambda: tuple(gdn((), precise=True)[:2]))
        w_pool_b = w_pool_b.reshape(w_pool.shape[1:])
        token_job = (proj_s, jnp.swapaxes(state_conv[l], 0, 1), gates_s, conv_w[l], gnw, state_gdn[l], n_heads)
        x1_p, oa_s, s_s, cs_new = _mix_out(oa_p, u_p, u_p, yp, w_pool_b, ps, w_out_b, npo, tm=MIX_TM, seq=seq,
                                           pos0=0, token_job=token_job)
        x1_s, hist_new = _mix_out(oa_s, u_s, jnp.swapaxes(state_pool[l], 0, 1), ys, w_pool_b, ps, w_out_b, npo,
                                  tm=dec_batch, seq=1, pos0=PAST_LEN)
        yp, ys = _mlp(x1_p, nm1, w_up_b, w_down_b, nm2, MLP_TM, MLP_TF, x1_s)
        outs[0].append(s_p)
        outs[1].append(proj_p.reshape(batch, seq, -1)[:, seq - (CONV_W - 1):, :d_qkv])
        outs[2].append(u_p.reshape(batch, seq, -1)[:, seq - POOL_BUF:])
        outs[3].append(s_s)
        outs[4].append(jnp.swapaxes(cs_new, 0, 1))
        outs[5].append(jnp.swapaxes(hist_new, 0, 1))

    return (yp.reshape(batch, seq, d_model), ys.reshape(dec_batch, dec_seq, d_model),
            *[jnp.stack(o) for o in outs])
```

```python
import functools

import jax
import jax.numpy as jnp
from jax import lax
from jax.experimental import pallas as pl
from jax.experimental.pallas import tpu as pltpu

F32 = jnp.float32
BF16 = jnp.bfloat16

EPS = 1e-6
HEAD_DIM = 128
CONV_W = 4
POOL_WINDOWS = (2, 4, 8, 16)
POOL_BUF = max(POOL_WINDOWS) - 1
PAST_LEN = 16384
HALO_CONV = 8
HALO_POOL = 16
GDN_CHUNK = 128
INV_BASE = 16
INV_FAST_NORM = 0.75
SUBLANES = 8
MASKED = -1e30
VMEM_LIMIT = 56 * 1024 * 1024

IN_PROJ_TM, IN_PROJ_TN, SAMPLE_TN = 1024, 1024, 1024
GDN_HEADS_PER_STEP, GDN_ROWS = 8, 256
MIX_TM = 512
MLP_TM, MLP_TF = 512, 1024


def _cparams(n_axes):
    return pltpu.CompilerParams(
        dimension_semantics=("arbitrary",) * n_axes, vmem_limit_bytes=VMEM_LIMIT)


def _dot(a, b):
    return jnp.dot(a.astype(BF16), b.astype(BF16), preferred_element_type=F32)


def _dot_nt(a, b):
    return lax.dot_general(a.astype(BF16), b.astype(BF16), (((1,), (1,)), ((), ())),
                           preferred_element_type=F32)


def _rms(x, w):
    return x * lax.rsqrt(jnp.mean(x * x, axis=-1, keepdims=True) + EPS) * w


def _silu(x):
    h = 0.5 * x
    return h + h * jnp.tanh(h)


def _softplus(x):
    return jnp.maximum(x, 0.0) + jnp.log(1.0 + jnp.exp(-jnp.abs(x)))


def _l2norm(x, scale=1.0):
    return x * (lax.rsqrt(jnp.sum(x * x, axis=-1, keepdims=True) + EPS) * scale)


def _inproj_kernel(x_ref, nw_ref, wa_ref, wt_ref, wc_ref, alog_ref, dtb_ref, *rest, n_heads, n_main, emit):
    if emit:
        out_ref, u_ref, gates_ref, wa_out, wt_out, wc_out, h_ref = rest
    else:
        out_ref, u_ref, gates_ref, h_ref = rest
    j = pl.program_id(1)

    def main_step(first):
        if first:
            h = _rms(x_ref[...], nw_ref[...]).astype(BF16)
            h_ref[...] = h
        else:
            h = h_ref[...]
        w = wa_ref[...].T.astype(BF16) if emit else wa_ref[...]
        if emit:
            wa_out[...] = w
        out_ref[...] = jnp.dot(h, w, preferred_element_type=F32)

    pl.when(j == 0)(lambda: main_step(True))
    pl.when((j > 0) & (j < n_main))(lambda: main_step(False))

    @pl.when(j == n_main)
    def _():
        if emit:
            n_gate = 2 * n_heads
            blk = wt_ref[...]
            pool_rows = jnp.concatenate([blk[n_gate:], wc_ref[...]], axis=0)
            gate_rows = jnp.concatenate([blk[:n_gate], jnp.zeros((HEAD_DIM - n_gate, blk.shape[1]), F32)], axis=0)
            wt = pool_rows.T.astype(BF16)
            wc = gate_rows.T.astype(BF16)
            wt_out[...] = wt
            wc_out[...] = wc
        else:
            wt = wt_ref[...]
            wc = wc_ref[...]
        h = h_ref[...]
        u_ref[...] = jnp.dot(h, wt, preferred_element_type=F32)
        ab = jnp.dot(h, wc, preferred_element_type=F32)
        lane = lax.broadcasted_iota(jnp.int32, ab.shape, 1)
        g = -jnp.exp(alog_ref[...]) * _softplus(ab + dtb_ref[...])
        gates_ref[...] = jnp.where(lane < n_heads, g, jax.nn.sigmoid(ab))


def _in_proj(x, norm_w, weights, n, d_pool, alog, dtb, n_heads, tm, tn, emit):
    m, d = x.shape
    n_main = n // tn
    n_gate = 2 * n_heads
    vec = pl.BlockSpec((1, HEAD_DIM), lambda i, j: (0, 0))
    out_specs = [
        pl.BlockSpec((tm, tn), lambda i, j: (i, jnp.minimum(j, n_main - 1))),
        pl.BlockSpec((tm, d_pool), lambda i, j: (i, 0)),
        pl.BlockSpec((tm, HEAD_DIM), lambda i, j: (i, 0)),
    ]
    out_shape = [jax.ShapeDtypeStruct((m, n), F32), jax.ShapeDtypeStruct((m, d_pool), F32),
                 jax.ShapeDtypeStruct((m, HEAD_DIM), F32)]
    wa_bf_spec = pl.BlockSpec((d, tn), lambda i, j: (0, jnp.minimum(j, n_main - 1)))
    wt_bf_spec = pl.BlockSpec((d, d_pool), lambda i, j: (0, 0), pipeline_mode=pl.Buffered(1))
    wc_bf_spec = pl.BlockSpec((d, HEAD_DIM), lambda i, j: (0, 0))
    if emit:
        assert n % d_pool == 0 and (n + d_pool) % n_gate == 0 and n_gate % SUBLANES == 0
        operands = (weights, weights, weights)
        w_specs = [
            pl.BlockSpec((tn, d), lambda i, j: (jnp.minimum(j, n_main - 1), 0)),
            pl.BlockSpec((d_pool, d), lambda i, j: (n // d_pool, 0), pipeline_mode=pl.Buffered(1)),
            pl.BlockSpec((n_gate, d), lambda i, j: ((n + d_pool) // n_gate, 0)),
        ]
        out_specs += [wa_bf_spec, pl.BlockSpec((d, d_pool), lambda i, j: (0, 0)), wc_bf_spec]
        out_shape += [jax.ShapeDtypeStruct((d, n), BF16), jax.ShapeDtypeStruct((d, d_pool), BF16),
                      jax.ShapeDtypeStruct((d, HEAD_DIM), BF16)]
    else:
        operands = weights
        w_specs = [wa_bf_spec, wt_bf_spec, wc_bf_spec]
    return pl.pallas_call(
        functools.partial(_inproj_kernel, n_heads=n_heads, n_main=n_main, emit=emit),
        grid=(m // tm, n_main + 1),
        in_specs=[
            pl.BlockSpec((tm, d), lambda i, j: (i, 0)),
            pl.BlockSpec((1, d), lambda i, j: (0, 0)),
            *w_specs,
            vec, vec,
        ],
        out_specs=out_specs,
        out_shape=out_shape,
        scratch_shapes=[pltpu.VMEM((tm, d), BF16)],
        compiler_params=_cparams(2),
        name="in_proj",
    )(x, norm_w, *operands, alog, dtb)


def _lane_column(x, lane_idx):
    lane = lax.broadcasted_iota(jnp.int32, x.shape, 1)
    col = jnp.sum(jnp.where(lane == lane_idx, x, 0.0), axis=1, keepdims=True)
    return jnp.broadcast_to(col, x.shape)


def _unit_lower_inverses(lms, precise):
    n = lms[0].shape[0]
    row = lax.broadcasted_iota(jnp.int32, (n, n), 0)
    col = lax.broadcasted_iota(jnp.int32, (n, n), 1)
    eye = jnp.where(row == col, 1.0, 0.0)

    def same_block(size):
        shift = size.bit_length() - 1
        return lax.shift_right_logical(row, shift) == lax.shift_right_logical(col, shift)

    if precise:
        op = lambda x: x
        mm = functools.partial(jnp.dot, preferred_element_type=F32, precision=lax.Precision.HIGHEST)
    else:
        op = lambda x: x.astype(BF16)
        mm = functools.partial(jnp.dot, preferred_element_type=F32)

    in_base = same_block(INV_BASE)
    lbs = [jnp.where(in_base, lm, 0.0) for lm in lms]
    qs = [eye - lb for lb in lbs]
    lbs = [op(lb) for lb in lbs]
    nks = [mm(lb, lb) for lb in lbs]
    power = 2
    while 2 * power < INV_BASE:
        nbs = [op(nk) for nk in nks]
        res = [mm(nb, jnp.concatenate([nb, op(q)], axis=1)) for nb, q in zip(nbs, qs)]
        nks = [r[:, :n] for r in res]
        qs = [q + r[:, n:] for q, r in zip(qs, res)]
        power *= 2
    ps = [q + mm(op(nk), op(q)) for nk, q in zip(nks, qs)]
    size = INV_BASE
    while size < n:
        off = jnp.logical_and(same_block(2 * size), jnp.logical_not(same_block(size)))
        cs = [op(jnp.where(off, lm, 0.0)) for lm in lms]
        pbs = [op(p) for p in ps]
        ts = [op(mm(c, pb)) for c, pb in zip(cs, pbs)]
        ps = [p - mm(pb, t) for p, pb, t in zip(ps, pbs, ts)]
        size *= 2
    return ps


def _gdn_prompt_kernel(q_ref, k_ref, v_ref, gate_ref, hq_ref, hk_ref, hv_ref, gates_ref,
                       cwq_ref, cwk_ref, cwv_ref, gnw_ref, *rest, n_heads, hs, rows, n_cast, precise):
    cast_in = rest[:n_cast]
    o_ref, sout_ref, norm_ref = rest[n_cast:n_cast + 3]
    cast_out = rest[n_cast + 3:2 * n_cast + 3]
    s_ref, ext_ref = rest[2 * n_cast + 3:]
    hg = pl.program_id(1)
    blk = pl.program_id(2)
    c = GDN_CHUNK

    for src, dst in zip(cast_in, cast_out):
        dst[...] = src[...].astype(dst.dtype)

    @pl.when(blk == 0)
    def _():
        s_ref[...] = jnp.zeros_like(s_ref)

    for a, (x_ref, halo_ref) in enumerate(((q_ref, hq_ref), (k_ref, hk_ref), (v_ref, hv_ref))):
        for hi in range(hs):
            lanes = slice(hi * HEAD_DIM, (hi + 1) * HEAD_DIM)
            ext_ref[a * hs + hi, 0:HALO_CONV, :] = jnp.where(blk > 0, halo_ref[:, lanes], 0.0)
            ext_ref[a * hs + hi, HALO_CONV:, :] = x_ref[:, lanes]

    def conv_silu(a, cw_ref, p):
        ci, hi = p
        lanes = slice(hi * HEAD_DIM, (hi + 1) * HEAD_DIM)
        acc = None
        for j in range(CONV_W):
            term = (cw_ref[j:j + 1, lanes]
                    * ext_ref[a * hs + hi, pl.ds(HALO_CONV - (CONV_W - 1) + j + ci * c, c), :])
            acc = term if acc is None else acc + term
        return _silu(acc)

    row = lax.broadcasted_iota(jnp.int32, (c, c), 0)
    col = lax.broadcasted_iota(jnp.int32, (c, c), 1)
    tril = row >= col
    strict = row > col
    tril_f = jnp.where(tril, 1.0, 0.0)
    gnw = gnw_ref[...]
    heads = range(hs)

    n_chunks = rows // c
    probs = [(ci, hi) for ci in range(n_chunks) for hi in heads]
    n_p = range(len(probs))
    gates = [gates_ref[ci * c:(ci + 1) * c, :] for ci in range(n_chunks)]
    gcum = [jnp.dot(tril_f, g, preferred_element_type=F32, precision=lax.Precision.HIGHEST) for g in gates]
    qc = [_l2norm(conv_silu(0, cwq_ref, p), HEAD_DIM ** -0.5) for p in probs]
    kc = [_l2norm(conv_silu(1, cwk_ref, p)) for p in probs]
    vc = [conv_silu(2, cwv_ref, p) for p in probs]
    gc = [_lane_column(gcum[ci], hg * hs + hi) for ci, hi in probs]
    bc = [_lane_column(gates[ci], hg * hs + hi + n_heads) for ci, hi in probs]
    decay = [jnp.exp(jnp.where(tril, g - g.T, MASKED)) for g in gc]
    egc = [jnp.exp(g) for g in gc]
    g_last = [g[c - 1:c, :] for g in gc]
    kb = [k.astype(BF16) for k in kc]
    kk = [_dot_nt(k, k) for k in kb]
    qk = [_dot_nt(q, k) for q, k in zip(qc, kb)]
    lm = [jnp.where(strict, kk[p] * bc[p] * decay[p], 0.0) for p in n_p]
    row_sums = [jnp.sum(jnp.abs(x), axis=1, keepdims=True) for x in lm]
    worst = jnp.max(functools.reduce(jnp.maximum, row_sums), axis=0, keepdims=True)
    norm_ref[...] = jnp.broadcast_to(worst.reshape(1, 1, 1), norm_ref.shape)
    t_inv = _unit_lower_inverses(lm, precise)
    uw = [_dot(t_inv[p], jnp.concatenate([vc[p] * bc[p], kc[p] * (bc[p] * egc[p])], axis=1)) for p in n_p]
    intra = [qk[p] * decay[p] for p in n_p]
    kdt = [(kc[p] * jnp.exp(g_last[p] - gc[p])).T for p in n_p]
    wq = [jnp.concatenate([uw[p][:, HEAD_DIM:], qc[p] * egc[p]], axis=0) for p in n_p]
    ik = [jnp.concatenate([intra[p], kdt[p]], axis=0) for p in n_p]

    s = [s_ref[hi] for hi in heads]
    for ci in range(n_chunks):
        ps = [ci * hs + hi for hi in heads]
        r1 = [_dot(wq[p], s[hi]) for hi, p in zip(heads, ps)]
        v_new = [uw[p][:, :HEAD_DIM] - r1[hi][:c] for hi, p in zip(heads, ps)]
        r2 = [_dot(ik[p], v_new[hi]) for hi, p in zip(heads, ps)]
        s = [s[hi] * jnp.exp(g_last[p]) + r2[hi][c:] for hi, p in zip(heads, ps)]
        for hi in heads:
            lanes = slice(hi * HEAD_DIM, (hi + 1) * HEAD_DIM)
            o = _rms(r1[hi][c:] + r2[hi][:c], gnw) * _silu(gate_ref[ci * c:(ci + 1) * c, lanes])
            o_ref[ci * c:(ci + 1) * c, lanes] = o.astype(o_ref.dtype)

    for hi in range(hs):
        s_ref[hi] = s[hi]

    @pl.when(blk == pl.num_programs(2) - 1)
    def _():
        for hi in range(hs):
            sout_ref[0, hi] = s[hi]


def _gdn_prompt(proj, gates, conv_w, gnw, to_cast, batch, seq, n_heads, hs, rows, precise):
    nblk = seq // rows
    hb = rows // HALO_CONV
    ng = n_heads // hs
    width = hs * HEAD_DIM
    steps = batch * ng * nblk

    def slab(w):
        return pl.BlockSpec((w.shape[0] // steps, w.shape[1]), lambda b, h, k: ((b * ng + h) * nblk + k, 0))

    def main(off):
        return pl.BlockSpec((rows, width), lambda b, h, k: (b * nblk + k, off * ng + h))

    def halo(off):
        return pl.BlockSpec(
            (HALO_CONV, width), lambda b, h, k: (jnp.maximum((b * nblk + k) * hb - 1, 0), off * ng + h))

    def cw(off):
        return pl.BlockSpec((CONV_W, width), lambda b, h, k: (0, off * ng + h))

    kern = functools.partial(_gdn_prompt_kernel, n_heads=n_heads, hs=hs, rows=rows, n_cast=len(to_cast),
                             precise=precise)
    return pl.pallas_call(
        kern,
        grid=(batch, ng, nblk),
        in_specs=[main(0), main(1), main(2), main(3), halo(0), halo(1), halo(2),
                  pl.BlockSpec((rows, HEAD_DIM), lambda b, h, k: (b * nblk + k, 0)),
                  cw(0), cw(1), cw(2),
                  pl.BlockSpec((1, HEAD_DIM), lambda b, h, k: (0, 0)),
                  *[slab(w) for w in to_cast]],
        out_specs=[
            pl.BlockSpec((rows, width), lambda b, h, k: (b * nblk + k, h)),
            pl.BlockSpec((1, hs, HEAD_DIM, HEAD_DIM), lambda b, h, k: (b, h, 0, 0)),
            pl.BlockSpec((1, SUBLANES, HEAD_DIM), lambda b, h, k: ((b * ng + h) * nblk + k, 0, 0)),
            *[slab(w) for w in to_cast],
        ],
        out_shape=[
            jax.ShapeDtypeStruct((batch * seq, n_heads * HEAD_DIM), BF16),
            jax.ShapeDtypeStruct((batch, n_heads, HEAD_DIM, HEAD_DIM), F32),
            jax.ShapeDtypeStruct((steps, SUBLANES, HEAD_DIM), F32),
            *[jax.ShapeDtypeStruct(w.shape, BF16) for w in to_cast],
        ],
        scratch_shapes=[pltpu.VMEM((hs, HEAD_DIM, HEAD_DIM), F32),
                        pltpu.VMEM((3 * hs, rows + HALO_CONV, HEAD_DIM), F32)],
        compiler_params=_cparams(3),
        name="gdn_prompt",
    )(proj, proj, proj, proj, proj, proj, proj, gates, conv_w, conv_w, conv_w, gnw, *to_cast)


def _gdn_token_steps(x_ref, cs_ref, gates_ref, cw_ref, gnw_ref, s_in_ref, o_ref, s_out_ref, cs_out_ref, n_heads):
    d = n_heads * HEAD_DIM
    sq = (HEAD_DIM, HEAD_DIM)
    x = x_ref[...]
    cw = cw_ref[...]
    conv = cw[CONV_W - 1:CONV_W, :] * x[:, :3 * d]
    for j in range(CONV_W - 1):
        conv = conv + cw[j:j + 1, :] * cs_ref[j]
    qkv = _silu(conv)
    for j in range(CONV_W - 2):
        cs_out_ref[j] = cs_ref[j + 1]
    cs_out_ref[CONV_W - 2] = x[:, :3 * d]

    gates = gates_ref[...]
    eg = jnp.exp(gates)
    gnw = gnw_ref[...]
    heads = range(n_heads)
    qn = [_l2norm(qkv[:, h * HEAD_DIM:(h + 1) * HEAD_DIM], HEAD_DIM ** -0.5) for h in heads]
    kn = [_l2norm(qkv[:, d + h * HEAD_DIM:d + (h + 1) * HEAD_DIM]) for h in heads]
    for t in range(x.shape[0]):
        row = slice(t, t + 1)
        v = [qkv[row, 2 * d + h * HEAD_DIM:2 * d + (h + 1) * HEAD_DIM] for h in heads]
        kb = [jnp.broadcast_to(kn[h][row], sq).T for h in heads]
        qb = [jnp.broadcast_to(qn[h][row], sq).T for h in heads]
        s1 = [s_in_ref[t, h] * eg[row, h:h + 1] for h in heads]
        ks = [jnp.sum(s1[h] * kb[h], axis=0, keepdims=True) for h in heads]
        delta = [(v[h] - ks[h]) * gates[row, n_heads + h:n_heads + h + 1] for h in heads]
        s2 = [s1[h] + kb[h] * delta[h] for h in heads]
        for h in heads:
            lanes = slice(h * HEAD_DIM, (h + 1) * HEAD_DIM)
            s_out_ref[t, h] = s2[h]
            o = jnp.sum(s2[h] * qb[h], axis=0, keepdims=True)
            o_ref[row, lanes] = _rms(o, gnw) * _silu(x[row, 3 * d + h * HEAD_DIM:3 * d + (h + 1) * HEAD_DIM])


def _mix_out_kernel(oa_ref, u_ref, hist_ref, wp_ref, ps_ref, wo_ref, *rest, tm, seq, pos0, n_heads):
    if n_heads:
        tok_in, out_ref, tok_out, rest = rest[:6], rest[6], rest[7:10], rest[10:]
        _gdn_token_steps(*tok_in, *tok_out, n_heads)
    else:
        out_ref, rest = rest[0], rest[1:]
    d_pool = u_ref.shape[1]
    d_gdn = oa_ref.shape[1]
    d = out_ref.shape[1]
    n_groups = len(POOL_WINDOWS)
    gd = d_pool // n_groups
    strips = gd // HEAD_DIM
    if seq > 1:
        (ext_ref,) = rest
        start = (pl.program_id(0) * tm) % seq
        for si in range(d_pool // HEAD_DIM):
            lanes = slice(si * HEAD_DIM, (si + 1) * HEAD_DIM)
            ext_ref[si, 0:HALO_POOL, :] = jnp.where(start > 0, hist_ref[:, lanes], 0.0)
            ext_ref[si, HALO_POOL:, :] = u_ref[:, lanes]
        pos = pos0 + start + lax.broadcasted_iota(jnp.int32, (tm, HEAD_DIM), 0)
    else:
        (hist_out_ref,) = rest
        for r in range(POOL_BUF - 1):
            hist_out_ref[r] = hist_ref[r + 1]
        hist_out_ref[POOL_BUF - 1] = u_ref[...]
        pos = jnp.full((tm, HEAD_DIM), pos0, jnp.int32)

    tn = d // n_groups
    oa = oa_ref[...].astype(BF16)
    parts = []
    for gi, win in enumerate(POOL_WINDOWS):
        cols = slice(gi * tn, (gi + 1) * tn)
        out_ref[:, cols] = jnp.dot(oa, wo_ref[:d_gdn, cols], preferred_element_type=F32)
        cnt = jnp.minimum(pos + 1, win).astype(F32)
        pooled = []
        for si in range(gi * strips, (gi + 1) * strips):
            lanes = slice(si * HEAD_DIM, (si + 1) * HEAD_DIM)
            cur = u_ref[:, lanes]
            wsum = cur
            for r in range(1, win):
                if seq > 1:
                    wsum = wsum + ext_ref[si, pl.ds(HALO_POOL - r, tm), :]
                else:
                    wsum = wsum + hist_ref[POOL_BUF - r, :, lanes]
            pooled.append(wsum / cnt - cur)
        ob = _dot(jnp.concatenate(pooled, axis=1), wp_ref[gi]) * ps_ref[:, gi * gd:(gi + 1) * gd]
        parts.append(ob.astype(BF16))
    ob_all = jnp.concatenate(parts, axis=1)
    for ci in range(n_groups):
        cols = slice(ci * tn, (ci + 1) * tn)
        out_ref[:, cols] += jnp.dot(ob_all, wo_ref[d_gdn:, cols], preferred_element_type=F32)


def _mix_out(o_a, u, hist, w_pool, pool_scale, w_out, tm, seq, pos0, token_job=None):
    m, d_gdn = o_a.shape
    d = w_out.shape[1]
    d_pool = pool_scale.shape[1]
    out_specs = [pl.BlockSpec((tm, d), lambda i: (i, 0))]
    out_shape = [jax.ShapeDtypeStruct((m, d), F32)]
    tok_specs, tok_operands, n_heads = [], [], 0
    if token_job is not None:
        proj, conv_state, gates, conv_w, gnw, state, n_heads = token_job
        tokens, dg = proj.shape[0], n_heads * HEAD_DIM
        nt = tokens // (m // tm)
        assert seq > 1 and nt * (m // tm) == tokens and nt % SUBLANES == 0
        cs_spec = pl.BlockSpec((CONV_W - 1, nt, 3 * dg), lambda i: (0, i, 0))
        st_spec = pl.BlockSpec((nt, n_heads, HEAD_DIM, HEAD_DIM), lambda i: (i, 0, 0, 0))
        tok_specs = [
            pl.BlockSpec((nt, 4 * dg), lambda i: (i, 0)),
            cs_spec,
            pl.BlockSpec((nt, HEAD_DIM), lambda i: (i, 0)),
            pl.BlockSpec((CONV_W, 3 * dg), lambda i: (0, 0)),
            pl.BlockSpec((1, HEAD_DIM), lambda i: (0, 0)),
            st_spec,
        ]
        tok_operands = [proj, conv_state, gates, conv_w, gnw, state]
        out_specs += [pl.BlockSpec((nt, dg), lambda i: (i, 0)), st_spec, cs_spec]
        out_shape += [jax.ShapeDtypeStruct((tokens, dg), F32), jax.ShapeDtypeStruct(state.shape, F32),
                      jax.ShapeDtypeStruct(conv_state.shape, F32)]
    if seq > 1:
        hb = tm // HALO_POOL
        hist_spec = pl.BlockSpec((HALO_POOL, d_pool), lambda i: (jnp.maximum(i * hb - 1, 0), 0))
        scratch = [pltpu.VMEM((d_pool // HEAD_DIM, tm + HALO_POOL, HEAD_DIM), F32)]
    else:
        hist_spec = pl.BlockSpec((POOL_BUF, tm, d_pool), lambda i: (0, i, 0))
        out_specs.append(hist_spec)
        out_shape.append(jax.ShapeDtypeStruct(hist.shape, F32))
        scratch = []
    kern = functools.partial(_mix_out_kernel, tm=tm, seq=seq, pos0=pos0, n_heads=n_heads)
    return pl.pallas_call(
        kern,
        grid=(m // tm,),
        in_specs=[
            pl.BlockSpec((tm, d_gdn), lambda i: (i, 0)),
            pl.BlockSpec((tm, d_pool), lambda i: (i, 0)),
            hist_spec,
            pl.BlockSpec(w_pool.shape, lambda i: (0, 0, 0), pipeline_mode=pl.Buffered(1)),
            pl.BlockSpec((1, d_pool), lambda i: (0, 0)),
            pl.BlockSpec(w_out.shape, lambda i: (0, 0), pipeline_mode=pl.Buffered(1)),
            *tok_specs,
        ],
        out_specs=out_specs,
        out_shape=out_shape,
        scratch_shapes=scratch,
        compiler_params=_cparams(1),
        name="mix_out",
    )(o_a, u, hist, w_pool, pool_scale, w_out, *tok_operands)


def _mlp_kernel(x_ref, mix_ref, nw0_ref, nw1_ref, wu_ref, wd_ref, nw2_ref, xe_ref, mixe_ref,
                out_ref, oute_ref, h_ref, x1_ref, he_ref, x1e_ref, *, n_ff_tiles):
    i = pl.program_id(0)
    j = pl.program_id(1)

    def rows_step(first, last, x_ref, mix_ref, h_ref, x1_ref, out_ref):
        if first:
            x1 = x_ref[...] + _rms(mix_ref[...], nw0_ref[...])
            x1_ref[...] = x1
            h = _rms(x1, nw1_ref[...]).astype(BF16)
            h_ref[...] = h
        else:
            h = h_ref[...]
        up = jnp.dot(h, wu_ref[...], preferred_element_type=F32)
        act = jnp.square(jnp.maximum(up, 0.0)).astype(BF16)
        part = jnp.dot(act, wd_ref[...], preferred_element_type=F32)
        acc = part if first else out_ref[...] + part
        out_ref[...] = x1_ref[...] + _rms(acc, nw2_ref[...]) if last else acc

    def step(first, last):
        rows_step(first, last, x_ref, mix_ref, h_ref, x1_ref, out_ref)
        pl.when(i == 0)(lambda: rows_step(first, last, xe_ref, mixe_ref, he_ref, x1e_ref, oute_ref))

    nj = n_ff_tiles
    if nj == 1:
        step(True, True)
    else:
        pl.when(j == 0)(lambda: step(True, False))
        pl.when((j > 0) & (j < nj - 1))(lambda: step(False, False))
        pl.when(j == nj - 1)(lambda: step(False, True))


def _mlp(x, mix, nw0, nw1, w_up, w_down, nw2, tm, tf, x_extra, mix_extra):
    m, d = x.shape
    me = x_extra.shape[0]
    f = w_up.shape[1]
    whole = lambda i, j: (0, 0)
    rows = pl.BlockSpec((tm, d), lambda i, j: (i, 0))
    rows_extra = pl.BlockSpec((me, d), whole)
    vec = pl.BlockSpec((1, d), whole)
    out, out_extra = pl.pallas_call(
        functools.partial(_mlp_kernel, n_ff_tiles=f // tf),
        grid=(m // tm, f // tf),
        in_specs=[
            rows, rows, vec, vec,
            pl.BlockSpec((d, tf), lambda i, j: (0, j)),
            pl.BlockSpec((tf, d), lambda i, j: (j, 0)),
            vec, rows_extra, rows_extra,
        ],
        out_specs=[rows, rows_extra],
        out_shape=[jax.ShapeDtypeStruct((m, d), F32), jax.ShapeDtypeStruct((me, d), F32)],
        scratch_shapes=[pltpu.VMEM((tm, d), BF16), pltpu.VMEM((tm, d), F32),
                        pltpu.VMEM((me, d), BF16), pltpu.VMEM((me, d), F32)],
        compiler_params=_cparams(2),
        name="mlp",
    )(x, mix, nw0, nw1, w_up, w_down, nw2, x_extra, mix_extra)
    return out, out_extra


def _pad_lanes(v, n=HEAD_DIM):
    return jnp.pad(v.reshape(1, -1), ((0, 0), (0, n - v.shape[-1])))


def kernel(x_prompt, x_sample, state_gdn, state_conv, state_pool, norm_pre_mix, w_in, conv_w, a_log, dt_bias,
           norm_gdn_out, w_pool, pool_scale, w_out, norm_post_mix, norm_pre_mlp, w_up, w_down, norm_post_mlp):
    batch, seq, d_model = x_prompt.shape
    dec_batch, dec_seq, _ = x_sample.shape
    assert dec_seq == 1
    depth = w_in.shape[0]
    n_heads = a_log.shape[1]
    d_gdn = n_heads * HEAD_DIM
    d_qkv = 3 * d_gdn
    d_pool = pool_scale.shape[1]
    o_gate = d_qkv + d_gdn

    yp = x_prompt.reshape(batch * seq, d_model)
    ys = x_sample.reshape(dec_batch, d_model)
    outs = [[] for _ in range(6)]
    for l in range(depth):
        alog = _pad_lanes(a_log[l])
        dtb = _pad_lanes(dt_bias[l])
        gnw = norm_gdn_out[l].reshape(1, HEAD_DIM)
        npm = norm_pre_mix[l].reshape(1, d_model)
        npo = norm_post_mix[l].reshape(1, d_model)
        nm1 = norm_pre_mlp[l].reshape(1, d_model)
        nm2 = norm_post_mlp[l].reshape(1, d_model)
        ps = pool_scale[l].reshape(1, d_pool)

        proj_s, u_s, gates_s, *w_in_b = _in_proj(
            ys, npm, jnp.swapaxes(w_in[l], 0, 1), o_gate, d_pool, alog, dtb, n_heads,
            tm=dec_batch, tn=SAMPLE_TN, emit=True)

        proj_p, u_p, gates_p = _in_proj(
            yp, npm, w_in_b, o_gate, d_pool, alog, dtb, n_heads, tm=IN_PROJ_TM, tn=IN_PROJ_TN, emit=False)
        to_cast = (w_up[l], w_down[l], w_out[l], w_pool[l].reshape(-1, w_pool.shape[-1]))
        gdn = functools.partial(_gdn_prompt, proj_p, gates_p, conv_w[l], gnw, batch=batch, seq=seq,
                                n_heads=n_heads, hs=GDN_HEADS_PER_STEP, rows=GDN_ROWS)
        oa_p, s_p, norm_p, w_up_b, w_down_b, w_out_b, w_pool_b = gdn(to_cast, precise=False)
        oa_p, s_p = lax.cond(jnp.max(norm_p) <= INV_FAST_NORM, lambda: (oa_p, s_p),
                             lambda: tuple(gdn((), precise=True)[:2]))
        w_pool_b = w_pool_b.reshape(w_pool.shape[1:])
        token_job = (proj_s, jnp.swapaxes(state_conv[l], 0, 1), gates_s, conv_w[l], gnw, state_gdn[l], n_heads)
        mix_p, oa_s, s_s, cs_new = _mix_out(oa_p, u_p, u_p, w_pool_b, ps, w_out_b, tm=MIX_TM, seq=seq, pos0=0,
                                            token_job=token_job)
        mix_s, hist_new = _mix_out(oa_s, u_s, jnp.swapaxes(state_pool[l], 0, 1), w_pool_b, ps, w_out_b,
                                   tm=dec_batch, seq=1, pos0=PAST_LEN)
        yp, ys = _mlp(yp, mix_p, npo, nm1, w_up_b, w_down_b, nm2, MLP_TM, MLP_TF, ys, mix_s)
        outs[0].append(s_p)
        outs[1].append(proj_p.reshape(batch, seq, -1)[:, seq - (CONV_W - 1):, :d_qkv])
        outs[2].append(u_p.reshape(batch, seq, -1)[:, seq - POOL_BUF:])
        outs[3].append(s_s)
        outs[4].append(jnp.swapaxes(cs_new, 0, 1))
        outs[5].append(jnp.swapaxes(hist_new, 0, 1))

    return (yp.reshape(batch, seq, d_model), ys.reshape(dec_batch, dec_seq, d_model),
            *[jnp.stack(o) for o in outs])
```

```python
import functools

import jax
import jax.numpy as jnp
from jax import lax
from jax.experimental import pallas as pl
from jax.experimental.pallas import tpu as pltpu

F32 = jnp.float32
BF16 = jnp.bfloat16

EPS = 1e-6
HEAD_DIM = 128
CONV_W = 4
POOL_WINDOWS = (2, 4, 8, 16)
POOL_BUF = max(POOL_WINDOWS) - 1
PAST_LEN = 16384
HALO_CONV = 8
HALO_POOL = 16
GDN_CHUNK = 128
INV_BASE = 16
INV_FAST_NORM = 0.75
SUBLANES = 8
MASKED = -1e30
VMEM_LIMIT = 56 * 1024 * 1024

IN_PROJ_TM, IN_PROJ_TN, SAMPLE_TN = 1024, 1024, 1024
GDN_HEADS_PER_STEP, GDN_ROWS = 8, 256
MIX_TM = 512
MLP_TM, MLP_TF = 512, 1024


def _cparams(n_axes):
    return pltpu.CompilerParams(
        dimension_semantics=("arbitrary",) * n_axes, vmem_limit_bytes=VMEM_LIMIT)


def _dot(a, b):
    return jnp.dot(a.astype(BF16), b.astype(BF16), preferred_element_type=F32)


def _dot_nt(a, b):
    return lax.dot_general(a.astype(BF16), b.astype(BF16), (((1,), (1,)), ((), ())),
                           preferred_element_type=F32)


def _rms(x, w):
    return x * lax.rsqrt(jnp.mean(x * x, axis=-1, keepdims=True) + EPS) * w


def _silu(x):
    h = 0.5 * x
    return h + h * jnp.tanh(h)


def _softplus(x):
    return jnp.maximum(x, 0.0) + jnp.log(1.0 + jnp.exp(-jnp.abs(x)))


def _l2norm(x, scale=1.0):
    return x * (lax.rsqrt(jnp.sum(x * x, axis=-1, keepdims=True) + EPS) * scale)


def _inproj_kernel(x_ref, nw_ref, wa_ref, wt_ref, wc_ref, alog_ref, dtb_ref, *rest, n_heads, n_main, emit):
    if emit:
        out_ref, u_ref, gates_ref, wa_out, wt_out, wc_out, h_ref = rest
    else:
        out_ref, u_ref, gates_ref, h_ref = rest
    j = pl.program_id(1)

    def main_step(first):
        if first:
            h = _rms(x_ref[...], nw_ref[...]).astype(BF16)
            h_ref[...] = h
        else:
            h = h_ref[...]
        w = wa_ref[...].T.astype(BF16) if emit else wa_ref[...]
        if emit:
            wa_out[...] = w
        out_ref[...] = jnp.dot(h, w, preferred_element_type=F32)

    pl.when(j == 0)(lambda: main_step(True))
    pl.when((j > 0) & (j < n_main))(lambda: main_step(False))

    @pl.when(j == n_main)
    def _():
        if emit:
            n_gate = 2 * n_heads
            blk = wt_ref[...]
            pool_rows = jnp.concatenate([blk[n_gate:], wc_ref[...]], axis=0)
            gate_rows = jnp.concatenate([blk[:n_gate], jnp.zeros((HEAD_DIM - n_gate, blk.shape[1]), F32)], axis=0)
            wt = pool_rows.T.astype(BF16)
            wc = gate_rows.T.astype(BF16)
            wt_out[...] = wt
            wc_out[...] = wc
        else:
            wt = wt_ref[...]
            wc = wc_ref[...]
        h = h_ref[...]
        u_ref[...] = jnp.dot(h, wt, preferred_element_type=F32)
        ab = jnp.dot(h, wc, preferred_element_type=F32)
        lane = lax.broadcasted_iota(jnp.int32, ab.shape, 1)
        g = -jnp.exp(alog_ref[...]) * _softplus(ab + dtb_ref[...])
        gates_ref[...] = jnp.where(lane < n_heads, g, jax.nn.sigmoid(ab))


def _in_proj(x, norm_w, weights, n, d_pool, alog, dtb, n_heads, tm, tn, emit):
    m, d = x.shape
    n_main = n // tn
    n_gate = 2 * n_heads
    vec = pl.BlockSpec((1, HEAD_DIM), lambda i, j: (0, 0))
    out_specs = [
        pl.BlockSpec((tm, tn), lambda i, j: (i, jnp.minimum(j, n_main - 1))),
        pl.BlockSpec((tm, d_pool), lambda i, j: (i, 0)),
        pl.BlockSpec((tm, HEAD_DIM), lambda i, j: (i, 0)),
    ]
    out_shape = [jax.ShapeDtypeStruct((m, n), F32), jax.ShapeDtypeStruct((m, d_pool), F32),
                 jax.ShapeDtypeStruct((m, HEAD_DIM), F32)]
    wa_bf_spec = pl.BlockSpec((d, tn), lambda i, j: (0, jnp.minimum(j, n_main - 1)))
    wt_bf_spec = pl.BlockSpec((d, d_pool), lambda i, j: (0, 0), pipeline_mode=pl.Buffered(1))
    wc_bf_spec = pl.BlockSpec((d, HEAD_DIM), lambda i, j: (0, 0))
    if emit:
        assert n % d_pool == 0 and (n + d_pool) % n_gate == 0 and n_gate % SUBLANES == 0
        operands = (weights, weights, weights)
        w_specs = [
            pl.BlockSpec((tn, d), lambda i, j: (jnp.minimum(j, n_main - 1), 0)),
            pl.BlockSpec((d_pool, d), lambda i, j: (n // d_pool, 0), pipeline_mode=pl.Buffered(1)),
            pl.BlockSpec((n_gate, d), lambda i, j: ((n + d_pool) // n_gate, 0)),
        ]
        out_specs += [wa_bf_spec, pl.BlockSpec((d, d_pool), lambda i, j: (0, 0)), wc_bf_spec]
        out_shape += [jax.ShapeDtypeStruct((d, n), BF16), jax.ShapeDtypeStruct((d, d_pool), BF16),
                      jax.ShapeDtypeStruct((d, HEAD_DIM), BF16)]
    else:
        operands = weights
        w_specs = [wa_bf_spec, wt_bf_spec, wc_bf_spec]
    return pl.pallas_call(
        functools.partial(_inproj_kernel, n_heads=n_heads, n_main=n_main, emit=emit),
        grid=(m // tm, n_main + 1),
        in_specs=[
            pl.BlockSpec((tm, d), lambda i, j: (i, 0)),
            pl.BlockSpec((1, d), lambda i, j: (0, 0)),
            *w_specs,
            vec, vec,
        ],
        out_specs=out_specs,
        out_shape=out_shape,
        scratch_shapes=[pltpu.VMEM((tm, d), BF16)],
        compiler_params=_cparams(2),
        name="in_proj",
    )(x, norm_w, *operands, alog, dtb)


def _lane_column(x, lane_idx):
    lane = lax.broadcasted_iota(jnp.int32, x.shape, 1)
    col = jnp.sum(jnp.where(lane == lane_idx, x, 0.0), axis=1, keepdims=True)
    return jnp.broadcast_to(col, x.shape)


def _unit_lower_inverses(lms, precise):
    n = lms[0].shape[0]
    row = lax.broadcasted_iota(jnp.int32, (n, n), 0)
    col = lax.broadcasted_iota(jnp.int32, (n, n), 1)
    eye = jnp.where(row == col, 1.0, 0.0)

    def same_block(size):
        shift = size.bit_length() - 1
        return lax.shift_right_logical(row, shift) == lax.shift_right_logical(col, shift)

    if precise:
        op = lambda x: x
        mm = functools.partial(jnp.dot, preferred_element_type=F32, precision=lax.Precision.HIGHEST)
        terms = INV_BASE
        in_base = same_block(INV_BASE)
        lbs = [jnp.where(in_base, lm, 0.0) for lm in lms]
    else:
        op = lambda x: x.astype(BF16)
        mm = functools.partial(jnp.dot, preferred_element_type=F32)
        terms = n // 2
        assert INV_FAST_NORM ** terms / (1.0 - INV_FAST_NORM) < 2.0 ** -24
        lbs = lms

    qs = [eye - lb for lb in lbs]
    lbs = [op(lb) for lb in lbs]
    nks = [mm(lb, lb) for lb in lbs]
    power = 2
    while 2 * power < terms:
        nbs = [op(nk) for nk in nks]
        res = [mm(nb, jnp.concatenate([nb, op(q)], axis=1)) for nb, q in zip(nbs, qs)]
        nks = [r[:, :n] for r in res]
        qs = [q + r[:, n:] for q, r in zip(qs, res)]
        power *= 2
    ps = [q + mm(op(nk), op(q)) for nk, q in zip(nks, qs)]
    if not precise:
        return ps
    size = INV_BASE
    while size < n:
        off = jnp.logical_and(same_block(2 * size), jnp.logical_not(same_block(size)))
        cs = [op(jnp.where(off, lm, 0.0)) for lm in lms]
        pbs = [op(p) for p in ps]
        ts = [op(mm(c, pb)) for c, pb in zip(cs, pbs)]
        ps = [p - mm(pb, t) for p, pb, t in zip(ps, pbs, ts)]
        size *= 2
    return ps


def _gdn_prompt_kernel(q_ref, k_ref, v_ref, gate_ref, hq_ref, hk_ref, hv_ref, gates_ref,
                       cwq_ref, cwk_ref, cwv_ref, gnw_ref, *rest, n_heads, hs, rows, n_cast, precise):
    cast_in = rest[:n_cast]
    o_ref, sout_ref, norm_ref = rest[n_cast:n_cast + 3]
    cast_out = rest[n_cast + 3:2 * n_cast + 3]
    s_ref, ext_ref = rest[2 * n_cast + 3:]
    hg = pl.program_id(1)
    blk = pl.program_id(2)
    c = GDN_CHUNK

    for src, dst in zip(cast_in, cast_out):
        dst[...] = src[...].astype(dst.dtype)

    @pl.when(blk == 0)
    def _():
        s_ref[...] = jnp.zeros_like(s_ref)

    for a, (x_ref, halo_ref) in enumerate(((q_ref, hq_ref), (k_ref, hk_ref), (v_ref, hv_ref))):
        for hi in range(hs):
            lanes = slice(hi * HEAD_DIM, (hi + 1) * HEAD_DIM)
            ext_ref[a * hs + hi, 0:HALO_CONV, :] = jnp.where(blk > 0, halo_ref[:, lanes], 0.0)
            ext_ref[a * hs + hi, HALO_CONV:, :] = x_ref[:, lanes]

    def conv_silu(a, cw_ref, p):
        ci, hi = p
        lanes = slice(hi * HEAD_DIM, (hi + 1) * HEAD_DIM)
        acc = None
        for j in range(CONV_W):
            term = (cw_ref[j:j + 1, lanes]
                    * ext_ref[a * hs + hi, pl.ds(HALO_CONV - (CONV_W - 1) + j + ci * c, c), :])
            acc = term if acc is None else acc + term
        return _silu(acc)

    row = lax.broadcasted_iota(jnp.int32, (c, c), 0)
    col = lax.broadcasted_iota(jnp.int32, (c, c), 1)
    tril = row >= col
    strict = row > col
    tril_f = jnp.where(tril, 1.0, 0.0)
    gnw = gnw_ref[...]
    heads = range(hs)

    n_chunks = rows // c
    probs = [(ci, hi) for ci in range(n_chunks) for hi in heads]
    n_p = range(len(probs))
    gates = [gates_ref[ci * c:(ci + 1) * c, :] for ci in range(n_chunks)]
    gcum = [jnp.dot(tril_f, g, preferred_element_type=F32, precision=lax.Precision.HIGHEST) for g in gates]
    qc = [_l2norm(conv_silu(0, cwq_ref, p), HEAD_DIM ** -0.5) for p in probs]
    kc = [_l2norm(conv_silu(1, cwk_ref, p)) for p in probs]
    vc = [conv_silu(2, cwv_ref, p) for p in probs]
    gc = [_lane_column(gcum[ci], hg * hs + hi) for ci, hi in probs]
    bc = [_lane_column(gates[ci], hg * hs + hi + n_heads) for ci, hi in probs]
    decay = [jnp.exp(jnp.where(tril, g - g.T, MASKED)) for g in gc]
    egc = [jnp.exp(g) for g in gc]
    g_last = [g[c - 1:c, :] for g in gc]
    kb = [k.astype(BF16) for k in kc]
    kk = [_dot_nt(k, k) for k in kb]
    qk = [_dot_nt(q, k) for q, k in zip(qc, kb)]
    lm = [jnp.where(strict, kk[p] * bc[p] * decay[p], 0.0) for p in n_p]
    row_sums = [jnp.sum(jnp.abs(x), axis=1, keepdims=True) for x in lm]
    worst = jnp.max(functools.reduce(jnp.maximum, row_sums), axis=0, keepdims=True)
    norm_ref[...] = jnp.broadcast_to(worst.reshape(1, 1, 1), norm_ref.shape)
    t_inv = _unit_lower_inverses(lm, precise)
    uw = [_dot(t_inv[p], jnp.concatenate([vc[p] * bc[p], kc[p] * (bc[p] * egc[p])], axis=1)) for p in n_p]
    intra = [qk[p] * decay[p] for p in n_p]
    kdt = [(kc[p] * jnp.exp(g_last[p] - gc[p])).T for p in n_p]
    wq = [jnp.concatenate([uw[p][:, HEAD_DIM:], qc[p] * egc[p]], axis=0) for p in n_p]
    ik = [jnp.concatenate([intra[p], kdt[p]], axis=0) for p in n_p]

    s = [s_ref[hi] for hi in heads]
    for ci in range(n_chunks):
        ps = [ci * hs + hi for hi in heads]
        r1 = [_dot(wq[p], s[hi]) for hi, p in zip(heads, ps)]
        v_new = [uw[p][:, :HEAD_DIM] - r1[hi][:c] for hi, p in zip(heads, ps)]
        r2 = [_dot(ik[p], v_new[hi]) for hi, p in zip(heads, ps)]
        s = [s[hi] * jnp.exp(g_last[p]) + r2[hi][c:] for hi, p in zip(heads, ps)]
        for hi in heads:
            lanes = slice(hi * HEAD_DIM, (hi + 1) * HEAD_DIM)
            o = _rms(r1[hi][c:] + r2[hi][:c], gnw) * _silu(gate_ref[ci * c:(ci + 1) * c, lanes])
            o_ref[ci * c:(ci + 1) * c, lanes] = o.astype(o_ref.dtype)

    for hi in range(hs):
        s_ref[hi] = s[hi]

    @pl.when(blk == pl.num_programs(2) - 1)
    def _():
        for hi in range(hs):
            sout_ref[0, hi] = s[hi]


def _gdn_prompt(proj, gates, conv_w, gnw, to_cast, batch, seq, n_heads, hs, rows, precise):
    nblk = seq // rows
    hb = rows // HALO_CONV
    ng = n_heads // hs
    width = hs * HEAD_DIM
    steps = batch * ng * nblk

    def slab(w):
        return pl.BlockSpec((w.shape[0] // steps, w.shape[1]), lambda b, h, k: ((b * ng + h) * nblk + k, 0))

    def main(off):
        return pl.BlockSpec((rows, width), lambda b, h, k: (b * nblk + k, off * ng + h))

    def halo(off):
        return pl.BlockSpec(
            (HALO_CONV, width), lambda b, h, k: (jnp.maximum((b * nblk + k) * hb - 1, 0), off * ng + h))

    def cw(off):
        return pl.BlockSpec((CONV_W, width), lambda b, h, k: (0, off * ng + h))

    kern = functools.partial(_gdn_prompt_kernel, n_heads=n_heads, hs=hs, rows=rows, n_cast=len(to_cast),
                             precise=precise)
    return pl.pallas_call(
        kern,
        grid=(batch, ng, nblk),
        in_specs=[main(0), main(1), main(2), main(3), halo(0), halo(1), halo(2),
                  pl.BlockSpec((rows, HEAD_DIM), lambda b, h, k: (b * nblk + k, 0)),
                  cw(0), cw(1), cw(2),
                  pl.BlockSpec((1, HEAD_DIM), lambda b, h, k: (0, 0)),
                  *[slab(w) for w in to_cast]],
        out_specs=[
            pl.BlockSpec((rows, width), lambda b, h, k: (b * nblk + k, h)),
            pl.BlockSpec((1, hs, HEAD_DIM, HEAD_DIM), lambda b, h, k: (b, h, 0, 0)),
            pl.BlockSpec((1, SUBLANES, HEAD_DIM), lambda b, h, k: ((b * ng + h) * nblk + k, 0, 0)),
            *[slab(w) for w in to_cast],
        ],
        out_shape=[
            jax.ShapeDtypeStruct((batch * seq, n_heads * HEAD_DIM), BF16),
            jax.ShapeDtypeStruct((batch, n_heads, HEAD_DIM, HEAD_DIM), F32),
            jax.ShapeDtypeStruct((steps, SUBLANES, HEAD_DIM), F32),
            *[jax.ShapeDtypeStruct(w.shape, BF16) for w in to_cast],
        ],
        scratch_shapes=[pltpu.VMEM((hs, HEAD_DIM, HEAD_DIM), F32),
                        pltpu.VMEM((3 * hs, rows + HALO_CONV, HEAD_DIM), F32)],
        compiler_params=_cparams(3),
        name="gdn_prompt",
    )(proj, proj, proj, proj, proj, proj, proj, gates, conv_w, conv_w, conv_w, gnw, *to_cast)


def _gdn_token_steps(x_ref, cs_ref, gates_ref, cw_ref, gnw_ref, s_in_ref, o_ref, s_out_ref, cs_out_ref, n_heads):
    d = n_heads * HEAD_DIM
    sq = (HEAD_DIM, HEAD_DIM)
    x = x_ref[...]
    cw = cw_ref[...]
    conv = cw[CONV_W - 1:CONV_W, :] * x[:, :3 * d]
    for j in range(CONV_W - 1):
        conv = conv + cw[j:j + 1, :] * cs_ref[j]
    qkv = _silu(conv)
    for j in range(CONV_W - 2):
        cs_out_ref[j] = cs_ref[j + 1]
    cs_out_ref[CONV_W - 2] = x[:, :3 * d]

    gates = gates_ref[...]
    eg = jnp.exp(gates)
    gnw = gnw_ref[...]
    heads = range(n_heads)
    qn = [_l2norm(qkv[:, h * HEAD_DIM:(h + 1) * HEAD_DIM], HEAD_DIM ** -0.5) for h in heads]
    kn = [_l2norm(qkv[:, d + h * HEAD_DIM:d + (h + 1) * HEAD_DIM]) for h in heads]
    for t in range(x.shape[0]):
        row = slice(t, t + 1)
        v = [qkv[row, 2 * d + h * HEAD_DIM:2 * d + (h + 1) * HEAD_DIM] for h in heads]
        kb = [jnp.broadcast_to(kn[h][row], sq).T for h in heads]
        qb = [jnp.broadcast_to(qn[h][row], sq).T for h in heads]
        s1 = [s_in_ref[t, h] * eg[row, h:h + 1] for h in heads]
        ks = [jnp.sum(s1[h] * kb[h], axis=0, keepdims=True) for h in heads]
        delta = [(v[h] - ks[h]) * gates[row, n_heads + h:n_heads + h + 1] for h in heads]
        s2 = [s1[h] + kb[h] * delta[h] for h in heads]
        for h in heads:
            lanes = slice(h * HEAD_DIM, (h + 1) * HEAD_DIM)
            s_out_ref[t, h] = s2[h]
            o = jnp.sum(s2[h] * qb[h], axis=0, keepdims=True)
            o_ref[row, lanes] = _rms(o, gnw) * _silu(x[row, 3 * d + h * HEAD_DIM:3 * d + (h + 1) * HEAD_DIM])


def _mix_out_kernel(oa_ref, u_ref, hist_ref, x_ref, wp_ref, ps_ref, wo_ref, nw_ref, *rest,
                    tm, seq, pos0, n_heads):
    if n_heads:
        tok_in, out_ref, tok_out, rest = rest[:6], rest[6], rest[7:10], rest[10:]
        _gdn_token_steps(*tok_in, *tok_out, n_heads)
    else:
        out_ref, rest = rest[0], rest[1:]
    d_pool = u_ref.shape[1]
    d_gdn = oa_ref.shape[1]
    d = out_ref.shape[1]
    n_groups = len(POOL_WINDOWS)
    gd = d_pool // n_groups
    strips = gd // HEAD_DIM
    if seq > 1:
        (ext_ref,) = rest
        start = (pl.program_id(0) * tm) % seq
        for si in range(d_pool // HEAD_DIM):
            lanes = slice(si * HEAD_DIM, (si + 1) * HEAD_DIM)
            ext_ref[si, 0:HALO_POOL, :] = jnp.where(start > 0, hist_ref[:, lanes], 0.0)
            ext_ref[si, HALO_POOL:, :] = u_ref[:, lanes]
        pos = pos0 + start + lax.broadcasted_iota(jnp.int32, (tm, HEAD_DIM), 0)
    else:
        (hist_out_ref,) = rest
        for r in range(POOL_BUF - 1):
            hist_out_ref[r] = hist_ref[r + 1]
        hist_out_ref[POOL_BUF - 1] = u_ref[...]
        pos = jnp.full((tm, HEAD_DIM), pos0, jnp.int32)

    tn = d // n_groups
    oa = oa_ref[...].astype(BF16)
    parts = []
    for gi, win in enumerate(POOL_WINDOWS):
        cols = slice(gi * tn, (gi + 1) * tn)
        out_ref[:, cols] = jnp.dot(oa, wo_ref[:d_gdn, cols], preferred_element_type=F32)
        cnt = jnp.minimum(pos + 1, win).astype(F32)
        pooled = []
        for si in range(gi * strips, (gi + 1) * strips):
            lanes = slice(si * HEAD_DIM, (si + 1) * HEAD_DIM)
            cur = u_ref[:, lanes]
            wsum = cur
            for r in range(1, win):
                if seq > 1:
                    wsum = wsum + ext_ref[si, pl.ds(HALO_POOL - r, tm), :]
                else:
                    wsum = wsum + hist_ref[POOL_BUF - r, :, lanes]
            pooled.append(wsum / cnt - cur)
        ob = _dot(jnp.concatenate(pooled, axis=1), wp_ref[gi]) * ps_ref[:, gi * gd:(gi + 1) * gd]
        parts.append(ob.astype(BF16))
    ob_all = jnp.concatenate(parts, axis=1)
    mix = [out_ref[:, ci * tn:(ci + 1) * tn]
           + jnp.dot(ob_all, wo_ref[d_gdn:, ci * tn:(ci + 1) * tn], preferred_element_type=F32)
           for ci in range(n_groups)]
    ss = sum(jnp.sum(m * m, axis=-1, keepdims=True) for m in mix)
    inv = lax.rsqrt(ss * (1.0 / d) + EPS)
    for ci in range(n_groups):
        cols = slice(ci * tn, (ci + 1) * tn)
        out_ref[:, cols] = x_ref[:, cols] + mix[ci] * inv * nw_ref[:, cols]


def _mix_out(o_a, u, hist, x, w_pool, pool_scale, w_out, norm_w, tm, seq, pos0, token_job=None):
    m, d = x.shape
    d_pool = pool_scale.shape[1]
    d_gdn = o_a.shape[1]
    out_specs = [pl.BlockSpec((tm, d), lambda i: (i, 0))]
    out_shape = [jax.ShapeDtypeStruct((m, d), F32)]
    tok_specs, tok_operands, n_heads = [], [], 0
    if token_job is not None:
        proj, conv_state, gates, conv_w, gnw, state, n_heads = token_job
        tokens, dg = proj.shape[0], n_heads * HEAD_DIM
        nt = tokens // (m // tm)
        assert seq > 1 and nt * (m // tm) == tokens and nt % SUBLANES == 0
        cs_spec = pl.BlockSpec((CONV_W - 1, nt, 3 * dg), lambda i: (0, i, 0))
        st_spec = pl.BlockSpec((nt, n_heads, HEAD_DIM, HEAD_DIM), lambda i: (i, 0, 0, 0))
        tok_specs = [
            pl.BlockSpec((nt, 4 * dg), lambda i: (i, 0)),
            cs_spec,
            pl.BlockSpec((nt, HEAD_DIM), lambda i: (i, 0)),
            pl.BlockSpec((CONV_W, 3 * dg), lambda i: (0, 0)),
            pl.BlockSpec((1, HEAD_DIM), lambda i: (0, 0)),
            st_spec,
        ]
        tok_operands = [proj, conv_state, gates, conv_w, gnw, state]
        out_specs += [pl.BlockSpec((nt, dg), lambda i: (i, 0)), st_spec, cs_spec]
        out_shape += [jax.ShapeDtypeStruct((tokens, dg), F32), jax.ShapeDtypeStruct(state.shape, F32),
                      jax.ShapeDtypeStruct(conv_state.shape, F32)]
    if seq > 1:
        hb = tm // HALO_POOL
        hist_spec = pl.BlockSpec((HALO_POOL, d_pool), lambda i: (jnp.maximum(i * hb - 1, 0), 0))
        scratch = [pltpu.VMEM((d_pool // HEAD_DIM, tm + HALO_POOL, HEAD_DIM), F32)]
    else:
        hist_spec = pl.BlockSpec((POOL_BUF, tm, d_pool), lambda i: (0, i, 0))
        out_specs.append(hist_spec)
        out_shape.append(jax.ShapeDtypeStruct(hist.shape, F32))
        scratch = []
    kern = functools.partial(_mix_out_kernel, tm=tm, seq=seq, pos0=pos0, n_heads=n_heads)
    return pl.pallas_call(
        kern,
        grid=(m // tm,),
        in_specs=[
            pl.BlockSpec((tm, d_gdn), lambda i: (i, 0)),
            pl.BlockSpec((tm, d_pool), lambda i: (i, 0)),
            hist_spec,
            pl.BlockSpec((tm, d), lambda i: (i, 0)),
            pl.BlockSpec(w_pool.shape, lambda i: (0, 0, 0), pipeline_mode=pl.Buffered(1)),
            pl.BlockSpec((1, d_pool), lambda i: (0, 0)),
            pl.BlockSpec(w_out.shape, lambda i: (0, 0), pipeline_mode=pl.Buffered(1)),
            pl.BlockSpec((1, d), lambda i: (0, 0)),
            *tok_specs,
        ],
        out_specs=out_specs,
        out_shape=out_shape,
        scratch_shapes=scratch,
        compiler_params=_cparams(1),
        name="mix_out",
    )(o_a, u, hist, x, w_pool, pool_scale, w_out, norm_w, *tok_operands)


def _mlp_kernel(x_ref, nw1_ref, wu_ref, wd_ref, nw2_ref, xe_ref, out_ref, oute_ref, h_ref, he_ref, *, n_ff_tiles):
    i = pl.program_id(0)
    j = pl.program_id(1)

    def rows_step(first, last, x_ref, h_ref, out_ref):
        if first:
            h = _rms(x_ref[...], nw1_ref[...]).astype(BF16)
            h_ref[...] = h
        else:
            h = h_ref[...]
        up = jnp.dot(h, wu_ref[...], preferred_element_type=F32)
        act = jnp.square(jnp.maximum(up, 0.0)).astype(BF16)
        part = jnp.dot(act, wd_ref[...], preferred_element_type=F32)
        acc = part if first else out_ref[...] + part
        out_ref[...] = x_ref[...] + _rms(acc, nw2_ref[...]) if last else acc

    def step(first, last):
        rows_step(first, last, x_ref, h_ref, out_ref)
        pl.when(i == 0)(lambda: rows_step(first, last, xe_ref, he_ref, oute_ref))

    nj = n_ff_tiles
    if nj == 1:
        step(True, True)
    else:
        pl.when(j == 0)(lambda: step(True, False))
        pl.when((j > 0) & (j < nj - 1))(lambda: step(False, False))
        pl.when(j == nj - 1)(lambda: step(False, True))


def _mlp(x, nw1, w_up, w_down, nw2, tm, tf, x_extra):
    m, d = x.shape
    me = x_extra.shape[0]
    f = w_up.shape[1]
    whole = lambda i, j: (0, 0)
    out, out_extra = pl.pallas_call(
        functools.partial(_mlp_kernel, n_ff_tiles=f // tf),
        grid=(m // tm, f // tf),
        in_specs=[
            pl.BlockSpec((tm, d), lambda i, j: (i, 0)),
            pl.BlockSpec((1, d), whole),
            pl.BlockSpec((d, tf), lambda i, j: (0, j)),
            pl.BlockSpec((tf, d), lambda i, j: (j, 0)),
            pl.BlockSpec((1, d), whole),
            pl.BlockSpec((me, d), whole),
        ],
        out_specs=[pl.BlockSpec((tm, d), lambda i, j: (i, 0)), pl.BlockSpec((me, d), whole)],
        out_shape=[jax.ShapeDtypeStruct((m, d), F32), jax.ShapeDtypeStruct((me, d), F32)],
        scratch_shapes=[pltpu.VMEM((tm, d), BF16), pltpu.VMEM((me, d), BF16)],
        compiler_params=_cparams(2),
        name="mlp",
    )(x, nw1, w_up, w_down, nw2, x_extra)
    return out, out_extra


def _pad_lanes(v, n=HEAD_DIM):
    return jnp.pad(v.reshape(1, -1), ((0, 0), (0, n - v.shape[-1])))


def kernel(x_prompt, x_sample, state_gdn, state_conv, state_pool, norm_pre_mix, w_in, conv_w, a_log, dt_bias,
           norm_gdn_out, w_pool, pool_scale, w_out, norm_post_mix, norm_pre_mlp, w_up, w_down, norm_post_mlp):
    batch, seq, d_model = x_prompt.shape
    dec_batch, dec_seq, _ = x_sample.shape
    assert dec_seq == 1
    depth = w_in.shape[0]
    n_heads = a_log.shape[1]
    d_gdn = n_heads * HEAD_DIM
    d_qkv = 3 * d_gdn
    d_pool = pool_scale.shape[1]
    o_gate = d_qkv + d_gdn

    yp = x_prompt.reshape(batch * seq, d_model)
    ys = x_sample.reshape(dec_batch, d_model)
    outs = [[] for _ in range(6)]
    for l in range(depth):
        alog = _pad_lanes(a_log[l])
        dtb = _pad_lanes(dt_bias[l])
        gnw = norm_gdn_out[l].reshape(1, HEAD_DIM)
        npm = norm_pre_mix[l].reshape(1, d_model)
        npo = norm_post_mix[l].reshape(1, d_model)
        nm1 = norm_pre_mlp[l].reshape(1, d_model)
        nm2 = norm_post_mlp[l].reshape(1, d_model)
        ps = pool_scale[l].reshape(1, d_pool)

        proj_s, u_s, gates_s, *w_in_b = _in_proj(
            ys, npm, jnp.swapaxes(w_in[l], 0, 1), o_gate, d_pool, alog, dtb, n_heads,
            tm=dec_batch, tn=SAMPLE_TN, emit=True)

        proj_p, u_p, gates_p = _in_proj(
            yp, npm, w_in_b, o_gate, d_pool, alog, dtb, n_heads, tm=IN_PROJ_TM, tn=IN_PROJ_TN, emit=False)
        to_cast = (w_up[l], w_down[l], w_out[l], w_pool[l].reshape(-1, w_pool.shape[-1]))
        gdn = functools.partial(_gdn_prompt, proj_p, gates_p, conv_w[l], gnw, batch=batch, seq=seq,
                                n_heads=n_heads, hs=GDN_HEADS_PER_STEP, rows=GDN_ROWS)
        oa_p, s_p, norm_p, w_up_b, w_down_b, w_out_b, w_pool_b = gdn(to_cast, precise=False)
        oa_p, s_p = lax.cond(jnp.max(norm_p) <= INV_FAST_NORM, lambda: (oa_p, s_p),
                             lambda: tuple(gdn((), precise=True)[:2]))
        w_pool_b = w_pool_b.reshape(w_pool.shape[1:])
        token_job = (proj_s, jnp.swapaxes(state_conv[l], 0, 1), gates_s, conv_w[l], gnw, state_gdn[l], n_heads)
        x1_p, oa_s, s_s, cs_new = _mix_out(oa_p, u_p, u_p, yp, w_pool_b, ps, w_out_b, npo, tm=MIX_TM, seq=seq,
                                           pos0=0, token_job=token_job)
        x1_s, hist_new = _mix_out(oa_s, u_s, jnp.swapaxes(state_pool[l], 0, 1), ys, w_pool_b, ps, w_out_b, npo,
                                  tm=dec_batch, seq=1, pos0=PAST_LEN)
        yp, ys = _mlp(x1_p, nm1, w_up_b, w_down_b, nm2, MLP_TM, MLP_TF, x1_s)
        outs[0].append(s_p)
        outs[1].append(proj_p.reshape(batch, seq, -1)[:, seq - (CONV_W - 1):, :d_qkv])
        outs[2].append(u_p.reshape(batch, seq, -1)[:, seq - POOL_BUF:])
        outs[3].append(s_s)
        outs[4].append(jnp.swapaxes(cs_new, 0, 1))
        outs[5].append(jnp.swapaxes(hist_new, 0, 1))

    return (yp.reshape(batch, seq, d_model), ys.reshape(dec_batch, dec_seq, d_model),
            *[jnp.stack(o) for o in outs])
```

```python
import functools

import jax
import jax.numpy as jnp
from jax import lax
from jax.experimental import pallas as pl
from jax.experimental.pallas import tpu as pltpu

F32 = jnp.float32
BF16 = jnp.bfloat16

EPS = 1e-6
HEAD_DIM = 128
CONV_W = 4
POOL_WINDOWS = (2, 4, 8, 16)
POOL_BUF = max(POOL_WINDOWS) - 1
PAST_LEN = 16384
HALO_CONV = 8
HALO_POOL = 16
GDN_CHUNK = 128
INV_BASE = 16
INV_FAST_NORM = 0.75
SUBLANES = 8
MASKED = -1e30
VMEM_LIMIT = 60 * 1024 * 1024

IN_PROJ_TM, IN_PROJ_TN, SAMPLE_TN = 1024, 1024, 1024
GDN_HEADS_PER_STEP, GDN_ROWS = 8, 256
MIX_TM = 512
MLP_TM, MLP_TF = 512, 2048


def _cparams(n_axes):
    return pltpu.CompilerParams(
        dimension_semantics=("arbitrary",) * n_axes, vmem_limit_bytes=VMEM_LIMIT)


def _dot(a, b):
    return jnp.dot(a.astype(BF16), b.astype(BF16), preferred_element_type=F32)


def _dot_nt(a, b):
    return lax.dot_general(a.astype(BF16), b.astype(BF16), (((1,), (1,)), ((), ())),
                           preferred_element_type=F32)


def _rms(x, w):
    return x * lax.rsqrt(jnp.mean(x * x, axis=-1, keepdims=True) + EPS) * w


def _silu(x):
    h = 0.5 * x
    return h + h * jnp.tanh(h)


def _softplus(x):
    return jnp.maximum(x, 0.0) + jnp.log(1.0 + jnp.exp(-jnp.abs(x)))


def _l2norm(x, scale=1.0):
    return x * (lax.rsqrt(jnp.sum(x * x, axis=-1, keepdims=True) + EPS) * scale)


def _inproj_kernel(x_ref, nw_ref, wa_ref, wt_ref, wc_ref, alog_ref, dtb_ref, *rest, n_heads, n_main, emit):
    if emit:
        out_ref, u_ref, gates_ref, wa_out, wt_out, wc_out, h_ref = rest
    else:
        out_ref, u_ref, gates_ref, h_ref = rest
    j = pl.program_id(1)

    def main_step(first):
        if first:
            h = _rms(x_ref[...], nw_ref[...]).astype(BF16)
            h_ref[...] = h
        else:
            h = h_ref[...]
        w = wa_ref[...].T.astype(BF16) if emit else wa_ref[...]
        if emit:
            wa_out[...] = w
        out_ref[...] = jnp.dot(h, w, preferred_element_type=F32)

    pl.when(j == 0)(lambda: main_step(True))
    pl.when((j > 0) & (j < n_main))(lambda: main_step(False))

    @pl.when(j == n_main)
    def _():
        if emit:
            n_gate = 2 * n_heads
            blk = wt_ref[...]
            pool_rows = jnp.concatenate([blk[n_gate:], wc_ref[...]], axis=0)
            gate_rows = jnp.concatenate([blk[:n_gate], jnp.zeros((HEAD_DIM - n_gate, blk.shape[1]), F32)], axis=0)
            wt = pool_rows.T.astype(BF16)
            wc = gate_rows.T.astype(BF16)
            wt_out[...] = wt
            wc_out[...] = wc
        else:
            wt = wt_ref[...]
            wc = wc_ref[...]
        h = h_ref[...]
        u_ref[...] = jnp.dot(h, wt, preferred_element_type=F32)
        ab = jnp.dot(h, wc, preferred_element_type=F32)
        lane = lax.broadcasted_iota(jnp.int32, ab.shape, 1)
        g = -jnp.exp(alog_ref[...]) * _softplus(ab + dtb_ref[...])
        gates_ref[...] = jnp.where(lane < n_heads, g, jax.nn.sigmoid(ab))


def _in_proj(x, norm_w, weights, n, d_pool, alog, dtb, n_heads, tm, tn, emit):
    m, d = x.shape
    n_main = n // tn
    n_gate = 2 * n_heads
    vec = pl.BlockSpec((1, HEAD_DIM), lambda i, j: (0, 0))
    out_specs = [
        pl.BlockSpec((tm, tn), lambda i, j: (i, jnp.minimum(j, n_main - 1))),
        pl.BlockSpec((tm, d_pool), lambda i, j: (i, 0)),
        pl.BlockSpec((tm, HEAD_DIM), lambda i, j: (i, 0)),
    ]
    out_shape = [jax.ShapeDtypeStruct((m, n), F32), jax.ShapeDtypeStruct((m, d_pool), F32),
                 jax.ShapeDtypeStruct((m, HEAD_DIM), F32)]
    wa_bf_spec = pl.BlockSpec((d, tn), lambda i, j: (0, jnp.minimum(j, n_main - 1)))
    wt_bf_spec = pl.BlockSpec((d, d_pool), lambda i, j: (0, 0), pipeline_mode=pl.Buffered(1))
    wc_bf_spec = pl.BlockSpec((d, HEAD_DIM), lambda i, j: (0, 0))
    if emit:
        assert n % d_pool == 0 and (n + d_pool) % n_gate == 0 and n_gate % SUBLANES == 0
        operands = (weights, weights, weights)
        w_specs = [
            pl.BlockSpec((tn, d), lambda i, j: (jnp.minimum(j, n_main - 1), 0)),
            pl.BlockSpec((d_pool, d), lambda i, j: (n // d_pool, 0), pipeline_mode=pl.Buffered(1)),
            pl.BlockSpec((n_gate, d), lambda i, j: ((n + d_pool) // n_gate, 0)),
        ]
        out_specs += [wa_bf_spec, pl.BlockSpec((d, d_pool), lambda i, j: (0, 0)), wc_bf_spec]
        out_shape += [jax.ShapeDtypeStruct((d, n), BF16), jax.ShapeDtypeStruct((d, d_pool), BF16),
                      jax.ShapeDtypeStruct((d, HEAD_DIM), BF16)]
    else:
        operands = weights
        w_specs = [wa_bf_spec, wt_bf_spec, wc_bf_spec]
    return pl.pallas_call(
        functools.partial(_inproj_kernel, n_heads=n_heads, n_main=n_main, emit=emit),
        grid=(m // tm, n_main + 1),
        in_specs=[
            pl.BlockSpec((tm, d), lambda i, j: (i, 0)),
            pl.BlockSpec((1, d), lambda i, j: (0, 0)),
            *w_specs,
            vec, vec,
        ],
        out_specs=out_specs,
        out_shape=out_shape,
        scratch_shapes=[pltpu.VMEM((tm, d), BF16)],
        compiler_params=_cparams(2),
        name="in_proj",
    )(x, norm_w, *operands, alog, dtb)


def _lane_column(x, lane_idx):
    lane = lax.broadcasted_iota(jnp.int32, x.shape, 1)
    col = jnp.sum(jnp.where(lane == lane_idx, x, 0.0), axis=1, keepdims=True)
    return jnp.broadcast_to(col, x.shape)


def _unit_lower_inverses(lms, precise):
    n = lms[0].shape[0]
    row = lax.broadcasted_iota(jnp.int32, (n, n), 0)
    col = lax.broadcasted_iota(jnp.int32, (n, n), 1)
    eye = jnp.where(row == col, 1.0, 0.0)

    def same_block(size):
        shift = size.bit_length() - 1
        return lax.shift_right_logical(row, shift) == lax.shift_right_logical(col, shift)

    if precise:
        op = lambda x: x
        mm = functools.partial(jnp.dot, preferred_element_type=F32, precision=lax.Precision.HIGHEST)
        terms = INV_BASE
        in_base = same_block(INV_BASE)
        lbs = [jnp.where(in_base, lm, 0.0) for lm in lms]
    else:
        op = lambda x: x.astype(BF16)
        mm = functools.partial(jnp.dot, preferred_element_type=F32)
        terms = n // 2
        assert INV_FAST_NORM ** terms / (1.0 - INV_FAST_NORM) < 2.0 ** -24
        lbs = lms

    qs = [eye - lb for lb in lbs]
    lbs = [op(lb) for lb in lbs]
    nks = [mm(lb, lb) for lb in lbs]
    power = 2
    while 2 * power < terms:
        nbs = [op(nk) for nk in nks]
        res = [mm(nb, jnp.concatenate([nb, op(q)], axis=1)) for nb, q in zip(nbs, qs)]
        nks = [r[:, :n] for r in res]
        qs = [q + r[:, n:] for q, r in zip(qs, res)]
        power *= 2
    ps = [q + mm(op(nk), op(q)) for nk, q in zip(nks, qs)]
    if not precise:
        return ps
    size = INV_BASE
    while size < n:
        off = jnp.logical_and(same_block(2 * size), jnp.logical_not(same_block(size)))
        cs = [op(jnp.where(off, lm, 0.0)) for lm in lms]
        pbs = [op(p) for p in ps]
        ts = [op(mm(c, pb)) for c, pb in zip(cs, pbs)]
        ps = [p - mm(pb, t) for p, pb, t in zip(ps, pbs, ts)]
        size *= 2
    return ps


def _gdn_prompt_kernel(q_ref, k_ref, v_ref, gate_ref, hq_ref, hk_ref, hv_ref, gates_ref,
                       cwq_ref, cwk_ref, cwv_ref, gnw_ref, *rest, n_heads, hs, rows, n_cast, precise):
    cast_in = rest[:n_cast]
    o_ref, sout_ref, norm_ref = rest[n_cast:n_cast + 3]
    cast_out = rest[n_cast + 3:2 * n_cast + 3]
    s_ref, ext_ref = rest[2 * n_cast + 3:]
    hg = pl.program_id(1)
    blk = pl.program_id(2)
    c = GDN_CHUNK

    for src, dst in zip(cast_in, cast_out):
        dst[...] = src[...].astype(dst.dtype)

    @pl.when(blk == 0)
    def _():
        s_ref[...] = jnp.zeros_like(s_ref)

    for a, (x_ref, halo_ref) in enumerate(((q_ref, hq_ref), (k_ref, hk_ref), (v_ref, hv_ref))):
        for hi in range(hs):
            lanes = slice(hi * HEAD_DIM, (hi + 1) * HEAD_DIM)
            ext_ref[a * hs + hi, 0:HALO_CONV, :] = jnp.where(blk > 0, halo_ref[:, lanes], 0.0)
            ext_ref[a * hs + hi, HALO_CONV:, :] = x_ref[:, lanes]

    def conv_silu(a, cw_ref, p):
        ci, hi = p
        lanes = slice(hi * HEAD_DIM, (hi + 1) * HEAD_DIM)
        acc = None
        for j in range(CONV_W):
            term = (cw_ref[j:j + 1, lanes]
                    * ext_ref[a * hs + hi, pl.ds(HALO_CONV - (CONV_W - 1) + j + ci * c, c), :])
            acc = term if acc is None else acc + term
        return _silu(acc)

    row = lax.broadcasted_iota(jnp.int32, (c, c), 0)
    col = lax.broadcasted_iota(jnp.int32, (c, c), 1)
    tril = row >= col
    strict = row > col
    tril_f = jnp.where(tril, 1.0, 0.0)
    gnw = gnw_ref[...]
    heads = range(hs)

    n_chunks = rows // c
    probs = [(ci, hi) for ci in range(n_chunks) for hi in heads]
    n_p = range(len(probs))
    gates = [gates_ref[ci * c:(ci + 1) * c, :] for ci in range(n_chunks)]
    gcum = [jnp.dot(tril_f, g, preferred_element_type=F32, precision=lax.Precision.HIGHEST) for g in gates]
    qc = [_l2norm(conv_silu(0, cwq_ref, p), HEAD_DIM ** -0.5) for p in probs]
    kc = [_l2norm(conv_silu(1, cwk_ref, p)) for p in probs]
    vc = [conv_silu(2, cwv_ref, p) for p in probs]
    gc = [_lane_column(gcum[ci], hg * hs + hi) for ci, hi in probs]
    bc = [_lane_column(gates[ci], hg * hs + hi + n_heads) for ci, hi in probs]
    decay = [jnp.exp(jnp.where(tril, g - g.T, MASKED)) for g in gc]
    egc = [jnp.exp(g) for g in gc]
    g_last = [g[c - 1:c, :] for g in gc]
    kb = [k.astype(BF16) for k in kc]
    kk = [_dot_nt(k, k) for k in kb]
    qk = [_dot_nt(q, k) for q, k in zip(qc, kb)]
    lm = [jnp.where(strict, kk[p] * bc[p] * decay[p], 0.0) for p in n_p]
    row_sums = [jnp.sum(jnp.abs(x), axis=1, keepdims=True) for x in lm]
    worst = jnp.max(functools.reduce(jnp.maximum, row_sums), axis=0, keepdims=True)
    norm_ref[...] = jnp.broadcast_to(worst.reshape(1, 1, 1), norm_ref.shape)
    t_inv = _unit_lower_inverses(lm, precise)
    uw = [_dot(t_inv[p], jnp.concatenate([vc[p] * bc[p], kc[p] * (bc[p] * egc[p])], axis=1)) for p in n_p]
    intra = [qk[p] * decay[p] for p in n_p]
    kdt = [(kc[p] * jnp.exp(g_last[p] - gc[p])).T for p in n_p]
    wq = [jnp.concatenate([uw[p][:, HEAD_DIM:], qc[p] * egc[p]], axis=0) for p in n_p]
    ik = [jnp.concatenate([intra[p], kdt[p]], axis=0) for p in n_p]

    s = [s_ref[hi] for hi in heads]
    for ci in range(n_chunks):
        ps = [ci * hs + hi for hi in heads]
        r1 = [_dot(wq[p], s[hi]) for hi, p in zip(heads, ps)]
        v_new = [uw[p][:, :HEAD_DIM] - r1[hi][:c] for hi, p in zip(heads, ps)]
        r2 = [_dot(ik[p], v_new[hi]) for hi, p in zip(heads, ps)]
        s = [s[hi] * jnp.exp(g_last[p]) + r2[hi][c:] for hi, p in zip(heads, ps)]
        for hi in heads:
            lanes = slice(hi * HEAD_DIM, (hi + 1) * HEAD_DIM)
            o = _rms(r1[hi][c:] + r2[hi][:c], gnw) * _silu(gate_ref[ci * c:(ci + 1) * c, lanes])
            o_ref[ci * c:(ci + 1) * c, lanes] = o.astype(o_ref.dtype)

    for hi in range(hs):
        s_ref[hi] = s[hi]

    @pl.when(blk == pl.num_programs(2) - 1)
    def _():
        for hi in range(hs):
            sout_ref[0, hi] = s[hi]


def _gdn_prompt(proj, gates, conv_w, gnw, to_cast, batch, seq, n_heads, hs, rows, precise):
    nblk = seq // rows
    hb = rows // HALO_CONV
    ng = n_heads // hs
    width = hs * HEAD_DIM
    steps = batch * ng * nblk

    def slab(w):
        return pl.BlockSpec((w.shape[0] // steps, w.shape[1]), lambda b, h, k: ((b * ng + h) * nblk + k, 0))

    def main(off):
        return pl.BlockSpec((rows, width), lambda b, h, k: (b * nblk + k, off * ng + h))

    def halo(off):
        return pl.BlockSpec(
            (HALO_CONV, width), lambda b, h, k: (jnp.maximum((b * nblk + k) * hb - 1, 0), off * ng + h))

    def cw(off):
        return pl.BlockSpec((CONV_W, width), lambda b, h, k: (0, off * ng + h))

    kern = functools.partial(_gdn_prompt_kernel, n_heads=n_heads, hs=hs, rows=rows, n_cast=len(to_cast),
                             precise=precise)
    return pl.pallas_call(
        kern,
        grid=(batch, ng, nblk),
        in_specs=[main(0), main(1), main(2), main(3), halo(0), halo(1), halo(2),
                  pl.BlockSpec((rows, HEAD_DIM), lambda b, h, k: (b * nblk + k, 0)),
                  cw(0), cw(1), cw(2),
                  pl.BlockSpec((1, HEAD_DIM), lambda b, h, k: (0, 0)),
                  *[slab(w) for w in to_cast]],
        out_specs=[
            pl.BlockSpec((rows, width), lambda b, h, k: (b * nblk + k, h)),
            pl.BlockSpec((1, hs, HEAD_DIM, HEAD_DIM), lambda b, h, k: (b, h, 0, 0)),
            pl.BlockSpec((1, SUBLANES, HEAD_DIM), lambda b, h, k: ((b * ng + h) * nblk + k, 0, 0)),
            *[slab(w) for w in to_cast],
        ],
        out_shape=[
            jax.ShapeDtypeStruct((batch * seq, n_heads * HEAD_DIM), BF16),
            jax.ShapeDtypeStruct((batch, n_heads, HEAD_DIM, HEAD_DIM), F32),
            jax.ShapeDtypeStruct((steps, SUBLANES, HEAD_DIM), F32),
            *[jax.ShapeDtypeStruct(w.shape, BF16) for w in to_cast],
        ],
        scratch_shapes=[pltpu.VMEM((hs, HEAD_DIM, HEAD_DIM), F32),
                        pltpu.VMEM((3 * hs, rows + HALO_CONV, HEAD_DIM), F32)],
        compiler_params=_cparams(3),
        name="gdn_prompt",
    )(proj, proj, proj, proj, proj, proj, proj, gates, conv_w, conv_w, conv_w, gnw, *to_cast)


def _gdn_token_steps(x_ref, cs_ref, gates_ref, cw_ref, gnw_ref, s_in_ref, o_ref, s_out_ref, cs_out_ref, n_heads):
    d = n_heads * HEAD_DIM
    sq = (HEAD_DIM, HEAD_DIM)
    x = x_ref[...]
    cw = cw_ref[...]
    conv = cw[CONV_W - 1:CONV_W, :] * x[:, :3 * d]
    for j in range(CONV_W - 1):
        conv = conv + cw[j:j + 1, :] * cs_ref[j]
    qkv = _silu(conv)
    for j in range(CONV_W - 2):
        cs_out_ref[j] = cs_ref[j + 1]
    cs_out_ref[CONV_W - 2] = x[:, :3 * d]

    gates = gates_ref[...]
    eg = jnp.exp(gates)
    gnw = gnw_ref[...]
    heads = range(n_heads)
    qn = [_l2norm(qkv[:, h * HEAD_DIM:(h + 1) * HEAD_DIM], HEAD_DIM ** -0.5) for h in heads]
    kn = [_l2norm(qkv[:, d + h * HEAD_DIM:d + (h + 1) * HEAD_DIM]) for h in heads]
    for t in range(x.shape[0]):
        row = slice(t, t + 1)
        v = [qkv[row, 2 * d + h * HEAD_DIM:2 * d + (h + 1) * HEAD_DIM] for h in heads]
        kb = [jnp.broadcast_to(kn[h][row], sq).T for h in heads]
        qb = [jnp.broadcast_to(qn[h][row], sq).T for h in heads]
        s1 = [s_in_ref[t, h] * eg[row, h:h + 1] for h in heads]
        ks = [jnp.sum(s1[h] * kb[h], axis=0, keepdims=True) for h in heads]
        delta = [(v[h] - ks[h]) * gates[row, n_heads + h:n_heads + h + 1] for h in heads]
        s2 = [s1[h] + kb[h] * delta[h] for h in heads]
        for h in heads:
            lanes = slice(h * HEAD_DIM, (h + 1) * HEAD_DIM)
            s_out_ref[t, h] = s2[h]
            o = jnp.sum(s2[h] * qb[h], axis=0, keepdims=True)
            o_ref[row, lanes] = _rms(o, gnw) * _silu(x[row, 3 * d + h * HEAD_DIM:3 * d + (h + 1) * HEAD_DIM])


def _mix_out_kernel(oa_ref, u_ref, hist_ref, x_ref, wp_ref, ps_ref, wo_ref, nw_ref, *rest,
                    tm, seq, pos0, n_heads):
    if n_heads:
        tok_in, out_ref, tok_out, rest = rest[:6], rest[6], rest[7:10], rest[10:]
        _gdn_token_steps(*tok_in, *tok_out, n_heads)
    else:
        out_ref, rest = rest[0], rest[1:]
    d_pool = u_ref.shape[1]
    d_gdn = oa_ref.shape[1]
    d = out_ref.shape[1]
    n_groups = len(POOL_WINDOWS)
    gd = d_pool // n_groups
    strips = gd // HEAD_DIM
    if seq > 1:
        (ext_ref,) = rest
        start = (pl.program_id(0) * tm) % seq
        for si in range(d_pool // HEAD_DIM):
            lanes = slice(si * HEAD_DIM, (si + 1) * HEAD_DIM)
            ext_ref[si, 0:HALO_POOL, :] = jnp.where(start > 0, hist_ref[:, lanes], 0.0)
            ext_ref[si, HALO_POOL:, :] = u_ref[:, lanes]
        pos = pos0 + start + lax.broadcasted_iota(jnp.int32, (tm, HEAD_DIM), 0)
    else:
        (hist_out_ref,) = rest
        for r in range(POOL_BUF - 1):
            hist_out_ref[r] = hist_ref[r + 1]
        hist_out_ref[POOL_BUF - 1] = u_ref[...]
        pos = jnp.full((tm, HEAD_DIM), pos0, jnp.int32)

    tn = d // n_groups
    oa = oa_ref[...].astype(BF16)
    parts = []
    for gi, win in enumerate(POOL_WINDOWS):
        cols = slice(gi * tn, (gi + 1) * tn)
        out_ref[:, cols] = jnp.dot(oa, wo_ref[:d_gdn, cols], preferred_element_type=F32)
        cnt = jnp.minimum(pos + 1, win).astype(F32)
        pooled = []
        for si in range(gi * strips, (gi + 1) * strips):
            lanes = slice(si * HEAD_DIM, (si + 1) * HEAD_DIM)
            cur = u_ref[:, lanes]
            wsum = cur
            for r in range(1, win):
                if seq > 1:
                    wsum = wsum + ext_ref[si, pl.ds(HALO_POOL - r, tm), :]
                else:
                    wsum = wsum + hist_ref[POOL_BUF - r, :, lanes]
            pooled.append(wsum / cnt - cur)
        ob = _dot(jnp.concatenate(pooled, axis=1), wp_ref[gi]) * ps_ref[:, gi * gd:(gi + 1) * gd]
        parts.append(ob.astype(BF16))
    ob_all = jnp.concatenate(parts, axis=1)
    mix = [out_ref[:, ci * tn:(ci + 1) * tn]
           + jnp.dot(ob_all, wo_ref[d_gdn:, ci * tn:(ci + 1) * tn], preferred_element_type=F32)
           for ci in range(n_groups)]
    ss = sum(jnp.sum(m * m, axis=-1, keepdims=True) for m in mix)
    inv = lax.rsqrt(ss * (1.0 / d) + EPS)
    for ci in range(n_groups):
        cols = slice(ci * tn, (ci + 1) * tn)
        out_ref[:, cols] = x_ref[:, cols] + mix[ci] * inv * nw_ref[:, cols]


def _mix_out(o_a, u, hist, x, w_pool, pool_scale, w_out, norm_w, tm, seq, pos0, token_job=None):
    m, d = x.shape
    d_pool = pool_scale.shape[1]
    d_gdn = o_a.shape[1]
    out_specs = [pl.BlockSpec((tm, d), lambda i: (i, 0))]
    out_shape = [jax.ShapeDtypeStruct((m, d), F32)]
    tok_specs, tok_operands, n_heads = [], [], 0
    if token_job is not None:
        proj, conv_state, gates, conv_w, gnw, state, n_heads = token_job
        tokens, dg = proj.shape[0], n_heads * HEAD_DIM
        nt = tokens // (m // tm)
        assert seq > 1 and nt * (m // tm) == tokens and nt % SUBLANES == 0
        cs_spec = pl.BlockSpec((CONV_W - 1, nt, 3 * dg), lambda i: (0, i, 0))
        st_spec = pl.BlockSpec((nt, n_heads, HEAD_DIM, HEAD_DIM), lambda i: (i, 0, 0, 0))
        tok_specs = [
            pl.BlockSpec((nt, 4 * dg), lambda i: (i, 0)),
            cs_spec,
            pl.BlockSpec((nt, HEAD_DIM), lambda i: (i, 0)),
            pl.BlockSpec((CONV_W, 3 * dg), lambda i: (0, 0)),
            pl.BlockSpec((1, HEAD_DIM), lambda i: (0, 0)),
            st_spec,
        ]
        tok_operands = [proj, conv_state, gates, conv_w, gnw, state]
        out_specs += [pl.BlockSpec((nt, dg), lambda i: (i, 0)), st_spec, cs_spec]
        out_shape += [jax.ShapeDtypeStruct((tokens, dg), F32), jax.ShapeDtypeStruct(state.shape, F32),
                      jax.ShapeDtypeStruct(conv_state.shape, F32)]
    if seq > 1:
        hb = tm // HALO_POOL
        hist_spec = pl.BlockSpec((HALO_POOL, d_pool), lambda i: (jnp.maximum(i * hb - 1, 0), 0))
        scratch = [pltpu.VMEM((d_pool // HEAD_DIM, tm + HALO_POOL, HEAD_DIM), F32)]
    else:
        hist_spec = pl.BlockSpec((POOL_BUF, tm, d_pool), lambda i: (0, i, 0))
        out_specs.append(hist_spec)
        out_shape.append(jax.ShapeDtypeStruct(hist.shape, F32))
        scratch = []
    kern = functools.partial(_mix_out_kernel, tm=tm, seq=seq, pos0=pos0, n_heads=n_heads)
    return pl.pallas_call(
        kern,
        grid=(m // tm,),
        in_specs=[
            pl.BlockSpec((tm, d_gdn), lambda i: (i, 0)),
            pl.BlockSpec((tm, d_pool), lambda i: (i, 0)),
            hist_spec,
            pl.BlockSpec((tm, d), lambda i: (i, 0)),
            pl.BlockSpec(w_pool.shape, lambda i: (0, 0, 0), pipeline_mode=pl.Buffered(1)),
            pl.BlockSpec((1, d_pool), lambda i: (0, 0)),
            pl.BlockSpec(w_out.shape, lambda i: (0, 0), pipeline_mode=pl.Buffered(1)),
            pl.BlockSpec((1, d), lambda i: (0, 0)),
            *tok_specs,
        ],
        out_specs=out_specs,
        out_shape=out_shape,
        scratch_shapes=scratch,
        compiler_params=_cparams(1),
        name="mix_out",
    )(o_a, u, hist, x, w_pool, pool_scale, w_out, norm_w, *tok_operands)


def _mlp_kernel(x_ref, nw1_ref, wu_ref, wd_ref, nw2_ref, xe_ref, out_ref, oute_ref, h_ref, he_ref, *, n_ff_tiles):
    i = pl.program_id(0)
    j = pl.program_id(1)

    def rows_step(first, last, x_ref, h_ref, out_ref):
        if first:
            h = _rms(x_ref[...], nw1_ref[...]).astype(BF16)
            h_ref[...] = h
        else:
            h = h_ref[...]
        up = jnp.dot(h, wu_ref[...], preferred_element_type=F32)
        act = jnp.square(jnp.maximum(up, 0.0)).astype(BF16)
        part = jnp.dot(act, wd_ref[...], preferred_element_type=F32)
        acc = part if first else out_ref[...] + part
        out_ref[...] = x_ref[...] + _rms(acc, nw2_ref[...]) if last else acc

    def step(first, last):
        rows_step(first, last, x_ref, h_ref, out_ref)
        pl.when(i == 0)(lambda: rows_step(first, last, xe_ref, he_ref, oute_ref))

    nj = n_ff_tiles
    if nj == 1:
        step(True, True)
    else:
        pl.when(j == 0)(lambda: step(True, False))
        pl.when((j > 0) & (j < nj - 1))(lambda: step(False, False))
        pl.when(j == nj - 1)(lambda: step(False, True))


def _mlp(x, nw1, w_up, w_down, nw2, tm, tf, x_extra):
    m, d = x.shape
    me = x_extra.shape[0]
    f = w_up.shape[1]
    whole = lambda i, j: (0, 0)
    out, out_extra = pl.pallas_call(
        functools.partial(_mlp_kernel, n_ff_tiles=f // tf),
        grid=(m // tm, f // tf),
        in_specs=[
            pl.BlockSpec((tm, d), lambda i, j: (i, 0)),
            pl.BlockSpec((1, d), whole),
            pl.BlockSpec((d, tf), lambda i, j: (0, j)),
            pl.BlockSpec((tf, d), lambda i, j: (j, 0)),
            pl.BlockSpec((1, d), whole),
            pl.BlockSpec((me, d), whole),
        ],
        out_specs=[pl.BlockSpec((tm, d), lambda i, j: (i, 0)), pl.BlockSpec((me, d), whole)],
        out_shape=[jax.ShapeDtypeStruct((m, d), F32), jax.ShapeDtypeStruct((me, d), F32)],
        scratch_shapes=[pltpu.VMEM((tm, d), BF16), pltpu.VMEM((me, d), BF16)],
        compiler_params=_cparams(2),
        name="mlp",
    )(x, nw1, w_up, w_down, nw2, x_extra)
    return out, out_extra


def _pad_lanes(v, n=HEAD_DIM):
    return jnp.pad(v.reshape(1, -1), ((0, 0), (0, n - v.shape[-1])))


def kernel(x_prompt, x_sample, state_gdn, state_conv, state_pool, norm_pre_mix, w_in, conv_w, a_log, dt_bias,
           norm_gdn_out, w_pool, pool_scale, w_out, norm_post_mix, norm_pre_mlp, w_up, w_down, norm_post_mlp):
    batch, seq, d_model = x_prompt.shape
    dec_batch, dec_seq, _ = x_sample.shape
    assert dec_seq == 1
    depth = w_in.shape[0]
    n_heads = a_log.shape[1]
    d_gdn = n_heads * HEAD_DIM
    d_qkv = 3 * d_gdn
    d_pool = pool_scale.shape[1]
    o_gate = d_qkv + d_gdn

    yp = x_prompt.reshape(batch * seq, d_model)
    ys = x_sample.reshape(dec_batch, d_model)
    outs = [[] for _ in range(6)]
    for l in range(depth):
        alog = _pad_lanes(a_log[l])
        dtb = _pad_lanes(dt_bias[l])
        gnw = norm_gdn_out[l].reshape(1, HEAD_DIM)
        npm = norm_pre_mix[l].reshape(1, d_model)
        npo = norm_post_mix[l].reshape(1, d_model)
        nm1 = norm_pre_mlp[l].reshape(1, d_model)
        nm2 = norm_post_mlp[l].reshape(1, d_model)
        ps = pool_scale[l].reshape(1, d_pool)

        proj_s, u_s, gates_s, *w_in_b = _in_proj(
            ys, npm, jnp.swapaxes(w_in[l], 0, 1), o_gate, d_pool, alog, dtb, n_heads,
            tm=dec_batch, tn=SAMPLE_TN, emit=True)

        proj_p, u_p, gates_p = _in_proj(
            yp, npm, w_in_b, o_gate, d_pool, alog, dtb, n_heads, tm=IN_PROJ_TM, tn=IN_PROJ_TN, emit=False)
        to_cast = (w_up[l], w_down[l], w_out[l], w_pool[l].reshape(-1, w_pool.shape[-1]))
        gdn = functools.partial(_gdn_prompt, proj_p, gates_p, conv_w[l], gnw, batch=batch, seq=seq,
                                n_heads=n_heads, hs=GDN_HEADS_PER_STEP, rows=GDN_ROWS)
        oa_p, s_p, norm_p, w_up_b, w_down_b, w_out_b, w_pool_b = gdn(to_cast, precise=False)
        oa_p, s_p = lax.cond(jnp.max(norm_p) <= INV_FAST_NORM, lambda: (oa_p, s_p),
                             lambda: tuple(gdn((), precise=True)[:2]))
        w_pool_b = w_pool_b.reshape(w_pool.shape[1:])
        token_job = (proj_s, jnp.swapaxes(state_conv[l], 0, 1), gates_s, conv_w[l], gnw, state_gdn[l], n_heads)
        x1_p, oa_s, s_s, cs_new = _mix_out(oa_p, u_p, u_p, yp, w_pool_b, ps, w_out_b, npo, tm=MIX_TM, seq=seq,
                                           pos0=0, token_job=token_job)
        x1_s, hist_new = _mix_out(oa_s, u_s, jnp.swapaxes(state_pool[l], 0, 1), ys, w_pool_b, ps, w_out_b, npo,
                                  tm=dec_batch, seq=1, pos0=PAST_LEN)
        yp, ys = _mlp(x1_p, nm1, w_up_b, w_down_b, nm2, MLP_TM, MLP_TF, x1_s)
        outs[0].append(s_p)
        outs[1].append(proj_p.reshape(batch, seq, -1)[:, seq - (CONV_W - 1):, :d_qkv])
        outs[2].append(u_p.reshape(batch, seq, -1)[:, seq - POOL_BUF:])
        outs[3].append(s_s)
        outs[4].append(jnp.swapaxes(cs_new, 0, 1))
        outs[5].append(jnp.swapaxes(hist_new, 0, 1))

    return (yp.reshape(batch, seq, d_model), ys.reshape(dec_batch, dec_seq, d_model),
            *[jnp.stack(o) for o in outs])
```

```python
import functools

import jax
import jax.numpy as jnp
from jax import lax
from jax.experimental import pallas as pl
from jax.experimental.pallas import tpu as pltpu

F32 = jnp.float32
BF16 = jnp.bfloat16

EPS = 1e-6
HEAD_DIM = 128
CONV_W = 4
POOL_WINDOWS = (2, 4, 8, 16)
POOL_BUF = max(POOL_WINDOWS) - 1
PAST_LEN = 16384
HALO_CONV = 8
HALO_POOL = 16
GDN_CHUNK = 128
INV_BASE = 16
INV_FAST_NORM = 0.75
SUBLANES = 8
MASKED = -1e30
VMEM_LIMIT = 56 * 1024 * 1024
MLP_VMEM_LIMIT = 60 * 1024 * 1024

IN_PROJ_TM, IN_PROJ_TN, SAMPLE_TN = 1024, 1024, 1024
GDN_HEADS_PER_STEP, GDN_ROWS = 8, 256
MIX_TM = 512
MLP_TM, MLP_TF = 512, 2048


def _cparams(n_axes, vmem_limit=VMEM_LIMIT):
    return pltpu.CompilerParams(
        dimension_semantics=("arbitrary",) * n_axes, vmem_limit_bytes=vmem_limit)


def _dot(a, b):
    return jnp.dot(a.astype(BF16), b.astype(BF16), preferred_element_type=F32)


def _dot_nt(a, b):
    return lax.dot_general(a.astype(BF16), b.astype(BF16), (((1,), (1,)), ((), ())),
                           preferred_element_type=F32)


def _rms(x, w):
    return x * lax.rsqrt(jnp.mean(x * x, axis=-1, keepdims=True) + EPS) * w


def _silu(x):
    h = 0.5 * x
    return h + h * jnp.tanh(h)


def _softplus(x):
    return jnp.maximum(x, 0.0) + jnp.log(1.0 + jnp.exp(-jnp.abs(x)))


def _l2norm(x, scale=1.0):
    return x * (lax.rsqrt(jnp.sum(x * x, axis=-1, keepdims=True) + EPS) * scale)


def _inproj_kernel(x_ref, nw_ref, wa_ref, wt_ref, wc_ref, alog_ref, dtb_ref, *rest, n_heads, n_main, emit):
    if emit:
        out_ref, u_ref, gates_ref, wa_out, wt_out, wc_out, h_ref = rest
    else:
        out_ref, u_ref, gates_ref, h_ref = rest
    j = pl.program_id(1)

    def main_step(first):
        if first:
            h = _rms(x_ref[...], nw_ref[...]).astype(BF16)
            h_ref[...] = h
        else:
            h = h_ref[...]
        w = wa_ref[...].T.astype(BF16) if emit else wa_ref[...]
        if emit:
            wa_out[...] = w
        out_ref[...] = jnp.dot(h, w, preferred_element_type=F32)

    pl.when(j == 0)(lambda: main_step(True))
    pl.when((j > 0) & (j < n_main))(lambda: main_step(False))

    @pl.when(j == n_main)
    def _():
        if emit:
            n_gate = 2 * n_heads
            blk = wt_ref[...]
            pool_rows = jnp.concatenate([blk[n_gate:], wc_ref[...]], axis=0)
            gate_rows = jnp.concatenate([blk[:n_gate], jnp.zeros((HEAD_DIM - n_gate, blk.shape[1]), F32)], axis=0)
            wt = pool_rows.T.astype(BF16)
            wc = gate_rows.T.astype(BF16)
            wt_out[...] = wt
            wc_out[...] = wc
        else:
            wt = wt_ref[...]
            wc = wc_ref[...]
        h = h_ref[...]
        u_ref[...] = jnp.dot(h, wt, preferred_element_type=F32)
        ab = jnp.dot(h, wc, preferred_element_type=F32)
        lane = lax.broadcasted_iota(jnp.int32, ab.shape, 1)
        g = -jnp.exp(alog_ref[...]) * _softplus(ab + dtb_ref[...])
        gates_ref[...] = jnp.where(lane < n_heads, g, jax.nn.sigmoid(ab))


def _in_proj(x, norm_w, weights, n, d_pool, alog, dtb, n_heads, tm, tn, emit):
    m, d = x.shape
    n_main = n // tn
    n_gate = 2 * n_heads
    vec = pl.BlockSpec((1, HEAD_DIM), lambda i, j: (0, 0))
    out_specs = [
        pl.BlockSpec((tm, tn), lambda i, j: (i, jnp.minimum(j, n_main - 1))),
        pl.BlockSpec((tm, d_pool), lambda i, j: (i, 0)),
        pl.BlockSpec((tm, HEAD_DIM), lambda i, j: (i, 0)),
    ]
    out_shape = [jax.ShapeDtypeStruct((m, n), F32), jax.ShapeDtypeStruct((m, d_pool), F32),
                 jax.ShapeDtypeStruct((m, HEAD_DIM), F32)]
    wa_bf_spec = pl.BlockSpec((d, tn), lambda i, j: (0, jnp.minimum(j, n_main - 1)))
    wt_bf_spec = pl.BlockSpec((d, d_pool), lambda i, j: (0, 0), pipeline_mode=pl.Buffered(1))
    wc_bf_spec = pl.BlockSpec((d, HEAD_DIM), lambda i, j: (0, 0))
    if emit:
        assert n % d_pool == 0 and (n + d_pool) % n_gate == 0 and n_gate % SUBLANES == 0
        operands = (weights, weights, weights)
        w_specs = [
            pl.BlockSpec((tn, d), lambda i, j: (jnp.minimum(j, n_main - 1), 0)),
            pl.BlockSpec((d_pool, d), lambda i, j: (n // d_pool, 0), pipeline_mode=pl.Buffered(1)),
            pl.BlockSpec((n_gate, d), lambda i, j: ((n + d_pool) // n_gate, 0)),
        ]
        out_specs += [wa_bf_spec, pl.BlockSpec((d, d_pool), lambda i, j: (0, 0)), wc_bf_spec]
        out_shape += [jax.ShapeDtypeStruct((d, n), BF16), jax.ShapeDtypeStruct((d, d_pool), BF16),
                      jax.ShapeDtypeStruct((d, HEAD_DIM), BF16)]
    else:
        operands = weights
        w_specs = [wa_bf_spec, wt_bf_spec, wc_bf_spec]
    return pl.pallas_call(
        functools.partial(_inproj_kernel, n_heads=n_heads, n_main=n_main, emit=emit),
        grid=(m // tm, n_main + 1),
        in_specs=[
            pl.BlockSpec((tm, d), lambda i, j: (i, 0)),
            pl.BlockSpec((1, d), lambda i, j: (0, 0)),
            *w_specs,
            vec, vec,
        ],
        out_specs=out_specs,
        out_shape=out_shape,
        scratch_shapes=[pltpu.VMEM((tm, d), BF16)],
        compiler_params=_cparams(2),
        name="in_proj",
    )(x, norm_w, *operands, alog, dtb)


def _lane_column(x, lane_idx):
    lane = lax.broadcasted_iota(jnp.int32, x.shape, 1)
    col = jnp.sum(jnp.where(lane == lane_idx, x, 0.0), axis=1, keepdims=True)
    return jnp.broadcast_to(col, x.shape)


def _unit_lower_inverses(lms, precise):
    n = lms[0].shape[0]
    row = lax.broadcasted_iota(jnp.int32, (n, n), 0)
    col = lax.broadcasted_iota(jnp.int32, (n, n), 1)
    eye = jnp.where(row == col, 1.0, 0.0)

    def same_block(size):
        shift = size.bit_length() - 1
        return lax.shift_right_logical(row, shift) == lax.shift_right_logical(col, shift)

    if precise:
        op = lambda x: x
        mm = functools.partial(jnp.dot, preferred_element_type=F32, precision=lax.Precision.HIGHEST)
        terms = INV_BASE
        in_base = same_block(INV_BASE)
        lbs = [jnp.where(in_base, lm, 0.0) for lm in lms]
    else:
        op = lambda x: x.astype(BF16)
        mm = functools.partial(jnp.dot, preferred_element_type=F32)
        terms = n // 2
        assert INV_FAST_NORM ** terms / (1.0 - INV_FAST_NORM) < 2.0 ** -24
        lbs = lms

    qs = [eye - lb for lb in lbs]
    lbs = [op(lb) for lb in lbs]
    nks = [mm(lb, lb) for lb in lbs]
    power = 2
    while 2 * power < terms:
        nbs = [op(nk) for nk in nks]
        res = [mm(nb, jnp.concatenate([nb, op(q)], axis=1)) for nb, q in zip(nbs, qs)]
        nks = [r[:, :n] for r in res]
        qs = [q + r[:, n:] for q, r in zip(qs, res)]
        power *= 2
    ps = [q + mm(op(nk), op(q)) for nk, q in zip(nks, qs)]
    if not precise:
        return ps
    size = INV_BASE
    while size < n:
        off = jnp.logical_and(same_block(2 * size), jnp.logical_not(same_block(size)))
        cs = [op(jnp.where(off, lm, 0.0)) for lm in lms]
        pbs = [op(p) for p in ps]
        ts = [op(mm(c, pb)) for c, pb in zip(cs, pbs)]
        ps = [p - mm(pb, t) for p, pb, t in zip(ps, pbs, ts)]
        size *= 2
    return ps


def _gdn_prompt_kernel(q_ref, k_ref, v_ref, gate_ref, hq_ref, hk_ref, hv_ref, gates_ref,
                       cwq_ref, cwk_ref, cwv_ref, gnw_ref, *rest, n_heads, hs, rows, n_cast, precise):
    cast_in = rest[:n_cast]
    o_ref, sout_ref, norm_ref = rest[n_cast:n_cast + 3]
    cast_out = rest[n_cast + 3:2 * n_cast + 3]
    s_ref, ext_ref = rest[2 * n_cast + 3:]
    hg = pl.program_id(1)
    blk = pl.program_id(2)
    c = GDN_CHUNK

    for src, dst in zip(cast_in, cast_out):
        dst[...] = src[...].astype(dst.dtype)

    @pl.when(blk == 0)
    def _():
        s_ref[...] = jnp.zeros_like(s_ref)

    for a, (x_ref, halo_ref) in enumerate(((q_ref, hq_ref), (k_ref, hk_ref), (v_ref, hv_ref))):
        for hi in range(hs):
            lanes = slice(hi * HEAD_DIM, (hi + 1) * HEAD_DIM)
            ext_ref[a * hs + hi, 0:HALO_CONV, :] = jnp.where(blk > 0, halo_ref[:, lanes], 0.0)
            ext_ref[a * hs + hi, HALO_CONV:, :] = x_ref[:, lanes]

    def conv_silu(a, cw_ref, p):
        ci, hi = p
        lanes = slice(hi * HEAD_DIM, (hi + 1) * HEAD_DIM)
        acc = None
        for j in range(CONV_W):
            term = (cw_ref[j:j + 1, lanes]
                    * ext_ref[a * hs + hi, pl.ds(HALO_CONV - (CONV_W - 1) + j + ci * c, c), :])
            acc = term if acc is None else acc + term
        return _silu(acc)

    row = lax.broadcasted_iota(jnp.int32, (c, c), 0)
    col = lax.broadcasted_iota(jnp.int32, (c, c), 1)
    tril = row >= col
    strict = row > col
    tril_f = jnp.where(tril, 1.0, 0.0)
    gnw = gnw_ref[...]
    heads = range(hs)

    n_chunks = rows // c
    probs = [(ci, hi) for ci in range(n_chunks) for hi in heads]
    n_p = range(len(probs))
    gates = [gates_ref[ci * c:(ci + 1) * c, :] for ci in range(n_chunks)]
    gcum = [jnp.dot(tril_f, g, preferred_element_type=F32, precision=lax.Precision.HIGHEST) for g in gates]
    qc = [_l2norm(conv_silu(0, cwq_ref, p), HEAD_DIM ** -0.5) for p in probs]
    kc = [_l2norm(conv_silu(1, cwk_ref, p)) for p in probs]
    vc = [conv_silu(2, cwv_ref, p) for p in probs]
    gc = [_lane_column(gcum[ci], hg * hs + hi) for ci, hi in probs]
    bc = [_lane_column(gates[ci], hg * hs + hi + n_heads) for ci, hi in probs]
    decay = [jnp.exp(jnp.where(tril, g - g.T, MASKED)) for g in gc]
    egc = [jnp.exp(g) for g in gc]
    g_last = [g[c - 1:c, :] for g in gc]
    kb = [k.astype(BF16) for k in kc]
    kk = [_dot_nt(k, k) for k in kb]
    qk = [_dot_nt(q, k) for q, k in zip(qc, kb)]
    lm = [jnp.where(strict, kk[p] * bc[p] * decay[p], 0.0) for p in n_p]
    row_sums = [jnp.sum(jnp.abs(x), axis=1, keepdims=True) for x in lm]
    worst = jnp.max(functools.reduce(jnp.maximum, row_sums), axis=0, keepdims=True)
    norm_ref[...] = jnp.broadcast_to(worst.reshape(1, 1, 1), norm_ref.shape)
    t_inv = _unit_lower_inverses(lm, precise)
    uw = [_dot(t_inv[p], jnp.concatenate([vc[p] * bc[p], kc[p] * (bc[p] * egc[p])], axis=1)) for p in n_p]
    intra = [qk[p] * decay[p] for p in n_p]
    kdt = [(kc[p] * jnp.exp(g_last[p] - gc[p])).T for p in n_p]
    wq = [jnp.concatenate([uw[p][:, HEAD_DIM:], qc[p] * egc[p]], axis=0) for p in n_p]
    ik = [jnp.concatenate([intra[p], kdt[p]], axis=0) for p in n_p]

    s = [s_ref[hi] for hi in heads]
    for ci in range(n_chunks):
        ps = [ci * hs + hi for hi in heads]
        r1 = [_dot(wq[p], s[hi]) for hi, p in zip(heads, ps)]
        v_new = [uw[p][:, :HEAD_DIM] - r1[hi][:c] for hi, p in zip(heads, ps)]
        r2 = [_dot(ik[p], v_new[hi]) for hi, p in zip(heads, ps)]
        s = [s[hi] * jnp.exp(g_last[p]) + r2[hi][c:] for hi, p in zip(heads, ps)]
        for hi in heads:
            lanes = slice(hi * HEAD_DIM, (hi + 1) * HEAD_DIM)
            o = _rms(r1[hi][c:] + r2[hi][:c], gnw) * _silu(gate_ref[ci * c:(ci + 1) * c, lanes])
            o_ref[ci * c:(ci + 1) * c, lanes] = o.astype(o_ref.dtype)

    for hi in range(hs):
        s_ref[hi] = s[hi]

    @pl.when(blk == pl.num_programs(2) - 1)
    def _():
        for hi in range(hs):
            sout_ref[0, hi] = s[hi]


def _gdn_prompt(proj, gates, conv_w, gnw, to_cast, batch, seq, n_heads, hs, rows, precise):
    nblk = seq // rows
    hb = rows // HALO_CONV
    ng = n_heads // hs
    width = hs * HEAD_DIM
    steps = batch * ng * nblk

    def slab(w):
        return pl.BlockSpec((w.shape[0] // steps, w.shape[1]), lambda b, h, k: ((b * ng + h) * nblk + k, 0))

    def main(off):
        return pl.BlockSpec((rows, width), lambda b, h, k: (b * nblk + k, off * ng + h))

    def halo(off):
        return pl.BlockSpec(
            (HALO_CONV, width), lambda b, h, k: (jnp.maximum((b * nblk + k) * hb - 1, 0), off * ng + h))

    def cw(off):
        return pl.BlockSpec((CONV_W, width), lambda b, h, k: (0, off * ng + h))

    kern = functools.partial(_gdn_prompt_kernel, n_heads=n_heads, hs=hs, rows=rows, n_cast=len(to_cast),
                             precise=precise)
    return pl.pallas_call(
        kern,
        grid=(batch, ng, nblk),
        in_specs=[main(0), main(1), main(2), main(3), halo(0), halo(1), halo(2),
                  pl.BlockSpec((rows, HEAD_DIM), lambda b, h, k: (b * nblk + k, 0)),
                  cw(0), cw(1), cw(2),
                  pl.BlockSpec((1, HEAD_DIM), lambda b, h, k: (0, 0)),
                  *[slab(w) for w in to_cast]],
        out_specs=[
            pl.BlockSpec((rows, width), lambda b, h, k: (b * nblk + k, h)),
            pl.BlockSpec((1, hs, HEAD_DIM, HEAD_DIM), lambda b, h, k: (b, h, 0, 0)),
            pl.BlockSpec((1, SUBLANES, HEAD_DIM), lambda b, h, k: ((b * ng + h) * nblk + k, 0, 0)),
            *[slab(w) for w in to_cast],
        ],
        out_shape=[
            jax.ShapeDtypeStruct((batch * seq, n_heads * HEAD_DIM), BF16),
            jax.ShapeDtypeStruct((batch, n_heads, HEAD_DIM, HEAD_DIM), F32),
            jax.ShapeDtypeStruct((steps, SUBLANES, HEAD_DIM), F32),
            *[jax.ShapeDtypeStruct(w.shape, BF16) for w in to_cast],
        ],
        scratch_shapes=[pltpu.VMEM((hs, HEAD_DIM, HEAD_DIM), F32),
                        pltpu.VMEM((3 * hs, rows + HALO_CONV, HEAD_DIM), F32)],
        compiler_params=_cparams(3),
        name="gdn_prompt",
    )(proj, proj, proj, proj, proj, proj, proj, gates, conv_w, conv_w, conv_w, gnw, *to_cast)


def _gdn_token_steps(x_ref, cs_ref, gates_ref, cw_ref, gnw_ref, s_in_ref, o_ref, s_out_ref, cs_out_ref, n_heads):
    d = n_heads * HEAD_DIM
    sq = (HEAD_DIM, HEAD_DIM)
    x = x_ref[...]
    cw = cw_ref[...]
    conv = cw[CONV_W - 1:CONV_W, :] * x[:, :3 * d]
    for j in range(CONV_W - 1):
        conv = conv + cw[j:j + 1, :] * cs_ref[j]
    qkv = _silu(conv)
    for j in range(CONV_W - 2):
        cs_out_ref[j] = cs_ref[j + 1]
    cs_out_ref[CONV_W - 2] = x[:, :3 * d]

    gates = gates_ref[...]
    eg = jnp.exp(gates)
    gnw = gnw_ref[...]
    heads = range(n_heads)
    qn = [_l2norm(qkv[:, h * HEAD_DIM:(h + 1) * HEAD_DIM], HEAD_DIM ** -0.5) for h in heads]
    kn = [_l2norm(qkv[:, d + h * HEAD_DIM:d + (h + 1) * HEAD_DIM]) for h in heads]
    for t in range(x.shape[0]):
        row = slice(t, t + 1)
        v = [qkv[row, 2 * d + h * HEAD_DIM:2 * d + (h + 1) * HEAD_DIM] for h in heads]
        kb = [jnp.broadcast_to(kn[h][row], sq).T for h in heads]
        qb = [jnp.broadcast_to(qn[h][row], sq).T for h in heads]
        s1 = [s_in_ref[t, h] * eg[row, h:h + 1] for h in heads]
        ks = [jnp.sum(s1[h] * kb[h], axis=0, keepdims=True) for h in heads]
        delta = [(v[h] - ks[h]) * gates[row, n_heads + h:n_heads + h + 1] for h in heads]
        s2 = [s1[h] + kb[h] * delta[h] for h in heads]
        for h in heads:
            lanes = slice(h * HEAD_DIM, (h + 1) * HEAD_DIM)
            s_out_ref[t, h] = s2[h]
            o = jnp.sum(s2[h] * qb[h], axis=0, keepdims=True)
            o_ref[row, lanes] = _rms(o, gnw) * _silu(x[row, 3 * d + h * HEAD_DIM:3 * d + (h + 1) * HEAD_DIM])


def _mix_out_kernel(oa_ref, u_ref, hist_ref, x_ref, wp_ref, ps_ref, wo_ref, nw_ref, *rest,
                    tm, seq, pos0, n_heads):
    if n_heads:
        tok_in, out_ref, tok_out, rest = rest[:6], rest[6], rest[7:10], rest[10:]
        _gdn_token_steps(*tok_in, *tok_out, n_heads)
    else:
        out_ref, rest = rest[0], rest[1:]
    d_pool = u_ref.shape[1]
    d_gdn = oa_ref.shape[1]
    d = out_ref.shape[1]
    n_groups = len(POOL_WINDOWS)
    gd = d_pool // n_groups
    strips = gd // HEAD_DIM
    if seq > 1:
        (ext_ref,) = rest
        start = (pl.program_id(0) * tm) % seq
        for si in range(d_pool // HEAD_DIM):
            lanes = slice(si * HEAD_DIM, (si + 1) * HEAD_DIM)
            ext_ref[si, 0:HALO_POOL, :] = jnp.where(start > 0, hist_ref[:, lanes], 0.0)
            ext_ref[si, HALO_POOL:, :] = u_ref[:, lanes]
        pos = pos0 + start + lax.broadcasted_iota(jnp.int32, (tm, HEAD_DIM), 0)
    else:
        (hist_out_ref,) = rest
        for r in range(POOL_BUF - 1):
            hist_out_ref[r] = hist_ref[r + 1]
        hist_out_ref[POOL_BUF - 1] = u_ref[...]
        pos = jnp.full((tm, HEAD_DIM), pos0, jnp.int32)

    tn = d // n_groups
    oa = oa_ref[...].astype(BF16)
    parts = []
    for gi, win in enumerate(POOL_WINDOWS):
        cols = slice(gi * tn, (gi + 1) * tn)
        out_ref[:, cols] = jnp.dot(oa, wo_ref[:d_gdn, cols], preferred_element_type=F32)
        cnt = jnp.minimum(pos + 1, win).astype(F32)
        pooled = []
        for si in range(gi * strips, (gi + 1) * strips):
            lanes = slice(si * HEAD_DIM, (si + 1) * HEAD_DIM)
            cur = u_ref[:, lanes]
            wsum = cur
            for r in range(1, win):
                if seq > 1:
                    wsum = wsum + ext_ref[si, pl.ds(HALO_POOL - r, tm), :]
                else:
                    wsum = wsum + hist_ref[POOL_BUF - r, :, lanes]
            pooled.append(wsum / cnt - cur)
        ob = _dot(jnp.concatenate(pooled, axis=1), wp_ref[gi]) * ps_ref[:, gi * gd:(gi + 1) * gd]
        parts.append(ob.astype(BF16))
    ob_all = jnp.concatenate(parts, axis=1)
    mix = [out_ref[:, ci * tn:(ci + 1) * tn]
           + jnp.dot(ob_all, wo_ref[d_gdn:, ci * tn:(ci + 1) * tn], preferred_element_type=F32)
           for ci in range(n_groups)]
    ss = sum(jnp.sum(m * m, axis=-1, keepdims=True) for m in mix)
    inv = lax.rsqrt(ss * (1.0 / d) + EPS)
    for ci in range(n_groups):
        cols = slice(ci * tn, (ci + 1) * tn)
        out_ref[:, cols] = x_ref[:, cols] + mix[ci] * inv * nw_ref[:, cols]


def _mix_out(o_a, u, hist, x, w_pool, pool_scale, w_out, norm_w, tm, seq, pos0, token_job=None):
    m, d = x.shape
    d_pool = pool_scale.shape[1]
    d_gdn = o_a.shape[1]
    out_specs = [pl.BlockSpec((tm, d), lambda i: (i, 0))]
    out_shape = [jax.ShapeDtypeStruct((m, d), F32)]
    tok_specs, tok_operands, n_heads = [], [], 0
    if token_job is not None:
        proj, conv_state, gates, conv_w, gnw, state, n_heads = token_job
        tokens, dg = proj.shape[0], n_heads * HEAD_DIM
        nt = tokens // (m // tm)
        assert seq > 1 and nt * (m // tm) == tokens and nt % SUBLANES == 0
        cs_spec = pl.BlockSpec((CONV_W - 1, nt, 3 * dg), lambda i: (0, i, 0))
        st_spec = pl.BlockSpec((nt, n_heads, HEAD_DIM, HEAD_DIM), lambda i: (i, 0, 0, 0))
        tok_specs = [
            pl.BlockSpec((nt, 4 * dg), lambda i: (i, 0)),
            cs_spec,
            pl.BlockSpec((nt, HEAD_DIM), lambda i: (i, 0)),
            pl.BlockSpec((CONV_W, 3 * dg), lambda i: (0, 0)),
            pl.BlockSpec((1, HEAD_DIM), lambda i: (0, 0)),
            st_spec,
        ]
        tok_operands = [proj, conv_state, gates, conv_w, gnw, state]
        out_specs += [pl.BlockSpec((nt, dg), lambda i: (i, 0)), st_spec, cs_spec]
        out_shape += [jax.ShapeDtypeStruct((tokens, dg), F32), jax.ShapeDtypeStruct(state.shape, F32),
                      jax.ShapeDtypeStruct(conv_state.shape, F32)]
    if seq > 1:
        hb = tm // HALO_POOL
        hist_spec = pl.BlockSpec((HALO_POOL, d_pool), lambda i: (jnp.maximum(i * hb - 1, 0), 0))
        scratch = [pltpu.VMEM((d_pool // HEAD_DIM, tm + HALO_POOL, HEAD_DIM), F32)]
    else:
        hist_spec = pl.BlockSpec((POOL_BUF, tm, d_pool), lambda i: (0, i, 0))
        out_specs.append(hist_spec)
        out_shape.append(jax.ShapeDtypeStruct(hist.shape, F32))
        scratch = []
    kern = functools.partial(_mix_out_kernel, tm=tm, seq=seq, pos0=pos0, n_heads=n_heads)
    return pl.pallas_call(
        kern,
        grid=(m // tm,),
        in_specs=[
            pl.BlockSpec((tm, d_gdn), lambda i: (i, 0)),
            pl.BlockSpec((tm, d_pool), lambda i: (i, 0)),
            hist_spec,
            pl.BlockSpec((tm, d), lambda i: (i, 0)),
            pl.BlockSpec(w_pool.shape, lambda i: (0, 0, 0), pipeline_mode=pl.Buffered(1)),
            pl.BlockSpec((1, d_pool), lambda i: (0, 0)),
            pl.BlockSpec(w_out.shape, lambda i: (0, 0), pipeline_mode=pl.Buffered(1)),
            pl.BlockSpec((1, d), lambda i: (0, 0)),
            *tok_specs,
        ],
        out_specs=out_specs,
        out_shape=out_shape,
        scratch_shapes=scratch,
        compiler_params=_cparams(1),
        name="mix_out",
    )(o_a, u, hist, x, w_pool, pool_scale, w_out, norm_w, *tok_operands)


def _mlp_kernel(x_ref, nw1_ref, wu_ref, wd_ref, nw2_ref, xe_ref, out_ref, oute_ref, h_ref, he_ref, *, n_ff_tiles):
    i = pl.program_id(0)
    j = pl.program_id(1)

    def rows_step(first, last, x_ref, h_ref, out_ref):
        if first:
            h = _rms(x_ref[...], nw1_ref[...]).astype(BF16)
            h_ref[...] = h
        else:
            h = h_ref[...]
        up = jnp.dot(h, wu_ref[...], preferred_element_type=F32)
        act = jnp.square(jnp.maximum(up, 0.0)).astype(BF16)
        part = jnp.dot(act, wd_ref[...], preferred_element_type=F32)
        acc = part if first else out_ref[...] + part
        out_ref[...] = x_ref[...] + _rms(acc, nw2_ref[...]) if last else acc

    def step(first, last):
        rows_step(first, last, x_ref, h_ref, out_ref)
        pl.when(i == 0)(lambda: rows_step(first, last, xe_ref, he_ref, oute_ref))

    nj = n_ff_tiles
    if nj == 1:
        step(True, True)
    else:
        pl.when(j == 0)(lambda: step(True, False))
        pl.when((j > 0) & (j < nj - 1))(lambda: step(False, False))
        pl.when(j == nj - 1)(lambda: step(False, True))


def _mlp(x, nw1, w_up, w_down, nw2, tm, tf, x_extra):
    m, d = x.shape
    me = x_extra.shape[0]
    f = w_up.shape[1]
    whole = lambda i, j: (0, 0)
    out, out_extra = pl.pallas_call(
        functools.partial(_mlp_kernel, n_ff_tiles=f // tf),
        grid=(m // tm, f // tf),
        in_specs=[
            pl.BlockSpec((tm, d), lambda i, j: (i, 0)),
            pl.BlockSpec((1, d), whole),
            pl.BlockSpec((d, tf), lambda i, j: (0, j)),
            pl.BlockSpec((tf, d), lambda i, j: (j, 0)),
            pl.BlockSpec((1, d), whole),
            pl.BlockSpec((me, d), whole),
        ],
        out_specs=[pl.BlockSpec((tm, d), lambda i, j: (i, 0)), pl.BlockSpec((me, d), whole)],
        out_shape=[jax.ShapeDtypeStruct((m, d), F32), jax.ShapeDtypeStruct((me, d), F32)],
        scratch_shapes=[pltpu.VMEM((tm, d), BF16), pltpu.VMEM((me, d), BF16)],
        compiler_params=_cparams(2, MLP_VMEM_LIMIT),
        name="mlp",
    )(x, nw1, w_up, w_down, nw2, x_extra)
    return out, out_extra


def _pad_lanes(v, n=HEAD_DIM):
    return jnp.pad(v.reshape(1, -1), ((0, 0), (0, n - v.shape[-1])))


def kernel(x_prompt, x_sample, state_gdn, state_conv, state_pool, norm_pre_mix, w_in, conv_w, a_log, dt_bias,
           norm_gdn_out, w_pool, pool_scale, w_out, norm_post_mix, norm_pre_mlp, w_up, w_down, norm_post_mlp):
    batch, seq, d_model = x_prompt.shape
    dec_batch, dec_seq, _ = x_sample.shape
    assert dec_seq == 1
    depth = w_in.shape[0]
    n_heads = a_log.shape[1]
    d_gdn = n_heads * HEAD_DIM
    d_qkv = 3 * d_gdn
    d_pool = pool_scale.shape[1]
    o_gate = d_qkv + d_gdn

    yp = x_prompt.reshape(batch * seq, d_model)
    ys = x_sample.reshape(dec_batch, d_model)
    outs = [[] for _ in range(6)]
    for l in range(depth):
        alog = _pad_lanes(a_log[l])
        dtb = _pad_lanes(dt_bias[l])
        gnw = norm_gdn_out[l].reshape(1, HEAD_DIM)
        npm = norm_pre_mix[l].reshape(1, d_model)
        npo = norm_post_mix[l].reshape(1, d_model)
        nm1 = norm_pre_mlp[l].reshape(1, d_model)
        nm2 = norm_post_mlp[l].reshape(1, d_model)
        ps = pool_scale[l].reshape(1, d_pool)

        proj_s, u_s, gates_s, *w_in_b = _in_proj(
            ys, npm, jnp.swapaxes(w_in[l], 0, 1), o_gate, d_pool, alog, dtb, n_heads,
            tm=dec_batch, tn=SAMPLE_TN, emit=True)

        proj_p, u_p, gates_p = _in_proj(
            yp, npm, w_in_b, o_gate, d_pool, alog, dtb, n_heads, tm=IN_PROJ_TM, tn=IN_PROJ_TN, emit=False)
        to_cast = (w_up[l], w_down[l], w_out[l], w_pool[l].reshape(-1, w_pool.shape[-1]))
        gdn = functools.partial(_gdn_prompt, proj_p, gates_p, conv_w[l], gnw, batch=batch, seq=seq,
                                n_heads=n_heads, hs=GDN_HEADS_PER_STEP, rows=GDN_ROWS)
        oa_p, s_p, norm_p, w_up_b, w_down_b, w_out_b, w_pool_b = gdn(to_cast, precise=False)
        oa_p, s_p = lax.cond(jnp.max(norm_p) <= INV_FAST_NORM, lambda: (oa_p, s_p),
                             lambda: tuple(gdn((), precise=True)[:2]))
        w_pool_b = w_pool_b.reshape(w_pool.shape[1:])
        token_job = (proj_s, jnp.swapaxes(state_conv[l], 0, 1), gates_s, conv_w[l], gnw, state_gdn[l], n_heads)
        x1_p, oa_s, s_s, cs_new = _mix_out(oa_p, u_p, u_p, yp, w_pool_b, ps, w_out_b, npo, tm=MIX_TM, seq=seq,
                                           pos0=0, token_job=token_job)
        x1_s, hist_new = _mix_out(oa_s, u_s, jnp.swapaxes(state_pool[l], 0, 1), ys, w_pool_b, ps, w_out_b, npo,
                                  tm=dec_batch, seq=1, pos0=PAST_LEN)
        yp, ys = _mlp(x1_p, nm1, w_up_b, w_down_b, nm2, MLP_TM, MLP_TF, x1_s)
        outs[0].append(s_p)
        outs[1].append(proj_p.reshape(batch, seq, -1)[:, seq - (CONV_W - 1):, :d_qkv])
        outs[2].append(u_p.reshape(batch, seq, -1)[:, seq - POOL_BUF:])
        outs[3].append(s_s)
        outs[4].append(jnp.swapaxes(cs_new, 0, 1))
        outs[5].append(jnp.swapaxes(hist_new, 0, 1))

    return (yp.reshape(batch, seq, d_model), ys.reshape(dec_batch, dec_seq, d_model),
            *[jnp.stack(o) for o in outs])
```

```python
import functools

import jax
import jax.numpy as jnp
from jax import lax
from jax.experimental import pallas as pl
from jax.experimental.pallas import tpu as pltpu

F32 = jnp.float32
BF16 = jnp.bfloat16

EPS = 1e-6
HEAD_DIM = 128
CONV_W = 4
POOL_WINDOWS = (2, 4, 8, 16)
POOL_BUF = max(POOL_WINDOWS) - 1
PAST_LEN = 16384
HALO_CONV = 8
HALO_POOL = 16
GDN_CHUNK = 128
INV_BASE = 16
INV_FAST_NORM = 0.75
SUBLANES = 8
MASKED = -1e30
VMEM_LIMIT = 56 * 1024 * 1024
MLP_VMEM_LIMIT = 60 * 1024 * 1024

IN_PROJ_TM, IN_PROJ_TN, SAMPLE_TN = 1024, 1024, 1024
GDN_HEADS_PER_STEP, GDN_ROWS = 8, 256
MIX_TM = 512
TOKEN_RING = 3
MLP_TM, MLP_TF = 512, 2048


def _cparams(n_axes, vmem_limit=VMEM_LIMIT):
    return pltpu.CompilerParams(
        dimension_semantics=("arbitrary",) * n_axes, vmem_limit_bytes=vmem_limit)


def _dot(a, b):
    return jnp.dot(a.astype(BF16), b.astype(BF16), preferred_element_type=F32)


def _dot_nt(a, b):
    return lax.dot_general(a.astype(BF16), b.astype(BF16), (((1,), (1,)), ((), ())),
                           preferred_element_type=F32)


def _rms(x, w):
    return x * lax.rsqrt(jnp.mean(x * x, axis=-1, keepdims=True) + EPS) * w


def _silu(x):
    h = 0.5 * x
    return h + h * jnp.tanh(h)


def _softplus(x):
    return jnp.maximum(x, 0.0) + jnp.log(1.0 + jnp.exp(-jnp.abs(x)))


def _l2norm(x, scale=1.0):
    return x * (lax.rsqrt(jnp.sum(x * x, axis=-1, keepdims=True) + EPS) * scale)


def _inproj_kernel(x_ref, nw_ref, wa_ref, wt_ref, wc_ref, alog_ref, dtb_ref, *rest, n_heads, n_main, emit):
    if emit:
        out_ref, u_ref, gates_ref, wa_out, wt_out, wc_out, h_ref = rest
    else:
        out_ref, u_ref, gates_ref, h_ref = rest
    j = pl.program_id(1)

    def main_step(first):
        if first:
            h = _rms(x_ref[...], nw_ref[...]).astype(BF16)
            h_ref[...] = h
        else:
            h = h_ref[...]
        w = wa_ref[...].T.astype(BF16) if emit else wa_ref[...]
        if emit:
            wa_out[...] = w
        out_ref[...] = jnp.dot(h, w, preferred_element_type=F32)

    pl.when(j == 0)(lambda: main_step(True))
    pl.when((j > 0) & (j < n_main))(lambda: main_step(False))

    @pl.when(j == n_main)
    def _():
        if emit:
            n_gate = 2 * n_heads
            blk = wt_ref[...]
            pool_rows = jnp.concatenate([blk[n_gate:], wc_ref[...]], axis=0)
            gate_rows = jnp.concatenate([blk[:n_gate], jnp.zeros((HEAD_DIM - n_gate, blk.shape[1]), F32)], axis=0)
            wt = pool_rows.T.astype(BF16)
            wc = gate_rows.T.astype(BF16)
            wt_out[...] = wt
            wc_out[...] = wc
        else:
            wt = wt_ref[...]
            wc = wc_ref[...]
        h = h_ref[...]
        u_ref[...] = jnp.dot(h, wt, preferred_element_type=F32)
        ab = jnp.dot(h, wc, preferred_element_type=F32)
        lane = lax.broadcasted_iota(jnp.int32, ab.shape, 1)
        g = -jnp.exp(alog_ref[...]) * _softplus(ab + dtb_ref[...])
        gates_ref[...] = jnp.where(lane < n_heads, g, jax.nn.sigmoid(ab))


def _in_proj(x, norm_w, weights, n, d_pool, alog, dtb, n_heads, tm, tn, emit):
    m, d = x.shape
    n_main = n // tn
    n_gate = 2 * n_heads
    vec = pl.BlockSpec((1, HEAD_DIM), lambda i, j: (0, 0))
    out_specs = [
        pl.BlockSpec((tm, tn), lambda i, j: (i, jnp.minimum(j, n_main - 1))),
        pl.BlockSpec((tm, d_pool), lambda i, j: (i, 0)),
        pl.BlockSpec((tm, HEAD_DIM), lambda i, j: (i, 0)),
    ]
    out_shape = [jax.ShapeDtypeStruct((m, n), F32), jax.ShapeDtypeStruct((m, d_pool), F32),
                 jax.ShapeDtypeStruct((m, HEAD_DIM), F32)]
    wa_bf_spec = pl.BlockSpec((d, tn), lambda i, j: (0, jnp.minimum(j, n_main - 1)))
    wt_bf_spec = pl.BlockSpec((d, d_pool), lambda i, j: (0, 0), pipeline_mode=pl.Buffered(1))
    wc_bf_spec = pl.BlockSpec((d, HEAD_DIM), lambda i, j: (0, 0))
    if emit:
        assert n % d_pool == 0 and (n + d_pool) % n_gate == 0 and n_gate % SUBLANES == 0
        operands = (weights, weights, weights)
        w_specs = [
            pl.BlockSpec((tn, d), lambda i, j: (jnp.minimum(j, n_main - 1), 0)),
            pl.BlockSpec((d_pool, d), lambda i, j: (n // d_pool, 0), pipeline_mode=pl.Buffered(1)),
            pl.BlockSpec((n_gate, d), lambda i, j: ((n + d_pool) // n_gate, 0)),
        ]
        out_specs += [wa_bf_spec, pl.BlockSpec((d, d_pool), lambda i, j: (0, 0)), wc_bf_spec]
        out_shape += [jax.ShapeDtypeStruct((d, n), BF16), jax.ShapeDtypeStruct((d, d_pool), BF16),
                      jax.ShapeDtypeStruct((d, HEAD_DIM), BF16)]
    else:
        operands = weights
        w_specs = [wa_bf_spec, wt_bf_spec, wc_bf_spec]
    return pl.pallas_call(
        functools.partial(_inproj_kernel, n_heads=n_heads, n_main=n_main, emit=emit),
        grid=(m // tm, n_main + 1),
        in_specs=[
            pl.BlockSpec((tm, d), lambda i, j: (i, 0)),
            pl.BlockSpec((1, d), lambda i, j: (0, 0)),
            *w_specs,
            vec, vec,
        ],
        out_specs=out_specs,
        out_shape=out_shape,
        scratch_shapes=[pltpu.VMEM((tm, d), BF16)],
        compiler_params=_cparams(2),
        name="in_proj",
    )(x, norm_w, *operands, alog, dtb)


def _lane_column(x, lane_idx):
    lane = lax.broadcasted_iota(jnp.int32, x.shape, 1)
    col = jnp.sum(jnp.where(lane == lane_idx, x, 0.0), axis=1, keepdims=True)
    return jnp.broadcast_to(col, x.shape)


def _unit_lower_inverses(lms, precise):
    n = lms[0].shape[0]
    row = lax.broadcasted_iota(jnp.int32, (n, n), 0)
    col = lax.broadcasted_iota(jnp.int32, (n, n), 1)
    eye = jnp.where(row == col, 1.0, 0.0)

    def same_block(size):
        shift = size.bit_length() - 1
        return lax.shift_right_logical(row, shift) == lax.shift_right_logical(col, shift)

    if precise:
        op = lambda x: x
        mm = functools.partial(jnp.dot, preferred_element_type=F32, precision=lax.Precision.HIGHEST)
        terms = INV_BASE
        in_base = same_block(INV_BASE)
        lbs = [jnp.where(in_base, lm, 0.0) for lm in lms]
    else:
        op = lambda x: x.astype(BF16)
        mm = functools.partial(jnp.dot, preferred_element_type=F32)
        terms = n // 2
        assert INV_FAST_NORM ** terms / (1.0 - INV_FAST_NORM) < 2.0 ** -24
        lbs = lms

    qs = [eye - lb for lb in lbs]
    lbs = [op(lb) for lb in lbs]
    nks = [mm(lb, lb) for lb in lbs]
    power = 2
    while 2 * power < terms:
        nbs = [op(nk) for nk in nks]
        res = [mm(nb, jnp.concatenate([nb, op(q)], axis=1)) for nb, q in zip(nbs, qs)]
        nks = [r[:, :n] for r in res]
        qs = [q + r[:, n:] for q, r in zip(qs, res)]
        power *= 2
    ps = [q + mm(op(nk), op(q)) for nk, q in zip(nks, qs)]
    if not precise:
        return ps
    size = INV_BASE
    while size < n:
        off = jnp.logical_and(same_block(2 * size), jnp.logical_not(same_block(size)))
        cs = [op(jnp.where(off, lm, 0.0)) for lm in lms]
        pbs = [op(p) for p in ps]
        ts = [op(mm(c, pb)) for c, pb in zip(cs, pbs)]
        ps = [p - mm(pb, t) for p, pb, t in zip(ps, pbs, ts)]
        size *= 2
    return ps


def _gdn_prompt_kernel(q_ref, k_ref, v_ref, gate_ref, hq_ref, hk_ref, hv_ref, gates_ref,
                       cwq_ref, cwk_ref, cwv_ref, gnw_ref, *rest, n_heads, hs, rows, n_cast, precise):
    cast_in = rest[:n_cast]
    o_ref, sout_ref, norm_ref = rest[n_cast:n_cast + 3]
    cast_out = rest[n_cast + 3:2 * n_cast + 3]
    s_ref, ext_ref = rest[2 * n_cast + 3:]
    hg = pl.program_id(1)
    blk = pl.program_id(2)
    c = GDN_CHUNK

    for src, dst in zip(cast_in, cast_out):
        dst[...] = src[...].astype(dst.dtype)

    @pl.when(blk == 0)
    def _():
        s_ref[...] = jnp.zeros_like(s_ref)

    for a, (x_ref, halo_ref) in enumerate(((q_ref, hq_ref), (k_ref, hk_ref), (v_ref, hv_ref))):
        for hi in range(hs):
            lanes = slice(hi * HEAD_DIM, (hi + 1) * HEAD_DIM)
            ext_ref[a * hs + hi, 0:HALO_CONV, :] = jnp.where(blk > 0, halo_ref[:, lanes], 0.0)
            ext_ref[a * hs + hi, HALO_CONV:, :] = x_ref[:, lanes]

    def conv_silu(a, cw_ref, p):
        ci, hi = p
        lanes = slice(hi * HEAD_DIM, (hi + 1) * HEAD_DIM)
        acc = None
        for j in range(CONV_W):
            term = (cw_ref[j:j + 1, lanes]
                    * ext_ref[a * hs + hi, pl.ds(HALO_CONV - (CONV_W - 1) + j + ci * c, c), :])
            acc = term if acc is None else acc + term
        return _silu(acc)

    row = lax.broadcasted_iota(jnp.int32, (c, c), 0)
    col = lax.broadcasted_iota(jnp.int32, (c, c), 1)
    tril = row >= col
    strict = row > col
    tril_f = jnp.where(tril, 1.0, 0.0)
    gnw = gnw_ref[...]
    heads = range(hs)

    n_chunks = rows // c
    probs = [(ci, hi) for ci in range(n_chunks) for hi in heads]
    n_p = range(len(probs))
    gates = [gates_ref[ci * c:(ci + 1) * c, :] for ci in range(n_chunks)]
    gcum = [jnp.dot(tril_f, g, preferred_element_type=F32, precision=lax.Precision.HIGHEST) for g in gates]
    qc = [_l2norm(conv_silu(0, cwq_ref, p), HEAD_DIM ** -0.5) for p in probs]
    kc = [_l2norm(conv_silu(1, cwk_ref, p)) for p in probs]
    vc = [conv_silu(2, cwv_ref, p) for p in probs]
    gc = [_lane_column(gcum[ci], hg * hs + hi) for ci, hi in probs]
    bc = [_lane_column(gates[ci], hg * hs + hi + n_heads) for ci, hi in probs]
    decay = [jnp.exp(jnp.where(tril, g - g.T, MASKED)) for g in gc]
    egc = [jnp.exp(g) for g in gc]
    g_last = [g[c - 1:c, :] for g in gc]
    kb = [k.astype(BF16) for k in kc]
    kk = [_dot_nt(k, k) for k in kb]
    qk = [_dot_nt(q, k) for q, k in zip(qc, kb)]
    lm = [jnp.where(strict, kk[p] * bc[p] * decay[p], 0.0) for p in n_p]
    row_sums = [jnp.sum(jnp.abs(x), axis=1, keepdims=True) for x in lm]
    worst = jnp.max(functools.reduce(jnp.maximum, row_sums), axis=0, keepdims=True)
    norm_ref[...] = jnp.broadcast_to(worst.reshape(1, 1, 1), norm_ref.shape)
    t_inv = _unit_lower_inverses(lm, precise)
    uw = [_dot(t_inv[p], jnp.concatenate([vc[p] * bc[p], kc[p] * (bc[p] * egc[p])], axis=1)) for p in n_p]
    intra = [qk[p] * decay[p] for p in n_p]
    kdt = [(kc[p] * jnp.exp(g_last[p] - gc[p])).T for p in n_p]
    wq = [jnp.concatenate([uw[p][:, HEAD_DIM:], qc[p] * egc[p]], axis=0) for p in n_p]
    ik = [jnp.concatenate([intra[p], kdt[p]], axis=0) for p in n_p]

    s = [s_ref[hi] for hi in heads]
    for ci in range(n_chunks):
        ps = [ci * hs + hi for hi in heads]
        r1 = [_dot(wq[p], s[hi]) for hi, p in zip(heads, ps)]
        v_new = [uw[p][:, :HEAD_DIM] - r1[hi][:c] for hi, p in zip(heads, ps)]
        r2 = [_dot(ik[p], v_new[hi]) for hi, p in zip(heads, ps)]
        s = [s[hi] * jnp.exp(g_last[p]) + r2[hi][c:] for hi, p in zip(heads, ps)]
        for hi in heads:
            lanes = slice(hi * HEAD_DIM, (hi + 1) * HEAD_DIM)
            o = _rms(r1[hi][c:] + r2[hi][:c], gnw) * _silu(gate_ref[ci * c:(ci + 1) * c, lanes])
            o_ref[ci * c:(ci + 1) * c, lanes] = o.astype(o_ref.dtype)

    for hi in range(hs):
        s_ref[hi] = s[hi]

    @pl.when(blk == pl.num_programs(2) - 1)
    def _():
        for hi in range(hs):
            sout_ref[0, hi] = s[hi]


def _gdn_prompt(proj, gates, conv_w, gnw, to_cast, batch, seq, n_heads, hs, rows, precise):
    nblk = seq // rows
    hb = rows // HALO_CONV
    ng = n_heads // hs
    width = hs * HEAD_DIM
    steps = batch * ng * nblk

    def slab(w):
        return pl.BlockSpec((w.shape[0] // steps, w.shape[1]), lambda b, h, k: ((b * ng + h) * nblk + k, 0))

    def main(off):
        return pl.BlockSpec((rows, width), lambda b, h, k: (b * nblk + k, off * ng + h))

    def halo(off):
        return pl.BlockSpec(
            (HALO_CONV, width), lambda b, h, k: (jnp.maximum((b * nblk + k) * hb - 1, 0), off * ng + h))

    def cw(off):
        return pl.BlockSpec((CONV_W, width), lambda b, h, k: (0, off * ng + h))

    kern = functools.partial(_gdn_prompt_kernel, n_heads=n_heads, hs=hs, rows=rows, n_cast=len(to_cast),
                             precise=precise)
    return pl.pallas_call(
        kern,
        grid=(batch, ng, nblk),
        in_specs=[main(0), main(1), main(2), main(3), halo(0), halo(1), halo(2),
                  pl.BlockSpec((rows, HEAD_DIM), lambda b, h, k: (b * nblk + k, 0)),
                  cw(0), cw(1), cw(2),
                  pl.BlockSpec((1, HEAD_DIM), lambda b, h, k: (0, 0)),
                  *[slab(w) for w in to_cast]],
        out_specs=[
            pl.BlockSpec((rows, width), lambda b, h, k: (b * nblk + k, h)),
            pl.BlockSpec((1, hs, HEAD_DIM, HEAD_DIM), lambda b, h, k: (b, h, 0, 0)),
            pl.BlockSpec((1, SUBLANES, HEAD_DIM), lambda b, h, k: ((b * ng + h) * nblk + k, 0, 0)),
            *[slab(w) for w in to_cast],
        ],
        out_shape=[
            jax.ShapeDtypeStruct((batch * seq, n_heads * HEAD_DIM), BF16),
            jax.ShapeDtypeStruct((batch, n_heads, HEAD_DIM, HEAD_DIM), F32),
            jax.ShapeDtypeStruct((steps, SUBLANES, HEAD_DIM), F32),
            *[jax.ShapeDtypeStruct(w.shape, BF16) for w in to_cast],
        ],
        scratch_shapes=[pltpu.VMEM((hs, HEAD_DIM, HEAD_DIM), F32),
                        pltpu.VMEM((3 * hs, rows + HALO_CONV, HEAD_DIM), F32)],
        compiler_params=_cparams(3),
        name="gdn_prompt",
    )(proj, proj, proj, proj, proj, proj, proj, gates, conv_w, conv_w, conv_w, gnw, *to_cast)


def _gdn_token_steps(x_ref, cs_ref, gates_ref, cw_ref, gnw_ref, s_in_ref, o_ref, s_out_ref, cs_out_ref, n_heads):
    d = n_heads * HEAD_DIM
    sq = (HEAD_DIM, HEAD_DIM)
    x = x_ref[...]
    cw = cw_ref[...]
    conv = cw[CONV_W - 1:CONV_W, :] * x[:, :3 * d]
    for j in range(CONV_W - 1):
        conv = conv + cw[j:j + 1, :] * cs_ref[j]
    qkv = _silu(conv)
    for j in range(CONV_W - 2):
        cs_out_ref[j] = cs_ref[j + 1]
    cs_out_ref[CONV_W - 2] = x[:, :3 * d]

    gates = gates_ref[...]
    eg = jnp.exp(gates)
    gnw = gnw_ref[...]
    heads = range(n_heads)
    qn = [_l2norm(qkv[:, h * HEAD_DIM:(h + 1) * HEAD_DIM], HEAD_DIM ** -0.5) for h in heads]
    kn = [_l2norm(qkv[:, d + h * HEAD_DIM:d + (h + 1) * HEAD_DIM]) for h in heads]
    for t in range(x.shape[0]):
        row = slice(t, t + 1)
        v = [qkv[row, 2 * d + h * HEAD_DIM:2 * d + (h + 1) * HEAD_DIM] for h in heads]
        kb = [jnp.broadcast_to(kn[h][row], sq).T for h in heads]
        qb = [jnp.broadcast_to(qn[h][row], sq).T for h in heads]
        s1 = [s_in_ref[t, h] * eg[row, h:h + 1] for h in heads]
        ks = [jnp.sum(s1[h] * kb[h], axis=0, keepdims=True) for h in heads]
        delta = [(v[h] - ks[h]) * gates[row, n_heads + h:n_heads + h + 1] for h in heads]
        s2 = [s1[h] + kb[h] * delta[h] for h in heads]
        for h in heads:
            lanes = slice(h * HEAD_DIM, (h + 1) * HEAD_DIM)
            s_out_ref[t, h] = s2[h]
            o = jnp.sum(s2[h] * qb[h], axis=0, keepdims=True)
            o_ref[row, lanes] = _rms(o, gnw) * _silu(x[row, 3 * d + h * HEAD_DIM:3 * d + (h + 1) * HEAD_DIM])


def _mix_out_kernel(oa_ref, u_ref, hist_ref, x_ref, wp_ref, ps_ref, wo_ref, nw_ref, *rest,
                    tm, seq, pos0, n_heads):
    if n_heads:
        tok_in, out_ref, tok_out = rest[:6], rest[6], rest[7:10]
        *rest, ring_ref, sem = rest[10:]
        state_hbm = tok_in[5]
        step, n_steps, nt = pl.program_id(0), pl.num_programs(0), ring_ref.shape[1]

        def state_copy(s):
            slot = s % TOKEN_RING
            return pltpu.make_async_copy(state_hbm.at[pl.ds(s * nt, nt)], ring_ref.at[slot], sem.at[slot])

        @pl.when(step == 0)
        def _():
            for s in range(TOKEN_RING - 1):
                state_copy(s).start()

        @pl.when(step + (TOKEN_RING - 1) < n_steps)
        def _():
            state_copy(step + (TOKEN_RING - 1)).start()

        state_copy(step).wait()
        _gdn_token_steps(*tok_in[:5], ring_ref.at[step % TOKEN_RING], *tok_out, n_heads)
    else:
        out_ref, rest = rest[0], rest[1:]
    d_pool = u_ref.shape[1]
    d_gdn = oa_ref.shape[1]
    d = out_ref.shape[1]
    n_groups = len(POOL_WINDOWS)
    gd = d_pool // n_groups
    strips = gd // HEAD_DIM
    if seq > 1:
        (ext_ref,) = rest
        start = (pl.program_id(0) * tm) % seq
        for si in range(d_pool // HEAD_DIM):
            lanes = slice(si * HEAD_DIM, (si + 1) * HEAD_DIM)
            ext_ref[si, 0:HALO_POOL, :] = jnp.where(start > 0, hist_ref[:, lanes], 0.0)
            ext_ref[si, HALO_POOL:, :] = u_ref[:, lanes]
        pos = pos0 + start + lax.broadcasted_iota(jnp.int32, (tm, HEAD_DIM), 0)
    else:
        (hist_out_ref,) = rest
        for r in range(POOL_BUF - 1):
            hist_out_ref[r] = hist_ref[r + 1]
        hist_out_ref[POOL_BUF - 1] = u_ref[...]
        pos = jnp.full((tm, HEAD_DIM), pos0, jnp.int32)

    tn = d // n_groups
    oa = oa_ref[...].astype(BF16)
    parts = []
    for gi, win in enumerate(POOL_WINDOWS):
        cols = slice(gi * tn, (gi + 1) * tn)
        out_ref[:, cols] = jnp.dot(oa, wo_ref[:d_gdn, cols], preferred_element_type=F32)
        cnt = jnp.minimum(pos + 1, win).astype(F32)
        pooled = []
        for si in range(gi * strips, (gi + 1) * strips):
            lanes = slice(si * HEAD_DIM, (si + 1) * HEAD_DIM)
            cur = u_ref[:, lanes]
            wsum = cur
            for r in range(1, win):
                if seq > 1:
                    wsum = wsum + ext_ref[si, pl.ds(HALO_POOL - r, tm), :]
                else:
                    wsum = wsum + hist_ref[POOL_BUF - r, :, lanes]
            pooled.append(wsum / cnt - cur)
        ob = _dot(jnp.concatenate(pooled, axis=1), wp_ref[gi]) * ps_ref[:, gi * gd:(gi + 1) * gd]
        parts.append(ob.astype(BF16))
    ob_all = jnp.concatenate(parts, axis=1)
    mix = [out_ref[:, ci * tn:(ci + 1) * tn]
           + jnp.dot(ob_all, wo_ref[d_gdn:, ci * tn:(ci + 1) * tn], preferred_element_type=F32)
           for ci in range(n_groups)]
    ss = sum(jnp.sum(m * m, axis=-1, keepdims=True) for m in mix)
    inv = lax.rsqrt(ss * (1.0 / d) + EPS)
    for ci in range(n_groups):
        cols = slice(ci * tn, (ci + 1) * tn)
        out_ref[:, cols] = x_ref[:, cols] + mix[ci] * inv * nw_ref[:, cols]


def _mix_out(o_a, u, hist, x, w_pool, pool_scale, w_out, norm_w, tm, seq, pos0, token_job=None):
    m, d = x.shape
    d_pool = pool_scale.shape[1]
    d_gdn = o_a.shape[1]
    out_specs = [pl.BlockSpec((tm, d), lambda i: (i, 0))]
    out_shape = [jax.ShapeDtypeStruct((m, d), F32)]
    tok_specs, tok_operands, n_heads = [], [], 0
    if token_job is not None:
        proj, conv_state, gates, conv_w, gnw, state, n_heads = token_job
        tokens, dg = proj.shape[0], n_heads * HEAD_DIM
        nt = tokens // (m // tm)
        assert seq > 1 and nt * (m // tm) == tokens and nt % SUBLANES == 0
        cs_spec = pl.BlockSpec((CONV_W - 1, nt, 3 * dg), lambda i: (0, i, 0))
        st_spec = pl.BlockSpec((nt, n_heads, HEAD_DIM, HEAD_DIM), lambda i: (i, 0, 0, 0))
        tok_specs = [
            pl.BlockSpec((nt, 4 * dg), lambda i: (i, 0)),
            cs_spec,
            pl.BlockSpec((nt, HEAD_DIM), lambda i: (i, 0)),
            pl.BlockSpec((CONV_W, 3 * dg), lambda i: (0, 0)),
            pl.BlockSpec((1, HEAD_DIM), lambda i: (0, 0)),
            pl.BlockSpec(memory_space=pl.ANY),
        ]
        tok_operands = [proj, conv_state, gates, conv_w, gnw, state]
        out_specs += [pl.BlockSpec((nt, dg), lambda i: (i, 0)), st_spec, cs_spec]
        out_shape += [jax.ShapeDtypeStruct((tokens, dg), F32), jax.ShapeDtypeStruct(state.shape, F32),
                      jax.ShapeDtypeStruct(conv_state.shape, F32)]
    if seq > 1:
        hb = tm // HALO_POOL
        hist_spec = pl.BlockSpec((HALO_POOL, d_pool), lambda i: (jnp.maximum(i * hb - 1, 0), 0))
        scratch = [pltpu.VMEM((d_pool // HEAD_DIM, tm + HALO_POOL, HEAD_DIM), F32)]
    else:
        hist_spec = pl.BlockSpec((POOL_BUF, tm, d_pool), lambda i: (0, i, 0))
        out_specs.append(hist_spec)
        out_shape.append(jax.ShapeDtypeStruct(hist.shape, F32))
        scratch = []
    if token_job is not None:
        assert m // tm >= TOKEN_RING - 1
        scratch += [pltpu.VMEM((TOKEN_RING, nt, n_heads, HEAD_DIM, HEAD_DIM), F32),
                    pltpu.SemaphoreType.DMA((TOKEN_RING,))]
    kern = functools.partial(_mix_out_kernel, tm=tm, seq=seq, pos0=pos0, n_heads=n_heads)
    return pl.pallas_call(
        kern,
        grid=(m // tm,),
        in_specs=[
            pl.BlockSpec((tm, d_gdn), lambda i: (i, 0)),
            pl.BlockSpec((tm, d_pool), lambda i: (i, 0)),
            hist_spec,
            pl.BlockSpec((tm, d), lambda i: (i, 0)),
            pl.BlockSpec(w_pool.shape, lambda i: (0, 0, 0), pipeline_mode=pl.Buffered(1)),
            pl.BlockSpec((1, d_pool), lambda i: (0, 0)),
            pl.BlockSpec(w_out.shape, lambda i: (0, 0), pipeline_mode=pl.Buffered(1)),
            pl.BlockSpec((1, d), lambda i: (0, 0)),
            *tok_specs,
        ],
        out_specs=out_specs,
        out_shape=out_shape,
        scratch_shapes=scratch,
        compiler_params=_cparams(1, VMEM_LIMIT if token_job is None else MLP_VMEM_LIMIT),
        name="mix_out",
    )(o_a, u, hist, x, w_pool, pool_scale, w_out, norm_w, *tok_operands)


def _mlp_kernel(x_ref, nw1_ref, wu_ref, wd_ref, nw2_ref, xe_ref, out_ref, oute_ref, h_ref, he_ref, *, n_ff_tiles):
    i = pl.program_id(0)
    j = pl.program_id(1)

    def rows_step(first, last, x_ref, h_ref, out_ref):
        if first:
            h = _rms(x_ref[...], nw1_ref[...]).astype(BF16)
            h_ref[...] = h
        else:
            h = h_ref[...]
        up = jnp.dot(h, wu_ref[...], preferred_element_type=F32)
        act = jnp.square(jnp.maximum(up, 0.0)).astype(BF16)
        part = jnp.dot(act, wd_ref[...], preferred_element_type=F32)
        acc = part if first else out_ref[...] + part
        out_ref[...] = x_ref[...] + _rms(acc, nw2_ref[...]) if last else acc

    def step(first, last):
        rows_step(first, last, x_ref, h_ref, out_ref)
        pl.when(i == 0)(lambda: rows_step(first, last, xe_ref, he_ref, oute_ref))

    nj = n_ff_tiles
    if nj == 1:
        step(True, True)
    else:
        pl.when(j == 0)(lambda: step(True, False))
        pl.when((j > 0) & (j < nj - 1))(lambda: step(False, False))
        pl.when(j == nj - 1)(lambda: step(False, True))


def _mlp(x, nw1, w_up, w_down, nw2, tm, tf, x_extra):
    m, d = x.shape
    me = x_extra.shape[0]
    f = w_up.shape[1]
    whole = lambda i, j: (0, 0)
    out, out_extra = pl.pallas_call(
        functools.partial(_mlp_kernel, n_ff_tiles=f // tf),
        grid=(m // tm, f // tf),
        in_specs=[
            pl.BlockSpec((tm, d), lambda i, j: (i, 0)),
            pl.BlockSpec((1, d), whole),
            pl.BlockSpec((d, tf), lambda i, j: (0, j)),
            pl.BlockSpec((tf, d), lambda i, j: (j, 0)),
            pl.BlockSpec((1, d), whole),
            pl.BlockSpec((me, d), whole),
        ],
        out_specs=[pl.BlockSpec((tm, d), lambda i, j: (i, 0)), pl.BlockSpec((me, d), whole)],
        out_shape=[jax.ShapeDtypeStruct((m, d), F32), jax.ShapeDtypeStruct((me, d), F32)],
        scratch_shapes=[pltpu.VMEM((tm, d), BF16), pltpu.VMEM((me, d), BF16)],
        compiler_params=_cparams(2, MLP_VMEM_LIMIT),
        name="mlp",
    )(x, nw1, w_up, w_down, nw2, x_extra)
    return out, out_extra


def _pad_lanes(v, n=HEAD_DIM):
    return jnp.pad(v.reshape(1, -1), ((0, 0), (0, n - v.shape[-1])))


def kernel(x_prompt, x_sample, state_gdn, state_conv, state_pool, norm_pre_mix, w_in, conv_w, a_log, dt_bias,
           norm_gdn_out, w_pool, pool_scale, w_out, norm_post_mix, norm_pre_mlp, w_up, w_down, norm_post_mlp):
    batch, seq, d_model = x_prompt.shape
    dec_batch, dec_seq, _ = x_sample.shape
    assert dec_seq == 1
    depth = w_in.shape[0]
    n_heads = a_log.shape[1]
    d_gdn = n_heads * HEAD_DIM
    d_qkv = 3 * d_gdn
    d_pool = pool_scale.shape[1]
    o_gate = d_qkv + d_gdn

    yp = x_prompt.reshape(batch * seq, d_model)
    ys = x_sample.reshape(dec_batch, d_model)
    outs = [[] for _ in range(6)]
    for l in range(depth):
        alog = _pad_lanes(a_log[l])
        dtb = _pad_lanes(dt_bias[l])
        gnw = norm_gdn_out[l].reshape(1, HEAD_DIM)
        npm = norm_pre_mix[l].reshape(1, d_model)
        npo = norm_post_mix[l].reshape(1, d_model)
        nm1 = norm_pre_mlp[l].reshape(1, d_model)
        nm2 = norm_post_mlp[l].reshape(1, d_model)
        ps = pool_scale[l].reshape(1, d_pool)

        proj_s, u_s, gates_s, *w_in_b = _in_proj(
            ys, npm, jnp.swapaxes(w_in[l], 0, 1), o_gate, d_pool, alog, dtb, n_heads,
            tm=dec_batch, tn=SAMPLE_TN, emit=True)

        proj_p, u_p, gates_p = _in_proj(
            yp, npm, w_in_b, o_gate, d_pool, alog, dtb, n_heads, tm=IN_PROJ_TM, tn=IN_PROJ_TN, emit=False)
        to_cast = (w_up[l], w_down[l], w_out[l], w_pool[l].reshape(-1, w_pool.shape[-1]))
        gdn = functools.partial(_gdn_prompt, proj_p, gates_p, conv_w[l], gnw, batch=batch, seq=seq,
                                n_heads=n_heads, hs=GDN_HEADS_PER_STEP, rows=GDN_ROWS)
        oa_p, s_p, norm_p, w_up_b, w_down_b, w_out_b, w_pool_b = gdn(to_cast, precise=False)
        oa_p, s_p = lax.cond(jnp.max(norm_p) <= INV_FAST_NORM, lambda: (oa_p, s_p),
                             lambda: tuple(gdn((), precise=True)[:2]))
        w_pool_b = w_pool_b.reshape(w_pool.shape[1:])
        token_job = (proj_s, jnp.swapaxes(state_conv[l], 0, 1), gates_s, conv_w[l], gnw, state_gdn[l], n_heads)
        x1_p, oa_s, s_s, cs_new = _mix_out(oa_p, u_p, u_p, yp, w_pool_b, ps, w_out_b, npo, tm=MIX_TM, seq=seq,
                                           pos0=0, token_job=token_job)
        x1_s, hist_new = _mix_out(oa_s, u_s, jnp.swapaxes(state_pool[l], 0, 1), ys, w_pool_b, ps, w_out_b, npo,
                                  tm=dec_batch, seq=1, pos0=PAST_LEN)
        yp, ys = _mlp(x1_p, nm1, w_up_b, w_down_b, nm2, MLP_TM, MLP_TF, x1_s)
        outs[0].append(s_p)
        outs[1].append(proj_p.reshape(batch, seq, -1)[:, seq - (CONV_W - 1):, :d_qkv])
        outs[2].append(u_p.reshape(batch, seq, -1)[:, seq - POOL_BUF:])
        outs[3].append(s_s)
        outs[4].append(jnp.swapaxes(cs_new, 0, 1))
        outs[5].append(jnp.swapaxes(hist_new, 0, 1))

    return (yp.reshape(batch, seq, d_model), ys.reshape(dec_batch, dec_seq, d_model),
            *[jnp.stack(o) for o in outs])
```

```python
import functools

import jax
import jax.numpy as jnp
from jax import lax
from jax.experimental import pallas as pl
from jax.experimental.pallas import tpu as pltpu

F32 = jnp.float32
BF16 = jnp.bfloat16

EPS = 1e-6
HEAD_DIM = 128
CONV_W = 4
POOL_WINDOWS = (2, 4, 8, 16)
POOL_BUF = max(POOL_WINDOWS) - 1
PAST_LEN = 16384
HALO_CONV = 8
HALO_POOL = 16
GDN_CHUNK = 128
INV_BASE = 16
INV_FAST_NORM = 0.75
SUBLANES = 8
MASKED = -1e30
VMEM_LIMIT = 56 * 1024 * 1024
MLP_VMEM_LIMIT = 60 * 1024 * 1024

IN_PROJ_TM, IN_PROJ_TN, SAMPLE_TN = 1024, 1024, 1024
GDN_HEADS_PER_STEP, GDN_ROWS = 8, 256
MIX_TM = 512
TOKEN_RING = 3
MLP_TM, MLP_TF = 512, 2048


def _cparams(n_axes, vmem_limit=VMEM_LIMIT):
    return pltpu.CompilerParams(
        dimension_semantics=("arbitrary",) * n_axes, vmem_limit_bytes=vmem_limit)


def _dot(a, b):
    return jnp.dot(a.astype(BF16), b.astype(BF16), preferred_element_type=F32)


def _dot_nt(a, b):
    return lax.dot_general(a.astype(BF16), b.astype(BF16), (((1,), (1,)), ((), ())),
                           preferred_element_type=F32)


def _rms(x, w):
    return x * lax.rsqrt(jnp.mean(x * x, axis=-1, keepdims=True) + EPS) * w


def _silu(x):
    h = 0.5 * x
    return h + h * jnp.tanh(h)


def _softplus(x):
    return jnp.maximum(x, 0.0) + jnp.log(1.0 + jnp.exp(-jnp.abs(x)))


def _l2norm(x, scale=1.0):
    return x * (lax.rsqrt(jnp.sum(x * x, axis=-1, keepdims=True) + EPS) * scale)


def _inproj_kernel(x_ref, nw_ref, wa_ref, wt_ref, wc_ref, alog_ref, dtb_ref, *rest, n_heads, n_main, emit):
    if emit:
        out_ref, u_ref, gates_ref, wa_out, wt_out, wc_out, h_ref = rest
    else:
        out_ref, u_ref, gates_ref, h_ref = rest
    j = pl.program_id(1)

    def main_step(first):
        if first:
            h = _rms(x_ref[...], nw_ref[...]).astype(BF16)
            h_ref[...] = h
        else:
            h = h_ref[...]
        w = wa_ref[...].T.astype(BF16) if emit else wa_ref[...]
        if emit:
            wa_out[...] = w
        out_ref[...] = jnp.dot(h, w, preferred_element_type=F32)

    pl.when(j == 0)(lambda: main_step(True))
    pl.when((j > 0) & (j < n_main))(lambda: main_step(False))

    @pl.when(j == n_main)
    def _():
        if emit:
            n_gate = 2 * n_heads
            blk = wt_ref[...]
            pool_rows = jnp.concatenate([blk[n_gate:], wc_ref[...]], axis=0)
            gate_rows = jnp.concatenate([blk[:n_gate], jnp.zeros((HEAD_DIM - n_gate, blk.shape[1]), F32)], axis=0)
            wt = pool_rows.T.astype(BF16)
            wc = gate_rows.T.astype(BF16)
            wt_out[...] = wt
            wc_out[...] = wc
        else:
            wt = wt_ref[...]
            wc = wc_ref[...]
        h = h_ref[...]
        u_ref[...] = jnp.dot(h, wt, preferred_element_type=F32)
        ab = jnp.dot(h, wc, preferred_element_type=F32)
        lane = lax.broadcasted_iota(jnp.int32, ab.shape, 1)
        g = -jnp.exp(alog_ref[...]) * _softplus(ab + dtb_ref[...])
        gates_ref[...] = jnp.where(lane < n_heads, g, jax.nn.sigmoid(ab))


def _in_proj(x, norm_w, weights, n, d_pool, alog, dtb, n_heads, tm, tn, emit):
    m, d = x.shape
    n_main = n // tn
    n_gate = 2 * n_heads
    vec = pl.BlockSpec((1, HEAD_DIM), lambda i, j: (0, 0))
    out_specs = [
        pl.BlockSpec((tm, tn), lambda i, j: (i, jnp.minimum(j, n_main - 1))),
        pl.BlockSpec((tm, d_pool), lambda i, j: (i, 0)),
        pl.BlockSpec((tm, HEAD_DIM), lambda i, j: (i, 0)),
    ]
    out_shape = [jax.ShapeDtypeStruct((m, n), F32), jax.ShapeDtypeStruct((m, d_pool), F32),
                 jax.ShapeDtypeStruct((m, HEAD_DIM), F32)]
    wa_bf_spec = pl.BlockSpec((d, tn), lambda i, j: (0, jnp.minimum(j, n_main - 1)))
    wt_bf_spec = pl.BlockSpec((d, d_pool), lambda i, j: (0, 0), pipeline_mode=pl.Buffered(1))
    wc_bf_spec = pl.BlockSpec((d, HEAD_DIM), lambda i, j: (0, 0))
    if emit:
        assert n % d_pool == 0 and (n + d_pool) % n_gate == 0 and n_gate % SUBLANES == 0
        operands = (weights, weights, weights)
        w_specs = [
            pl.BlockSpec((tn, d), lambda i, j: (jnp.minimum(j, n_main - 1), 0)),
            pl.BlockSpec((d_pool, d), lambda i, j: (n // d_pool, 0), pipeline_mode=pl.Buffered(1)),
            pl.BlockSpec((n_gate, d), lambda i, j: ((n + d_pool) // n_gate, 0)),
        ]
        out_specs += [wa_bf_spec, pl.BlockSpec((d, d_pool), lambda i, j: (0, 0)), wc_bf_spec]
        out_shape += [jax.ShapeDtypeStruct((d, n), BF16), jax.ShapeDtypeStruct((d, d_pool), BF16),
                      jax.ShapeDtypeStruct((d, HEAD_DIM), BF16)]
    else:
        operands = weights
        w_specs = [wa_bf_spec, wt_bf_spec, wc_bf_spec]
    return pl.pallas_call(
        functools.partial(_inproj_kernel, n_heads=n_heads, n_main=n_main, emit=emit),
        grid=(m // tm, n_main + 1),
        in_specs=[
            pl.BlockSpec((tm, d), lambda i, j: (i, 0)),
            pl.BlockSpec((1, d), lambda i, j: (0, 0)),
            *w_specs,
            vec, vec,
        ],
        out_specs=out_specs,
        out_shape=out_shape,
        scratch_shapes=[pltpu.VMEM((tm, d), BF16)],
        compiler_params=_cparams(2),
        name="in_proj",
    )(x, norm_w, *operands, alog, dtb)


def _lane_column(x, lane_idx):
    lane = lax.broadcasted_iota(jnp.int32, x.shape, 1)
    col = jnp.sum(jnp.where(lane == lane_idx, x, 0.0), axis=1, keepdims=True)
    return jnp.broadcast_to(col, x.shape)


def _unit_lower_inverses(lms, precise):
    n = lms[0].shape[0]
    row = lax.broadcasted_iota(jnp.int32, (n, n), 0)
    col = lax.broadcasted_iota(jnp.int32, (n, n), 1)
    eye = jnp.where(row == col, 1.0, 0.0)

    def same_block(size):
        shift = size.bit_length() - 1
        return lax.shift_right_logical(row, shift) == lax.shift_right_logical(col, shift)

    if precise:
        op = lambda x: x
        mm = functools.partial(jnp.dot, preferred_element_type=F32, precision=lax.Precision.HIGHEST)
        terms = INV_BASE
        in_base = same_block(INV_BASE)
        lbs = [jnp.where(in_base, lm, 0.0) for lm in lms]
    else:
        op = lambda x: x.astype(BF16)
        mm = functools.partial(jnp.dot, preferred_element_type=F32)
        terms = n // 2
        assert INV_FAST_NORM ** terms / (1.0 - INV_FAST_NORM) < 2.0 ** -24
        lbs = lms

    qs = [eye - lb for lb in lbs]
    lbs = [op(lb) for lb in lbs]
    nks = [mm(lb, lb) for lb in lbs]
    power = 2
    while 2 * power < terms:
        nbs = [op(nk) for nk in nks]
        res = [mm(nb, jnp.concatenate([nb, op(q)], axis=1)) for nb, q in zip(nbs, qs)]
        nks = [r[:, :n] for r in res]
        qs = [q + r[:, n:] for q, r in zip(qs, res)]
        power *= 2
    ps = [q + mm(op(nk), op(q)) for nk, q in zip(nks, qs)]
    if not precise:
        return ps
    size = INV_BASE
    while size < n:
        off = jnp.logical_and(same_block(2 * size), jnp.logical_not(same_block(size)))
        cs = [op(jnp.where(off, lm, 0.0)) for lm in lms]
        pbs = [op(p) for p in ps]
        ts = [op(mm(c, pb)) for c, pb in zip(cs, pbs)]
        ps = [p - mm(pb, t) for p, pb, t in zip(ps, pbs, ts)]
        size *= 2
    return ps


def _gdn_prompt_kernel(q_ref, k_ref, v_ref, gate_ref, hq_ref, hk_ref, hv_ref, gates_ref,
                       cwq_ref, cwk_ref, cwv_ref, gnw_ref, *rest, n_heads, hs, rows, n_cast, precise):
    cast_in = rest[:n_cast]
    o_ref, sout_ref, norm_ref = rest[n_cast:n_cast + 3]
    cast_out = rest[n_cast + 3:2 * n_cast + 3]
    s_ref, ext_ref = rest[2 * n_cast + 3:]
    hg = pl.program_id(1)
    blk = pl.program_id(2)
    c = GDN_CHUNK

    for src, dst in zip(cast_in, cast_out):
        dst[...] = src[...].astype(dst.dtype)

    @pl.when(blk == 0)
    def _():
        s_ref[...] = jnp.zeros_like(s_ref)

    for a, (x_ref, halo_ref) in enumerate(((q_ref, hq_ref), (k_ref, hk_ref), (v_ref, hv_ref))):
        for hi in range(hs):
            lanes = slice(hi * HEAD_DIM, (hi + 1) * HEAD_DIM)
            ext_ref[a * hs + hi, 0:HALO_CONV, :] = jnp.where(blk > 0, halo_ref[:, lanes], 0.0)
            ext_ref[a * hs + hi, HALO_CONV:, :] = x_ref[:, lanes]

    def conv_silu(a, cw_ref, p):
        ci, hi = p
        lanes = slice(hi * HEAD_DIM, (hi + 1) * HEAD_DIM)
        acc = None
        for j in range(CONV_W):
            term = (cw_ref[j:j + 1, lanes]
                    * ext_ref[a * hs + hi, pl.ds(HALO_CONV - (CONV_W - 1) + j + ci * c, c), :])
            acc = term if acc is None else acc + term
        return _silu(acc)

    row = lax.broadcasted_iota(jnp.int32, (c, c), 0)
    col = lax.broadcasted_iota(jnp.int32, (c, c), 1)
    tril = row >= col
    strict = row > col
    tril_f = jnp.where(tril, 1.0, 0.0)
    gnw = gnw_ref[...]
    heads = range(hs)

    n_chunks = rows // c
    probs = [(ci, hi) for ci in range(n_chunks) for hi in heads]
    n_p = range(len(probs))
    gates = [gates_ref[ci * c:(ci + 1) * c, :] for ci in range(n_chunks)]
    gcum = [jnp.dot(tril_f, g, preferred_element_type=F32, precision=lax.Precision.HIGHEST) for g in gates]
    qc = [_l2norm(conv_silu(0, cwq_ref, p), HEAD_DIM ** -0.5) for p in probs]
    kc = [_l2norm(conv_silu(1, cwk_ref, p)) for p in probs]
    vc = [conv_silu(2, cwv_ref, p) for p in probs]
    gc = [_lane_column(gcum[ci], hg * hs + hi) for ci, hi in probs]
    bc = [_lane_column(gates[ci], hg * hs + hi + n_heads) for ci, hi in probs]
    decay = [jnp.exp(jnp.where(tril, g - g.T, MASKED)) for g in gc]
    egc = [jnp.exp(g) for g in gc]
    g_last = [g[c - 1:c, :] for g in gc]
    kb = [k.astype(BF16) for k in kc]
    kk = [_dot_nt(k, k) for k in kb]
    qk = [_dot_nt(q, k) for q, k in zip(qc, kb)]
    lm = [jnp.where(strict, kk[p] * bc[p] * decay[p], 0.0) for p in n_p]
    row_sums = [jnp.sum(jnp.abs(x), axis=1, keepdims=True) for x in lm]
    worst = jnp.max(functools.reduce(jnp.maximum, row_sums), axis=0, keepdims=True)
    norm_ref[...] = jnp.broadcast_to(worst.reshape(1, 1, 1), norm_ref.shape)
    t_inv = _unit_lower_inverses(lm, precise)
    uw = [_dot(t_inv[p], jnp.concatenate([vc[p] * bc[p], kc[p] * (bc[p] * egc[p])], axis=1)) for p in n_p]
    intra = [qk[p] * decay[p] for p in n_p]
    kdt = [(kc[p] * jnp.exp(g_last[p] - gc[p])).T for p in n_p]
    wq = [jnp.concatenate([uw[p][:, HEAD_DIM:], qc[p] * egc[p]], axis=0) for p in n_p]
    ik = [jnp.concatenate([intra[p], kdt[p]], axis=0) for p in n_p]

    s = [s_ref[hi] for hi in heads]
    for ci in range(n_chunks):
        ps = [ci * hs + hi for hi in heads]
        r1 = [_dot(wq[p], s[hi]) for hi, p in zip(heads, ps)]
        v_new = [uw[p][:, :HEAD_DIM] - r1[hi][:c] for hi, p in zip(heads, ps)]
        r2 = [_dot(ik[p], v_new[hi]) for hi, p in zip(heads, ps)]
        s = [s[hi] * jnp.exp(g_last[p]) + r2[hi][c:] for hi, p in zip(heads, ps)]
        for hi in heads:
            lanes = slice(hi * HEAD_DIM, (hi + 1) * HEAD_DIM)
            o = _rms(r1[hi][c:] + r2[hi][:c], gnw) * _silu(gate_ref[ci * c:(ci + 1) * c, lanes])
            o_ref[ci * c:(ci + 1) * c, lanes] = o.astype(o_ref.dtype)

    for hi in range(hs):
        s_ref[hi] = s[hi]

    @pl.when(blk == pl.num_programs(2) - 1)
    def _():
        for hi in range(hs):
            sout_ref[0, hi] = s[hi]


def _gdn_prompt(proj, gates, conv_w, gnw, to_cast, batch, seq, n_heads, hs, rows, precise):
    nblk = seq // rows
    hb = rows // HALO_CONV
    ng = n_heads // hs
    width = hs * HEAD_DIM
    steps = batch * ng * nblk

    def slab(w):
        return pl.BlockSpec((w.shape[0] // steps, w.shape[1]), lambda b, h, k: ((b * ng + h) * nblk + k, 0))

    def main(off):
        return pl.BlockSpec((rows, width), lambda b, h, k: (b * nblk + k, off * ng + h))

    def halo(off):
        return pl.BlockSpec(
            (HALO_CONV, width), lambda b, h, k: (jnp.maximum((b * nblk + k) * hb - 1, 0), off * ng + h))

    def cw(off):
        return pl.BlockSpec((CONV_W, width), lambda b, h, k: (0, off * ng + h))

    kern = functools.partial(_gdn_prompt_kernel, n_heads=n_heads, hs=hs, rows=rows, n_cast=len(to_cast),
                             precise=precise)
    return pl.pallas_call(
        kern,
        grid=(batch, ng, nblk),
        in_specs=[main(0), main(1), main(2), main(3), halo(0), halo(1), halo(2),
                  pl.BlockSpec((rows, HEAD_DIM), lambda b, h, k: (b * nblk + k, 0)),
                  cw(0), cw(1), cw(2),
                  pl.BlockSpec((1, HEAD_DIM), lambda b, h, k: (0, 0)),
                  *[slab(w) for w in to_cast]],
        out_specs=[
            pl.BlockSpec((rows, width), lambda b, h, k: (b * nblk + k, h)),
            pl.BlockSpec((1, hs, HEAD_DIM, HEAD_DIM), lambda b, h, k: (b, h, 0, 0)),
            pl.BlockSpec((1, SUBLANES, HEAD_DIM), lambda b, h, k: ((b * ng + h) * nblk + k, 0, 0)),
            *[slab(w) for w in to_cast],
        ],
        out_shape=[
            jax.ShapeDtypeStruct((batch * seq, n_heads * HEAD_DIM), BF16),
            jax.ShapeDtypeStruct((batch, n_heads, HEAD_DIM, HEAD_DIM), F32),
            jax.ShapeDtypeStruct((steps, SUBLANES, HEAD_DIM), F32),
            *[jax.ShapeDtypeStruct(w.shape, BF16) for w in to_cast],
        ],
        scratch_shapes=[pltpu.VMEM((hs, HEAD_DIM, HEAD_DIM), F32),
                        pltpu.VMEM((3 * hs, rows + HALO_CONV, HEAD_DIM), F32)],
        compiler_params=_cparams(3),
        name="gdn_prompt",
    )(proj, proj, proj, proj, proj, proj, proj, gates, conv_w, conv_w, conv_w, gnw, *to_cast)


def _gdn_token_steps(x_ref, cs_ref, gates_ref, cw_ref, gnw_ref, s_in_ref, o_ref, s_out_ref, cs_out_ref, n_heads):
    d = n_heads * HEAD_DIM
    sq = (HEAD_DIM, HEAD_DIM)
    x = x_ref[...]
    cw = cw_ref[...]
    conv = cw[CONV_W - 1:CONV_W, :] * x[:, :3 * d]
    for j in range(CONV_W - 1):
        conv = conv + cw[j:j + 1, :] * cs_ref[j]
    qkv = _silu(conv)
    for j in range(CONV_W - 2):
        cs_out_ref[j] = cs_ref[j + 1]
    cs_out_ref[CONV_W - 2] = x[:, :3 * d]

    gates = gates_ref[...]
    eg = jnp.exp(gates)
    gnw = gnw_ref[...]
    heads = range(n_heads)
    qn = [_l2norm(qkv[:, h * HEAD_DIM:(h + 1) * HEAD_DIM], HEAD_DIM ** -0.5) for h in heads]
    kn = [_l2norm(qkv[:, d + h * HEAD_DIM:d + (h + 1) * HEAD_DIM]) for h in heads]
    for t in range(x.shape[0]):
        row = slice(t, t + 1)
        v = [qkv[row, 2 * d + h * HEAD_DIM:2 * d + (h + 1) * HEAD_DIM] for h in heads]
        kb = [jnp.broadcast_to(kn[h][row], sq).T for h in heads]
        qb = [jnp.broadcast_to(qn[h][row], sq).T for h in heads]
        s1 = [s_in_ref[t, h] * eg[row, h:h + 1] for h in heads]
        ks = [jnp.sum(s1[h] * kb[h], axis=0, keepdims=True) for h in heads]
        delta = [(v[h] - ks[h]) * gates[row, n_heads + h:n_heads + h + 1] for h in heads]
        s2 = [s1[h] + kb[h] * delta[h] for h in heads]
        for h in heads:
            lanes = slice(h * HEAD_DIM, (h + 1) * HEAD_DIM)
            s_out_ref[t, h] = s2[h]
            o = jnp.sum(s2[h] * qb[h], axis=0, keepdims=True)
            o_ref[row, lanes] = _rms(o, gnw) * _silu(x[row, 3 * d + h * HEAD_DIM:3 * d + (h + 1) * HEAD_DIM])


def _mix_out_kernel(oa_ref, u_ref, hist_ref, x_ref, wp_ref, ps_ref, wo_ref, nw_ref, *rest,
                    tm, seq, pos0, n_heads):
    if n_heads:
        tok_in, out_ref, tok_out = rest[:6], rest[6], rest[7:10]
        *rest, ring_ref, xring_ref, sem, xsem = rest[10:]
        step, n_steps = pl.program_id(0), pl.num_programs(0)
        streams = ((tok_in[5], ring_ref, sem), (x_ref, xring_ref, xsem))

        def ring_copies(s):
            slot = s % TOKEN_RING
            return [pltpu.make_async_copy(src.at[pl.ds(s * ring.shape[1], ring.shape[1])], ring.at[slot], sm.at[slot])
                    for src, ring, sm in streams]

        @pl.when(step == 0)
        def _():
            for s in range(TOKEN_RING - 1):
                for copy in ring_copies(s):
                    copy.start()

        @pl.when(step + (TOKEN_RING - 1) < n_steps)
        def _():
            for copy in ring_copies(step + (TOKEN_RING - 1)):
                copy.start()

        for copy in ring_copies(step):
            copy.wait()
        x_ref = xring_ref.at[step % TOKEN_RING]
        _gdn_token_steps(*tok_in[:5], ring_ref.at[step % TOKEN_RING], *tok_out, n_heads)
    else:
        out_ref, rest = rest[0], rest[1:]
    d_pool = u_ref.shape[1]
    d_gdn = oa_ref.shape[1]
    d = out_ref.shape[1]
    n_groups = len(POOL_WINDOWS)
    gd = d_pool // n_groups
    strips = gd // HEAD_DIM
    if seq > 1:
        (ext_ref,) = rest
        start = (pl.program_id(0) * tm) % seq
        for si in range(d_pool // HEAD_DIM):
            lanes = slice(si * HEAD_DIM, (si + 1) * HEAD_DIM)
            ext_ref[si, 0:HALO_POOL, :] = jnp.where(start > 0, hist_ref[:, lanes], 0.0)
            ext_ref[si, HALO_POOL:, :] = u_ref[:, lanes]
        pos = pos0 + start + lax.broadcasted_iota(jnp.int32, (tm, HEAD_DIM), 0)
    else:
        (hist_out_ref,) = rest
        for r in range(POOL_BUF - 1):
            hist_out_ref[r] = hist_ref[r + 1]
        hist_out_ref[POOL_BUF - 1] = u_ref[...]
        pos = jnp.full((tm, HEAD_DIM), pos0, jnp.int32)

    tn = d // n_groups
    oa = oa_ref[...].astype(BF16)
    parts = []
    for gi, win in enumerate(POOL_WINDOWS):
        cols = slice(gi * tn, (gi + 1) * tn)
        out_ref[:, cols] = jnp.dot(oa, wo_ref[:d_gdn, cols], preferred_element_type=F32)
        cnt = jnp.minimum(pos + 1, win).astype(F32)
        pooled = []
        for si in range(gi * strips, (gi + 1) * strips):
            lanes = slice(si * HEAD_DIM, (si + 1) * HEAD_DIM)
            cur = u_ref[:, lanes]
            wsum = cur
            for r in range(1, win):
                if seq > 1:
                    wsum = wsum + ext_ref[si, pl.ds(HALO_POOL - r, tm), :]
                else:
                    wsum = wsum + hist_ref[POOL_BUF - r, :, lanes]
            pooled.append(wsum / cnt - cur)
        ob = _dot(jnp.concatenate(pooled, axis=1), wp_ref[gi]) * ps_ref[:, gi * gd:(gi + 1) * gd]
        parts.append(ob.astype(BF16))
    ob_all = jnp.concatenate(parts, axis=1)
    mix = [out_ref[:, ci * tn:(ci + 1) * tn]
           + jnp.dot(ob_all, wo_ref[d_gdn:, ci * tn:(ci + 1) * tn], preferred_element_type=F32)
           for ci in range(n_groups)]
    ss = sum(jnp.sum(m * m, axis=-1, keepdims=True) for m in mix)
    inv = lax.rsqrt(ss * (1.0 / d) + EPS)
    for ci in range(n_groups):
        cols = slice(ci * tn, (ci + 1) * tn)
        out_ref[:, cols] = x_ref[:, cols] + mix[ci] * inv * nw_ref[:, cols]


def _mix_out(o_a, u, hist, x, w_pool, pool_scale, w_out, norm_w, tm, seq, pos0, token_job=None):
    m, d = x.shape
    d_pool = pool_scale.shape[1]
    d_gdn = o_a.shape[1]
    out_specs = [pl.BlockSpec((tm, d), lambda i: (i, 0))]
    out_shape = [jax.ShapeDtypeStruct((m, d), F32)]
    tok_specs, tok_operands, n_heads = [], [], 0
    if token_job is not None:
        proj, conv_state, gates, conv_w, gnw, state, n_heads = token_job
        tokens, dg = proj.shape[0], n_heads * HEAD_DIM
        nt = tokens // (m // tm)
        assert seq > 1 and nt * (m // tm) == tokens and nt % SUBLANES == 0
        cs_spec = pl.BlockSpec((CONV_W - 1, nt, 3 * dg), lambda i: (0, i, 0))
        st_spec = pl.BlockSpec((nt, n_heads, HEAD_DIM, HEAD_DIM), lambda i: (i, 0, 0, 0))
        tok_specs = [
            pl.BlockSpec((nt, 4 * dg), lambda i: (i, 0)),
            cs_spec,
            pl.BlockSpec((nt, HEAD_DIM), lambda i: (i, 0)),
            pl.BlockSpec((CONV_W, 3 * dg), lambda i: (0, 0)),
            pl.BlockSpec((1, HEAD_DIM), lambda i: (0, 0)),
            pl.BlockSpec(memory_space=pl.ANY),
        ]
        tok_operands = [proj, conv_state, gates, conv_w, gnw, state]
        out_specs += [pl.BlockSpec((nt, dg), lambda i: (i, 0)), st_spec, cs_spec]
        out_shape += [jax.ShapeDtypeStruct((tokens, dg), F32), jax.ShapeDtypeStruct(state.shape, F32),
                      jax.ShapeDtypeStruct(conv_state.shape, F32)]
    if seq > 1:
        hb = tm // HALO_POOL
        hist_spec = pl.BlockSpec((HALO_POOL, d_pool), lambda i: (jnp.maximum(i * hb - 1, 0), 0))
        scratch = [pltpu.VMEM((d_pool // HEAD_DIM, tm + HALO_POOL, HEAD_DIM), F32)]
    else:
        hist_spec = pl.BlockSpec((POOL_BUF, tm, d_pool), lambda i: (0, i, 0))
        out_specs.append(hist_spec)
        out_shape.append(jax.ShapeDtypeStruct(hist.shape, F32))
        scratch = []
    if token_job is not None:
        assert m // tm >= TOKEN_RING - 1
        scratch += [pltpu.VMEM((TOKEN_RING, nt, n_heads, HEAD_DIM, HEAD_DIM), F32),
                    pltpu.VMEM((TOKEN_RING, tm, d), F32),
                    pltpu.SemaphoreType.DMA((TOKEN_RING,)), pltpu.SemaphoreType.DMA((TOKEN_RING,))]
    kern = functools.partial(_mix_out_kernel, tm=tm, seq=seq, pos0=pos0, n_heads=n_heads)
    return pl.pallas_call(
        kern,
        grid=(m // tm,),
        in_specs=[
            pl.BlockSpec((tm, d_gdn), lambda i: (i, 0)),
            pl.BlockSpec((tm, d_pool), lambda i: (i, 0)),
            hist_spec,
            pl.BlockSpec((tm, d), lambda i: (i, 0)) if token_job is None else pl.BlockSpec(memory_space=pl.ANY),
            pl.BlockSpec(w_pool.shape, lambda i: (0, 0, 0), pipeline_mode=pl.Buffered(1)),
            pl.BlockSpec((1, d_pool), lambda i: (0, 0)),
            pl.BlockSpec(w_out.shape, lambda i: (0, 0), pipeline_mode=pl.Buffered(1)),
            pl.BlockSpec((1, d), lambda i: (0, 0)),
            *tok_specs,
        ],
        out_specs=out_specs,
        out_shape=out_shape,
        scratch_shapes=scratch,
        compiler_params=_cparams(1, VMEM_LIMIT if token_job is None else 63 * 1024 * 1024),
        name="mix_out",
    )(o_a, u, hist, x, w_pool, pool_scale, w_out, norm_w, *tok_operands)


def _mlp_kernel(x_ref, nw1_ref, wu_ref, wd_ref, nw2_ref, xe_ref, out_ref, oute_ref, h_ref, he_ref, *, n_ff_tiles):
    i = pl.program_id(0)
    j = pl.program_id(1)

    def rows_step(first, last, x_ref, h_ref, out_ref):
        if first:
            h = _rms(x_ref[...], nw1_ref[...]).astype(BF16)
            h_ref[...] = h
        else:
            h = h_ref[...]
        up = jnp.dot(h, wu_ref[...], preferred_element_type=F32)
        act = jnp.square(jnp.maximum(up, 0.0)).astype(BF16)
        part = jnp.dot(act, wd_ref[...], preferred_element_type=F32)
        acc = part if first else out_ref[...] + part
        out_ref[...] = x_ref[...] + _rms(acc, nw2_ref[...]) if last else acc

    def step(first, last):
        rows_step(first, last, x_ref, h_ref, out_ref)
        pl.when(i == 0)(lambda: rows_step(first, last, xe_ref, he_ref, oute_ref))

    nj = n_ff_tiles
    if nj == 1:
        step(True, True)
    else:
        pl.when(j == 0)(lambda: step(True, False))
        pl.when((j > 0) & (j < nj - 1))(lambda: step(False, False))
        pl.when(j == nj - 1)(lambda: step(False, True))


def _mlp(x, nw1, w_up, w_down, nw2, tm, tf, x_extra):
    m, d = x.shape
    me = x_extra.shape[0]
    f = w_up.shape[1]
    whole = lambda i, j: (0, 0)
    out, out_extra = pl.pallas_call(
        functools.partial(_mlp_kernel, n_ff_tiles=f // tf),
        grid=(m // tm, f // tf),
        in_specs=[
            pl.BlockSpec((tm, d), lambda i, j: (i, 0)),
            pl.BlockSpec((1, d), whole),
            pl.BlockSpec((d, tf), lambda i, j: (0, j)),
            pl.BlockSpec((tf, d), lambda i, j: (j, 0)),
            pl.BlockSpec((1, d), whole),
            pl.BlockSpec((me, d), whole),
        ],
        out_specs=[pl.BlockSpec((tm, d), lambda i, j: (i, 0)), pl.BlockSpec((me, d), whole)],
        out_shape=[jax.ShapeDtypeStruct((m, d), F32), jax.ShapeDtypeStruct((me, d), F32)],
        scratch_shapes=[pltpu.VMEM((tm, d), BF16), pltpu.VMEM((me, d), BF16)],
        compiler_params=_cparams(2, MLP_VMEM_LIMIT),
        name="mlp",
    )(x, nw1, w_up, w_down, nw2, x_extra)
    return out, out_extra


def _pad_lanes(v, n=HEAD_DIM):
    return jnp.pad(v.reshape(1, -1), ((0, 0), (0, n - v.shape[-1])))


def kernel(x_prompt, x_sample, state_gdn, state_conv, state_pool, norm_pre_mix, w_in, conv_w, a_log, dt_bias,
           norm_gdn_out, w_pool, pool_scale, w_out, norm_post_mix, norm_pre_mlp, w_up, w_down, norm_post_mlp):
    batch, seq, d_model = x_prompt.shape
    dec_batch, dec_seq, _ = x_sample.shape
    assert dec_seq == 1
    depth = w_in.shape[0]
    n_heads = a_log.shape[1]
    d_gdn = n_heads * HEAD_DIM
    d_qkv = 3 * d_gdn
    d_pool = pool_scale.shape[1]
    o_gate = d_qkv + d_gdn

    yp = x_prompt.reshape(batch * seq, d_model)
    ys = x_sample.reshape(dec_batch, d_model)
    outs = [[] for _ in range(6)]
    for l in range(depth):
        alog = _pad_lanes(a_log[l])
        dtb = _pad_lanes(dt_bias[l])
        gnw = norm_gdn_out[l].reshape(1, HEAD_DIM)
        npm = norm_pre_mix[l].reshape(1, d_model)
        npo = norm_post_mix[l].reshape(1, d_model)
        nm1 = norm_pre_mlp[l].reshape(1, d_model)
        nm2 = norm_post_mlp[l].reshape(1, d_model)
        ps = pool_scale[l].reshape(1, d_pool)

        proj_s, u_s, gates_s, *w_in_b = _in_proj(
            ys, npm, jnp.swapaxes(w_in[l], 0, 1), o_gate, d_pool, alog, dtb, n_heads,
            tm=dec_batch, tn=SAMPLE_TN, emit=True)

        proj_p, u_p, gates_p = _in_proj(
            yp, npm, w_in_b, o_gate, d_pool, alog, dtb, n_heads, tm=IN_PROJ_TM, tn=IN_PROJ_TN, emit=False)
        to_cast = (w_up[l], w_down[l], w_out[l], w_pool[l].reshape(-1, w_pool.shape[-1]))
        gdn = functools.partial(_gdn_prompt, proj_p, gates_p, conv_w[l], gnw, batch=batch, seq=seq,
                                n_heads=n_heads, hs=GDN_HEADS_PER_STEP, rows=GDN_ROWS)
        oa_p, s_p, norm_p, w_up_b, w_down_b, w_out_b, w_pool_b = gdn(to_cast, precise=False)
        oa_p, s_p = lax.cond(jnp.max(norm_p) <= INV_FAST_NORM, lambda: (oa_p, s_p),
                             lambda: tuple(gdn((), precise=True)[:2]))
        w_pool_b = w_pool_b.reshape(w_pool.shape[1:])
        token_job = (proj_s, jnp.swapaxes(state_conv[l], 0, 1), gates_s, conv_w[l], gnw, state_gdn[l], n_heads)
        x1_p, oa_s, s_s, cs_new = _mix_out(oa_p, u_p, u_p, yp, w_pool_b, ps, w_out_b, npo, tm=MIX_TM, seq=seq,
                                           pos0=0, token_job=token_job)
        x1_s, hist_new = _mix_out(oa_s, u_s, jnp.swapaxes(state_pool[l], 0, 1), ys, w_pool_b, ps, w_out_b, npo,
                                  tm=dec_batch, seq=1, pos0=PAST_LEN)
        yp, ys = _mlp(x1_p, nm1, w_up_b, w_down_b, nm2, MLP_TM, MLP_TF, x1_s)
        outs[0].append(s_p)
        outs[1].append(proj_p.reshape(batch, seq, -1)[:, seq - (CONV_W - 1):, :d_qkv])
        outs[2].append(u_p.reshape(batch, seq, -1)[:, seq - POOL_BUF:])
        outs[3].append(s_s)
        outs[4].append(jnp.swapaxes(cs_new, 0, 1))
        outs[5].append(jnp.swapaxes(hist_new, 0, 1))

    return (yp.reshape(batch, seq, d_model), ys.reshape(dec_batch, dec_seq, d_model),
            *[jnp.stack(o) for o in outs])
```

```python
import functools

import jax
import jax.numpy as jnp
from jax import lax
from jax.experimental import pallas as pl
from jax.experimental.pallas import tpu as pltpu

F32 = jnp.float32
BF16 = jnp.bfloat16

EPS = 1e-6
HEAD_DIM = 128
CONV_W = 4
POOL_WINDOWS = (2, 4, 8, 16)
POOL_BUF = max(POOL_WINDOWS) - 1
PAST_LEN = 16384
HALO_CONV = 8
HALO_POOL = 16
GDN_CHUNK = 128
INV_BASE = 16
INV_FAST_NORM = 0.75
SUBLANES = 8
MASKED = -1e30
VMEM_LIMIT = 56 * 1024 * 1024
MLP_VMEM_LIMIT = 60 * 1024 * 1024

IN_PROJ_TM, IN_PROJ_TN, SAMPLE_TN = 1024, 1024, 1024
GDN_HEADS_PER_STEP, GDN_ROWS = 8, 256
MIX_TM = 512
TOKEN_RING = 3
MLP_TM, MLP_TF = 512, 2048


def _cparams(n_axes, vmem_limit=VMEM_LIMIT):
    return pltpu.CompilerParams(
        dimension_semantics=("arbitrary",) * n_axes, vmem_limit_bytes=vmem_limit)


def _dot(a, b):
    return jnp.dot(a.astype(BF16), b.astype(BF16), preferred_element_type=F32)


def _dot_nt(a, b):
    return lax.dot_general(a.astype(BF16), b.astype(BF16), (((1,), (1,)), ((), ())),
                           preferred_element_type=F32)


def _rms(x, w):
    return x * lax.rsqrt(jnp.mean(x * x, axis=-1, keepdims=True) + EPS) * w


def _silu(x):
    h = 0.5 * x
    return h + h * jnp.tanh(h)


def _softplus(x):
    return jnp.maximum(x, 0.0) + jnp.log(1.0 + jnp.exp(-jnp.abs(x)))


def _l2norm(x, scale=1.0):
    return x * (lax.rsqrt(jnp.sum(x * x, axis=-1, keepdims=True) + EPS) * scale)


def _inproj_kernel(x_ref, nw_ref, wa_ref, wt_ref, wc_ref, alog_ref, dtb_ref, *rest, n_heads, n_main, emit):
    if emit:
        out_ref, u_ref, gates_ref, wa_out, wt_out, wc_out, h_ref = rest
    else:
        out_ref, u_ref, gates_ref, h_ref = rest
    j = pl.program_id(1)

    def main_step(first):
        if first:
            h = _rms(x_ref[...], nw_ref[...]).astype(BF16)
            h_ref[...] = h
        else:
            h = h_ref[...]
        w = wa_ref[...].T.astype(BF16) if emit else wa_ref[...]
        if emit:
            wa_out[...] = w
        out_ref[...] = jnp.dot(h, w, preferred_element_type=F32)

    pl.when(j == 0)(lambda: main_step(True))
    pl.when((j > 0) & (j < n_main))(lambda: main_step(False))

    @pl.when(j == n_main)
    def _():
        if emit:
            n_gate = 2 * n_heads
            blk = wt_ref[...]
            pool_rows = jnp.concatenate([blk[n_gate:], wc_ref[...]], axis=0)
            gate_rows = jnp.concatenate([blk[:n_gate], jnp.zeros((HEAD_DIM - n_gate, blk.shape[1]), F32)], axis=0)
            wt = pool_rows.T.astype(BF16)
            wc = gate_rows.T.astype(BF16)
            wt_out[...] = wt
            wc_out[...] = wc
        else:
            wt = wt_ref[...]
            wc = wc_ref[...]
        h = h_ref[...]
        u_ref[...] = jnp.dot(h, wt, preferred_element_type=F32)
        ab = jnp.dot(h, wc, preferred_element_type=F32)
        lane = lax.broadcasted_iota(jnp.int32, ab.shape, 1)
        g = -jnp.exp(alog_ref[...]) * _softplus(ab + dtb_ref[...])
        gates_ref[...] = jnp.where(lane < n_heads, g, jax.nn.sigmoid(ab))


def _in_proj(x, norm_w, weights, n, d_pool, alog, dtb, n_heads, tm, tn, emit):
    m, d = x.shape
    n_main = n // tn
    n_gate = 2 * n_heads
    vec = pl.BlockSpec((1, HEAD_DIM), lambda i, j: (0, 0))
    out_specs = [
        pl.BlockSpec((tm, tn), lambda i, j: (i, jnp.minimum(j, n_main - 1))),
        pl.BlockSpec((tm, d_pool), lambda i, j: (i, 0)),
        pl.BlockSpec((tm, HEAD_DIM), lambda i, j: (i, 0)),
    ]
    out_shape = [jax.ShapeDtypeStruct((m, n), F32), jax.ShapeDtypeStruct((m, d_pool), F32),
                 jax.ShapeDtypeStruct((m, HEAD_DIM), F32)]
    wa_bf_spec = pl.BlockSpec((d, tn), lambda i, j: (0, jnp.minimum(j, n_main - 1)))
    wt_bf_spec = pl.BlockSpec((d, d_pool), lambda i, j: (0, 0), pipeline_mode=pl.Buffered(1))
    wc_bf_spec = pl.BlockSpec((d, HEAD_DIM), lambda i, j: (0, 0))
    if emit:
        assert n % d_pool == 0 and (n + d_pool) % n_gate == 0 and n_gate % SUBLANES == 0
        operands = (weights, weights, weights)
        w_specs = [
            pl.BlockSpec((tn, d), lambda i, j: (jnp.minimum(j, n_main - 1), 0)),
            pl.BlockSpec((d_pool, d), lambda i, j: (n // d_pool, 0), pipeline_mode=pl.Buffered(1)),
            pl.BlockSpec((n_gate, d), lambda i, j: ((n + d_pool) // n_gate, 0)),
        ]
        out_specs += [wa_bf_spec, pl.BlockSpec((d, d_pool), lambda i, j: (0, 0)), wc_bf_spec]
        out_shape += [jax.ShapeDtypeStruct((d, n), BF16), jax.ShapeDtypeStruct((d, d_pool), BF16),
                      jax.ShapeDtypeStruct((d, HEAD_DIM), BF16)]
    else:
        operands = weights
        w_specs = [wa_bf_spec, wt_bf_spec, wc_bf_spec]
    return pl.pallas_call(
        functools.partial(_inproj_kernel, n_heads=n_heads, n_main=n_main, emit=emit),
        grid=(m // tm, n_main + 1),
        in_specs=[
            pl.BlockSpec((tm, d), lambda i, j: (i, 0)),
            pl.BlockSpec((1, d), lambda i, j: (0, 0)),
            *w_specs,
            vec, vec,
        ],
        out_specs=out_specs,
        out_shape=out_shape,
        scratch_shapes=[pltpu.VMEM((tm, d), BF16)],
        compiler_params=_cparams(2),
        name="in_proj",
    )(x, norm_w, *operands, alog, dtb)


def _lane_column(x, lane_idx):
    lane = lax.broadcasted_iota(jnp.int32, x.shape, 1)
    col = jnp.sum(jnp.where(lane == lane_idx, x, 0.0), axis=1, keepdims=True)
    return jnp.broadcast_to(col, x.shape)


def _unit_lower_inverses(lms, precise):
    n = lms[0].shape[0]
    row = lax.broadcasted_iota(jnp.int32, (n, n), 0)
    col = lax.broadcasted_iota(jnp.int32, (n, n), 1)
    eye = jnp.where(row == col, 1.0, 0.0)

    def same_block(size):
        shift = size.bit_length() - 1
        return lax.shift_right_logical(row, shift) == lax.shift_right_logical(col, shift)

    if precise:
        op = lambda x: x
        mm = functools.partial(jnp.dot, preferred_element_type=F32, precision=lax.Precision.HIGHEST)
        terms = INV_BASE
        in_base = same_block(INV_BASE)
        lbs = [jnp.where(in_base, lm, 0.0) for lm in lms]
    else:
        op = lambda x: x.astype(BF16)
        mm = functools.partial(jnp.dot, preferred_element_type=F32)
        terms = n // 2
        assert INV_FAST_NORM ** terms / (1.0 - INV_FAST_NORM) < 2.0 ** -24
        lbs = lms

    qs = [eye - lb for lb in lbs]
    lbs = [op(lb) for lb in lbs]
    nks = [mm(lb, lb) for lb in lbs]
    power = 2
    while 2 * power < terms:
        nbs = [op(nk) for nk in nks]
        res = [mm(nb, jnp.concatenate([nb, op(q)], axis=1)) for nb, q in zip(nbs, qs)]
        nks = [r[:, :n] for r in res]
        qs = [q + r[:, n:] for q, r in zip(qs, res)]
        power *= 2
    ps = [q + mm(op(nk), op(q)) for nk, q in zip(nks, qs)]
    if not precise:
        return ps
    size = INV_BASE
    while size < n:
        off = jnp.logical_and(same_block(2 * size), jnp.logical_not(same_block(size)))
        cs = [op(jnp.where(off, lm, 0.0)) for lm in lms]
        pbs = [op(p) for p in ps]
        ts = [op(mm(c, pb)) for c, pb in zip(cs, pbs)]
        ps = [p - mm(pb, t) for p, pb, t in zip(ps, pbs, ts)]
        size *= 2
    return ps


def _gdn_prompt_kernel(q_ref, k_ref, v_ref, gate_ref, hq_ref, hk_ref, hv_ref, gates_ref,
                       cwq_ref, cwk_ref, cwv_ref, gnw_ref, *rest, n_heads, hs, rows, n_cast, precise):
    cast_in = rest[:n_cast]
    o_ref, sout_ref, norm_ref = rest[n_cast:n_cast + 3]
    cast_out = rest[n_cast + 3:2 * n_cast + 3]
    s_ref, ext_ref = rest[2 * n_cast + 3:]
    hg = pl.program_id(1)
    blk = pl.program_id(2)
    c = GDN_CHUNK

    for src, dst in zip(cast_in, cast_out):
        dst[...] = src[...].astype(dst.dtype)

    @pl.when(blk == 0)
    def _():
        s_ref[...] = jnp.zeros_like(s_ref)

    for a, (x_ref, halo_ref) in enumerate(((q_ref, hq_ref), (k_ref, hk_ref), (v_ref, hv_ref))):
        for hi in range(hs):
            lanes = slice(hi * HEAD_DIM, (hi + 1) * HEAD_DIM)
            ext_ref[a * hs + hi, 0:HALO_CONV, :] = jnp.where(blk > 0, halo_ref[:, lanes], 0.0)
            ext_ref[a * hs + hi, HALO_CONV:, :] = x_ref[:, lanes]

    def conv_silu(a, cw_ref, p):
        ci, hi = p
        lanes = slice(hi * HEAD_DIM, (hi + 1) * HEAD_DIM)
        acc = None
        for j in range(CONV_W):
            term = (cw_ref[j:j + 1, lanes]
                    * ext_ref[a * hs + hi, pl.ds(HALO_CONV - (CONV_W - 1) + j + ci * c, c), :])
            acc = term if acc is None else acc + term
        return _silu(acc)

    row = lax.broadcasted_iota(jnp.int32, (c, c), 0)
    col = lax.broadcasted_iota(jnp.int32, (c, c), 1)
    tril = row >= col
    strict = row > col
    tril_f = jnp.where(tril, 1.0, 0.0)
    gnw = gnw_ref[...]
    heads = range(hs)

    n_chunks = rows // c
    probs = [(ci, hi) for ci in range(n_chunks) for hi in heads]
    n_p = range(len(probs))
    gates = [gates_ref[ci * c:(ci + 1) * c, :] for ci in range(n_chunks)]
    gcum = [jnp.dot(tril_f, g, preferred_element_type=F32, precision=lax.Precision.HIGHEST) for g in gates]
    qc = [_l2norm(conv_silu(0, cwq_ref, p), HEAD_DIM ** -0.5) for p in probs]
    kc = [_l2norm(conv_silu(1, cwk_ref, p)) for p in probs]
    vc = [conv_silu(2, cwv_ref, p) for p in probs]
    gc = [_lane_column(gcum[ci], hg * hs + hi) for ci, hi in probs]
    bc = [_lane_column(gates[ci], hg * hs + hi + n_heads) for ci, hi in probs]
    decay = [jnp.exp(jnp.where(tril, g - g.T, MASKED)) for g in gc]
    egc = [jnp.exp(g) for g in gc]
    g_last = [g[c - 1:c, :] for g in gc]
    kb = [k.astype(BF16) for k in kc]
    kk = [_dot_nt(k, k) for k in kb]
    qk = [_dot_nt(q, k) for q, k in zip(qc, kb)]
    lm = [jnp.where(strict, kk[p] * bc[p] * decay[p], 0.0) for p in n_p]
    row_sums = [jnp.sum(jnp.abs(x), axis=1, keepdims=True) for x in lm]
    worst = jnp.max(functools.reduce(jnp.maximum, row_sums), axis=0, keepdims=True)
    norm_ref[...] = jnp.broadcast_to(worst.reshape(1, 1, 1), norm_ref.shape)
    t_inv = _unit_lower_inverses(lm, precise)
    uw = [_dot(t_inv[p], jnp.concatenate([vc[p] * bc[p], kc[p] * (bc[p] * egc[p])], axis=1)) for p in n_p]
    intra = [qk[p] * decay[p] for p in n_p]
    kdt = [(kc[p] * jnp.exp(g_last[p] - gc[p])).T for p in n_p]
    wq = [jnp.concatenate([uw[p][:, HEAD_DIM:], qc[p] * egc[p]], axis=0) for p in n_p]
    ik = [jnp.concatenate([intra[p], kdt[p]], axis=0) for p in n_p]

    s = [s_ref[hi] for hi in heads]
    for ci in range(n_chunks):
        ps = [ci * hs + hi for hi in heads]
        r1 = [_dot(wq[p], s[hi]) for hi, p in zip(heads, ps)]
        v_new = [uw[p][:, :HEAD_DIM] - r1[hi][:c] for hi, p in zip(heads, ps)]
        r2 = [_dot(ik[p], v_new[hi]) for hi, p in zip(heads, ps)]
        s = [s[hi] * jnp.exp(g_last[p]) + r2[hi][c:] for hi, p in zip(heads, ps)]
        for hi in heads:
            lanes = slice(hi * HEAD_DIM, (hi + 1) * HEAD_DIM)
            o = _rms(r1[hi][c:] + r2[hi][:c], gnw) * _silu(gate_ref[ci * c:(ci + 1) * c, lanes])
            o_ref[ci * c:(ci + 1) * c, lanes] = o.astype(o_ref.dtype)

    for hi in range(hs):
        s_ref[hi] = s[hi]

    @pl.when(blk == pl.num_programs(2) - 1)
    def _():
        for hi in range(hs):
            sout_ref[0, hi] = s[hi]


def _gdn_prompt(proj, gates, conv_w, gnw, to_cast, batch, seq, n_heads, hs, rows, precise):
    nblk = seq // rows
    hb = rows // HALO_CONV
    ng = n_heads // hs
    width = hs * HEAD_DIM
    steps = batch * ng * nblk

    def slab(w):
        return pl.BlockSpec((w.shape[0] // steps, w.shape[1]), lambda b, h, k: ((b * ng + h) * nblk + k, 0))

    def main(off):
        return pl.BlockSpec((rows, width), lambda b, h, k: (b * nblk + k, off * ng + h))

    def halo(off):
        return pl.BlockSpec(
            (HALO_CONV, width), lambda b, h, k: (jnp.maximum((b * nblk + k) * hb - 1, 0), off * ng + h))

    def cw(off):
        return pl.BlockSpec((CONV_W, width), lambda b, h, k: (0, off * ng + h))

    kern = functools.partial(_gdn_prompt_kernel, n_heads=n_heads, hs=hs, rows=rows, n_cast=len(to_cast),
                             precise=precise)
    return pl.pallas_call(
        kern,
        grid=(batch, ng, nblk),
        in_specs=[main(0), main(1), main(2), main(3), halo(0), halo(1), halo(2),
                  pl.BlockSpec((rows, HEAD_DIM), lambda b, h, k: (b * nblk + k, 0)),
                  cw(0), cw(1), cw(2),
                  pl.BlockSpec((1, HEAD_DIM), lambda b, h, k: (0, 0)),
                  *[slab(w) for w in to_cast]],
        out_specs=[
            pl.BlockSpec((rows, width), lambda b, h, k: (b * nblk + k, h)),
            pl.BlockSpec((1, hs, HEAD_DIM, HEAD_DIM), lambda b, h, k: (b, h, 0, 0)),
            pl.BlockSpec((1, SUBLANES, HEAD_DIM), lambda b, h, k: ((b * ng + h) * nblk + k, 0, 0)),
            *[slab(w) for w in to_cast],
        ],
        out_shape=[
            jax.ShapeDtypeStruct((batch * seq, n_heads * HEAD_DIM), BF16),
            jax.ShapeDtypeStruct((batch, n_heads, HEAD_DIM, HEAD_DIM), F32),
            jax.ShapeDtypeStruct((steps, SUBLANES, HEAD_DIM), F32),
            *[jax.ShapeDtypeStruct(w.shape, BF16) for w in to_cast],
        ],
        scratch_shapes=[pltpu.VMEM((hs, HEAD_DIM, HEAD_DIM), F32),
                        pltpu.VMEM((3 * hs, rows + HALO_CONV, HEAD_DIM), F32)],
        compiler_params=_cparams(3),
        name="gdn_prompt",
    )(proj, proj, proj, proj, proj, proj, proj, gates, conv_w, conv_w, conv_w, gnw, *to_cast)


def _gdn_token_steps(x_ref, cs_ref, gates_ref, cw_ref, gnw_ref, s_in_ref, o_ref, s_out_ref, cs_out_ref, n_heads):
    d = n_heads * HEAD_DIM
    sq = (HEAD_DIM, HEAD_DIM)
    x = x_ref[...]
    cw = cw_ref[...]
    conv = cw[CONV_W - 1:CONV_W, :] * x[:, :3 * d]
    for j in range(CONV_W - 1):
        conv = conv + cw[j:j + 1, :] * cs_ref[j]
    qkv = _silu(conv)
    for j in range(CONV_W - 2):
        cs_out_ref[j] = cs_ref[j + 1]
    cs_out_ref[CONV_W - 2] = x[:, :3 * d]

    gates = gates_ref[...]
    eg = jnp.exp(gates)
    gnw = gnw_ref[...]
    heads = range(n_heads)
    qn = [_l2norm(qkv[:, h * HEAD_DIM:(h + 1) * HEAD_DIM], HEAD_DIM ** -0.5) for h in heads]
    kn = [_l2norm(qkv[:, d + h * HEAD_DIM:d + (h + 1) * HEAD_DIM]) for h in heads]
    for t in range(x.shape[0]):
        row = slice(t, t + 1)
        v = [qkv[row, 2 * d + h * HEAD_DIM:2 * d + (h + 1) * HEAD_DIM] for h in heads]
        kb = [jnp.broadcast_to(kn[h][row], sq).T for h in heads]
        qb = [jnp.broadcast_to(qn[h][row], sq).T for h in heads]
        s1 = [s_in_ref[t, h] * eg[row, h:h + 1] for h in heads]
        ks = [jnp.sum(s1[h] * kb[h], axis=0, keepdims=True) for h in heads]
        delta = [(v[h] - ks[h]) * gates[row, n_heads + h:n_heads + h + 1] for h in heads]
        s2 = [s1[h] + kb[h] * delta[h] for h in heads]
        for h in heads:
            lanes = slice(h * HEAD_DIM, (h + 1) * HEAD_DIM)
            s_out_ref[t, h] = s2[h]
            o = jnp.sum(s2[h] * qb[h], axis=0, keepdims=True)
            o_ref[row, lanes] = _rms(o, gnw) * _silu(x[row, 3 * d + h * HEAD_DIM:3 * d + (h + 1) * HEAD_DIM])


def _mix_out_kernel(oa_ref, u_ref, hist_ref, x_ref, wp_ref, ps_ref, wo_ref, nw_ref, *rest,
                    tm, seq, pos0, n_heads):
    if n_heads:
        tok_in, out_ref, tok_out = rest[:6], rest[6], rest[7:10]
        *rest, ring_ref, sem = rest[10:]
        state_hbm = tok_in[5]
        step, n_steps, nt = pl.program_id(0), pl.num_programs(0), ring_ref.shape[1]

        def state_copy(s):
            slot = s % TOKEN_RING
            return pltpu.make_async_copy(state_hbm.at[pl.ds(s * nt, nt)], ring_ref.at[slot], sem.at[slot])

        @pl.when(step == 0)
        def _():
            for s in range(TOKEN_RING - 1):
                state_copy(s).start(priority=1)

        @pl.when(step + (TOKEN_RING - 1) < n_steps)
        def _():
            state_copy(step + (TOKEN_RING - 1)).start(priority=1)

        state_copy(step).wait()
        _gdn_token_steps(*tok_in[:5], ring_ref.at[step % TOKEN_RING], *tok_out, n_heads)
    else:
        out_ref, rest = rest[0], rest[1:]
    d_pool = u_ref.shape[1]
    d_gdn = oa_ref.shape[1]
    d = out_ref.shape[1]
    n_groups = len(POOL_WINDOWS)
    gd = d_pool // n_groups
    strips = gd // HEAD_DIM
    if seq > 1:
        (ext_ref,) = rest
        start = (pl.program_id(0) * tm) % seq
        for si in range(d_pool // HEAD_DIM):
            lanes = slice(si * HEAD_DIM, (si + 1) * HEAD_DIM)
            ext_ref[si, 0:HALO_POOL, :] = jnp.where(start > 0, hist_ref[:, lanes], 0.0)
            ext_ref[si, HALO_POOL:, :] = u_ref[:, lanes]
        pos = pos0 + start + lax.broadcasted_iota(jnp.int32, (tm, HEAD_DIM), 0)
    else:
        (hist_out_ref,) = rest
        for r in range(POOL_BUF - 1):
            hist_out_ref[r] = hist_ref[r + 1]
        hist_out_ref[POOL_BUF - 1] = u_ref[...]
        pos = jnp.full((tm, HEAD_DIM), pos0, jnp.int32)

    tn = d // n_groups
    oa = oa_ref[...].astype(BF16)
    parts = []
    for gi, win in enumerate(POOL_WINDOWS):
        cols = slice(gi * tn, (gi + 1) * tn)
        out_ref[:, cols] = jnp.dot(oa, wo_ref[:d_gdn, cols], preferred_element_type=F32)
        cnt = jnp.minimum(pos + 1, win).astype(F32)
        pooled = []
        for si in range(gi * strips, (gi + 1) * strips):
            lanes = slice(si * HEAD_DIM, (si + 1) * HEAD_DIM)
            cur = u_ref[:, lanes]
            wsum = cur
            for r in range(1, win):
                if seq > 1:
                    wsum = wsum + ext_ref[si, pl.ds(HALO_POOL - r, tm), :]
                else:
                    wsum = wsum + hist_ref[POOL_BUF - r, :, lanes]
            pooled.append(wsum / cnt - cur)
        ob = _dot(jnp.concatenate(pooled, axis=1), wp_ref[gi]) * ps_ref[:, gi * gd:(gi + 1) * gd]
        parts.append(ob.astype(BF16))
    ob_all = jnp.concatenate(parts, axis=1)
    mix = [out_ref[:, ci * tn:(ci + 1) * tn]
           + jnp.dot(ob_all, wo_ref[d_gdn:, ci * tn:(ci + 1) * tn], preferred_element_type=F32)
           for ci in range(n_groups)]
    ss = sum(jnp.sum(m * m, axis=-1, keepdims=True) for m in mix)
    inv = lax.rsqrt(ss * (1.0 / d) + EPS)
    for ci in range(n_groups):
        cols = slice(ci * tn, (ci + 1) * tn)
        out_ref[:, cols] = x_ref[:, cols] + mix[ci] * inv * nw_ref[:, cols]


def _mix_out(o_a, u, hist, x, w_pool, pool_scale, w_out, norm_w, tm, seq, pos0, token_job=None):
    m, d = x.shape
    d_pool = pool_scale.shape[1]
    d_gdn = o_a.shape[1]
    out_specs = [pl.BlockSpec((tm, d), lambda i: (i, 0))]
    out_shape = [jax.ShapeDtypeStruct((m, d), F32)]
    tok_specs, tok_operands, n_heads = [], [], 0
    if token_job is not None:
        proj, conv_state, gates, conv_w, gnw, state, n_heads = token_job
        tokens, dg = proj.shape[0], n_heads * HEAD_DIM
        nt = tokens // (m // tm)
        assert seq > 1 and nt * (m // tm) == tokens and nt % SUBLANES == 0
        cs_spec = pl.BlockSpec((CONV_W - 1, nt, 3 * dg), lambda i: (0, i, 0))
        st_spec = pl.BlockSpec((nt, n_heads, HEAD_DIM, HEAD_DIM), lambda i: (i, 0, 0, 0))
        tok_specs = [
            pl.BlockSpec((nt, 4 * dg), lambda i: (i, 0)),
            cs_spec,
            pl.BlockSpec((nt, HEAD_DIM), lambda i: (i, 0)),
            pl.BlockSpec((CONV_W, 3 * dg), lambda i: (0, 0)),
            pl.BlockSpec((1, HEAD_DIM), lambda i: (0, 0)),
            pl.BlockSpec(memory_space=pl.ANY),
        ]
        tok_operands = [proj, conv_state, gates, conv_w, gnw, state]
        out_specs += [pl.BlockSpec((nt, dg), lambda i: (i, 0)), st_spec, cs_spec]
        out_shape += [jax.ShapeDtypeStruct((tokens, dg), F32), jax.ShapeDtypeStruct(state.shape, F32),
                      jax.ShapeDtypeStruct(conv_state.shape, F32)]
    if seq > 1:
        hb = tm // HALO_POOL
        hist_spec = pl.BlockSpec((HALO_POOL, d_pool), lambda i: (jnp.maximum(i * hb - 1, 0), 0))
        scratch = [pltpu.VMEM((d_pool // HEAD_DIM, tm + HALO_POOL, HEAD_DIM), F32)]
    else:
        hist_spec = pl.BlockSpec((POOL_BUF, tm, d_pool), lambda i: (0, i, 0))
        out_specs.append(hist_spec)
        out_shape.append(jax.ShapeDtypeStruct(hist.shape, F32))
        scratch = []
    if token_job is not None:
        assert m // tm >= TOKEN_RING - 1
        scratch += [pltpu.VMEM((TOKEN_RING, nt, n_heads, HEAD_DIM, HEAD_DIM), F32),
                    pltpu.SemaphoreType.DMA((TOKEN_RING,))]
    kern = functools.partial(_mix_out_kernel, tm=tm, seq=seq, pos0=pos0, n_heads=n_heads)
    return pl.pallas_call(
        kern,
        grid=(m // tm,),
        in_specs=[
            pl.BlockSpec((tm, d_gdn), lambda i: (i, 0)),
            pl.BlockSpec((tm, d_pool), lambda i: (i, 0)),
            hist_spec,
            pl.BlockSpec((tm, d), lambda i: (i, 0)),
            pl.BlockSpec(w_pool.shape, lambda i: (0, 0, 0), pipeline_mode=pl.Buffered(1)),
            pl.BlockSpec((1, d_pool), lambda i: (0, 0)),
            pl.BlockSpec(w_out.shape, lambda i: (0, 0), pipeline_mode=pl.Buffered(1)),
            pl.BlockSpec((1, d), lambda i: (0, 0)),
            *tok_specs,
        ],
        out_specs=out_specs,
        out_shape=out_shape,
        scratch_shapes=scratch,
        compiler_params=_cparams(1, VMEM_LIMIT if token_job is None else MLP_VMEM_LIMIT),
        name="mix_out",
    )(o_a, u, hist, x, w_pool, pool_scale, w_out, norm_w, *tok_operands)


def _mlp_kernel(x_ref, nw1_ref, wu_ref, wd_ref, nw2_ref, xe_ref, out_ref, oute_ref, h_ref, he_ref, *, n_ff_tiles):
    i = pl.program_id(0)
    j = pl.program_id(1)

    def rows_step(first, last, x_ref, h_ref, out_ref):
        if first:
            h = _rms(x_ref[...], nw1_ref[...]).astype(BF16)
            h_ref[...] = h
        else:
            h = h_ref[...]
        up = jnp.dot(h, wu_ref[...], preferred_element_type=F32)
        act = jnp.square(jnp.maximum(up, 0.0)).astype(BF16)
        part = jnp.dot(act, wd_ref[...], preferred_element_type=F32)
        acc = part if first else out_ref[...] + part
        out_ref[...] = x_ref[...] + _rms(acc, nw2_ref[...]) if last else acc

    def step(first, last):
        rows_step(first, last, x_ref, h_ref, out_ref)
        pl.when(i == 0)(lambda: rows_step(first, last, xe_ref, he_ref, oute_ref))

    nj = n_ff_tiles
    if nj == 1:
        step(True, True)
    else:
        pl.when(j == 0)(lambda: step(True, False))
        pl.when((j > 0) & (j < nj - 1))(lambda: step(False, False))
        pl.when(j == nj - 1)(lambda: step(False, True))


def _mlp(x, nw1, w_up, w_down, nw2, tm, tf, x_extra):
    m, d = x.shape
    me = x_extra.shape[0]
    f = w_up.shape[1]
    whole = lambda i, j: (0, 0)
    out, out_extra = pl.pallas_call(
        functools.partial(_mlp_kernel, n_ff_tiles=f // tf),
        grid=(m // tm, f // tf),
        in_specs=[
            pl.BlockSpec((tm, d), lambda i, j: (i, 0)),
            pl.BlockSpec((1, d), whole),
            pl.BlockSpec((d, tf), lambda i, j: (0, j)),
            pl.BlockSpec((tf, d), lambda i, j: (j, 0)),
            pl.BlockSpec((1, d), whole),
            pl.BlockSpec((me, d), whole),
        ],
        out_specs=[pl.BlockSpec((tm, d), lambda i, j: (i, 0)), pl.BlockSpec((me, d), whole)],
        out_shape=[jax.ShapeDtypeStruct((m, d), F32), jax.ShapeDtypeStruct((me, d), F32)],
        scratch_shapes=[pltpu.VMEM((tm, d), BF16), pltpu.VMEM((me, d), BF16)],
        compiler_params=_cparams(2, MLP_VMEM_LIMIT),
        name="mlp",
    )(x, nw1, w_up, w_down, nw2, x_extra)
    return out, out_extra


def _pad_lanes(v, n=HEAD_DIM):
    return jnp.pad(v.reshape(1, -1), ((0, 0), (0, n - v.shape[-1])))


def kernel(x_prompt, x_sample, state_gdn, state_conv, state_pool, norm_pre_mix, w_in, conv_w, a_log, dt_bias,
           norm_gdn_out, w_pool, pool_scale, w_out, norm_post_mix, norm_pre_mlp, w_up, w_down, norm_post_mlp):
    batch, seq, d_model = x_prompt.shape
    dec_batch, dec_seq, _ = x_sample.shape
    assert dec_seq == 1
    depth = w_in.shape[0]
    n_heads = a_log.shape[1]
    d_gdn = n_heads * HEAD_DIM
    d_qkv = 3 * d_gdn
    d_pool = pool_scale.shape[1]
    o_gate = d_qkv + d_gdn

    yp = x_prompt.reshape(batch * seq, d_model)
    ys = x_sample.reshape(dec_batch, d_model)
    outs = [[] for _ in range(6)]
    for l in range(depth):
        alog = _pad_lanes(a_log[l])
        dtb = _pad_lanes(dt_bias[l])
        gnw = norm_gdn_out[l].reshape(1, HEAD_DIM)
        npm = norm_pre_mix[l].reshape(1, d_model)
        npo = norm_post_mix[l].reshape(1, d_model)
        nm1 = norm_pre_mlp[l].reshape(1, d_model)
        nm2 = norm_post_mlp[l].reshape(1, d_model)
        ps = pool_scale[l].reshape(1, d_pool)

        proj_s, u_s, gates_s, *w_in_b = _in_proj(
            ys, npm, jnp.swapaxes(w_in[l], 0, 1), o_gate, d_pool, alog, dtb, n_heads,
            tm=dec_batch, tn=SAMPLE_TN, emit=True)

        proj_p, u_p, gates_p = _in_proj(
            yp, npm, w_in_b, o_gate, d_pool, alog, dtb, n_heads, tm=IN_PROJ_TM, tn=IN_PROJ_TN, emit=False)
        to_cast = (w_up[l], w_down[l], w_out[l], w_pool[l].reshape(-1, w_pool.shape[-1]))
        gdn = functools.partial(_gdn_prompt, proj_p, gates_p, conv_w[l], gnw, batch=batch, seq=seq,
                                n_heads=n_heads, hs=GDN_HEADS_PER_STEP, rows=GDN_ROWS)
        oa_p, s_p, norm_p, w_up_b, w_down_b, w_out_b, w_pool_b = gdn(to_cast, precise=False)
        oa_p, s_p = lax.cond(jnp.max(norm_p) <= INV_FAST_NORM, lambda: (oa_p, s_p),
                             lambda: tuple(gdn((), precise=True)[:2]))
        w_pool_b = w_pool_b.reshape(w_pool.shape[1:])
        token_job = (proj_s, jnp.swapaxes(state_conv[l], 0, 1), gates_s, conv_w[l], gnw, state_gdn[l], n_heads)
        x1_p, oa_s, s_s, cs_new = _mix_out(oa_p, u_p, u_p, yp, w_pool_b, ps, w_out_b, npo, tm=MIX_TM, seq=seq,
                                           pos0=0, token_job=token_job)
        x1_s, hist_new = _mix_out(oa_s, u_s, jnp.swapaxes(state_pool[l], 0, 1), ys, w_pool_b, ps, w_out_b, npo,
                                  tm=dec_batch, seq=1, pos0=PAST_LEN)
        yp, ys = _mlp(x1_p, nm1, w_up_b, w_down_b, nm2, MLP_TM, MLP_TF, x1_s)
        outs[0].append(s_p)
        outs[1].append(proj_p.reshape(batch, seq, -1)[:, seq - (CONV_W - 1):, :d_qkv])
        outs[2].append(u_p.reshape(batch, seq, -1)[:, seq - POOL_BUF:])
        outs[3].append(s_s)
        outs[4].append(jnp.swapaxes(cs_new, 0, 1))
        outs[5].append(jnp.swapaxes(hist_new, 0, 1))

    return (yp.reshape(batch, seq, d_model), ys.reshape(dec_batch, dec_seq, d_model),
            *[jnp.stack(o) for o in outs])
```
